```python
import math
import jax
import jax.numpy as jnp
from jax import lax
import numpy as np

D_MODEL = 1024
BATCH = 8
SEQ = 4096
DEPTH = 2

GRID_W = 64
CTX_LEN = 256
QBLOCK = 128
ROPE_THETA = 10000.0
NORM_EPS = 1e-6
N_MOD = 6

MIX_WIDTH = D_MODEL
HEAD_DIM = 64
GQA_HEADS = 8
GQA_KV_HEADS = 2
GQA_GROUP = GQA_HEADS // GQA_KV_HEADS
GQA_WIDTH = GQA_HEADS * HEAD_DIM
DIFF_HEADS = 4
DIFF_QK_DIM = HEAD_DIM // 2
DIFF_V_DIM = HEAD_DIM
DIFF_WIDTH = DIFF_HEADS * DIFF_V_DIM
MLA_HEADS = 4
MLA_Q_RANK = 192
MLA_KV_RANK = 128
MLA_NOPE_DIM = 64
MLA_ROPE_DIM = 32
MLA_V_DIM = 64
MLA_WIDTH = MLA_HEADS * MLA_V_DIM
IN_SPLITS = (GQA_HEADS * HEAD_DIM, GQA_KV_HEADS * HEAD_DIM, GQA_KV_HEADS * HEAD_DIM,
             DIFF_HEADS * 2 * DIFF_QK_DIM, DIFF_HEADS * 2 * DIFF_QK_DIM, DIFF_HEADS * DIFF_V_DIM,
             MLA_Q_RANK, MLA_KV_RANK, MLA_ROPE_DIM)
IN_WIDTH = sum(IN_SPLITS)
D_FF_DENSE = 2816
N_EXPERTS = 8
TOP_K = 2
D_FF_EXPERT = 1792

kernel_name = "hybrid_diffusion_prefix_block"


def _rmsnorm(x, g):
    xf = x.astype(jnp.float32)
    y = xf * lax.rsqrt(jnp.mean(xf * xf, axis=-1, keepdims=True) + NORM_EPS)
    return (y * g.astype(jnp.float32)).astype(x.dtype)


def _modulate(h, shift, scale):
    return h * (1 + scale) + shift


def _split_cols(y):
    out, start = [], 0
    for w in IN_SPLITS:
        out.append(y[..., start:start + w])
        start += w
    return out


def _axial_rope_table(rows, cols, dim):
    quarter = dim // 4
    inv_freq = ROPE_THETA ** (-jnp.arange(quarter, dtype=jnp.float32) / quarter)
    ang = jnp.concatenate([rows[:, None] * inv_freq, cols[:, None] * inv_freq], axis=-1)
    return jnp.cos(ang)[:, None, :], jnp.sin(ang)[:, None, :]


def _rope(x, table):
    cos, sin = table[0].astype(x.dtype), table[1].astype(x.dtype)
    xr = x.reshape(*x.shape[:-1], x.shape[-1] // 2, 2)
    x1, x2 = xr[..., 0], xr[..., 1]
    return jnp.stack([x1 * cos - x2 * sin, x1 * sin + x2 * cos], axis=-1).reshape(x.shape)


def _softmax32(s, scale):
    return jax.nn.softmax(s.astype(jnp.float32) * scale, axis=-1)


def _sweep(fn, *qs):
    B, S = qs[0].shape[:2]
    nb = S // QBLOCK
    blocks = tuple(jnp.moveaxis(q.reshape(B, nb, QBLOCK, *q.shape[2:]), 1, 0) for q in qs)
    out = lax.map(lambda blk: fn(*blk), blocks)
    return jnp.moveaxis(out, 0, 1).reshape(B, S, *out.shape[3:])


def _gqa_attend(k, v):
    scale = HEAD_DIM ** -0.5

    def fn(q):
        p = _softmax32(jnp.einsum('bqhgd,bkhd->bhgqk', q, k), scale).astype(v.dtype)
        return jnp.einsum('bhgqk,bkhd->bqhgd', p, v)
    return fn


def _diff_attend(k, v, lam):
    scale = DIFF_QK_DIM ** -0.5

    def fn(q):
        p = _softmax32(jnp.einsum('bqhjd,bkhjd->bhjqk', q, k), scale)
        a = (p[:, :, 0] - lam * p[:, :, 1]).astype(v.dtype)
        return jnp.einsum('bhqk,bkhd->bqhd', a, v)
    return fn


def _mla_attend(k_nope, k_rope, v):
    scale = (MLA_NOPE_DIM + MLA_ROPE_DIM) ** -0.5

    def fn(q_nope, q_rope):
        s = jnp.einsum('bqhd,bkhd->bhqk', q_nope, k_nope) + jnp.einsum('bqhd,bkd->bhqk', q_rope, k_rope)
        p = _softmax32(s, scale).astype(v.dtype)
        return jnp.einsum('bhqk,bkhd->bqhd', p, v)
    return fn


def _gqa_mixer(lat, ctxp, gq, gk, table, need_ctx):
    q, k, v = lat
    qc, kc, vc = ctxp
    B, S = q.shape[:2]
    C = qc.shape[1]
    q = _rope(_rmsnorm(q.reshape(B, S, GQA_HEADS, HEAD_DIM), gq), table)
    k = _rope(_rmsnorm(k.reshape(B, S, GQA_KV_HEADS, HEAD_DIM), gk), table)
    v = v.reshape(B, S, GQA_KV_HEADS, HEAD_DIM)
    kc = _rmsnorm(kc.reshape(B, C, GQA_KV_HEADS, HEAD_DIM), gk)
    vc = vc.reshape(B, C, GQA_KV_HEADS, HEAD_DIM)
    fn = _gqa_attend(jnp.concatenate([kc, k], axis=1), jnp.concatenate([vc, v], axis=1))
    o = _sweep(fn, q.reshape(B, S, GQA_KV_HEADS, GQA_GROUP, HEAD_DIM)).reshape(B, S, GQA_WIDTH)
    oc = None
    if need_ctx:
        qc = _rmsnorm(qc.reshape(B, C, GQA_HEADS, HEAD_DIM), gq).reshape(B, C, GQA_KV_HEADS, GQA_GROUP, HEAD_DIM)
        oc = _gqa_attend(kc, vc)(qc).reshape(B, C, GQA_WIDTH)
    return o, oc


def _diff_mixer(lat, ctxp, lq1, lk1, lq2, lk2, g_sub, lam_init, table, need_ctx):
    q, k, v = lat
    qc, kc, vc = ctxp
    B, S = q.shape[:2]
    C = qc.shape[1]
    f32 = jnp.float32
    lam = (jnp.exp(jnp.sum(lq1.astype(f32) * lk1.astype(f32)))
           - jnp.exp(jnp.sum(lq2.astype(f32) * lk2.astype(f32))) + lam_init)
    q = _rope(q.reshape(B, S, DIFF_HEADS * 2, DIFF_QK_DIM), table).reshape(B, S, DIFF_HEADS, 2, DIFF_QK_DIM)
    k = _rope(k.reshape(B, S, DIFF_HEADS * 2, DIFF_QK_DIM), table).reshape(B, S, DIFF_HEADS, 2, DIFF_QK_DIM)
    v = v.reshape(B, S, DIFF_HEADS, DIFF_V_DIM)
    kc = kc.reshape(B, C, DIFF_HEADS, 2, DIFF_QK_DIM)
    vc = vc.reshape(B, C, DIFF_HEADS, DIFF_V_DIM)
    fn = _diff_attend(jnp.concatenate([kc, k], axis=1), jnp.concatenate([vc, v], axis=1), lam)
    o = _sweep(fn, q)
    o = (_rmsnorm(o, g_sub) * (1 - lam_init)).reshape(B, S, DIFF_WIDTH)
    oc = None
    if need_ctx:
        oc = _diff_attend(kc, vc, lam)(qc.reshape(B, C, DIFF_HEADS, 2, DIFF_QK_DIM))
        oc = (_rmsnorm(oc, g_sub) * (1 - lam_init)).reshape(B, C, DIFF_WIDTH)
    return o, oc


def _mla_mixer(lat, ctxp, g_cq, g_ckv, w_uq, w_ukv, table, need_ctx):
    cq, ckv, kr = lat
    cqc, ckvc, krc = ctxp
    B, S = cq.shape[:2]
    C = cqc.shape[1]

    def up_q(t):
        y = (_rmsnorm(t, g_cq) @ w_uq).reshape(*t.shape[:2], MLA_HEADS, MLA_NOPE_DIM + MLA_ROPE_DIM)
        return y[..., :MLA_NOPE_DIM], y[..., MLA_NOPE_DIM:]

    def up_kv(t):
        y = (_rmsnorm(t, g_ckv) @ w_ukv).reshape(*t.shape[:2], MLA_HEADS, MLA_NOPE_DIM + MLA_V_DIM)
        return y[..., :MLA_NOPE_DIM], y[..., MLA_NOPE_DIM:]

    qn, qr = up_q(cq)
    qr = _rope(qr, table)
    kn, v = up_kv(ckv)
    kr = _rope(kr[:, :, None, :], table)[:, :, 0]
    knc, vc = up_kv(ckvc)
    fn = _mla_attend(jnp.concatenate([knc, kn], axis=1), jnp.concatenate([krc, kr], axis=1),
                     jnp.concatenate([vc, v], axis=1))
    o = _sweep(fn, qn, qr).reshape(B, S, MLA_WIDTH)
    oc = None
    if need_ctx:
        qnc, qrc = up_q(cqc)
        oc = _mla_attend(knc, krc, vc)(qnc, qrc).reshape(B, C, MLA_WIDTH)
    return o, oc


def _swiglu(h, w_in, w_out):
    gate, up = jnp.split(h @ w_in, 2, axis=-1)
    return (jax.nn.silu(gate) * up) @ w_out


def _moe_swiglu(h, w_router, w_in, w_out):
    logits = (h @ w_router).astype(jnp.float32)
    top_val, top_idx = lax.top_k(logits, TOP_K)
    top_w = jax.nn.softmax(top_val, axis=-1)
    gates = jnp.sum(jax.nn.one_hot(top_idx, N_EXPERTS, dtype=jnp.float32) * top_w[..., None], axis=-2)
    gates = gates.astype(h.dtype)
    out = jnp.zeros_like(h)
    for e in range(N_EXPERTS):
        out = out + gates[..., e:e + 1] * _swiglu(h, w_in[e], w_out[e])
    return out


def setup_inputs(seed: int = 0) -> dict:
    key = jax.random.key(seed)
    ks = jax.random.split(key, 32)
    n_dense = (DEPTH + 1) // 2
    n_moe = DEPTH // 2
    f32 = jnp.float32

    def nrm(k, shape, s):
        return jax.random.normal(k, shape, f32) * s

    def gain(k, shape):
        return 1.0 + 0.1 * jax.random.normal(k, shape, f32)

    return {
        "x": nrm(ks[0], (BATCH, SEQ, D_MODEL), 1.0),
        "c": nrm(ks[1], (BATCH, D_MODEL), 1.0),
        "ctx": nrm(ks[2], (BATCH, CTX_LEN, D_MODEL), 1.0),
        "c_ctx": nrm(ks[3], (D_MODEL,), 1.0),
        "w_mod": nrm(ks[4], (DEPTH, D_MODEL, N_MOD * D_MODEL), 0.5 * D_MODEL ** -0.5),
        "b_mod": nrm(ks[5], (DEPTH, N_MOD * D_MODEL), 0.02),
        "g_attn": gain(ks[6], (DEPTH, D_MODEL)),
        "g_ffn": gain(ks[7], (DEPTH, D_MODEL)),
        "w_in": nrm(ks[8], (DEPTH, D_MODEL, IN_WIDTH), D_MODEL ** -0.5),
        "w_out": nrm(ks[9], (DEPTH, MIX_WIDTH, D_MODEL), MIX_WIDTH ** -0.5),
        "gqa_gq": gain(ks[10], (DEPTH, HEAD_DIM)),
        "gqa_gk": gain(ks[11], (DEPTH, HEAD_DIM)),
        "diff_lq1": nrm(ks[12], (DEPTH, DIFF_QK_DIM), 0.1),
        "diff_lk1": nrm(ks[13], (DEPTH, DIFF_QK_DIM), 0.1),
        "diff_lq2": nrm(ks[14], (DEPTH, DIFF_QK_DIM), 0.1),
        "diff_lk2": nrm(ks[15], (DEPTH, DIFF_QK_DIM), 0.1),
        "diff_gsub": gain(ks[16], (DEPTH, DIFF_V_DIM)),
        "mla_gcq": gain(ks[17], (DEPTH, MLA_Q_RANK)),
        "mla_gckv": gain(ks[18], (DEPTH, MLA_KV_RANK)),
        "mla_wuq": nrm(ks[19], (DEPTH, MLA_Q_RANK, MLA_HEADS * (MLA_NOPE_DIM + MLA_ROPE_DIM)), MLA_Q_RANK ** -0.5),
        "mla_wukv": nrm(ks[20], (DEPTH, MLA_KV_RANK, MLA_HEADS * (MLA_NOPE_DIM + MLA_V_DIM)), MLA_KV_RANK ** -0.5),
        "ffn_w_in": nrm(ks[21], (n_dense, D_MODEL, 2 * D_FF_DENSE), D_MODEL ** -0.5),
        "ffn_w_out": nrm(ks[22], (n_dense, D_FF_DENSE, D_MODEL), D_FF_DENSE ** -0.5),
        "moe_router": nrm(ks[23], (n_moe, D_MODEL, N_EXPERTS), D_MODEL ** -0.5),
        "moe_w_in": nrm(ks[24], (n_moe, N_EXPERTS, D_MODEL, 2 * D_FF_EXPERT), D_MODEL ** -0.5),
        "moe_w_out": nrm(ks[25], (n_moe, N_EXPERTS, D_FF_EXPERT, D_MODEL), D_FF_EXPERT ** -0.5),
        "g_final": gain(ks[26], (D_MODEL,)),
    }


def reference(x, c, ctx, c_ctx, w_mod, b_mod, g_attn, g_ffn, w_in, w_out, gqa_gq, gqa_gk,
              diff_lq1, diff_lk1, diff_lq2, diff_lk2, diff_gsub, mla_gcq, mla_gckv, mla_wuq, mla_wukv,
              ffn_w_in, ffn_w_out, moe_router, moe_w_in, moe_w_out, g_final):
    B, S, D = x.shape
    ROWS = S // GRID_W
    t = jnp.arange(ROWS * GRID_W)
    rows = (t // GRID_W).astype(jnp.float32)
    cols = (t % GRID_W).astype(jnp.float32)
    table_a = _axial_rope_table(rows, cols, HEAD_DIM)
    table_b = _axial_rope_table(rows, cols, DIFF_QK_DIM)
    table_c = _axial_rope_table(rows, cols, MLA_ROPE_DIM)
    sc = jax.nn.silu(c)
    scc = jax.nn.silu(c_ctx)
    xc = ctx
    for l in range(DEPTH):
        need_ctx = l < DEPTH - 1
        mod = (sc @ w_mod[l] + b_mod[l]).reshape(B, N_MOD, 1, D)
        modc = (scc @ w_mod[l] + b_mod[l]).reshape(N_MOD, D)
        h = _modulate(_rmsnorm(x, g_attn[l]), mod[:, 0], mod[:, 1])
        hc = _modulate(_rmsnorm(xc, g_attn[l]), modc[0], modc[1])
        p = _split_cols(h @ w_in[l])
        pc = _split_cols(hc @ w_in[l])
        oa, oac = _gqa_mixer(p[0:3], pc[0:3], gqa_gq[l], gqa_gk[l], table_a, need_ctx)
        ob, obc = _diff_mixer(p[3:6], pc[3:6], diff_lq1[l], diff_lk1[l], diff_lq2[l], diff_lk2[l],
                              diff_gsub[l], 0.8 - 0.6 * math.exp(-0.3 * l), table_b, need_ctx)
        om, omc = _mla_mixer(p[6:9], pc[6:9], mla_gcq[l], mla_gckv[l], mla_wuq[l], mla_wukv[l],
                             table_c, need_ctx)
        x = x + mod[:, 2] * (jnp.concatenate([oa, ob, om], axis=-1) @ w_out[l])
        if need_ctx:
            xc = xc + modc[2] * (jnp.concatenate([oac, obc, omc], axis=-1) @ w_out[l])
        if l % 2 == 0:
            wi, wo = ffn_w_in[l // 2], ffn_w_out[l // 2]
            ffn = lambda u: _swiglu(u, wi, wo)
        else:
            wr, wi, wo = moe_router[l // 2], moe_w_in[l // 2], moe_w_out[l // 2]
            ffn = lambda u: _moe_swiglu(u, wr, wi, wo)
        h2 = _modulate(_rmsnorm(x, g_ffn[l]), mod[:, 3], mod[:, 4])
        x = x + mod[:, 5] * ffn(h2)
        if need_ctx:
            h2c = _modulate(_rmsnorm(xc, g_ffn[l]), modc[3], modc[4])
            xc = xc + modc[5] * ffn(h2c)
    return _rmsnorm(x, g_final)
```

```python
import functools
import math

import numpy as np
import jax
import jax.numpy as jnp
from jax import lax
from jax.experimental import pallas as pl
from jax.experimental.pallas import tpu as pltpu

LANES = 128
MXU_TILE = 256
VMEM_LIMIT = 56 * 1024 * 1024

NORM_EPS = 1e-6
ROPE_THETA = 10000.0
GRID_W = 64
N_MOD = 6
HEAD_DIM = 64
GQA_HEADS, GQA_KV_HEADS = 8, 2
DIFF_HEADS, DIFF_QK_DIM, DIFF_V_DIM = 4, 32, 64
MLA_HEADS, MLA_Q_RANK, MLA_KV_RANK = 4, 192, 128
MLA_NOPE_DIM, MLA_ROPE_DIM, MLA_V_DIM = 64, 32, 64
N_EXPERTS, TOP_K = 8, 2

ROW_TILE = 256
KEY_CHUNK = MXU_TILE
FFN_CHUNK = MXU_TILE
FFN_ROWS = 512

F32 = jnp.float32
BF16 = jnp.bfloat16
HIGHEST = lax.Precision.HIGHEST


def _params(*sem):
    return pltpu.CompilerParams(dimension_semantics=sem, vmem_limit_bytes=VMEM_LIMIT)


def _lane_iota(shape):
    return lax.broadcasted_iota(jnp.int32, shape, len(shape) - 1)


def _mod_body(c_ref, w_ref, b_ref, o_ref):
    c = c_ref[...]
    sc = c * jax.nn.sigmoid(c)
    o_ref[...] = jnp.dot(sc, w_ref[...], precision=HIGHEST,
                         preferred_element_type=F32) + b_ref[...]


def _modulation(c_all, w, b):
    rows, d = c_all.shape
    n = w.shape[1]
    tn = 1536
    return pl.pallas_call(
        _mod_body,
        grid=(n // tn,),
        in_specs=[pl.BlockSpec((rows, d), lambda j: (0, 0)),
                  pl.BlockSpec((d, tn), lambda j: (0, j)),
                  pl.BlockSpec((1, tn), lambda j: (0, j))],
        out_specs=pl.BlockSpec((rows, tn), lambda j: (0, j)),
        out_shape=jax.ShapeDtypeStruct((rows, n), F32),
        compiler_params=_params("arbitrary"),
        name="modulation",
    )(c_all, w, b.reshape(1, n))


def _rms_rows(x, width):
    return lax.rsqrt(jnp.sum(x * x, axis=-1, keepdims=True) * (1.0 / width) + NORM_EPS)


def _rope(t, cos, sin_signed, half):
    lane = _lane_iota(t.shape)
    partner = jnp.where((lane & half) == 0,
                        pltpu.roll(t, LANES - half, 1), pltpu.roll(t, half, 1))
    return t * cos + partner * sin_signed


def _proj_body(x_ref, xc_ref, mod_ref, g_ref, w_ref, ca_ref, sa_ref, cb_ref, sb_ref,
               gq_ref, gk_ref, bd_ref, gcq_ref, gckv_ref, wuq_ref, wukv_ref,
               qa_ref, ka_ref, va_ref, qb_ref, kb_ref, vb_ref, qm_ref, km_ref, vm_ref,
               *, scale_a, scale_b, scale_m):
    i = pl.program_id(1)
    x = jnp.where(i == 0, xc_ref[0], x_ref[0])
    d = x.shape[-1]
    mod = mod_ref[0]
    y = x * _rms_rows(x, d) * g_ref[...]
    h = (y * (1.0 + mod[1:2]) + mod[0:1]).astype(BF16)
    p = jnp.dot(h, w_ref[...], preferred_element_type=F32)

    ca, sa, cb, sb = ca_ref[...], sa_ref[...], cb_ref[...], sb_ref[...]
    bd = bd_ref[...]

    def head_norm(t, g):
        ms = jnp.dot(t * t, bd, precision=HIGHEST, preferred_element_type=F32)
        return t * lax.rsqrt(ms + NORM_EPS) * g

    for j in range(4):
        t = head_norm(p[:, j * LANES:(j + 1) * LANES], gq_ref[...])
        qa_ref[0, :, j * LANES:(j + 1) * LANES] = (_rope(t, ca, sa, 32) * scale_a).astype(BF16)
    for j in range(2):
        t = head_norm(p[:, 512 + j * LANES:512 + (j + 1) * LANES], gk_ref[...])
        ka_ref[0, :, j * LANES:(j + 1) * LANES] = _rope(t, ca, sa, 32).astype(BF16)
    va_ref[0] = p[:, 768:1024].astype(BF16)

    for j in range(2):
        t = p[:, 1024 + j * LANES:1024 + (j + 1) * LANES]
        qb_ref[0, :, j * LANES:(j + 1) * LANES] = (_rope(t, cb, sb, 16) * scale_b).astype(BF16)
        t = p[:, 1280 + j * LANES:1280 + (j + 1) * LANES]
        kb_ref[0, :, j * LANES:(j + 1) * LANES] = _rope(t, cb, sb, 16).astype(BF16)
    vb_ref[0] = p[:, 1536:1792].astype(BF16)

    cq = p[:, 1792:2048]
    cqn = (cq * _rms_rows(cq, MLA_Q_RANK) * gcq_ref[...]).astype(BF16)
    uq = jnp.dot(cqn, wuq_ref[...], preferred_element_type=F32)
    ckv = p[:, 2048:2176]
    ckvn = (ckv * _rms_rows(ckv, MLA_KV_RANK) * gckv_ref[...]).astype(BF16)
    ukv = jnp.dot(ckvn, wukv_ref[...], preferred_element_type=F32)
    kr = _rope(p[:, 2176:2304], cb, sb, 16).astype(BF16)
    for pr in range(2):
        base = pr * 2 * LANES
        qm_ref[0, :, base:base + LANES] = (uq[:, base:base + LANES] * scale_m).astype(BF16)
        qr = _rope(uq[:, base + LANES:base + 2 * LANES], cb, sb, 16)
        qm_ref[0, :, base + LANES:base + 2 * LANES] = (qr * scale_m).astype(BF16)
        km_ref[0, :, base:base + LANES] = ukv[:, pr * LANES:(pr + 1) * LANES].astype(BF16)
        km_ref[0, :, base + LANES:base + 2 * LANES] = kr
    vm_ref[0] = ukv[:, 256:512].astype(BF16)


def _project(x, xc, mod3, g, w, tables, gq2, gk2, bd, gcq, gckv, wuq, wukv):
    b, s, d = x.shape
    c = xc.shape[1]
    t = c + s
    tm = ROW_TILE
    assert c == tm and s % tm == 0
    nt = t // tm
    ctx_row = b
    widths = (512, 256, 256, 256, 256, 256, 512, 512, 256)
    body = functools.partial(_proj_body, scale_a=HEAD_DIM ** -0.5, scale_b=DIFF_QK_DIM ** -0.5,
                             scale_m=(MLA_NOPE_DIM + MLA_ROPE_DIM) ** -0.5)
    const = lambda shape: pl.BlockSpec(shape, lambda bi, i: (0,) * len(shape))
    tab = pl.BlockSpec((tm, LANES), lambda bi, i: (i, 0))
    return pl.pallas_call(
        body,
        grid=(b, nt),
        in_specs=[pl.BlockSpec((1, tm, d), lambda bi, i: (bi, jnp.maximum(i - 1, 0), 0)),
                  pl.BlockSpec((1, tm, d), lambda bi, i: (bi, 0, 0)),
                  pl.BlockSpec((1, N_MOD, d), lambda bi, i: (jnp.where(i == 0, ctx_row, bi), 0, 0)),
                  const((1, d)), const(w.shape), tab, tab, tab, tab,
                  const((1, LANES)), const((1, LANES)), const((LANES, LANES)),
                  const(gcq.shape), const(gckv.shape), const(wuq.shape), const(wukv.shape)],
        out_specs=[pl.BlockSpec((1, tm, wd), lambda bi, i: (bi, i, 0)) for wd in widths],
        out_shape=[jax.ShapeDtypeStruct((b, t, wd), BF16) for wd in widths],
        compiler_params=_params("parallel", "arbitrary"),
        name="project",
    )(x, xc, mod3, g, w, *tables, gq2, gk2, bd, gcq, gckv, wuq, wukv)


def _attn_core(qs, k_ref, v_ref, s_ref, m_ref, l_ref, acc_ref, n_chunks):
    m_ref[...] = jnp.full(m_ref.shape, -jnp.inf, F32)

    def scores(c, carry):
        kc = k_ref[0, pl.ds(pl.multiple_of(c * KEY_CHUNK, KEY_CHUNK), KEY_CHUNK), :]
        s = lax.dot_general(qs, kc, (((1,), (1,)), ((), ())), preferred_element_type=F32)
        s_ref[c] = s
        m_ref[...] = jnp.maximum(m_ref[...], jnp.maximum(s[:, :LANES], s[:, LANES:]))
        return carry

    lax.fori_loop(0, n_chunks, scores, 0)
    m_ref[...] = jnp.broadcast_to(jnp.max(m_ref[...], axis=-1, keepdims=True), m_ref.shape)
    l_ref[...] = jnp.zeros(l_ref.shape, F32)
    acc_ref[...] = jnp.zeros(acc_ref.shape, F32)

    def weighted(c, carry):
        m = m_ref[...]
        s = s_ref[c]
        p0 = jnp.exp(s[:, :LANES] - m)
        p1 = jnp.exp(s[:, LANES:] - m)
        l_ref[...] += p0 + p1
        pb = jnp.concatenate([p0, p1], axis=1).astype(BF16)
        vc = v_ref[0, pl.ds(pl.multiple_of(c * KEY_CHUNK, KEY_CHUNK), KEY_CHUNK), :]
        acc_ref[...] += jnp.dot(pb, vc, preferred_element_type=F32)
        return carry

    lax.fori_loop(0, n_chunks, weighted, 0)
    l = jnp.sum(l_ref[...], axis=-1, keepdims=True)
    return acc_ref[...] * (1.0 / l)


def _gqa_body(q_ref, k_ref, v_ref, o_ref, s_ref, m_ref, l_ref, acc_ref, *, n_chunks):
    tq = q_ref.shape[1]
    lane = _lane_iota((tq, LANES))
    lo = lane < HEAD_DIM
    heads = []
    for j in range(2):
        t = q_ref[0, :, j * LANES:(j + 1) * LANES].astype(F32)
        heads.append(jnp.where(lo, t, 0.0))
        heads.append(jnp.where(lo, pltpu.roll(t, HEAD_DIM, 1), 0.0))
    qs = jnp.concatenate(heads, axis=0).astype(BF16)
    o = _attn_core(qs, k_ref, v_ref, s_ref, m_ref, l_ref, acc_ref, n_chunks)
    for j in range(2):
        even = o[(2 * j) * tq:(2 * j + 1) * tq]
        odd = o[(2 * j + 1) * tq:(2 * j + 2) * tq]
        o_ref[0, :, j * LANES:(j + 1) * LANES] = jnp.where(
            lo, even, pltpu.roll(odd, HEAD_DIM, 1)).astype(o_ref.dtype)


def _diff_body(q_ref, k_ref, v_ref, lq1_ref, lk1_ref, lq2_ref, lk2_ref, gsub_ref, o_ref,
               s_ref, m_ref, l_ref, acc_ref, *, n_chunks, lam_init):
    tq = q_ref.shape[1]
    lane = _lane_iota((tq, LANES))
    t = q_ref[0]
    zero = jnp.zeros_like(t)
    qs = jnp.concatenate([jnp.where((lane // DIFF_QK_DIM) == j, t, zero) for j in range(4)], axis=0)
    o = _attn_core(qs, k_ref, v_ref, s_ref, m_ref, l_ref, acc_ref, n_chunks)
    lam = (jnp.exp(jnp.sum(lq1_ref[...] * lk1_ref[...], axis=-1, keepdims=True))
           - jnp.exp(jnp.sum(lq2_ref[...] * lk2_ref[...], axis=-1, keepdims=True)) + lam_init)
    lo = lane < DIFF_V_DIM
    d = jnp.where(lo, o[0:tq] - lam * o[tq:2 * tq], o[2 * tq:3 * tq] - lam * o[3 * tq:4 * tq])
    sq = d * d
    ms_lo = jnp.sum(jnp.where(lo, sq, 0.0), axis=-1, keepdims=True) * (1.0 / DIFF_V_DIM)
    ms_hi = jnp.sum(jnp.where(lo, 0.0, sq), axis=-1, keepdims=True) * (1.0 / DIFF_V_DIM)
    r = jnp.where(lo, lax.rsqrt(ms_lo + NORM_EPS), lax.rsqrt(ms_hi + NORM_EPS))
    o_ref[0] = ((d * r * gsub_ref[...]) * (1.0 - lam_init)).astype(o_ref.dtype)


def _mla_body(q_ref, k_ref, v_ref, o_ref, s_ref, m_ref, l_ref, acc_ref, *, n_chunks):
    tq = q_ref.shape[1]
    wq = q_ref.shape[2]
    lane = _lane_iota((tq, wq))
    t = q_ref[0]
    zero = jnp.zeros_like(t)
    first = (lane < MLA_NOPE_DIM) | ((lane >= LANES) & (lane < LANES + MLA_ROPE_DIM))
    second = ((lane >= MLA_NOPE_DIM) & (lane < LANES)) | (
        (lane >= LANES + MLA_ROPE_DIM) & (lane < LANES + 2 * MLA_ROPE_DIM))
    qs = jnp.concatenate([jnp.where(first, t, zero), jnp.where(second, t, zero)], axis=0)
    o = _attn_core(qs, k_ref, v_ref, s_ref, m_ref, l_ref, acc_ref, n_chunks)
    lo = _lane_iota((tq, LANES)) < MLA_V_DIM
    o_ref[0] = jnp.where(lo, o[0:tq], o[tq:2 * tq]).astype(o_ref.dtype)


def _attention(body, q, k, v, extra, *, n_groups, wq, wk, n_s, q_tiles, q_tile0, n_keys):
    b = q.shape[0]
    tq = ROW_TILE
    wo = wq if body is _gqa_body else LANES
    n_chunks = n_keys // KEY_CHUNK
    m_rows = n_s * tq
    extra_specs = [pl.BlockSpec(e.shape, lambda bi, g, i: (0, 0)) for e in extra]
    return pl.pallas_call(
        functools.partial(body, n_chunks=n_chunks),
        grid=(b, n_groups, q_tiles),
        in_specs=[pl.BlockSpec((1, tq, wq), lambda bi, g, i: (bi, i + q_tile0, g)),
                  pl.BlockSpec((1, n_keys, wk), lambda bi, g, i: (bi, 0, g)),
                  pl.BlockSpec((1, n_keys, LANES), lambda bi, g, i: (bi, 0, g))] + extra_specs,
        out_specs=pl.BlockSpec((1, tq, wo), lambda bi, g, i: (bi, i, g)),
        out_shape=jax.ShapeDtypeStruct((b, q_tiles * tq, n_groups * wo), BF16),
        scratch_shapes=[pltpu.VMEM((n_chunks, m_rows, KEY_CHUNK), F32),
                        pltpu.VMEM((m_rows, LANES), F32),
                        pltpu.VMEM((m_rows, LANES), F32),
                        pltpu.VMEM((m_rows, LANES), F32)],
        compiler_params=_params("parallel", "parallel", "arbitrary"),
        name=body.__name__.strip("_"),
    )(q, k, v, *extra)


def _router_gates(logits):
    lane = _lane_iota(logits.shape).astype(F32)
    neg = jnp.float32(-jnp.inf)
    z = jnp.where(lane < N_EXPERTS, logits, neg)
    m1 = jnp.max(z, axis=-1, keepdims=True)
    i1 = jnp.min(jnp.where(z == m1, lane, float(LANES)), axis=-1, keepdims=True)
    z2 = jnp.where(lane == i1, neg, z)
    m2 = jnp.max(z2, axis=-1, keepdims=True)
    i2 = jnp.min(jnp.where(z2 == m2, lane, float(LANES)), axis=-1, keepdims=True)
    e2 = jnp.exp(m2 - m1)
    den = 1.0 + e2
    return jnp.where(lane == i1, 1.0 / den, 0.0) + jnp.where(lane == i2, e2 / den, 0.0)


def _outproj_body(x_ref, oa_ref, ob_ref, om_ref, w_ref, mod_ref, g_ref, *rest, routed):
    if routed:
        wr_ref, xo_ref, h_ref, gates_ref = rest
    else:
        xo_ref, h_ref = rest
    mod = mod_ref[0]
    wa, wb = oa_ref.shape[2], ob_ref.shape[2]
    y = jnp.dot(oa_ref[0], w_ref[0:wa, :], preferred_element_type=F32)
    y += jnp.dot(ob_ref[0], w_ref[wa:wa + wb, :], preferred_element_type=F32)
    y += jnp.dot(om_ref[0], w_ref[wa + wb:, :], preferred_element_type=F32)
    x = x_ref[0] + mod[2:3] * y
    xo_ref[0] = x
    n = x * _rms_rows(x, x.shape[-1]) * g_ref[...]
    h = n * (1.0 + mod[4:5]) + mod[3:4]
    h_ref[0] = h.astype(BF16)
    if routed:
        logits = jnp.dot(h, wr_ref[...], precision=HIGHEST, preferred_element_type=F32)
        gates_ref[0] = _router_gates(logits)


def _out_project(x, oa, ob, om, w, mod3, g, mod_row, w_router=None):
    b, s, d = x.shape
    tm = ROW_TILE
    routed = w_router is not None
    row = lambda bi, i: (bi, i, 0)
    in_specs = [pl.BlockSpec((1, tm, d), row),
                pl.BlockSpec((1, tm, oa.shape[2]), row),
                pl.BlockSpec((1, tm, ob.shape[2]), row),
                pl.BlockSpec((1, tm, om.shape[2]), row),
                pl.BlockSpec(w.shape, lambda bi, i: (0, 0)),
                pl.BlockSpec((1, N_MOD, d), lambda bi, i: (mod_row(bi), 0, 0)),
                pl.BlockSpec((1, d), lambda bi, i: (0, 0))]
    out_specs = [pl.BlockSpec((1, tm, d), row), pl.BlockSpec((1, tm, d), row)]
    out_shape = [jax.ShapeDtypeStruct((b, s, d), F32), jax.ShapeDtypeStruct((b, s, d), BF16)]
    args = [x, oa, ob, om, w, mod3, g]
    if routed:
        in_specs.append(pl.BlockSpec(w_router.shape, lambda bi, i: (0, 0)))
        out_specs.append(pl.BlockSpec((1, tm, LANES), row))
        out_shape.append(jax.ShapeDtypeStruct((b, s, LANES), F32))
        args.append(w_router)
    return pl.pallas_call(
        functools.partial(_outproj_body, routed=routed),
        grid=(b, s // tm),
        in_specs=in_specs, out_specs=out_specs, out_shape=out_shape,
        compiler_params=_params("parallel", "arbitrary"),
        name="out_project_routed" if routed else "out_project",
    )(*args)


def _swiglu_chunks(h, wg_ref, wu_ref, wo_ref, acc_ref, lead):
    n_chunks = wg_ref.shape[len(lead)]
    acc_ref[...] = jnp.zeros(acc_ref.shape, F32)

    def chunk(j, carry):
        gate = jnp.dot(h, wg_ref[lead + (j,)], preferred_element_type=F32)
        up = jnp.dot(h, wu_ref[lead + (j,)], preferred_element_type=F32)
        a = (gate * jax.nn.sigmoid(gate) * up).astype(BF16)
        acc_ref[...] += jnp.dot(a, wo_ref[lead + (j,)], preferred_element_type=F32)
        return carry

    lax.fori_loop(0, n_chunks, chunk, 0)


def _residual_out(x, gate, y, gf_ref, final_norm):
    x = x + gate * y
    if final_norm:
        x = x * _rms_rows(x, x.shape[-1]) * gf_ref[...]
    return x


def _ffn_body(x_ref, h_ref, wg_ref, wu_ref, wo_ref, mod_ref, gf_ref, o_ref, acc_ref, *, final_norm):
    _swiglu_chunks(h_ref[0], wg_ref, wu_ref, wo_ref, acc_ref, ())
    o_ref[0] = _residual_out(x_ref[0], mod_ref[0][5:6], acc_ref[...], gf_ref, final_norm)


def _dense_ffn(x, h, wg, wu, wo, mod3, mod_row, g_final, final_norm):
    b, s, d = x.shape
    tm = min(FFN_ROWS, s)
    row = lambda bi, i: (bi, i, 0)
    resident = lambda a: pl.BlockSpec(a.shape, lambda bi, i: (0, 0, 0),
                                      pipeline_mode=pl.Buffered(1))
    return pl.pallas_call(
        functools.partial(_ffn_body, final_norm=final_norm),
        grid=(b, s // tm),
        in_specs=[pl.BlockSpec((1, tm, d), row), pl.BlockSpec((1, tm, d), row),
                  resident(wg), resident(wu), resident(wo),
                  pl.BlockSpec((1, N_MOD, d), lambda bi, i: (mod_row(bi), 0, 0)),
                  pl.BlockSpec((1, d), lambda bi, i: (0, 0))],
        out_specs=pl.BlockSpec((1, tm, d), row),
        out_shape=jax.ShapeDtypeStruct((b, s, d), F32),
        scratch_shapes=[pltpu.VMEM((tm, d), F32)],
        compiler_params=_params("parallel", "arbitrary"),
        name="dense_ffn",
    )(x, h, wg, wu, wo, mod3, g_final)


def _moe_body(x_ref, h_ref, gates_ref, wg_ref, wu_ref, wo_ref, mod_ref, gf_ref, o_ref,
              acc_ref, tot_ref, *, final_norm):
    e = pl.program_id(2)

    @pl.when(e == 0)
    def _():
        tot_ref[...] = jnp.zeros(tot_ref.shape, F32)

    _swiglu_chunks(h_ref[0], wg_ref, wu_ref, wo_ref, acc_ref, (0,))
    gates = gates_ref[0]
    gate_e = jnp.sum(jnp.where(_lane_iota(gates.shape) == e, gates, 0.0), axis=-1, keepdims=True)
    tot_ref[...] += gate_e * acc_ref[...]

    @pl.when(e == pl.num_programs(2) - 1)
    def _():
        o_ref[0] = _residual_out(x_ref[0], mod_ref[0][5:6], tot_ref[...], gf_ref, final_norm)


def _moe_ffn(x, h, gates, wg, wu, wo, mod3, mod_row, g_final, final_norm):
    b, s, d = x.shape
    tm = min(FFN_ROWS, s)
    row = lambda bi, i, e: (bi, i, 0)
    expert = lambda a: pl.BlockSpec((1,) + a.shape[1:], lambda bi, i, e: (e, 0, 0, 0))
    return pl.pallas_call(
        functools.partial(_moe_body, final_norm=final_norm),
        grid=(b, s // tm, wg.shape[0]),
        in_specs=[pl.BlockSpec((1, tm, d), row), pl.BlockSpec((1, tm, d), row),
                  pl.BlockSpec((1, tm, LANES), row),
                  expert(wg), expert(wu), expert(wo),
                  pl.BlockSpec((1, N_MOD, d), lambda bi, i, e: (mod_row(bi), 0, 0)),
                  pl.BlockSpec((1, d), lambda bi, i, e: (0, 0))],
        out_specs=pl.BlockSpec((1, tm, d), row),
        out_shape=jax.ShapeDtypeStruct((b, s, d), F32),
        scratch_shapes=[pltpu.VMEM((tm, d), F32), pltpu.VMEM((tm, d), F32)],
        compiler_params=_params("parallel", "parallel", "arbitrary"),
        name="moe_ffn",
    )(x, h, gates, wg, wu, wo, mod3, g_final)


def _deinterleave(n):
    return np.concatenate([np.arange(0, n, 2), np.arange(1, n, 2)])


def _in_proj_columns(pad):
    pads = lambda n: np.full((n,), pad)
    cols = []
    for hd in range(GQA_HEADS):
        cols.append(64 * hd + _deinterleave(64))
    for g in range(GQA_KV_HEADS):
        cols += [512 + 64 * g + _deinterleave(64), pads(64)]
    for g in range(GQA_KV_HEADS):
        cols += [640 + 64 * g + np.arange(64), pads(64)]
    for u in range(2 * DIFF_HEADS):
        cols.append(768 + 32 * u + _deinterleave(32))
    for u in range(2 * DIFF_HEADS):
        cols.append(1024 + 32 * u + _deinterleave(32))
    cols.append(1280 + np.arange(256))
    cols += [1536 + np.arange(MLA_Q_RANK), pads(64)]
    cols.append(1728 + np.arange(MLA_KV_RANK))
    cols += [1856 + _deinterleave(32), 1856 + _deinterleave(32), pads(64)]
    return np.concatenate(cols)


def _uq_columns(pad):
    per = MLA_NOPE_DIM + MLA_ROPE_DIM
    cols = []
    for pr in range(2):
        h0, h1 = 2 * pr, 2 * pr + 1
        cols += [per * h0 + np.arange(64), per * h1 + np.arange(64),
                 per * h0 + 64 + _deinterleave(32), per * h1 + 64 + _deinterleave(32),
                 np.full((64,), pad)]
    return np.concatenate(cols)


def _ukv_columns():
    per = MLA_NOPE_DIM + MLA_V_DIM
    k = [per * hd + np.arange(64) for hd in range(MLA_HEADS)]
    v = [per * hd + 64 + np.arange(64) for hd in range(MLA_HEADS)]
    return np.concatenate(k + v)


def _take_cols(w, cols):
    w_ext = jnp.concatenate([w, jnp.zeros((w.shape[0], 1), w.dtype)], axis=1)
    return jnp.take(w_ext, jnp.asarray(cols), axis=1)


def _rope_tables(s, c):
    t = jnp.arange(s)
    rows = (t // GRID_W).astype(F32)
    cols = (t % GRID_W).astype(F32)
    out = []
    for dim in (HEAD_DIM, DIFF_QK_DIM):
        quarter = dim // 4
        half = dim // 2
        inv_freq = ROPE_THETA ** (-jnp.arange(quarter, dtype=F32) / quarter)
        ang = jnp.concatenate([rows[:, None] * inv_freq, cols[:, None] * inv_freq], axis=-1)
        lane = np.arange(LANES)
        idx = (lane % dim) % half
        sign = np.where((lane % dim) < half, -1.0, 1.0).astype(np.float32)
        cos = jnp.cos(ang)[:, idx]
        sin = jnp.sin(ang)[:, idx] * sign
        out.append(jnp.concatenate([jnp.ones((c, LANES), F32), cos], axis=0))
        out.append(jnp.concatenate([jnp.zeros((c, LANES), F32), sin], axis=0))
    return out


def _chunked_in(w, n_hidden):
    d = w.shape[0]
    nc = n_hidden // FFN_CHUNK
    g = w[:, :n_hidden].reshape(d, nc, FFN_CHUNK).transpose(1, 0, 2).astype(BF16)
    u = w[:, n_hidden:].reshape(d, nc, FFN_CHUNK).transpose(1, 0, 2).astype(BF16)
    return g, u


def kernel(x, c, ctx, c_ctx, w_mod, b_mod, g_attn, g_ffn, w_in, w_out, gqa_gq, gqa_gk,
           diff_lq1, diff_lk1, diff_lq2, diff_lk2, diff_gsub, mla_gcq, mla_gckv, mla_wuq, mla_wukv,
           ffn_w_in, ffn_w_out, moe_router, moe_w_in, moe_w_out, g_final):
    b, s, d = x.shape
    n_ctx = ctx.shape[1]
    depth = w_mod.shape[0]
    t_all = n_ctx + s
    tables = _rope_tables(s, n_ctx)
    in_cols = _in_proj_columns(w_in.shape[2])
    uq_cols = _uq_columns(mla_wuq.shape[2])
    ukv_cols = _ukv_columns()
    perm64 = _deinterleave(HEAD_DIM)
    bd = jnp.asarray(np.kron(np.eye(LANES // HEAD_DIM), np.full((HEAD_DIM, HEAD_DIM), 1.0 / HEAD_DIM)), F32)

    mod_rows = 16
    c_all = jnp.zeros((mod_rows, d), F32).at[:b].set(c).at[b].set(c_ctx)
    pad_lanes = lambda v: jnp.zeros((1, LANES), F32).at[0, :v.shape[0]].set(v)

    xc = ctx
    for l in range(depth):
        need_ctx = l < depth - 1
        lam_init = 0.8 - 0.6 * math.exp(-0.3 * l)
        mod3 = _modulation(c_all, w_mod[l], b_mod[l]).reshape(mod_rows, N_MOD, d)

        w_in_p = _take_cols(w_in[l], in_cols).astype(BF16)
        wuq_p = jnp.zeros((2 * LANES, 512), F32).at[:MLA_Q_RANK].set(
            _take_cols(mla_wuq[l], uq_cols)).astype(BF16)
        wukv_p = jnp.take(mla_wukv[l], jnp.asarray(ukv_cols), axis=1).astype(BF16)
        gq2 = jnp.tile(gqa_gq[l][perm64], 2).reshape(1, LANES)
        gk2 = jnp.tile(gqa_gk[l][perm64], 2).reshape(1, LANES)
        gcq = jnp.zeros((1, 2 * LANES), F32).at[0, :MLA_Q_RANK].set(mla_gcq[l])
        gckv = mla_gckv[l].reshape(1, MLA_KV_RANK)

        qa, ka, va, qb, kb, vb, qm, km, vm = _project(
            x, xc, mod3, g_attn[l].reshape(1, d), w_in_p, tables, gq2, gk2, bd, gcq, gckv, wuq_p, wukv_p)

        diff_extra = [pad_lanes(diff_lq1[l]), pad_lanes(diff_lk1[l]), pad_lanes(diff_lq2[l]),
                      pad_lanes(diff_lk2[l]), jnp.tile(diff_gsub[l], 2).reshape(1, LANES)]
        diff_body = functools.partial(_diff_body, lam_init=lam_init)
        diff_body.__name__ = "_diff_body"

        def attend(q_tiles, q_tile0, n_keys):
            kw = dict(q_tiles=q_tiles, q_tile0=q_tile0, n_keys=n_keys)
            oa = _attention(_gqa_body, qa, ka, va, [], n_groups=2, wq=256, wk=LANES, n_s=4, **kw)
            ob = _attention(diff_body, qb, kb, vb, diff_extra, n_groups=2, wq=LANES, wk=LANES, n_s=4, **kw)
            om = _attention(_mla_body, qm, km, vm, [], n_groups=2, wq=256, wk=256, n_s=2, **kw)
            return oa, ob, om

        w_out_b = w_out[l].astype(BF16)
        g2 = g_ffn[l].reshape(1, d)
        dense = l % 2 == 0
        if dense:
            wg, wu = _chunked_in(ffn_w_in[l // 2], ffn_w_out.shape[1])
            wo = ffn_w_out[l // 2].reshape(-1, FFN_CHUNK, d).astype(BF16)
            w_router = None
        else:
            n_hidden = moe_w_out.shape[2]
            wi = moe_w_in[l // 2]
            wg = wi[:, :, :n_hidden].reshape(N_EXPERTS, d, -1, FFN_CHUNK).transpose(0, 2, 1, 3).astype(BF16)
            wu = wi[:, :, n_hidden:].reshape(N_EXPERTS, d, -1, FFN_CHUNK).transpose(0, 2, 1, 3).astype(BF16)
            wo = moe_w_out[l // 2].reshape(N_EXPERTS, -1, FFN_CHUNK, d).astype(BF16)
            w_router = jnp.zeros((d, LANES), F32).at[:, :N_EXPERTS].set(moe_router[l // 2])
        last = l == depth - 1

        def channel_mix(xs, attn_out, mod_row, final_norm):
            res = _out_project(xs, *attn_out, w_out_b, mod3, g2, mod_row, w_router)
            gf = g_final.reshape(1, d)
            if dense:
                return _dense_ffn(res[0], res[1], wg, wu, wo, mod3, mod_row, gf, final_norm)
            return _moe_ffn(res[0], res[1], res[2], wg, wu, wo, mod3, mod_row, gf, final_norm)

        x_new = channel_mix(x, attend(s // ROW_TILE, n_ctx // ROW_TILE, t_all), lambda bi: bi, last)
        if need_ctx:
            xc = channel_mix(xc, attend(n_ctx // ROW_TILE, 0, n_ctx), lambda bi: b, False)
        x = x_new
    return x
```

```python
import functools
import math

import numpy as np
import jax
import jax.numpy as jnp
from jax import lax
from jax.experimental import pallas as pl
from jax.experimental.pallas import tpu as pltpu

LANES = 128
MXU_TILE = 256
VMEM_LIMIT = 56 * 1024 * 1024

NORM_EPS = 1e-6
ROPE_THETA = 10000.0
GRID_W = 64
N_MOD = 6
HEAD_DIM = 64
GQA_HEADS, GQA_KV_HEADS = 8, 2
DIFF_HEADS, DIFF_QK_DIM, DIFF_V_DIM = 4, 32, 64
MLA_HEADS, MLA_Q_RANK, MLA_KV_RANK = 4, 192, 128
MLA_NOPE_DIM, MLA_ROPE_DIM, MLA_V_DIM = 64, 32, 64
N_EXPERTS, TOP_K = 8, 2

ROW_TILE = 256
KEY_CHUNK = MXU_TILE
FFN_CHUNK = MXU_TILE
FFN_ROWS = 512

F32 = jnp.float32
BF16 = jnp.bfloat16
HIGHEST = lax.Precision.HIGHEST


def _params(*sem):
    return pltpu.CompilerParams(dimension_semantics=sem, vmem_limit_bytes=VMEM_LIMIT)


def _lane_iota(shape):
    return lax.broadcasted_iota(jnp.int32, shape, len(shape) - 1)


def _mod_body(c_ref, w_ref, b_ref, o_ref):
    c = c_ref[...]
    sc = c * jax.nn.sigmoid(c)
    o_ref[...] = jnp.dot(sc, w_ref[...], precision=HIGHEST,
                         preferred_element_type=F32) + b_ref[...]


def _modulation(c_all, w, b):
    rows, d = c_all.shape
    n = w.shape[1]
    tn = 1536
    return pl.pallas_call(
        _mod_body,
        grid=(n // tn,),
        in_specs=[pl.BlockSpec((rows, d), lambda j: (0, 0)),
                  pl.BlockSpec((d, tn), lambda j: (0, j)),
                  pl.BlockSpec((1, tn), lambda j: (0, j))],
        out_specs=pl.BlockSpec((rows, tn), lambda j: (0, j)),
        out_shape=jax.ShapeDtypeStruct((rows, n), F32),
        compiler_params=_params("arbitrary"),
        name="modulation",
    )(c_all, w, b.reshape(1, n))


def _rms_rows(x, width):
    return lax.rsqrt(jnp.sum(x * x, axis=-1, keepdims=True) * (1.0 / width) + NORM_EPS)


def _rope(t, cos, sin_signed, half):
    lane = _lane_iota(t.shape)
    partner = jnp.where((lane & half) == 0,
                        pltpu.roll(t, LANES - half, 1), pltpu.roll(t, half, 1))
    return t * cos + partner * sin_signed


def _with_ones(t):
    return jnp.where(_lane_iota(t.shape) == HEAD_DIM, 1.0, t)


def _proj_body(x_ref, xc_ref, mod_ref, g_ref, w_ref, ca_ref, sa_ref, cb_ref, sb_ref,
               gq_ref, gk_ref, bd_ref, gcq_ref, gckv_ref, wuq_ref, wukv_ref,
               qa_ref, ka_ref, va_ref, qb_ref, kb_ref, vb_ref, qm_ref, km_ref, vm_ref,
               *, scale_a, scale_b, scale_m):
    i = pl.program_id(1)
    is_ctx = i == pl.num_programs(1) - 1
    x = jnp.where(is_ctx, xc_ref[0], x_ref[0])
    d = x.shape[-1]
    mod = mod_ref[0]
    y = x * _rms_rows(x, d) * g_ref[...]
    h = (y * (1.0 + mod[1:2]) + mod[0:1]).astype(BF16)
    p = jnp.dot(h, w_ref[...], preferred_element_type=F32)

    ca, sa, cb, sb = ca_ref[...], sa_ref[...], cb_ref[...], sb_ref[...]
    bd = bd_ref[...]
    tile = lambda ref, j: (0, slice(None), slice(j * LANES, (j + 1) * LANES))
    cols = lambda a, base, j: a[:, base + j * LANES:base + (j + 1) * LANES]

    def head_norm(t, g):
        ms = jnp.dot(t * t, bd, precision=HIGHEST, preferred_element_type=F32)
        return t * lax.rsqrt(ms + NORM_EPS) * g

    for j in range(4):
        t = head_norm(cols(p, 0, j), gq_ref[...])
        qa_ref[tile(qa_ref, j)] = (_rope(t, ca, sa, 32) * scale_a).astype(BF16)
    for j in range(2):
        t = head_norm(cols(p, 512, j), gk_ref[...])
        ka_ref[tile(ka_ref, j)] = _rope(t, ca, sa, 32).astype(BF16)
        va_ref[tile(va_ref, j)] = _with_ones(cols(p, 768, j)).astype(BF16)

    for j in range(2):
        qb_ref[tile(qb_ref, j)] = (_rope(cols(p, 1024, j), cb, sb, 16) * scale_b).astype(BF16)
        kb_ref[tile(kb_ref, j)] = _rope(cols(p, 1280, j), cb, sb, 16).astype(BF16)
    for j in range(4):
        vb_ref[tile(vb_ref, j)] = _with_ones(cols(p, 1536, j)).astype(BF16)

    cq = p[:, 2048:2304]
    cqn = (cq * _rms_rows(cq, MLA_Q_RANK) * gcq_ref[...]).astype(BF16)
    uq = jnp.dot(cqn, wuq_ref[...], preferred_element_type=F32)
    ckv = p[:, 2304:2432]
    ckvn = (ckv * _rms_rows(ckv, MLA_KV_RANK) * gckv_ref[...]).astype(BF16)
    ukv = jnp.dot(ckvn, wukv_ref[...], preferred_element_type=F32)
    kr = _rope(p[:, 2432:2560], cb, sb, 16).astype(BF16)
    for pr in range(2):
        qm_ref[tile(qm_ref, 2 * pr)] = (cols(uq, 0, 2 * pr) * scale_m).astype(BF16)
        qr = _rope(cols(uq, 0, 2 * pr + 1), cb, sb, 16)
        qm_ref[tile(qm_ref, 2 * pr + 1)] = (qr * scale_m).astype(BF16)
        km_ref[tile(km_ref, 2 * pr)] = cols(ukv, 0, pr).astype(BF16)
        km_ref[tile(km_ref, 2 * pr + 1)] = kr
    for j in range(4):
        vm_ref[tile(vm_ref, j)] = _with_ones(cols(ukv, 256, j)).astype(BF16)


def _project(x, xc, mod3, g, w, tables, gq2, gk2, bd, gcq, gckv, wuq, wukv):
    b, s, d = x.shape
    c = xc.shape[1]
    t = c + s
    tm = ROW_TILE
    assert c == tm and s % tm == 0
    nt = t // tm
    ctx_row = b
    widths = (512, 256, 256, 256, 256, 512, 512, 512, 512)
    log2e = math.log2(math.e)
    body = functools.partial(_proj_body, scale_a=HEAD_DIM ** -0.5 * log2e,
                             scale_b=DIFF_QK_DIM ** -0.5 * log2e,
                             scale_m=(MLA_NOPE_DIM + MLA_ROPE_DIM) ** -0.5 * log2e)
    const = lambda shape: pl.BlockSpec(shape, lambda bi, i: (0,) * len(shape))
    tab = pl.BlockSpec((tm, LANES), lambda bi, i: (i, 0))
    return pl.pallas_call(
        body,
        grid=(b, nt),
        in_specs=[pl.BlockSpec((1, tm, d), lambda bi, i: (bi, jnp.minimum(i, nt - 2), 0)),
                  pl.BlockSpec((1, tm, d), lambda bi, i: (bi, 0, 0)),
                  pl.BlockSpec((1, N_MOD, d), lambda bi, i: (jnp.where(i == nt - 1, ctx_row, bi), 0, 0)),
                  const((1, d)), const(w.shape), tab, tab, tab, tab,
                  const((1, LANES)), const((1, LANES)), const((LANES, LANES)),
                  const(gcq.shape), const(gckv.shape), const(wuq.shape), const(wukv.shape)],
        out_specs=[pl.BlockSpec((1, tm, wd), lambda bi, i: (bi, i, 0)) for wd in widths],
        out_shape=[jax.ShapeDtypeStruct((b, t, wd), BF16) for wd in widths],
        compiler_params=_params("parallel", "arbitrary"),
        name="project",
    )(x, xc, mod3, g, w, *tables, gq2, gk2, bd, gcq, gckv, wuq, wukv)


def _attn_step(qs, k_ref, v_ref, s_ref, m_ref, n_chunks, n_vtiles):
    step = pl.program_id(0)

    @pl.when(step == 0)
    def _():
        s_ref[...] = jnp.zeros(s_ref.shape, F32)
        m_ref[...] = jnp.zeros(m_ref.shape, F32)

    cur = step % 2
    m_rows = qs.shape[0]
    rows = m_rows // n_vtiles
    acc = [None] * n_vtiles
    for c in range(n_chunks):
        keys = slice(c * KEY_CHUNK, (c + 1) * KEY_CHUNK)
        s_new = lax.dot_general(qs, k_ref[0, keys, :], (((1,), (1,)), ((), ())),
                                preferred_element_type=F32)
        s_old = s_ref[c]
        m_old = m_ref[1 - cur]
        pb = jnp.concatenate([jnp.exp2(s_old[:, :LANES] - m_old), jnp.exp2(s_old[:, LANES:] - m_old)],
                             axis=1).astype(BF16)
        for g in range(n_vtiles):
            part = jnp.dot(pb[g * rows:(g + 1) * rows], v_ref[0, keys, g * LANES:(g + 1) * LANES],
                           preferred_element_type=F32)
            acc[g] = part if acc[g] is None else acc[g] + part
        s_ref[c] = s_new
        mc = jnp.maximum(s_new[:, :LANES], s_new[:, LANES:])
        m_ref[cur] = mc if c == 0 else jnp.maximum(m_ref[cur], mc)
    m_ref[cur] = jnp.broadcast_to(jnp.max(m_ref[cur], axis=-1, keepdims=True), m_ref.shape[1:])
    o = jnp.concatenate(acc, axis=0) if n_vtiles > 1 else acc[0]
    return o * (1.0 / o[:, HEAD_DIM:HEAD_DIM + 1])


def _pack_heads(even, odd):
    lo = _lane_iota(even.shape) < HEAD_DIM
    return jnp.where(lo, even, pltpu.roll(odd, HEAD_DIM, 1))


def _gqa_body(q_ref, k_ref, v_ref, o_ref, s_ref, m_ref, *, n_chunks):
    tq = q_ref.shape[1]
    lo = _lane_iota((tq, LANES)) < HEAD_DIM
    heads = []
    for j in range(2):
        t = q_ref[0, :, j * LANES:(j + 1) * LANES].astype(F32)
        heads.append(jnp.where(lo, t, 0.0))
        heads.append(jnp.where(lo, pltpu.roll(t, HEAD_DIM, 1), 0.0))
    qs = jnp.concatenate(heads, axis=0).astype(BF16)
    o = _attn_step(qs, k_ref, v_ref, s_ref, m_ref, n_chunks, 1)
    for j in range(2):
        o_ref[0, :, j * LANES:(j + 1) * LANES] = _pack_heads(
            o[(2 * j) * tq:(2 * j + 1) * tq], o[(2 * j + 1) * tq:(2 * j + 2) * tq]).astype(o_ref.dtype)


def _diff_body(q_ref, k_ref, v_ref, lq1_ref, lk1_ref, lq2_ref, lk2_ref, gsub_ref, o_ref,
               s_ref, m_ref, *, n_chunks, lam_init):
    tq = q_ref.shape[1]
    lane = _lane_iota((tq, LANES))
    t = q_ref[0]
    zero = jnp.zeros_like(t)
    qs = jnp.concatenate([jnp.where((lane // DIFF_QK_DIM) == j, t, zero) for j in range(4)], axis=0)
    o = _attn_step(qs, k_ref, v_ref, s_ref, m_ref, n_chunks, 2)
    lam = (jnp.exp(jnp.sum(lq1_ref[...] * lk1_ref[...], axis=-1, keepdims=True))
           - jnp.exp(jnp.sum(lq2_ref[...] * lk2_ref[...], axis=-1, keepdims=True)) + lam_init)
    d = _pack_heads(o[0:tq] - lam * o[tq:2 * tq], o[2 * tq:3 * tq] - lam * o[3 * tq:4 * tq])
    lo = lane < DIFF_V_DIM
    sq = d * d
    ms_lo = jnp.sum(jnp.where(lo, sq, 0.0), axis=-1, keepdims=True) * (1.0 / DIFF_V_DIM)
    ms_hi = jnp.sum(jnp.where(lo, 0.0, sq), axis=-1, keepdims=True) * (1.0 / DIFF_V_DIM)
    r = jnp.where(lo, lax.rsqrt(ms_lo + NORM_EPS), lax.rsqrt(ms_hi + NORM_EPS))
    o_ref[0] = ((d * r * gsub_ref[...]) * (1.0 - lam_init)).astype(o_ref.dtype)


def _mla_body(q_ref, k_ref, v_ref, o_ref, s_ref, m_ref, *, n_chunks):
    tq = q_ref.shape[1]
    wq = q_ref.shape[2]
    lane = _lane_iota((tq, wq))
    t = q_ref[0]
    zero = jnp.zeros_like(t)
    first = (lane < MLA_NOPE_DIM) | ((lane >= LANES) & (lane < LANES + MLA_ROPE_DIM))
    second = ((lane >= MLA_NOPE_DIM) & (lane < LANES)) | (
        (lane >= LANES + MLA_ROPE_DIM) & (lane < LANES + 2 * MLA_ROPE_DIM))
    qs = jnp.concatenate([jnp.where(first, t, zero), jnp.where(second, t, zero)], axis=0)
    o = _attn_step(qs, k_ref, v_ref, s_ref, m_ref, n_chunks, 2)
    o_ref[0] = _pack_heads(o[0:tq], o[tq:2 * tq]).astype(o_ref.dtype)


def _attention(body, q, k, v, extra, *, n_groups, wq, wk, wv, n_s, tq, q_tiles, q_tile0, key0, n_keys):
    b = q.shape[0]
    wo = wq if body is _gqa_body else LANES
    n_chunks = n_keys // KEY_CHUNK
    key_blk = key0 // n_keys
    assert key_blk * n_keys == key0
    m_rows = n_s * tq
    n_tiles = b * n_groups * q_tiles

    def split(t):
        return t // (n_groups * q_tiles), (t // q_tiles) % n_groups, t % q_tiles

    cur = lambda j: split(jnp.minimum(j, n_tiles - 1))
    prev = lambda j: split(jnp.maximum(j - 1, 0))
    extra_specs = [pl.BlockSpec(e.shape, lambda j: (0, 0)) for e in extra]
    return pl.pallas_call(
        functools.partial(body, n_chunks=n_chunks),
        grid=(n_tiles + 1,),
        in_specs=[pl.BlockSpec((1, tq, wq), lambda j: (cur(j)[0], cur(j)[2] + q_tile0, cur(j)[1])),
                  pl.BlockSpec((1, n_keys, wk), lambda j: (cur(j)[0], key_blk, cur(j)[1])),
                  pl.BlockSpec((1, n_keys, wv), lambda j: (prev(j)[0], key_blk, prev(j)[1]))] + extra_specs,
        out_specs=pl.BlockSpec((1, tq, wo), lambda j: (prev(j)[0], prev(j)[2], prev(j)[1])),
        out_shape=jax.ShapeDtypeStruct((b, q_tiles * tq, n_groups * wo), BF16),
        scratch_shapes=[pltpu.VMEM((n_chunks, m_rows, KEY_CHUNK), F32),
                        pltpu.VMEM((2, m_rows, LANES), F32)],
        compiler_params=_params("arbitrary"),
        name=body.__name__.strip("_"),
    )(q, k, v, *extra)


def _router_gates(logits):
    lane = _lane_iota(logits.shape).astype(F32)
    neg = jnp.float32(-jnp.inf)
    z = jnp.where(lane < N_EXPERTS, logits, neg)
    m1 = jnp.max(z, axis=-1, keepdims=True)
    i1 = jnp.min(jnp.where(z == m1, lane, float(LANES)), axis=-1, keepdims=True)
    z2 = jnp.where(lane == i1, neg, z)
    m2 = jnp.max(z2, axis=-1, keepdims=True)
    i2 = jnp.min(jnp.where(z2 == m2, lane, float(LANES)), axis=-1, keepdims=True)
    e2 = jnp.exp(m2 - m1)
    den = 1.0 + e2
    return jnp.where(lane == i1, 1.0 / den, 0.0) + jnp.where(lane == i2, e2 / den, 0.0)


def _outproj_body(x_ref, oa_ref, ob_ref, om_ref, w_ref, mod_ref, g_ref, *rest, routed):
    if routed:
        wr_ref, xo_ref, h_ref, gates_ref = rest
    else:
        xo_ref, h_ref = rest
    mod = mod_ref[0]
    wa, wb = oa_ref.shape[2], ob_ref.shape[2]
    y = jnp.dot(oa_ref[0], w_ref[0:wa, :], preferred_element_type=F32)
    y += jnp.dot(ob_ref[0], w_ref[wa:wa + wb, :], preferred_element_type=F32)
    y += jnp.dot(om_ref[0], w_ref[wa + wb:, :], preferred_element_type=F32)
    x = x_ref[0] + mod[2:3] * y
    xo_ref[0] = x
    n = x * _rms_rows(x, x.shape[-1]) * g_ref[...]
    h = n * (1.0 + mod[4:5]) + mod[3:4]
    h_ref[0] = h.astype(BF16)
    if routed:
        logits = jnp.dot(h, wr_ref[...], precision=HIGHEST, preferred_element_type=F32)
        gates_ref[0] = _router_gates(logits)


def _out_project(x, oa, ob, om, w, mod3, g, mod_row, w_router=None):
    b, s, d = x.shape
    tm = ROW_TILE
    routed = w_router is not None
    row = lambda bi, i: (bi, i, 0)
    in_specs = [pl.BlockSpec((1, tm, d), row),
                pl.BlockSpec((1, tm, oa.shape[2]), row),
                pl.BlockSpec((1, tm, ob.shape[2]), row),
                pl.BlockSpec((1, tm, om.shape[2]), row),
                pl.BlockSpec(w.shape, lambda bi, i: (0, 0)),
                pl.BlockSpec((1, N_MOD, d), lambda bi, i: (mod_row(bi), 0, 0)),
                pl.BlockSpec((1, d), lambda bi, i: (0, 0))]
    out_specs = [pl.BlockSpec((1, tm, d), row), pl.BlockSpec((1, tm, d), row)]
    out_shape = [jax.ShapeDtypeStruct((b, s, d), F32), jax.ShapeDtypeStruct((b, s, d), BF16)]
    args = [x, oa, ob, om, w, mod3, g]
    if routed:
        in_specs.append(pl.BlockSpec(w_router.shape, lambda bi, i: (0, 0)))
        out_specs.append(pl.BlockSpec((1, tm, LANES), row))
        out_shape.append(jax.ShapeDtypeStruct((b, s, LANES), F32))
        args.append(w_router)
    return pl.pallas_call(
        functools.partial(_outproj_body, routed=routed),
        grid=(b, s // tm),
        in_specs=in_specs, out_specs=out_specs, out_shape=out_shape,
        compiler_params=_params("parallel", "arbitrary"),
        name="out_project_routed" if routed else "out_project",
    )(*args)


def _swiglu_chunks(h, wg_ref, wu_ref, wo_ref, acc_ref, lead):
    n_chunks = wg_ref.shape[len(lead)]
    acc_ref[...] = jnp.zeros(acc_ref.shape, F32)

    def chunk(j, carry):
        gate = jnp.dot(h, wg_ref[lead + (j,)], preferred_element_type=F32)
        up = jnp.dot(h, wu_ref[lead + (j,)], preferred_element_type=F32)
        a = (gate * jax.nn.sigmoid(gate) * up).astype(BF16)
        acc_ref[...] += jnp.dot(a, wo_ref[lead + (j,)], preferred_element_type=F32)
        return carry

    lax.fori_loop(0, n_chunks, chunk, 0)


def _residual_out(x, gate, y, gf_ref, final_norm):
    x = x + gate * y
    if final_norm:
        x = x * _rms_rows(x, x.shape[-1]) * gf_ref[...]
    return x


def _ffn_body(x_ref, h_ref, wg_ref, wu_ref, wo_ref, mod_ref, gf_ref, o_ref, acc_ref, *, final_norm):
    _swiglu_chunks(h_ref[0], wg_ref, wu_ref, wo_ref, acc_ref, ())
    o_ref[0] = _residual_out(x_ref[0], mod_ref[0][5:6], acc_ref[...], gf_ref, final_norm)


def _dense_ffn(x, h, wg, wu, wo, mod3, mod_row, g_final, final_norm):
    b, s, d = x.shape
    tm = min(FFN_ROWS, s)
    row = lambda bi, i: (bi, i, 0)
    resident = lambda a: pl.BlockSpec(a.shape, lambda bi, i: (0, 0, 0),
                                      pipeline_mode=pl.Buffered(1))
    return pl.pallas_call(
        functools.partial(_ffn_body, final_norm=final_norm),
        grid=(b, s // tm),
        in_specs=[pl.BlockSpec((1, tm, d), row), pl.BlockSpec((1, tm, d), row),
                  resident(wg), resident(wu), resident(wo),
                  pl.BlockSpec((1, N_MOD, d), lambda bi, i: (mod_row(bi), 0, 0)),
                  pl.BlockSpec((1, d), lambda bi, i: (0, 0))],
        out_specs=pl.BlockSpec((1, tm, d), row),
        out_shape=jax.ShapeDtypeStruct((b, s, d), F32),
        scratch_shapes=[pltpu.VMEM((tm, d), F32)],
        compiler_params=_params("parallel", "arbitrary"),
        name="dense_ffn",
    )(x, h, wg, wu, wo, mod3, g_final)


def _moe_body(x_ref, h_ref, gates_ref, wg_ref, wu_ref, wo_ref, mod_ref, gf_ref, o_ref,
              acc_ref, tot_ref, *, final_norm):
    e = pl.program_id(2)

    @pl.when(e == 0)
    def _():
        tot_ref[...] = jnp.zeros(tot_ref.shape, F32)

    _swiglu_chunks(h_ref[0], wg_ref, wu_ref, wo_ref, acc_ref, (0,))
    gates = gates_ref[0]
    gate_e = jnp.sum(jnp.where(_lane_iota(gates.shape) == e, gates, 0.0), axis=-1, keepdims=True)
    tot_ref[...] += gate_e * acc_ref[...]

    @pl.when(e == pl.num_programs(2) - 1)
    def _():
        o_ref[0] = _residual_out(x_ref[0], mod_ref[0][5:6], tot_ref[...], gf_ref, final_norm)


def _moe_ffn(x, h, gates, wg, wu, wo, mod3, mod_row, g_final, final_norm):
    b, s, d = x.shape
    tm = min(FFN_ROWS, s)
    row = lambda bi, i, e: (bi, i, 0)
    expert = lambda a: pl.BlockSpec((1,) + a.shape[1:], lambda bi, i, e: (e, 0, 0, 0))
    return pl.pallas_call(
        functools.partial(_moe_body, final_norm=final_norm),
        grid=(b, s // tm, wg.shape[0]),
        in_specs=[pl.BlockSpec((1, tm, d), row), pl.BlockSpec((1, tm, d), row),
                  pl.BlockSpec((1, tm, LANES), row),
                  expert(wg), expert(wu), expert(wo),
                  pl.BlockSpec((1, N_MOD, d), lambda bi, i, e: (mod_row(bi), 0, 0)),
                  pl.BlockSpec((1, d), lambda bi, i, e: (0, 0))],
        out_specs=pl.BlockSpec((1, tm, d), row),
        out_shape=jax.ShapeDtypeStruct((b, s, d), F32),
        scratch_shapes=[pltpu.VMEM((tm, d), F32), pltpu.VMEM((tm, d), F32)],
        compiler_params=_params("parallel", "parallel", "arbitrary"),
        name="moe_ffn",
    )(x, h, gates, wg, wu, wo, mod3, g_final)


def _deinterleave(n):
    return np.concatenate([np.arange(0, n, 2), np.arange(1, n, 2)])


def _in_proj_columns(pad):
    pads = lambda n: np.full((n,), pad)
    cols = []
    for hd in range(GQA_HEADS):
        cols.append(64 * hd + _deinterleave(64))
    for g in range(GQA_KV_HEADS):
        cols += [512 + 64 * g + _deinterleave(64), pads(64)]
    for g in range(GQA_KV_HEADS):
        cols += [640 + 64 * g + np.arange(64), pads(64)]
    for u in range(2 * DIFF_HEADS):
        cols.append(768 + 32 * u + _deinterleave(32))
    for u in range(2 * DIFF_HEADS):
        cols.append(1024 + 32 * u + _deinterleave(32))
    for hd in range(DIFF_HEADS):
        cols += [1280 + 64 * hd + np.arange(64), pads(64)]
    cols += [1536 + np.arange(MLA_Q_RANK), pads(64)]
    cols.append(1728 + np.arange(MLA_KV_RANK))
    cols += [1856 + _deinterleave(32), 1856 + _deinterleave(32), pads(64)]
    return np.concatenate(cols)


def _uq_columns(pad):
    per = MLA_NOPE_DIM + MLA_ROPE_DIM
    cols = []
    for pr in range(2):
        h0, h1 = 2 * pr, 2 * pr + 1
        cols += [per * h0 + np.arange(64), per * h1 + np.arange(64),
                 per * h0 + 64 + _deinterleave(32), per * h1 + 64 + _deinterleave(32),
                 np.full((64,), pad)]
    return np.concatenate(cols)


def _ukv_columns(pad):
    per = MLA_NOPE_DIM + MLA_V_DIM
    k = [per * hd + np.arange(64) for hd in range(MLA_HEADS)]
    v = []
    for hd in range(MLA_HEADS):
        v += [per * hd + 64 + np.arange(64), np.full((64,), pad)]
    return np.concatenate(k + v)


def _take_cols(w, cols):
    w_ext = jnp.concatenate([w, jnp.zeros((w.shape[0], 1), w.dtype)], axis=1)
    return jnp.take(w_ext, jnp.asarray(cols), axis=1)


def _rope_tables(s, c):
    t = jnp.arange(s)
    rows = (t // GRID_W).astype(F32)
    cols = (t % GRID_W).astype(F32)
    out = []
    for dim in (HEAD_DIM, DIFF_QK_DIM):
        quarter = dim // 4
        half = dim // 2
        inv_freq = ROPE_THETA ** (-jnp.arange(quarter, dtype=F32) / quarter)
        ang = jnp.concatenate([rows[:, None] * inv_freq, cols[:, None] * inv_freq], axis=-1)
        lane = np.arange(LANES)
        idx = (lane % dim) % half
        sign = np.where((lane % dim) < half, -1.0, 1.0).astype(np.float32)
        cos = jnp.cos(ang)[:, idx]
        sin = jnp.sin(ang)[:, idx] * sign
        out.append(jnp.concatenate([cos, jnp.ones((c, LANES), F32)], axis=0))
        out.append(jnp.concatenate([sin, jnp.zeros((c, LANES), F32)], axis=0))
    return out


def _chunked_in(w, n_hidden):
    d = w.shape[0]
    nc = n_hidden // FFN_CHUNK
    g = w[:, :n_hidden].reshape(d, nc, FFN_CHUNK).transpose(1, 0, 2).astype(BF16)
    u = w[:, n_hidden:].reshape(d, nc, FFN_CHUNK).transpose(1, 0, 2).astype(BF16)
    return g, u


def kernel(x, c, ctx, c_ctx, w_mod, b_mod, g_attn, g_ffn, w_in, w_out, gqa_gq, gqa_gk,
           diff_lq1, diff_lk1, diff_lq2, diff_lk2, diff_gsub, mla_gcq, mla_gckv, mla_wuq, mla_wukv,
           ffn_w_in, ffn_w_out, moe_router, moe_w_in, moe_w_out, g_final):
    b, s, d = x.shape
    n_ctx = ctx.shape[1]
    depth = w_mod.shape[0]
    t_all = n_ctx + s
    tables = _rope_tables(s, n_ctx)
    in_cols = _in_proj_columns(w_in.shape[2])
    uq_cols = _uq_columns(mla_wuq.shape[2])
    ukv_cols = _ukv_columns(mla_wukv.shape[2])
    perm64 = _deinterleave(HEAD_DIM)
    bd = jnp.asarray(np.kron(np.eye(LANES // HEAD_DIM), np.full((HEAD_DIM, HEAD_DIM), 1.0 / HEAD_DIM)), F32)

    mod_rows = 16
    c_all = jnp.zeros((mod_rows, d), F32).at[:b].set(c).at[b].set(c_ctx)
    pad_lanes = lambda v: jnp.zeros((1, LANES), F32).at[0, :v.shape[0]].set(v)

    xc = ctx
    for l in range(depth):
        need_ctx = l < depth - 1
        lam_init = 0.8 - 0.6 * math.exp(-0.3 * l)
        mod3 = _modulation(c_all, w_mod[l], b_mod[l]).reshape(mod_rows, N_MOD, d)

        w_in_p = _take_cols(w_in[l], in_cols).astype(BF16)
        wuq_p = jnp.zeros((2 * LANES, 512), F32).at[:MLA_Q_RANK].set(
            _take_cols(mla_wuq[l], uq_cols)).astype(BF16)
        wukv_p = _take_cols(mla_wukv[l], ukv_cols).astype(BF16)
        gq2 = jnp.tile(gqa_gq[l][perm64], 2).reshape(1, LANES)
        gk2 = jnp.tile(gqa_gk[l][perm64], 2).reshape(1, LANES)
        gcq = jnp.zeros((1, 2 * LANES), F32).at[0, :MLA_Q_RANK].set(mla_gcq[l])
        gckv = mla_gckv[l].reshape(1, MLA_KV_RANK)

        qa, ka, va, qb, kb, vb, qm, km, vm = _project(
            x, xc, mod3, g_attn[l].reshape(1, d), w_in_p, tables, gq2, gk2, bd, gcq, gckv, wuq_p, wukv_p)

        diff_extra = [pad_lanes(diff_lq1[l]), pad_lanes(diff_lk1[l]), pad_lanes(diff_lq2[l]),
                      pad_lanes(diff_lk2[l]), jnp.tile(diff_gsub[l], 2).reshape(1, LANES)]
        diff_body = functools.partial(_diff_body, lam_init=lam_init)
        diff_body.__name__ = "_diff_body"

        def attend(q_rows, q_row0, key0, n_keys, tq_mla):
            kw = dict(key0=key0, n_keys=n_keys)
            tq = ROW_TILE
            oa = _attention(_gqa_body, qa, ka, va, [], n_groups=2, wq=256, wk=LANES, wv=LANES, n_s=4,
                            tq=tq, q_tiles=q_rows // tq, q_tile0=q_row0 // tq, **kw)
            ob = _attention(diff_body, qb, kb, vb, diff_extra, n_groups=2, wq=LANES, wk=LANES, wv=256,
                            n_s=4, tq=tq, q_tiles=q_rows // tq, q_tile0=q_row0 // tq, **kw)
            om = _attention(_mla_body, qm, km, vm, [], n_groups=2, wq=256, wk=256, wv=256, n_s=2,
                            tq=tq_mla, q_tiles=q_rows // tq_mla, q_tile0=q_row0 // tq_mla, **kw)
            return oa, ob, om

        w_out_b = w_out[l].astype(BF16)
        g2 = g_ffn[l].reshape(1, d)
        dense = l % 2 == 0
        if dense:
            wg, wu = _chunked_in(ffn_w_in[l // 2], ffn_w_out.shape[1])
            wo = ffn_w_out[l // 2].reshape(-1, FFN_CHUNK, d).astype(BF16)
            w_router = None
        else:
            n_hidden = moe_w_out.shape[2]
            wi = moe_w_in[l // 2]
            wg = wi[:, :, :n_hidden].reshape(N_EXPERTS, d, -1, FFN_CHUNK).transpose(0, 2, 1, 3).astype(BF16)
            wu = wi[:, :, n_hidden:].reshape(N_EXPERTS, d, -1, FFN_CHUNK).transpose(0, 2, 1, 3).astype(BF16)
            wo = moe_w_out[l // 2].reshape(N_EXPERTS, -1, FFN_CHUNK, d).astype(BF16)
            w_router = jnp.zeros((d, LANES), F32).at[:, :N_EXPERTS].set(moe_router[l // 2])
        last = l == depth - 1

        def channel_mix(xs, attn_out, mod_row, final_norm):
            res = _out_project(xs, *attn_out, w_out_b, mod3, g2, mod_row, w_router)
            gf = g_final.reshape(1, d)
            if dense:
                return _dense_ffn(res[0], res[1], wg, wu, wo, mod3, mod_row, gf, final_norm)
            return _moe_ffn(res[0], res[1], res[2], wg, wu, wo, mod3, mod_row, gf, final_norm)

        x_new = channel_mix(x, attend(s, 0, 0, t_all, 2 * ROW_TILE), lambda bi: bi, last)
        if need_ctx:
            xc = channel_mix(xc, attend(n_ctx, s, s, n_ctx, ROW_TILE), lambda bi: b, False)
        x = x_new
    return x
```

```python
import functools
import math

import numpy as np
import jax
import jax.numpy as jnp
from jax import lax
from jax.experimental import pallas as pl
from jax.experimental.pallas import tpu as pltpu

LANES = 128
MXU_TILE = 256
VMEM_LIMIT = 56 * 1024 * 1024

NORM_EPS = 1e-6
ROPE_THETA = 10000.0
GRID_W = 64
N_MOD = 6
HEAD_DIM = 64
GQA_HEADS, GQA_KV_HEADS = 8, 2
DIFF_HEADS, DIFF_QK_DIM, DIFF_V_DIM = 4, 32, 64
MLA_HEADS, MLA_Q_RANK, MLA_KV_RANK = 4, 192, 128
MLA_NOPE_DIM, MLA_ROPE_DIM, MLA_V_DIM = 64, 32, 64
N_EXPERTS, TOP_K = 8, 2

ROW_TILE = 256
KEY_CHUNK = MXU_TILE
FFN_CHUNK = MXU_TILE
FFN_ROWS = 512

F32 = jnp.float32
BF16 = jnp.bfloat16
HIGHEST = lax.Precision.HIGHEST


def _params(*sem):
    return pltpu.CompilerParams(dimension_semantics=sem, vmem_limit_bytes=VMEM_LIMIT)


def _lane_iota(shape):
    return lax.broadcasted_iota(jnp.int32, shape, len(shape) - 1)


def _mod_body(c_ref, w_ref, b_ref, o_ref):
    c = c_ref[...]
    sc = c * jax.nn.sigmoid(c)
    o_ref[...] = jnp.dot(sc, w_ref[...], precision=HIGHEST,
                         preferred_element_type=F32) + b_ref[...]


def _modulation(c_all, w, b):
    rows, d = c_all.shape
    n = w.shape[1]
    tn = 1536
    return pl.pallas_call(
        _mod_body,
        grid=(n // tn,),
        in_specs=[pl.BlockSpec((rows, d), lambda j: (0, 0)),
                  pl.BlockSpec((d, tn), lambda j: (0, j)),
                  pl.BlockSpec((1, tn), lambda j: (0, j))],
        out_specs=pl.BlockSpec((rows, tn), lambda j: (0, j)),
        out_shape=jax.ShapeDtypeStruct((rows, n), F32),
        compiler_params=_params("arbitrary"),
        name="modulation",
    )(c_all, w, b.reshape(1, n))


def _rms_rows(x, width):
    return lax.rsqrt(jnp.sum(x * x, axis=-1, keepdims=True) * (1.0 / width) + NORM_EPS)


def _rope(t, cos, sin_signed, half):
    lane = _lane_iota(t.shape)
    partner = jnp.where((lane & half) == 0,
                        pltpu.roll(t, LANES - half, 1), pltpu.roll(t, half, 1))
    return t * cos + partner * sin_signed


def _with_ones(t):
    return jnp.where(_lane_iota(t.shape) == HEAD_DIM, 1.0, t)


def _proj_body(x_ref, xc_ref, mod_ref, g_ref, w_ref, ca_ref, sa_ref, cb_ref, sb_ref,
               gq_ref, gk_ref, bd_ref, gcq_ref, gckv_ref, wuq_ref, wukv_ref,
               qa_ref, ka_ref, va_ref, qb_ref, kb_ref, vb_ref, qm_ref, km_ref, vm_ref,
               *, scale_a, scale_b, scale_m):
    i = pl.program_id(1)
    is_ctx = i == pl.num_programs(1) - 1
    x = jnp.where(is_ctx, xc_ref[0], x_ref[0])
    d = x.shape[-1]
    mod = mod_ref[0]
    y = x * _rms_rows(x, d) * g_ref[...]
    h = (y * (1.0 + mod[1:2]) + mod[0:1]).astype(BF16)
    p = jnp.dot(h, w_ref[...], preferred_element_type=F32)

    ca, sa, cb, sb = ca_ref[...], sa_ref[...], cb_ref[...], sb_ref[...]
    bd = bd_ref[...]
    tile = lambda ref, j: (0, slice(None), slice(j * LANES, (j + 1) * LANES))
    cols = lambda a, base, j: a[:, base + j * LANES:base + (j + 1) * LANES]

    def head_norm(t, g):
        ms = jnp.dot(t * t, bd, precision=HIGHEST, preferred_element_type=F32)
        return t * lax.rsqrt(ms + NORM_EPS) * g

    for j in range(4):
        t = head_norm(cols(p, 0, j), gq_ref[...])
        qa_ref[tile(qa_ref, j)] = (_rope(t, ca, sa, 32) * scale_a).astype(BF16)
    for j in range(2):
        t = head_norm(cols(p, 512, j), gk_ref[...])
        ka_ref[tile(ka_ref, j)] = _rope(t, ca, sa, 32).astype(BF16)
        va_ref[tile(va_ref, j)] = _with_ones(cols(p, 768, j)).astype(BF16)

    for j in range(2):
        qb_ref[tile(qb_ref, j)] = (_rope(cols(p, 1024, j), cb, sb, 16) * scale_b).astype(BF16)
        kb_ref[tile(kb_ref, j)] = _rope(cols(p, 1280, j), cb, sb, 16).astype(BF16)
    vb_ref[0] = p[:, 1536:1792].astype(BF16)

    cq = p[:, 1792:2048]
    cqn = (cq * _rms_rows(cq, MLA_Q_RANK) * gcq_ref[...]).astype(BF16)
    uq = jnp.dot(cqn, wuq_ref[...], preferred_element_type=F32)
    ckv = p[:, 2048:2176]
    ckvn = (ckv * _rms_rows(ckv, MLA_KV_RANK) * gckv_ref[...]).astype(BF16)
    ukv = jnp.dot(ckvn, wukv_ref[...], preferred_element_type=F32)
    kr = _rope(p[:, 2176:2304], cb, sb, 16).astype(BF16)
    for pr in range(2):
        qm_ref[tile(qm_ref, 2 * pr)] = (cols(uq, 0, 2 * pr) * scale_m).astype(BF16)
        qr = _rope(cols(uq, 0, 2 * pr + 1), cb, sb, 16)
        qm_ref[tile(qm_ref, 2 * pr + 1)] = (qr * scale_m).astype(BF16)
        km_ref[tile(km_ref, 2 * pr)] = cols(ukv, 0, pr).astype(BF16)
        km_ref[tile(km_ref, 2 * pr + 1)] = kr
    vm_ref[0] = ukv[:, 256:512].astype(BF16)


def _project(x, xc, mod3, g, w, tables, gq2, gk2, bd, gcq, gckv, wuq, wukv):
    b, s, d = x.shape
    c = xc.shape[1]
    t = c + s
    tm = ROW_TILE
    assert c == tm and s % tm == 0
    nt = t // tm
    ctx_row = b
    widths = (512, 256, 256, 256, 256, 256, 512, 512, 256)
    log2e = math.log2(math.e)
    body = functools.partial(_proj_body, scale_a=HEAD_DIM ** -0.5 * log2e,
                             scale_b=DIFF_QK_DIM ** -0.5 * log2e,
                             scale_m=(MLA_NOPE_DIM + MLA_ROPE_DIM) ** -0.5 * log2e)
    const = lambda shape: pl.BlockSpec(shape, lambda bi, i: (0,) * len(shape))
    tab = pl.BlockSpec((tm, LANES), lambda bi, i: (i, 0))
    return pl.pallas_call(
        body,
        grid=(b, nt),
        in_specs=[pl.BlockSpec((1, tm, d), lambda bi, i: (bi, jnp.minimum(i, nt - 2), 0)),
                  pl.BlockSpec((1, tm, d), lambda bi, i: (bi, 0, 0)),
                  pl.BlockSpec((1, N_MOD, d), lambda bi, i: (jnp.where(i == nt - 1, ctx_row, bi), 0, 0)),
                  const((1, d)), const(w.shape), tab, tab, tab, tab,
                  const((1, LANES)), const((1, LANES)), const((LANES, LANES)),
                  const(gcq.shape), const(gckv.shape), const(wuq.shape), const(wukv.shape)],
        out_specs=[pl.BlockSpec((1, tm, wd), lambda bi, i: (bi, i, 0)) for wd in widths],
        out_shape=[jax.ShapeDtypeStruct((b, t, wd), BF16) for wd in widths],
        compiler_params=_params("parallel", "arbitrary"),
        name="project",
    )(x, xc, mod3, g, w, *tables, gq2, gk2, bd, gcq, gckv, wuq, wukv)


def _attn_step(qs, k_ref, v_ref, s_ref, mp_ref, mb_ref, acc_ref, l_ref, n_chunks, ones_lane):
    step = pl.program_id(0)

    @pl.when(step == 0)
    def _():
        s_ref[...] = jnp.zeros(s_ref.shape, F32)
        mp_ref[...] = jnp.zeros(mp_ref.shape, F32)
        acc_ref[...] = jnp.ones(acc_ref.shape, F32)
        l_ref[...] = jnp.ones(l_ref.shape, F32)

    raw = acc_ref[...]
    if ones_lane:
        row_sum = raw[:, HEAD_DIM:HEAD_DIM + 1]
    else:
        row_sum = jnp.sum(l_ref[...], axis=-1, keepdims=True)
    done = raw * (1.0 / row_sum)

    cur = step % 2
    mb_ref[...] = jnp.broadcast_to(jnp.max(mp_ref[1 - cur], axis=-1, keepdims=True), mb_ref.shape)
    acc = None
    l_acc = None
    for c in range(n_chunks):
        keys = slice(c * KEY_CHUNK, (c + 1) * KEY_CHUNK)
        s_new = lax.dot_general(qs, k_ref[0, keys, :], (((1,), (1,)), ((), ())),
                                preferred_element_type=F32)
        s_old = s_ref[c]
        m_old = mb_ref[...]
        p0 = jnp.exp2(s_old[:, :LANES] - m_old)
        p1 = jnp.exp2(s_old[:, LANES:] - m_old)
        if not ones_lane:
            l_acc = p0 + p1 if l_acc is None else l_acc + (p0 + p1)
        part = jnp.dot(jnp.concatenate([p0, p1], axis=1).astype(BF16), v_ref[0, keys, :],
                       preferred_element_type=F32)
        acc = part if acc is None else acc + part
        s_ref[c] = s_new
        mc = jnp.maximum(s_new[:, :LANES], s_new[:, LANES:])
        mp_ref[cur] = mc if c == 0 else jnp.maximum(mp_ref[cur], mc)
    acc_ref[...] = acc
    if not ones_lane:
        l_ref[...] = l_acc
    return done


def _pack_heads(even, odd):
    lo = _lane_iota(even.shape) < HEAD_DIM
    return jnp.where(lo, even, pltpu.roll(odd, HEAD_DIM, 1))


def _gqa_body(q_ref, k_ref, v_ref, o_ref, *scratch, n_chunks):
    tq = q_ref.shape[1]
    lo = _lane_iota((tq, LANES)) < HEAD_DIM
    heads = []
    for j in range(2):
        t = q_ref[0, :, j * LANES:(j + 1) * LANES].astype(F32)
        heads.append(jnp.where(lo, t, 0.0))
        heads.append(jnp.where(lo, pltpu.roll(t, HEAD_DIM, 1), 0.0))
    qs = jnp.concatenate(heads, axis=0).astype(BF16)
    o = _attn_step(qs, k_ref, v_ref, *scratch, n_chunks, True)
    for j in range(2):
        o_ref[0, :, j * LANES:(j + 1) * LANES] = _pack_heads(
            o[(2 * j) * tq:(2 * j + 1) * tq], o[(2 * j + 1) * tq:(2 * j + 2) * tq]).astype(o_ref.dtype)


def _diff_body(q_ref, k_ref, v_ref, lq1_ref, lk1_ref, lq2_ref, lk2_ref, gsub_ref, o_ref,
               *scratch, n_chunks, lam_init):
    tq = q_ref.shape[1]
    lane = _lane_iota((tq, LANES))
    t = q_ref[0]
    zero = jnp.zeros_like(t)
    qs = jnp.concatenate([jnp.where((lane // DIFF_QK_DIM) == j, t, zero) for j in range(4)], axis=0)
    o = _attn_step(qs, k_ref, v_ref, *scratch, n_chunks, False)
    lam = (jnp.exp(jnp.sum(lq1_ref[...] * lk1_ref[...], axis=-1, keepdims=True))
           - jnp.exp(jnp.sum(lq2_ref[...] * lk2_ref[...], axis=-1, keepdims=True)) + lam_init)
    lo = lane < DIFF_V_DIM
    d = jnp.where(lo, o[0:tq] - lam * o[tq:2 * tq], o[2 * tq:3 * tq] - lam * o[3 * tq:4 * tq])
    sq = d * d
    ms_lo = jnp.sum(jnp.where(lo, sq, 0.0), axis=-1, keepdims=True) * (1.0 / DIFF_V_DIM)
    ms_hi = jnp.sum(jnp.where(lo, 0.0, sq), axis=-1, keepdims=True) * (1.0 / DIFF_V_DIM)
    r = jnp.where(lo, lax.rsqrt(ms_lo + NORM_EPS), lax.rsqrt(ms_hi + NORM_EPS))
    o_ref[0] = ((d * r * gsub_ref[...]) * (1.0 - lam_init)).astype(o_ref.dtype)


def _mla_body(q_ref, k_ref, v_ref, o_ref, *scratch, n_chunks):
    tq = q_ref.shape[1]
    wq = q_ref.shape[2]
    lane = _lane_iota((tq, wq))
    t = q_ref[0]
    zero = jnp.zeros_like(t)
    first = (lane < MLA_NOPE_DIM) | ((lane >= LANES) & (lane < LANES + MLA_ROPE_DIM))
    second = ((lane >= MLA_NOPE_DIM) & (lane < LANES)) | (
        (lane >= LANES + MLA_ROPE_DIM) & (lane < LANES + 2 * MLA_ROPE_DIM))
    qs = jnp.concatenate([jnp.where(first, t, zero), jnp.where(second, t, zero)], axis=0)
    o = _attn_step(qs, k_ref, v_ref, *scratch, n_chunks, False)
    lo = _lane_iota((tq, LANES)) < MLA_V_DIM
    o_ref[0] = jnp.where(lo, o[0:tq], o[tq:2 * tq]).astype(o_ref.dtype)


def _attention(body, q, k, v, extra, *, n_groups, wq, wk, wv, n_s, tq, q_tiles, q_tile0, key0, n_keys):
    b = q.shape[0]
    wo = wq if body is _gqa_body else LANES
    n_chunks = n_keys // KEY_CHUNK
    key_blk = key0 // n_keys
    assert key_blk * n_keys == key0
    m_rows = n_s * tq
    n_tiles = b * n_groups * q_tiles

    def split(t):
        return t // (n_groups * q_tiles), (t // q_tiles) % n_groups, t % q_tiles

    cur = lambda j: split(jnp.minimum(j, n_tiles - 1))
    prev = lambda j: split(jnp.clip(j - 1, 0, n_tiles - 1))
    done = lambda j: split(jnp.maximum(j - 2, 0))
    extra_specs = [pl.BlockSpec(e.shape, lambda j: (0, 0)) for e in extra]
    return pl.pallas_call(
        functools.partial(body, n_chunks=n_chunks),
        grid=(n_tiles + 2,),
        in_specs=[pl.BlockSpec((1, tq, wq), lambda j: (cur(j)[0], cur(j)[2] + q_tile0, cur(j)[1])),
                  pl.BlockSpec((1, n_keys, wk), lambda j: (cur(j)[0], key_blk, cur(j)[1])),
                  pl.BlockSpec((1, n_keys, wv), lambda j: (prev(j)[0], key_blk, prev(j)[1]))] + extra_specs,
        out_specs=pl.BlockSpec((1, tq, wo), lambda j: (done(j)[0], done(j)[2], done(j)[1])),
        out_shape=jax.ShapeDtypeStruct((b, q_tiles * tq, n_groups * wo), BF16),
        scratch_shapes=[pltpu.VMEM((n_chunks, m_rows, KEY_CHUNK), F32),
                        pltpu.VMEM((2, m_rows, LANES), F32),
                        pltpu.VMEM((m_rows, LANES), F32),
                        pltpu.VMEM((m_rows, LANES), F32),
                        pltpu.VMEM((m_rows, LANES), F32)],
        compiler_params=_params("arbitrary"),
        name=body.__name__.strip("_"),
    )(q, k, v, *extra)


def _router_gates(logits):
    lane = _lane_iota(logits.shape).astype(F32)
    neg = jnp.float32(-jnp.inf)
    z = jnp.where(lane < N_EXPERTS, logits, neg)
    m1 = jnp.max(z, axis=-1, keepdims=True)
    i1 = jnp.min(jnp.where(z == m1, lane, float(LANES)), axis=-1, keepdims=True)
    z2 = jnp.where(lane == i1, neg, z)
    m2 = jnp.max(z2, axis=-1, keepdims=True)
    i2 = jnp.min(jnp.where(z2 == m2, lane, float(LANES)), axis=-1, keepdims=True)
    e2 = jnp.exp(m2 - m1)
    den = 1.0 + e2
    return jnp.where(lane == i1, 1.0 / den, 0.0) + jnp.where(lane == i2, e2 / den, 0.0)


def _outproj_body(x_ref, oa_ref, ob_ref, om_ref, w_ref, mod_ref, g_ref, *rest, routed):
    if routed:
        wr_ref, xo_ref, h_ref, gates_ref = rest
    else:
        xo_ref, h_ref = rest
    mod = mod_ref[0]
    wa, wb = oa_ref.shape[2], ob_ref.shape[2]
    y = jnp.dot(oa_ref[0], w_ref[0:wa, :], preferred_element_type=F32)
    y += jnp.dot(ob_ref[0], w_ref[wa:wa + wb, :], preferred_element_type=F32)
    y += jnp.dot(om_ref[0], w_ref[wa + wb:, :], preferred_element_type=F32)
    x = x_ref[0] + mod[2:3] * y
    xo_ref[0] = x
    n = x * _rms_rows(x, x.shape[-1]) * g_ref[...]
    h = n * (1.0 + mod[4:5]) + mod[3:4]
    h_ref[0] = h.astype(BF16)
    if routed:
        logits = jnp.dot(h, wr_ref[...], precision=HIGHEST, preferred_element_type=F32)
        gates_ref[0] = _router_gates(logits)


def _out_project(x, oa, ob, om, w, mod3, g, mod_row, w_router=None):
    b, s, d = x.shape
    tm = ROW_TILE
    routed = w_router is not None
    row = lambda bi, i: (bi, i, 0)
    in_specs = [pl.BlockSpec((1, tm, d), row),
                pl.BlockSpec((1, tm, oa.shape[2]), row),
                pl.BlockSpec((1, tm, ob.shape[2]), row),
                pl.BlockSpec((1, tm, om.shape[2]), row),
                pl.BlockSpec(w.shape, lambda bi, i: (0, 0)),
                pl.BlockSpec((1, N_MOD, d), lambda bi, i: (mod_row(bi), 0, 0)),
                pl.BlockSpec((1, d), lambda bi, i: (0, 0))]
    out_specs = [pl.BlockSpec((1, tm, d), row), pl.BlockSpec((1, tm, d), row)]
    out_shape = [jax.ShapeDtypeStruct((b, s, d), F32), jax.ShapeDtypeStruct((b, s, d), BF16)]
    args = [x, oa, ob, om, w, mod3, g]
    if routed:
        in_specs.append(pl.BlockSpec(w_router.shape, lambda bi, i: (0, 0)))
        out_specs.append(pl.BlockSpec((1, tm, LANES), row))
        out_shape.append(jax.ShapeDtypeStruct((b, s, LANES), F32))
        args.append(w_router)
    return pl.pallas_call(
        functools.partial(_outproj_body, routed=routed),
        grid=(b, s // tm),
        in_specs=in_specs, out_specs=out_specs, out_shape=out_shape,
        compiler_params=_params("parallel", "arbitrary"),
        name="out_project_routed" if routed else "out_project",
    )(*args)


def _swiglu_chunks(h, wg_ref, wu_ref, wo_ref, lead):
    acc = None
    for j in range(wg_ref.shape[len(lead)]):
        gate = jnp.dot(h, wg_ref[lead + (j,)], preferred_element_type=F32)
        up = jnp.dot(h, wu_ref[lead + (j,)], preferred_element_type=F32)
        a = (gate * jax.nn.sigmoid(gate) * up).astype(BF16)
        part = jnp.dot(a, wo_ref[lead + (j,)], preferred_element_type=F32)
        acc = part if acc is None else acc + part
    return acc


def _residual_out(x, gate, y, gf_ref, final_norm):
    x = x + gate * y
    if final_norm:
        x = x * _rms_rows(x, x.shape[-1]) * gf_ref[...]
    return x


def _ffn_body(x_ref, h_ref, wg_ref, wu_ref, wo_ref, mod_ref, gf_ref, o_ref, *, final_norm):
    y = _swiglu_chunks(h_ref[0], wg_ref, wu_ref, wo_ref, ())
    o_ref[0] = _residual_out(x_ref[0], mod_ref[0][5:6], y, gf_ref, final_norm)


def _dense_ffn(x, h, wg, wu, wo, mod3, mod_row, g_final, final_norm):
    b, s, d = x.shape
    tm = min(FFN_ROWS, s)
    row = lambda bi, i: (bi, i, 0)
    resident = lambda a: pl.BlockSpec(a.shape, lambda bi, i: (0, 0, 0),
                                      pipeline_mode=pl.Buffered(1))
    return pl.pallas_call(
        functools.partial(_ffn_body, final_norm=final_norm),
        grid=(b, s // tm),
        in_specs=[pl.BlockSpec((1, tm, d), row), pl.BlockSpec((1, tm, d), row),
                  resident(wg), resident(wu), resident(wo),
                  pl.BlockSpec((1, N_MOD, d), lambda bi, i: (mod_row(bi), 0, 0)),
                  pl.BlockSpec((1, d), lambda bi, i: (0, 0))],
        out_specs=pl.BlockSpec((1, tm, d), row),
        out_shape=jax.ShapeDtypeStruct((b, s, d), F32),
        compiler_params=_params("parallel", "arbitrary"),
        name="dense_ffn",
    )(x, h, wg, wu, wo, mod3, g_final)


def _moe_body(x_ref, h_ref, gates_ref, wg_ref, wu_ref, wo_ref, mod_ref, gf_ref, o_ref,
              tot_ref, *, final_norm):
    e = pl.program_id(2)

    @pl.when(e == 0)
    def _():
        tot_ref[...] = jnp.zeros(tot_ref.shape, F32)

    y = _swiglu_chunks(h_ref[0], wg_ref, wu_ref, wo_ref, (0,))
    gates = gates_ref[0]
    gate_e = jnp.sum(jnp.where(_lane_iota(gates.shape) == e, gates, 0.0), axis=-1, keepdims=True)
    tot_ref[...] += gate_e * y

    @pl.when(e == pl.num_programs(2) - 1)
    def _():
        o_ref[0] = _residual_out(x_ref[0], mod_ref[0][5:6], tot_ref[...], gf_ref, final_norm)


def _moe_ffn(x, h, gates, wg, wu, wo, mod3, mod_row, g_final, final_norm):
    b, s, d = x.shape
    tm = min(FFN_ROWS, s)
    row = lambda bi, i, e: (bi, i, 0)
    expert = lambda a: pl.BlockSpec((1,) + a.shape[1:], lambda bi, i, e: (e, 0, 0, 0))
    return pl.pallas_call(
        functools.partial(_moe_body, final_norm=final_norm),
        grid=(b, s // tm, wg.shape[0]),
        in_specs=[pl.BlockSpec((1, tm, d), row), pl.BlockSpec((1, tm, d), row),
                  pl.BlockSpec((1, tm, LANES), row),
                  expert(wg), expert(wu), expert(wo),
                  pl.BlockSpec((1, N_MOD, d), lambda bi, i, e: (mod_row(bi), 0, 0)),
                  pl.BlockSpec((1, d), lambda bi, i, e: (0, 0))],
        out_specs=pl.BlockSpec((1, tm, d), row),
        out_shape=jax.ShapeDtypeStruct((b, s, d), F32),
        scratch_shapes=[pltpu.VMEM((tm, d), F32)],
        compiler_params=_params("parallel", "parallel", "arbitrary"),
        name="moe_ffn",
    )(x, h, gates, wg, wu, wo, mod3, g_final)


def _deinterleave(n):
    return np.concatenate([np.arange(0, n, 2), np.arange(1, n, 2)])


def _in_proj_columns(pad):
    pads = lambda n: np.full((n,), pad)
    cols = []
    for hd in range(GQA_HEADS):
        cols.append(64 * hd + _deinterleave(64))
    for g in range(GQA_KV_HEADS):
        cols += [512 + 64 * g + _deinterleave(64), pads(64)]
    for g in range(GQA_KV_HEADS):
        cols += [640 + 64 * g + np.arange(64), pads(64)]
    for u in range(2 * DIFF_HEADS):
        cols.append(768 + 32 * u + _deinterleave(32))
    for u in range(2 * DIFF_HEADS):
        cols.append(1024 + 32 * u + _deinterleave(32))
    cols.append(1280 + np.arange(256))
    cols += [1536 + np.arange(MLA_Q_RANK), pads(64)]
    cols.append(1728 + np.arange(MLA_KV_RANK))
    cols += [1856 + _deinterleave(32), 1856 + _deinterleave(32), pads(64)]
    return np.concatenate(cols)


def _uq_columns(pad):
    per = MLA_NOPE_DIM + MLA_ROPE_DIM
    cols = []
    for pr in range(2):
        h0, h1 = 2 * pr, 2 * pr + 1
        cols += [per * h0 + np.arange(64), per * h1 + np.arange(64),
                 per * h0 + 64 + _deinterleave(32), per * h1 + 64 + _deinterleave(32),
                 np.full((64,), pad)]
    return np.concatenate(cols)


def _ukv_columns():
    per = MLA_NOPE_DIM + MLA_V_DIM
    k = [per * hd + np.arange(64) for hd in range(MLA_HEADS)]
    v = [per * hd + 64 + np.arange(64) for hd in range(MLA_HEADS)]
    return np.concatenate(k + v)


def _take_cols(w, cols):
    w_ext = jnp.concatenate([w, jnp.zeros((w.shape[0], 1), w.dtype)], axis=1)
    return jnp.take(w_ext, jnp.asarray(cols), axis=1)


def _rope_tables(s, c):
    t = jnp.arange(s)
    rows = (t // GRID_W).astype(F32)
    cols = (t % GRID_W).astype(F32)
    out = []
    for dim in (HEAD_DIM, DIFF_QK_DIM):
        quarter = dim // 4
        half = dim // 2
        inv_freq = ROPE_THETA ** (-jnp.arange(quarter, dtype=F32) / quarter)
        ang = jnp.concatenate([rows[:, None] * inv_freq, cols[:, None] * inv_freq], axis=-1)
        lane = np.arange(LANES)
        idx = (lane % dim) % half
        sign = np.where((lane % dim) < half, -1.0, 1.0).astype(np.float32)
        cos = jnp.cos(ang)[:, idx]
        sin = jnp.sin(ang)[:, idx] * sign
        out.append(jnp.concatenate([cos, jnp.ones((c, LANES), F32)], axis=0))
        out.append(jnp.concatenate([sin, jnp.zeros((c, LANES), F32)], axis=0))
    return out


def _chunked_in(w, n_hidden):
    d = w.shape[0]
    nc = n_hidden // FFN_CHUNK
    g = w[:, :n_hidden].reshape(d, nc, FFN_CHUNK).transpose(1, 0, 2).astype(BF16)
    u = w[:, n_hidden:].reshape(d, nc, FFN_CHUNK).transpose(1, 0, 2).astype(BF16)
    return g, u


def kernel(x, c, ctx, c_ctx, w_mod, b_mod, g_attn, g_ffn, w_in, w_out, gqa_gq, gqa_gk,
           diff_lq1, diff_lk1, diff_lq2, diff_lk2, diff_gsub, mla_gcq, mla_gckv, mla_wuq, mla_wukv,
           ffn_w_in, ffn_w_out, moe_router, moe_w_in, moe_w_out, g_final):
    b, s, d = x.shape
    n_ctx = ctx.shape[1]
    depth = w_mod.shape[0]
    t_all = n_ctx + s
    tables = _rope_tables(s, n_ctx)
    in_cols = _in_proj_columns(w_in.shape[2])
    uq_cols = _uq_columns(mla_wuq.shape[2])
    ukv_cols = _ukv_columns()
    perm64 = _deinterleave(HEAD_DIM)
    bd = jnp.asarray(np.kron(np.eye(LANES // HEAD_DIM), np.full((HEAD_DIM, HEAD_DIM), 1.0 / HEAD_DIM)), F32)

    mod_rows = 16
    c_all = jnp.zeros((mod_rows, d), F32).at[:b].set(c).at[b].set(c_ctx)
    pad_lanes = lambda v: jnp.zeros((1, LANES), F32).at[0, :v.shape[0]].set(v)

    xc = ctx
    for l in range(depth):
        need_ctx = l < depth - 1
        lam_init = 0.8 - 0.6 * math.exp(-0.3 * l)
        mod3 = _modulation(c_all, w_mod[l], b_mod[l]).reshape(mod_rows, N_MOD, d)

        w_in_p = _take_cols(w_in[l], in_cols).astype(BF16)
        wuq_p = jnp.zeros((2 * LANES, 512), F32).at[:MLA_Q_RANK].set(
            _take_cols(mla_wuq[l], uq_cols)).astype(BF16)
        wukv_p = jnp.take(mla_wukv[l], jnp.asarray(ukv_cols), axis=1).astype(BF16)
        gq2 = jnp.tile(gqa_gq[l][perm64], 2).reshape(1, LANES)
        gk2 = jnp.tile(gqa_gk[l][perm64], 2).reshape(1, LANES)
        gcq = jnp.zeros((1, 2 * LANES), F32).at[0, :MLA_Q_RANK].set(mla_gcq[l])
        gckv = mla_gckv[l].reshape(1, MLA_KV_RANK)

        qa, ka, va, qb, kb, vb, qm, km, vm = _project(
            x, xc, mod3, g_attn[l].reshape(1, d), w_in_p, tables, gq2, gk2, bd, gcq, gckv, wuq_p, wukv_p)

        diff_extra = [pad_lanes(diff_lq1[l]), pad_lanes(diff_lk1[l]), pad_lanes(diff_lq2[l]),
                      pad_lanes(diff_lk2[l]), jnp.tile(diff_gsub[l], 2).reshape(1, LANES)]
        diff_body = functools.partial(_diff_body, lam_init=lam_init)
        diff_body.__name__ = "_diff_body"

        def attend(q_rows, q_row0, key0, n_keys, tq_mla):
            kw = dict(key0=key0, n_keys=n_keys)
            tq = ROW_TILE
            oa = _attention(_gqa_body, qa, ka, va, [], n_groups=2, wq=256, wk=LANES, wv=LANES, n_s=4,
                            tq=tq, q_tiles=q_rows // tq, q_tile0=q_row0 // tq, **kw)
            ob = _attention(diff_body, qb, kb, vb, diff_extra, n_groups=2, wq=LANES, wk=LANES, wv=LANES,
                            n_s=4, tq=tq, q_tiles=q_rows // tq, q_tile0=q_row0 // tq, **kw)
            om = _attention(_mla_body, qm, km, vm, [], n_groups=2, wq=256, wk=256, wv=LANES, n_s=2,
                            tq=tq_mla, q_tiles=q_rows // tq_mla, q_tile0=q_row0 // tq_mla, **kw)
            return oa, ob, om

        w_out_b = w_out[l].astype(BF16)
        g2 = g_ffn[l].reshape(1, d)
        dense = l % 2 == 0
        if dense:
            wg, wu = _chunked_in(ffn_w_in[l // 2], ffn_w_out.shape[1])
            wo = ffn_w_out[l // 2].reshape(-1, FFN_CHUNK, d).astype(BF16)
            w_router = None
        else:
            n_hidden = moe_w_out.shape[2]
            wi = moe_w_in[l // 2]
            wg = wi[:, :, :n_hidden].reshape(N_EXPERTS, d, -1, FFN_CHUNK).transpose(0, 2, 1, 3).astype(BF16)
            wu = wi[:, :, n_hidden:].reshape(N_EXPERTS, d, -1, FFN_CHUNK).transpose(0, 2, 1, 3).astype(BF16)
            wo = moe_w_out[l // 2].reshape(N_EXPERTS, -1, FFN_CHUNK, d).astype(BF16)
            w_router = jnp.zeros((d, LANES), F32).at[:, :N_EXPERTS].set(moe_router[l // 2])
        last = l == depth - 1

        def channel_mix(xs, attn_out, mod_row, final_norm):
            res = _out_project(xs, *attn_out, w_out_b, mod3, g2, mod_row, w_router)
            gf = g_final.reshape(1, d)
            if dense:
                return _dense_ffn(res[0], res[1], wg, wu, wo, mod3, mod_row, gf, final_norm)
            return _moe_ffn(res[0], res[1], res[2], wg, wu, wo, mod3, mod_row, gf, final_norm)

        x_new = channel_mix(x, attend(s, 0, 0, t_all, 2 * ROW_TILE), lambda bi: bi, last)
        if need_ctx:
            xc = channel_mix(xc, attend(n_ctx, s, s, n_ctx, ROW_TILE), lambda bi: b, False)
        x = x_new
    return x
```

```python
import functools
import math

import numpy as np
import jax
import jax.numpy as jnp
from jax import lax
from jax.experimental import pallas as pl
from jax.experimental.pallas import tpu as pltpu

LANES = 128
MXU_TILE = 256
VMEM_LIMIT = 56 * 1024 * 1024

NORM_EPS = 1e-6
ROPE_THETA = 10000.0
GRID_W = 64
N_MOD = 6
HEAD_DIM = 64
GQA_HEADS, GQA_KV_HEADS = 8, 2
DIFF_HEADS, DIFF_QK_DIM, DIFF_V_DIM = 4, 32, 64
MLA_HEADS, MLA_Q_RANK, MLA_KV_RANK = 4, 192, 128
MLA_NOPE_DIM, MLA_ROPE_DIM, MLA_V_DIM = 64, 32, 64
N_EXPERTS, TOP_K = 8, 2
GATE_ROWS = 16

ROW_TILE = 256
KEY_CHUNK = MXU_TILE
FFN_CHUNK = MXU_TILE
FFN_ROWS = 512
MOE_SLOTS = 256

F32 = jnp.float32
BF16 = jnp.bfloat16
HIGHEST = lax.Precision.HIGHEST


def _params(*sem):
    return pltpu.CompilerParams(dimension_semantics=sem, vmem_limit_bytes=VMEM_LIMIT)


def _lane_iota(shape):
    return lax.broadcasted_iota(jnp.int32, shape, len(shape) - 1)


def _mod_body(c_ref, w_ref, b_ref, o_ref):
    c = c_ref[...]
    sc = c * jax.nn.sigmoid(c)
    o_ref[...] = jnp.dot(sc, w_ref[...], precision=HIGHEST,
                         preferred_element_type=F32) + b_ref[...]


def _modulation(c_all, w, b):
    rows, d = c_all.shape
    n = w.shape[1]
    tn = 1536
    return pl.pallas_call(
        _mod_body,
        grid=(n // tn,),
        in_specs=[pl.BlockSpec((rows, d), lambda j: (0, 0)),
                  pl.BlockSpec((d, tn), lambda j: (0, j)),
                  pl.BlockSpec((1, tn), lambda j: (0, j))],
        out_specs=pl.BlockSpec((rows, tn), lambda j: (0, j)),
        out_shape=jax.ShapeDtypeStruct((rows, n), F32),
        compiler_params=_params("arbitrary"),
        name="modulation",
    )(c_all, w, b.reshape(1, n))


def _rms_rows(x, width):
    return lax.rsqrt(jnp.sum(x * x, axis=-1, keepdims=True) * (1.0 / width) + NORM_EPS)


def _rope(t, cos, sin_signed, half):
    lane = _lane_iota(t.shape)
    partner = jnp.where((lane & half) == 0,
                        pltpu.roll(t, LANES - half, 1), pltpu.roll(t, half, 1))
    return t * cos + partner * sin_signed


def _with_ones(t):
    return jnp.where(_lane_iota(t.shape) == HEAD_DIM, 1.0, t)


def _proj_body(x_ref, xc_ref, mod_ref, g_ref, w_ref, ca_ref, sa_ref, cb_ref, sb_ref,
               gq_ref, gk_ref, bd_ref, gcq_ref, gckv_ref, wuq_ref, wukv_ref,
               qa_ref, ka_ref, va_ref, qb_ref, kb_ref, vb_ref, qm_ref, km_ref, vm_ref,
               *, scale_a, scale_b, scale_m):
    i = pl.program_id(1)
    is_ctx = i == pl.num_programs(1) - 1
    x = jnp.where(is_ctx, xc_ref[0], x_ref[0])
    d = x.shape[-1]
    mod = mod_ref[0]
    y = x * _rms_rows(x, d) * g_ref[...]
    h = (y * (1.0 + mod[1:2]) + mod[0:1]).astype(BF16)
    p = jnp.dot(h, w_ref[...], preferred_element_type=F32)

    ca, sa, cb, sb = ca_ref[...], sa_ref[...], cb_ref[...], sb_ref[...]
    bd = bd_ref[...]
    tile = lambda ref, j: (0, slice(None), slice(j * LANES, (j + 1) * LANES))
    cols = lambda a, base, j: a[:, base + j * LANES:base + (j + 1) * LANES]

    def head_norm(t, g):
        ms = jnp.dot(t * t, bd, precision=HIGHEST, preferred_element_type=F32)
        return t * lax.rsqrt(ms + NORM_EPS) * g

    for j in range(4):
        t = head_norm(cols(p, 0, j), gq_ref[...])
        qa_ref[tile(qa_ref, j)] = (_rope(t, ca, sa, 32) * scale_a).astype(BF16)
    for j in range(2):
        t = head_norm(cols(p, 512, j), gk_ref[...])
        ka_ref[tile(ka_ref, j)] = _rope(t, ca, sa, 32).astype(BF16)
        va_ref[tile(va_ref, j)] = _with_ones(cols(p, 768, j)).astype(BF16)

    for j in range(2):
        qb_ref[tile(qb_ref, j)] = (_rope(cols(p, 1024, j), cb, sb, 16) * scale_b).astype(BF16)
        kb_ref[tile(kb_ref, j)] = _rope(cols(p, 1280, j), cb, sb, 16).astype(BF16)
    vb_ref[0] = p[:, 1536:1792].astype(BF16)

    cq = p[:, 1792:2048]
    cqn = (cq * _rms_rows(cq, MLA_Q_RANK) * gcq_ref[...]).astype(BF16)
    uq = jnp.dot(cqn, wuq_ref[...], preferred_element_type=F32)
    ckv = p[:, 2048:2176]
    ckvn = (ckv * _rms_rows(ckv, MLA_KV_RANK) * gckv_ref[...]).astype(BF16)
    ukv = jnp.dot(ckvn, wukv_ref[...], preferred_element_type=F32)
    kr = _rope(p[:, 2176:2304], cb, sb, 16).astype(BF16)
    for pr in range(2):
        qm_ref[tile(qm_ref, 2 * pr)] = (cols(uq, 0, 2 * pr) * scale_m).astype(BF16)
        qr = _rope(cols(uq, 0, 2 * pr + 1), cb, sb, 16)
        qm_ref[tile(qm_ref, 2 * pr + 1)] = (qr * scale_m).astype(BF16)
        km_ref[tile(km_ref, 2 * pr)] = cols(ukv, 0, pr).astype(BF16)
        km_ref[tile(km_ref, 2 * pr + 1)] = kr
    vm_ref[0] = ukv[:, 256:512].astype(BF16)


def _project(x, xc, mod3, g, w, tables, gq2, gk2, bd, gcq, gckv, wuq, wukv):
    b, s, d = x.shape
    c = xc.shape[1]
    t = c + s
    tm = ROW_TILE
    assert c == tm and s % tm == 0
    nt = t // tm
    ctx_row = b
    widths = (512, 256, 256, 256, 256, 256, 512, 512, 256)
    log2e = math.log2(math.e)
    body = functools.partial(_proj_body, scale_a=HEAD_DIM ** -0.5 * log2e,
                             scale_b=DIFF_QK_DIM ** -0.5 * log2e,
                             scale_m=(MLA_NOPE_DIM + MLA_ROPE_DIM) ** -0.5 * log2e)
    const = lambda shape: pl.BlockSpec(shape, lambda bi, i: (0,) * len(shape))
    tab = pl.BlockSpec((tm, LANES), lambda bi, i: (i, 0))
    return pl.pallas_call(
        body,
        grid=(b, nt),
        in_specs=[pl.BlockSpec((1, tm, d), lambda bi, i: (bi, jnp.minimum(i, nt - 2), 0)),
                  pl.BlockSpec((1, tm, d), lambda bi, i: (bi, 0, 0)),
                  pl.BlockSpec((1, N_MOD, d), lambda bi, i: (jnp.where(i == nt - 1, ctx_row, bi), 0, 0)),
                  const((1, d)), const(w.shape), tab, tab, tab, tab,
                  const((1, LANES)), const((1, LANES)), const((LANES, LANES)),
                  const(gcq.shape), const(gckv.shape), const(wuq.shape), const(wukv.shape)],
        out_specs=[pl.BlockSpec((1, tm, wd), lambda bi, i: (bi, i, 0)) for wd in widths],
        out_shape=[jax.ShapeDtypeStruct((b, t, wd), BF16) for wd in widths],
        compiler_params=_params("parallel", "arbitrary"),
        name="project",
    )(x, xc, mod3, g, w, *tables, gq2, gk2, bd, gcq, gckv, wuq, wukv)


def _attn_step(qs, k_ref, v_ref, s_ref, mp_ref, mb_ref, acc_ref, l_ref, n_chunks, ones_lane):
    step = pl.program_id(0)

    @pl.when(step == 0)
    def _():
        s_ref[...] = jnp.zeros(s_ref.shape, F32)
        mp_ref[...] = jnp.zeros(mp_ref.shape, F32)
        acc_ref[...] = jnp.ones(acc_ref.shape, F32)
        l_ref[...] = jnp.ones(l_ref.shape, F32)

    raw = acc_ref[...]
    if ones_lane:
        row_sum = raw[:, HEAD_DIM:HEAD_DIM + 1]
    else:
        row_sum = jnp.sum(l_ref[...], axis=-1, keepdims=True)
    done = raw * (1.0 / row_sum)

    cur = step % 2
    mb_ref[...] = jnp.broadcast_to(jnp.max(mp_ref[1 - cur], axis=-1, keepdims=True), mb_ref.shape)
    acc = None
    l_acc = None
    for c in range(n_chunks):
        keys = slice(c * KEY_CHUNK, (c + 1) * KEY_CHUNK)
        s_new = lax.dot_general(qs, k_ref[0, keys, :], (((1,), (1,)), ((), ())),
                                preferred_element_type=F32)
        s_old = s_ref[c]
        m_old = mb_ref[...]
        p0 = jnp.exp2(s_old[:, :LANES] - m_old)
        p1 = jnp.exp2(s_old[:, LANES:] - m_old)
        if not ones_lane:
            l_acc = p0 + p1 if l_acc is None else l_acc + (p0 + p1)
        part = jnp.dot(jnp.concatenate([p0, p1], axis=1).astype(BF16), v_ref[0, keys, :],
                       preferred_element_type=F32)
        acc = part if acc is None else acc + part
        s_ref[c] = s_new
        mc = jnp.maximum(s_new[:, :LANES], s_new[:, LANES:])
        mp_ref[cur] = mc if c == 0 else jnp.maximum(mp_ref[cur], mc)
    acc_ref[...] = acc
    if not ones_lane:
        l_ref[...] = l_acc
    return done


def _pack_heads(even, odd):
    lo = _lane_iota(even.shape) < HEAD_DIM
    return jnp.where(lo, even, pltpu.roll(odd, HEAD_DIM, 1))


def _gqa_body(q_ref, k_ref, v_ref, o_ref, *scratch, n_chunks):
    tq = q_ref.shape[1]
    lo = _lane_iota((tq, LANES)) < HEAD_DIM
    heads = []
    for j in range(2):
        t = q_ref[0, :, j * LANES:(j + 1) * LANES].astype(F32)
        heads.append(jnp.where(lo, t, 0.0))
        heads.append(jnp.where(lo, pltpu.roll(t, HEAD_DIM, 1), 0.0))
    qs = jnp.concatenate(heads, axis=0).astype(BF16)
    o = _attn_step(qs, k_ref, v_ref, *scratch, n_chunks, True)
    for j in range(2):
        o_ref[0, :, j * LANES:(j + 1) * LANES] = _pack_heads(
            o[(2 * j) * tq:(2 * j + 1) * tq], o[(2 * j + 1) * tq:(2 * j + 2) * tq]).astype(o_ref.dtype)


def _diff_body(q_ref, k_ref, v_ref, lq1_ref, lk1_ref, lq2_ref, lk2_ref, gsub_ref, o_ref,
               *scratch, n_chunks, lam_init):
    tq = q_ref.shape[1]
    lane = _lane_iota((tq, LANES))
    t = q_ref[0]
    zero = jnp.zeros_like(t)
    qs = jnp.concatenate([jnp.where((lane // DIFF_QK_DIM) == j, t, zero) for j in range(4)], axis=0)
    o = _attn_step(qs, k_ref, v_ref, *scratch, n_chunks, False)
    lam = (jnp.exp(jnp.sum(lq1_ref[...] * lk1_ref[...], axis=-1, keepdims=True))
           - jnp.exp(jnp.sum(lq2_ref[...] * lk2_ref[...], axis=-1, keepdims=True)) + lam_init)
    lo = lane < DIFF_V_DIM
    d = jnp.where(lo, o[0:tq] - lam * o[tq:2 * tq], o[2 * tq:3 * tq] - lam * o[3 * tq:4 * tq])
    sq = d * d
    ms_lo = jnp.sum(jnp.where(lo, sq, 0.0), axis=-1, keepdims=True) * (1.0 / DIFF_V_DIM)
    ms_hi = jnp.sum(jnp.where(lo, 0.0, sq), axis=-1, keepdims=True) * (1.0 / DIFF_V_DIM)
    r = jnp.where(lo, lax.rsqrt(ms_lo + NORM_EPS), lax.rsqrt(ms_hi + NORM_EPS))
    o_ref[0] = ((d * r * gsub_ref[...]) * (1.0 - lam_init)).astype(o_ref.dtype)


def _mla_body(q_ref, k_ref, v_ref, o_ref, *scratch, n_chunks):
    tq = q_ref.shape[1]
    wq = q_ref.shape[2]
    lane = _lane_iota((tq, wq))
    t = q_ref[0]
    zero = jnp.zeros_like(t)
    first = (lane < MLA_NOPE_DIM) | ((lane >= LANES) & (lane < LANES + MLA_ROPE_DIM))
    second = ((lane >= MLA_NOPE_DIM) & (lane < LANES)) | (
        (lane >= LANES + MLA_ROPE_DIM) & (lane < LANES + 2 * MLA_ROPE_DIM))
    qs = jnp.concatenate([jnp.where(first, t, zero), jnp.where(second, t, zero)], axis=0)
    o = _attn_step(qs, k_ref, v_ref, *scratch, n_chunks, False)
    lo = _lane_iota((tq, LANES)) < MLA_V_DIM
    o_ref[0] = jnp.where(lo, o[0:tq], o[tq:2 * tq]).astype(o_ref.dtype)


def _attention(body, q, k, v, extra, *, n_groups, wq, wk, wv, n_s, tq, q_tiles, q_tile0, key0, n_keys):
    b = q.shape[0]
    wo = wq if body is _gqa_body else LANES
    n_chunks = n_keys // KEY_CHUNK
    key_blk = key0 // n_keys
    assert key_blk * n_keys == key0
    m_rows = n_s * tq
    n_tiles = b * n_groups * q_tiles

    def split(t):
        return t // (n_groups * q_tiles), (t // q_tiles) % n_groups, t % q_tiles

    cur = lambda j: split(jnp.minimum(j, n_tiles - 1))
    prev = lambda j: split(jnp.clip(j - 1, 0, n_tiles - 1))
    done = lambda j: split(jnp.maximum(j - 2, 0))
    extra_specs = [pl.BlockSpec(e.shape, lambda j: (0, 0)) for e in extra]
    return pl.pallas_call(
        functools.partial(body, n_chunks=n_chunks),
        grid=(n_tiles + 2,),
        in_specs=[pl.BlockSpec((1, tq, wq), lambda j: (cur(j)[0], cur(j)[2] + q_tile0, cur(j)[1])),
                  pl.BlockSpec((1, n_keys, wk), lambda j: (cur(j)[0], key_blk, cur(j)[1])),
                  pl.BlockSpec((1, n_keys, wv), lambda j: (prev(j)[0], key_blk, prev(j)[1]))] + extra_specs,
        out_specs=pl.BlockSpec((1, tq, wo), lambda j: (done(j)[0], done(j)[2], done(j)[1])),
        out_shape=jax.ShapeDtypeStruct((b, q_tiles * tq, n_groups * wo), BF16),
        scratch_shapes=[pltpu.VMEM((n_chunks, m_rows, KEY_CHUNK), F32),
                        pltpu.VMEM((2, m_rows, LANES), F32),
                        pltpu.VMEM((m_rows, LANES), F32),
                        pltpu.VMEM((m_rows, LANES), F32),
                        pltpu.VMEM((m_rows, LANES), F32)],
        compiler_params=_params("arbitrary"),
        name=body.__name__.strip("_"),
    )(q, k, v, *extra)


def _router_gates(logits):
    lane = _lane_iota(logits.shape).astype(F32)
    neg = jnp.float32(-jnp.inf)
    z = jnp.where(lane < N_EXPERTS, logits, neg)
    m1 = jnp.max(z, axis=-1, keepdims=True)
    i1 = jnp.min(jnp.where(z == m1, lane, float(LANES)), axis=-1, keepdims=True)
    z2 = jnp.where(lane == i1, neg, z)
    m2 = jnp.max(z2, axis=-1, keepdims=True)
    i2 = jnp.min(jnp.where(z2 == m2, lane, float(LANES)), axis=-1, keepdims=True)
    e2 = jnp.exp(m2 - m1)
    den = 1.0 + e2
    return jnp.where(lane == i1, 1.0 / den, 0.0) + jnp.where(lane == i2, e2 / den, 0.0)


def _outproj_body(x_ref, oa_ref, ob_ref, om_ref, w_ref, mod_ref, g_ref, *rest, routed):
    if routed:
        wr_ref, xo_ref, h_ref, gates_ref, gates_t_ref = rest
    else:
        xo_ref, h_ref = rest
    mod = mod_ref[0]
    wa, wb = oa_ref.shape[2], ob_ref.shape[2]
    y = jnp.dot(oa_ref[0], w_ref[0:wa, :], preferred_element_type=F32)
    y += jnp.dot(ob_ref[0], w_ref[wa:wa + wb, :], preferred_element_type=F32)
    y += jnp.dot(om_ref[0], w_ref[wa + wb:, :], preferred_element_type=F32)
    x = x_ref[0] + mod[2:3] * y
    xo_ref[0] = x
    n = x * _rms_rows(x, x.shape[-1]) * g_ref[...]
    h = n * (1.0 + mod[4:5]) + mod[3:4]
    h_ref[0] = h.astype(BF16)
    if routed:
        logits = jnp.dot(h, wr_ref[...], precision=HIGHEST, preferred_element_type=F32)
        gates = _router_gates(logits)
        gates_ref[0] = gates
        gates_t_ref[0] = gates.T[:gates_t_ref.shape[1]]


def _out_project(x, oa, ob, om, w, mod3, g, mod_row, w_router=None):
    b, s, d = x.shape
    tm = ROW_TILE
    routed = w_router is not None
    row = lambda bi, i: (bi, i, 0)
    in_specs = [pl.BlockSpec((1, tm, d), row),
                pl.BlockSpec((1, tm, oa.shape[2]), row),
                pl.BlockSpec((1, tm, ob.shape[2]), row),
                pl.BlockSpec((1, tm, om.shape[2]), row),
                pl.BlockSpec(w.shape, lambda bi, i: (0, 0)),
                pl.BlockSpec((1, N_MOD, d), lambda bi, i: (mod_row(bi), 0, 0)),
                pl.BlockSpec((1, d), lambda bi, i: (0, 0))]
    out_specs = [pl.BlockSpec((1, tm, d), row), pl.BlockSpec((1, tm, d), row)]
    out_shape = [jax.ShapeDtypeStruct((b, s, d), F32), jax.ShapeDtypeStruct((b, s, d), BF16)]
    args = [x, oa, ob, om, w, mod3, g]
    if routed:
        in_specs.append(pl.BlockSpec(w_router.shape, lambda bi, i: (0, 0)))
        out_specs.append(pl.BlockSpec((1, tm, LANES), row))
        out_shape.append(jax.ShapeDtypeStruct((b, s, LANES), F32))
        out_specs.append(pl.BlockSpec((1, GATE_ROWS, tm), lambda bi, i: (bi, 0, i)))
        out_shape.append(jax.ShapeDtypeStruct((b, GATE_ROWS, s), F32))
        args.append(w_router)
    return pl.pallas_call(
        functools.partial(_outproj_body, routed=routed),
        grid=(b, s // tm),
        in_specs=in_specs, out_specs=out_specs, out_shape=out_shape,
        compiler_params=_params("parallel", "arbitrary"),
        name="out_project_routed" if routed else "out_project",
    )(*args)


def _swiglu_chunks(h, wg_ref, wu_ref, wo_ref, lead):
    acc = None
    for j in range(wg_ref.shape[len(lead)]):
        gate = jnp.dot(h, wg_ref[lead + (j,)], preferred_element_type=F32)
        up = jnp.dot(h, wu_ref[lead + (j,)], preferred_element_type=F32)
        a = (gate * jax.nn.sigmoid(gate) * up).astype(BF16)
        part = jnp.dot(a, wo_ref[lead + (j,)], preferred_element_type=F32)
        acc = part if acc is None else acc + part
    return acc


def _residual_out(x, gate, y, gf_ref, final_norm):
    x = x + gate * y
    if final_norm:
        x = x * _rms_rows(x, x.shape[-1]) * gf_ref[...]
    return x


def _ffn_body(x_ref, h_ref, wg_ref, wu_ref, wo_ref, mod_ref, gf_ref, o_ref, *, final_norm):
    y = _swiglu_chunks(h_ref[0], wg_ref, wu_ref, wo_ref, ())
    o_ref[0] = _residual_out(x_ref[0], mod_ref[0][5:6], y, gf_ref, final_norm)


def _dense_ffn(x, h, wg, wu, wo, mod3, mod_row, g_final, final_norm):
    b, s, d = x.shape
    tm = min(FFN_ROWS, s)
    row = lambda bi, i: (bi, i, 0)
    resident = lambda a: pl.BlockSpec(a.shape, lambda bi, i: (0, 0, 0),
                                      pipeline_mode=pl.Buffered(1))
    return pl.pallas_call(
        functools.partial(_ffn_body, final_norm=final_norm),
        grid=(b, s // tm),
        in_specs=[pl.BlockSpec((1, tm, d), row), pl.BlockSpec((1, tm, d), row),
                  resident(wg), resident(wu), resident(wo),
                  pl.BlockSpec((1, N_MOD, d), lambda bi, i: (mod_row(bi), 0, 0)),
                  pl.BlockSpec((1, d), lambda bi, i: (0, 0))],
        out_specs=pl.BlockSpec((1, tm, d), row),
        out_shape=jax.ShapeDtypeStruct((b, s, d), F32),
        compiler_params=_params("parallel", "arbitrary"),
        name="dense_ffn",
    )(x, h, wg, wu, wo, mod3, g_final)


def _moe_body(x_ref, h_ref, gates_ref, gates_t_ref, tri_ref, tri_t_ref, wg_ref, wu_ref, wo_ref,
              mod_ref, gf_ref, o_ref, tot_ref, rank_ref, rank_t_ref, *, final_norm):
    e = pl.program_id(2)
    tm = h_ref.shape[1]
    gates = gates_ref[0]
    lane = _lane_iota(gates.shape)

    @pl.when(e == 0)
    def _():
        tot_ref[...] = jnp.zeros(tot_ref.shape, F32)
        rank_ref[...] = jnp.dot(tri_ref[...], (gates > 0.0).astype(BF16), preferred_element_type=F32)
        rank_t_ref[...] = jnp.dot((gates_t_ref[0] > 0.0).astype(BF16), tri_t_ref[...],
                                  preferred_element_type=F32)

    pick = lambda a: jnp.sum(jnp.where(lane == e, a, 0.0), axis=-1, keepdims=True)
    gate_e = pick(gates)
    rank_e = pick(rank_ref[...])
    h = h_ref[0]

    def compact():
        slots = MOE_SLOTS
        rank_row = rank_t_ref[pl.ds(e, 1), :]
        live_row = gates_t_ref[0, pl.ds(e, 1), :] > 0.0
        slot_sub = lax.broadcasted_iota(jnp.int32, (slots, tm), 0).astype(F32) + 1.0
        gather = jnp.where((rank_row == slot_sub) & live_row, 1.0, 0.0).astype(BF16)
        rows = jnp.dot(gather, h, preferred_element_type=F32).astype(BF16)
        y = _swiglu_chunks(rows, wg_ref, wu_ref, wo_ref, (0,))
        y_hi = y.astype(BF16)
        y_lo = (y - y_hi.astype(F32)).astype(BF16)
        slot_lane = _lane_iota((tm, slots)).astype(F32) + 1.0
        scatter = jnp.where((rank_e == slot_lane) & (gate_e > 0.0), 1.0, 0.0).astype(BF16)
        return (jnp.dot(scatter, y_hi, preferred_element_type=F32)
                + jnp.dot(scatter, y_lo, preferred_element_type=F32))

    def dense():
        return _swiglu_chunks(h, wg_ref, wu_ref, wo_ref, (0,))

    y = lax.cond(jnp.max(rank_e) <= float(MOE_SLOTS), compact, dense)
    tot_ref[...] += gate_e * y

    @pl.when(e == pl.num_programs(2) - 1)
    def _():
        o_ref[0] = _residual_out(x_ref[0], mod_ref[0][5:6], tot_ref[...], gf_ref, final_norm)


def _moe_ffn(x, h, gates, gates_t, wg, wu, wo, mod3, mod_row, g_final, final_norm):
    b, s, d = x.shape
    tm = min(FFN_ROWS, s)
    tri = jnp.asarray(np.tril(np.ones((tm, tm), np.float32)), BF16)
    row = lambda bi, i, e: (bi, i, 0)
    const = lambda bi, i, e: (0, 0)
    expert = lambda a: pl.BlockSpec((1,) + a.shape[1:], lambda bi, i, e: (e, 0, 0, 0))
    return pl.pallas_call(
        functools.partial(_moe_body, final_norm=final_norm),
        grid=(b, s // tm, wg.shape[0]),
        in_specs=[pl.BlockSpec((1, tm, d), row), pl.BlockSpec((1, tm, d), row),
                  pl.BlockSpec((1, tm, LANES), row),
                  pl.BlockSpec((1, GATE_ROWS, tm), lambda bi, i, e: (bi, 0, i)),
                  pl.BlockSpec((tm, tm), const), pl.BlockSpec((tm, tm), const),
                  expert(wg), expert(wu), expert(wo),
                  pl.BlockSpec((1, N_MOD, d), lambda bi, i, e: (mod_row(bi), 0, 0)),
                  pl.BlockSpec((1, d), const)],
        out_specs=pl.BlockSpec((1, tm, d), row),
        out_shape=jax.ShapeDtypeStruct((b, s, d), F32),
        scratch_shapes=[pltpu.VMEM((tm, d), F32), pltpu.VMEM((tm, LANES), F32),
                        pltpu.VMEM((GATE_ROWS, tm), F32)],
        compiler_params=_params("parallel", "parallel", "arbitrary"),
        name="moe_ffn",
    )(x, h, gates, gates_t, tri, tri.T, wg, wu, wo, mod3, g_final)


def _deinterleave(n):
    return np.concatenate([np.arange(0, n, 2), np.arange(1, n, 2)])


def _in_proj_columns(pad):
    pads = lambda n: np.full((n,), pad)
    cols = []
    for hd in range(GQA_HEADS):
        cols.append(64 * hd + _deinterleave(64))
    for g in range(GQA_KV_HEADS):
        cols += [512 + 64 * g + _deinterleave(64), pads(64)]
    for g in range(GQA_KV_HEADS):
        cols += [640 + 64 * g + np.arange(64), pads(64)]
    for u in range(2 * DIFF_HEADS):
        cols.append(768 + 32 * u + _deinterleave(32))
    for u in range(2 * DIFF_HEADS):
        cols.append(1024 + 32 * u + _deinterleave(32))
    cols.append(1280 + np.arange(256))
    cols += [1536 + np.arange(MLA_Q_RANK), pads(64)]
    cols.append(1728 + np.arange(MLA_KV_RANK))
    cols += [1856 + _deinterleave(32), 1856 + _deinterleave(32), pads(64)]
    return np.concatenate(cols)


def _uq_columns(pad):
    per = MLA_NOPE_DIM + MLA_ROPE_DIM
    cols = []
    for pr in range(2):
        h0, h1 = 2 * pr, 2 * pr + 1
        cols += [per * h0 + np.arange(64), per * h1 + np.arange(64),
                 per * h0 + 64 + _deinterleave(32), per * h1 + 64 + _deinterleave(32),
                 np.full((64,), pad)]
    return np.concatenate(cols)


def _ukv_columns():
    per = MLA_NOPE_DIM + MLA_V_DIM
    k = [per * hd + np.arange(64) for hd in range(MLA_HEADS)]
    v = [per * hd + 64 + np.arange(64) for hd in range(MLA_HEADS)]
    return np.concatenate(k + v)


def _take_cols(w, cols):
    w_ext = jnp.concatenate([w, jnp.zeros((w.shape[0], 1), w.dtype)], axis=1)
    return jnp.take(w_ext, jnp.asarray(cols), axis=1)


def _rope_tables(s, c):
    t = jnp.arange(s)
    rows = (t // GRID_W).astype(F32)
    cols = (t % GRID_W).astype(F32)
    out = []
    for dim in (HEAD_DIM, DIFF_QK_DIM):
        quarter = dim // 4
        half = dim // 2
        inv_freq = ROPE_THETA ** (-jnp.arange(quarter, dtype=F32) / quarter)
        ang = jnp.concatenate([rows[:, None] * inv_freq, cols[:, None] * inv_freq], axis=-1)
        lane = np.arange(LANES)
        idx = (lane % dim) % half
        sign = np.where((lane % dim) < half, -1.0, 1.0).astype(np.float32)
        cos = jnp.cos(ang)[:, idx]
        sin = jnp.sin(ang)[:, idx] * sign
        out.append(jnp.concatenate([cos, jnp.ones((c, LANES), F32)], axis=0))
        out.append(jnp.concatenate([sin, jnp.zeros((c, LANES), F32)], axis=0))
    return out


def _chunked_in(w, n_hidden):
    d = w.shape[0]
    nc = n_hidden // FFN_CHUNK
    g = w[:, :n_hidden].reshape(d, nc, FFN_CHUNK).transpose(1, 0, 2).astype(BF16)
    u = w[:, n_hidden:].reshape(d, nc, FFN_CHUNK).transpose(1, 0, 2).astype(BF16)
    return g, u


def kernel(x, c, ctx, c_ctx, w_mod, b_mod, g_attn, g_ffn, w_in, w_out, gqa_gq, gqa_gk,
           diff_lq1, diff_lk1, diff_lq2, diff_lk2, diff_gsub, mla_gcq, mla_gckv, mla_wuq, mla_wukv,
           ffn_w_in, ffn_w_out, moe_router, moe_w_in, moe_w_out, g_final):
    b, s, d = x.shape
    n_ctx = ctx.shape[1]
    depth = w_mod.shape[0]
    t_all = n_ctx + s
    tables = _rope_tables(s, n_ctx)
    in_cols = _in_proj_columns(w_in.shape[2])
    uq_cols = _uq_columns(mla_wuq.shape[2])
    ukv_cols = _ukv_columns()
    perm64 = _deinterleave(HEAD_DIM)
    bd = jnp.asarray(np.kron(np.eye(LANES // HEAD_DIM), np.full((HEAD_DIM, HEAD_DIM), 1.0 / HEAD_DIM)), F32)

    mod_rows = 16
    c_all = jnp.zeros((mod_rows, d), F32).at[:b].set(c).at[b].set(c_ctx)
    pad_lanes = lambda v: jnp.zeros((1, LANES), F32).at[0, :v.shape[0]].set(v)

    xc = ctx
    for l in range(depth):
        need_ctx = l < depth - 1
        lam_init = 0.8 - 0.6 * math.exp(-0.3 * l)
        mod3 = _modulation(c_all, w_mod[l], b_mod[l]).reshape(mod_rows, N_MOD, d)

        w_in_p = _take_cols(w_in[l], in_cols).astype(BF16)
        wuq_p = jnp.zeros((2 * LANES, 512), F32).at[:MLA_Q_RANK].set(
            _take_cols(mla_wuq[l], uq_cols)).astype(BF16)
        wukv_p = jnp.take(mla_wukv[l], jnp.asarray(ukv_cols), axis=1).astype(BF16)
        gq2 = jnp.tile(gqa_gq[l][perm64], 2).reshape(1, LANES)
        gk2 = jnp.tile(gqa_gk[l][perm64], 2).reshape(1, LANES)
        gcq = jnp.zeros((1, 2 * LANES), F32).at[0, :MLA_Q_RANK].set(mla_gcq[l])
        gckv = mla_gckv[l].reshape(1, MLA_KV_RANK)

        qa, ka, va, qb, kb, vb, qm, km, vm = _project(
            x, xc, mod3, g_attn[l].reshape(1, d), w_in_p, tables, gq2, gk2, bd, gcq, gckv, wuq_p, wukv_p)

        diff_extra = [pad_lanes(diff_lq1[l]), pad_lanes(diff_lk1[l]), pad_lanes(diff_lq2[l]),
                      pad_lanes(diff_lk2[l]), jnp.tile(diff_gsub[l], 2).reshape(1, LANES)]
        diff_body = functools.partial(_diff_body, lam_init=lam_init)
        diff_body.__name__ = "_diff_body"

        def attend(q_rows, q_row0, key0, n_keys, tq_mla):
            kw = dict(key0=key0, n_keys=n_keys)
            tq = ROW_TILE
            oa = _attention(_gqa_body, qa, ka, va, [], n_groups=2, wq=256, wk=LANES, wv=LANES, n_s=4,
                            tq=tq, q_tiles=q_rows // tq, q_tile0=q_row0 // tq, **kw)
            ob = _attention(diff_body, qb, kb, vb, diff_extra, n_groups=2, wq=LANES, wk=LANES, wv=LANES,
                            n_s=4, tq=tq, q_tiles=q_rows // tq, q_tile0=q_row0 // tq, **kw)
            om = _attention(_mla_body, qm, km, vm, [], n_groups=2, wq=256, wk=256, wv=LANES, n_s=2,
                            tq=tq_mla, q_tiles=q_rows // tq_mla, q_tile0=q_row0 // tq_mla, **kw)
            return oa, ob, om

        w_out_b = w_out[l].astype(BF16)
        g2 = g_ffn[l].reshape(1, d)
        dense = l % 2 == 0
        if dense:
            wg, wu = _chunked_in(ffn_w_in[l // 2], ffn_w_out.shape[1])
            wo = ffn_w_out[l // 2].reshape(-1, FFN_CHUNK, d).astype(BF16)
            w_router = None
        else:
            n_hidden = moe_w_out.shape[2]
            wi = moe_w_in[l // 2]
            wg = wi[:, :, :n_hidden].reshape(N_EXPERTS, d, -1, FFN_CHUNK).transpose(0, 2, 1, 3).astype(BF16)
            wu = wi[:, :, n_hidden:].reshape(N_EXPERTS, d, -1, FFN_CHUNK).transpose(0, 2, 1, 3).astype(BF16)
            wo = moe_w_out[l // 2].reshape(N_EXPERTS, -1, FFN_CHUNK, d).astype(BF16)
            w_router = jnp.zeros((d, LANES), F32).at[:, :N_EXPERTS].set(moe_router[l // 2])
        last = l == depth - 1

        def channel_mix(xs, attn_out, mod_row, final_norm):
            res = _out_project(xs, *attn_out, w_out_b, mod3, g2, mod_row, w_router)
            gf = g_final.reshape(1, d)
            if dense:
                return _dense_ffn(res[0], res[1], wg, wu, wo, mod3, mod_row, gf, final_norm)
            return _moe_ffn(res[0], res[1], res[2], res[3], wg, wu, wo, mod3, mod_row, gf, final_norm)

        x_new = channel_mix(x, attend(s, 0, 0, t_all, 2 * ROW_TILE), lambda bi: bi, last)
        if need_ctx:
            xc = channel_mix(xc, attend(n_ctx, s, s, n_ctx, ROW_TILE), lambda bi: b, False)
        x = x_new
    return x
```

```python
import functools
import math

import numpy as np
import jax
import jax.numpy as jnp
from jax import lax
from jax.experimental import pallas as pl
from jax.experimental.pallas import tpu as pltpu

LANES = 128
MXU_TILE = 256
VMEM_LIMIT = 60 * 1024 * 1024

NORM_EPS = 1e-6
ROPE_THETA = 10000.0
GRID_W = 64
N_MOD = 6
HEAD_DIM = 64
GQA_HEADS, GQA_KV_HEADS = 8, 2
DIFF_HEADS, DIFF_QK_DIM, DIFF_V_DIM = 4, 32, 64
MLA_HEADS, MLA_Q_RANK, MLA_KV_RANK = 4, 192, 128
MLA_NOPE_DIM, MLA_ROPE_DIM, MLA_V_DIM = 64, 32, 64
N_EXPERTS, TOP_K = 8, 2
GATE_ROWS = 16

ROW_TILE = 256
KEY_CHUNK = MXU_TILE
FFN_CHUNK = MXU_TILE
FFN_ROWS = 512
MOE_ROWS = 1024
MOE_SLOTS = 320

F32 = jnp.float32
BF16 = jnp.bfloat16
HIGHEST = lax.Precision.HIGHEST


def _params(*sem):
    return pltpu.CompilerParams(dimension_semantics=sem, vmem_limit_bytes=VMEM_LIMIT)


def _lane_iota(shape):
    return lax.broadcasted_iota(jnp.int32, shape, len(shape) - 1)


def _mod_body(c_ref, w_ref, b_ref, o_ref):
    c = c_ref[...]
    sc = c * jax.nn.sigmoid(c)
    o_ref[...] = jnp.dot(sc, w_ref[...], precision=HIGHEST,
                         preferred_element_type=F32) + b_ref[...]


def _modulation(c_all, w, b):
    rows, d = c_all.shape
    n = w.shape[1]
    tn = 1536
    return pl.pallas_call(
        _mod_body,
        grid=(n // tn,),
        in_specs=[pl.BlockSpec((rows, d), lambda j: (0, 0)),
                  pl.BlockSpec((d, tn), lambda j: (0, j)),
                  pl.BlockSpec((1, tn), lambda j: (0, j))],
        out_specs=pl.BlockSpec((rows, tn), lambda j: (0, j)),
        out_shape=jax.ShapeDtypeStruct((rows, n), F32),
        compiler_params=_params("arbitrary"),
        name="modulation",
    )(c_all, w, b.reshape(1, n))


def _rms_rows(x, width):
    return lax.rsqrt(jnp.sum(x * x, axis=-1, keepdims=True) * (1.0 / width) + NORM_EPS)


def _rope(t, cos, sin_signed, half):
    lane = _lane_iota(t.shape)
    partner = jnp.where((lane & half) == 0,
                        pltpu.roll(t, LANES - half, 1), pltpu.roll(t, half, 1))
    return t * cos + partner * sin_signed


def _with_ones(t):
    return jnp.where(_lane_iota(t.shape) == HEAD_DIM, 1.0, t)


def _proj_body(x_ref, xc_ref, mod_ref, g_ref, w_ref, ca_ref, sa_ref, cb_ref, sb_ref,
               gq_ref, gk_ref, bd_ref, gcq_ref, gckv_ref, wuq_ref, wukv_ref,
               qa_ref, ka_ref, va_ref, qb_ref, kb_ref, vb_ref, qm_ref, km_ref, vm_ref,
               *, scale_a, scale_b, scale_m):
    i = pl.program_id(1)
    is_ctx = i == pl.num_programs(1) - 1
    x = jnp.where(is_ctx, xc_ref[0], x_ref[0])
    d = x.shape[-1]
    mod = mod_ref[0]
    y = x * _rms_rows(x, d) * g_ref[...]
    h = (y * (1.0 + mod[1:2]) + mod[0:1]).astype(BF16)
    p = jnp.dot(h, w_ref[...], preferred_element_type=F32)

    ca, sa, cb, sb = ca_ref[...], sa_ref[...], cb_ref[...], sb_ref[...]
    bd = bd_ref[...]
    tile = lambda ref, j: (0, slice(None), slice(j * LANES, (j + 1) * LANES))
    cols = lambda a, base, j: a[:, base + j * LANES:base + (j + 1) * LANES]

    def head_norm(t, g):
        ms = jnp.dot(t * t, bd, precision=HIGHEST, preferred_element_type=F32)
        return t * lax.rsqrt(ms + NORM_EPS) * g

    for j in range(4):
        t = head_norm(cols(p, 0, j), gq_ref[...])
        qa_ref[tile(qa_ref, j)] = (_rope(t, ca, sa, 32) * scale_a).astype(BF16)
    for j in range(2):
        t = head_norm(cols(p, 512, j), gk_ref[...])
        ka_ref[tile(ka_ref, j)] = _rope(t, ca, sa, 32).astype(BF16)
        va_ref[tile(va_ref, j)] = _with_ones(cols(p, 768, j)).astype(BF16)

    for j in range(2):
        qb_ref[tile(qb_ref, j)] = (_rope(cols(p, 1024, j), cb, sb, 16) * scale_b).astype(BF16)
        kb_ref[tile(kb_ref, j)] = _rope(cols(p, 1280, j), cb, sb, 16).astype(BF16)
    vb_ref[0] = p[:, 1536:1792].astype(BF16)

    cq = p[:, 1792:2048]
    cqn = (cq * _rms_rows(cq, MLA_Q_RANK) * gcq_ref[...]).astype(BF16)
    uq = jnp.dot(cqn, wuq_ref[...], preferred_element_type=F32)
    ckv = p[:, 2048:2176]
    ckvn = (ckv * _rms_rows(ckv, MLA_KV_RANK) * gckv_ref[...]).astype(BF16)
    ukv = jnp.dot(ckvn, wukv_ref[...], preferred_element_type=F32)
    kr = _rope(p[:, 2176:2304], cb, sb, 16).astype(BF16)
    for pr in range(2):
        qm_ref[tile(qm_ref, 2 * pr)] = (cols(uq, 0, 2 * pr) * scale_m).astype(BF16)
        qr = _rope(cols(uq, 0, 2 * pr + 1), cb, sb, 16)
        qm_ref[tile(qm_ref, 2 * pr + 1)] = (qr * scale_m).astype(BF16)
        km_ref[tile(km_ref, 2 * pr)] = cols(ukv, 0, pr).astype(BF16)
        km_ref[tile(km_ref, 2 * pr + 1)] = kr
    vm_ref[0] = ukv[:, 256:512].astype(BF16)


def _project(x, xc, mod3, g, w, tables, gq2, gk2, bd, gcq, gckv, wuq, wukv):
    b, s, d = x.shape
    c = xc.shape[1]
    t = c + s
    tm = ROW_TILE
    assert c == tm and s % tm == 0
    nt = t // tm
    ctx_row = b
    widths = (512, 256, 256, 256, 256, 256, 512, 512, 256)
    log2e = math.log2(math.e)
    body = functools.partial(_proj_body, scale_a=HEAD_DIM ** -0.5 * log2e,
                             scale_b=DIFF_QK_DIM ** -0.5 * log2e,
                             scale_m=(MLA_NOPE_DIM + MLA_ROPE_DIM) ** -0.5 * log2e)
    const = lambda shape: pl.BlockSpec(shape, lambda bi, i: (0,) * len(shape))
    tab = pl.BlockSpec((tm, LANES), lambda bi, i: (i, 0))
    return pl.pallas_call(
        body,
        grid=(b, nt),
        in_specs=[pl.BlockSpec((1, tm, d), lambda bi, i: (bi, jnp.minimum(i, nt - 2), 0)),
                  pl.BlockSpec((1, tm, d), lambda bi, i: (bi, 0, 0)),
                  pl.BlockSpec((1, N_MOD, d), lambda bi, i: (jnp.where(i == nt - 1, ctx_row, bi), 0, 0)),
                  const((1, d)), const(w.shape), tab, tab, tab, tab,
                  const((1, LANES)), const((1, LANES)), const((LANES, LANES)),
                  const(gcq.shape), const(gckv.shape), const(wuq.shape), const(wukv.shape)],
        out_specs=[pl.BlockSpec((1, tm, wd), lambda bi, i: (bi, i, 0)) for wd in widths],
        out_shape=[jax.ShapeDtypeStruct((b, t, wd), BF16) for wd in widths],
        compiler_params=_params("parallel", "arbitrary"),
        name="project",
    )(x, xc, mod3, g, w, *tables, gq2, gk2, bd, gcq, gckv, wuq, wukv)


def _attn_step(qs, k_ref, v_ref, s_ref, mp_ref, mb_ref, acc_ref, l_ref, n_chunks, ones_lane):
    step = pl.program_id(0)

    @pl.when(step == 0)
    def _():
        s_ref[...] = jnp.zeros(s_ref.shape, F32)
        mp_ref[...] = jnp.zeros(mp_ref.shape, F32)
        acc_ref[...] = jnp.ones(acc_ref.shape, F32)
        l_ref[...] = jnp.ones(l_ref.shape, F32)

    raw = acc_ref[...]
    if ones_lane:
        row_sum = raw[:, HEAD_DIM:HEAD_DIM + 1]
    else:
        row_sum = jnp.sum(l_ref[...], axis=-1, keepdims=True)
    done = raw * (1.0 / row_sum)

    cur = step % 2
    mb_ref[...] = jnp.broadcast_to(jnp.max(mp_ref[1 - cur], axis=-1, keepdims=True), mb_ref.shape)
    acc = None
    l_acc = None
    for c in range(n_chunks):
        keys = slice(c * KEY_CHUNK, (c + 1) * KEY_CHUNK)
        s_new = lax.dot_general(qs, k_ref[0, keys, :], (((1,), (1,)), ((), ())),
                                preferred_element_type=F32)
        s_old = s_ref[c]
        m_old = mb_ref[...]
        p0 = jnp.exp2(s_old[:, :LANES] - m_old)
        p1 = jnp.exp2(s_old[:, LANES:] - m_old)
        if not ones_lane:
            l_acc = p0 + p1 if l_acc is None else l_acc + (p0 + p1)
        part = jnp.dot(jnp.concatenate([p0, p1], axis=1).astype(BF16), v_ref[0, keys, :],
                       preferred_element_type=F32)
        acc = part if acc is None else acc + part
        s_ref[c] = s_new
        mc = jnp.maximum(s_new[:, :LANES], s_new[:, LANES:])
        mp_ref[cur] = mc if c == 0 else jnp.maximum(mp_ref[cur], mc)
    acc_ref[...] = acc
    if not ones_lane:
        l_ref[...] = l_acc
    return done


def _pack_heads(even, odd):
    lo = _lane_iota(even.shape) < HEAD_DIM
    return jnp.where(lo, even, pltpu.roll(odd, HEAD_DIM, 1))


def _gqa_body(q_ref, k_ref, v_ref, o_ref, *scratch, n_chunks):
    tq = q_ref.shape[1]
    lo = _lane_iota((tq, LANES)) < HEAD_DIM
    heads = []
    for j in range(2):
        t = q_ref[0, :, j * LANES:(j + 1) * LANES].astype(F32)
        heads.append(jnp.where(lo, t, 0.0))
        heads.append(jnp.where(lo, pltpu.roll(t, HEAD_DIM, 1), 0.0))
    qs = jnp.concatenate(heads, axis=0).astype(BF16)
    o = _attn_step(qs, k_ref, v_ref, *scratch, n_chunks, True)
    for j in range(2):
        o_ref[0, :, j * LANES:(j + 1) * LANES] = _pack_heads(
            o[(2 * j) * tq:(2 * j + 1) * tq], o[(2 * j + 1) * tq:(2 * j + 2) * tq]).astype(o_ref.dtype)


def _diff_body(q_ref, k_ref, v_ref, lq1_ref, lk1_ref, lq2_ref, lk2_ref, gsub_ref, o_ref,
               *scratch, n_chunks, lam_init):
    tq = q_ref.shape[1]
    lane = _lane_iota((tq, LANES))
    t = q_ref[0]
    zero = jnp.zeros_like(t)
    qs = jnp.concatenate([jnp.where((lane // DIFF_QK_DIM) == j, t, zero) for j in range(4)], axis=0)
    o = _attn_step(qs, k_ref, v_ref, *scratch, n_chunks, False)
    lam = (jnp.exp(jnp.sum(lq1_ref[...] * lk1_ref[...], axis=-1, keepdims=True))
           - jnp.exp(jnp.sum(lq2_ref[...] * lk2_ref[...], axis=-1, keepdims=True)) + lam_init)
    lo = lane < DIFF_V_DIM
    d = jnp.where(lo, o[0:tq] - lam * o[tq:2 * tq], o[2 * tq:3 * tq] - lam * o[3 * tq:4 * tq])
    sq = d * d
    ms_lo = jnp.sum(jnp.where(lo, sq, 0.0), axis=-1, keepdims=True) * (1.0 / DIFF_V_DIM)
    ms_hi = jnp.sum(jnp.where(lo, 0.0, sq), axis=-1, keepdims=True) * (1.0 / DIFF_V_DIM)
    r = jnp.where(lo, lax.rsqrt(ms_lo + NORM_EPS), lax.rsqrt(ms_hi + NORM_EPS))
    o_ref[0] = ((d * r * gsub_ref[...]) * (1.0 - lam_init)).astype(o_ref.dtype)


def _mla_body(q_ref, k_ref, v_ref, o_ref, *scratch, n_chunks):
    tq = q_ref.shape[1]
    wq = q_ref.shape[2]
    lane = _lane_iota((tq, wq))
    t = q_ref[0]
    zero = jnp.zeros_like(t)
    first = (lane < MLA_NOPE_DIM) | ((lane >= LANES) & (lane < LANES + MLA_ROPE_DIM))
    second = ((lane >= MLA_NOPE_DIM) & (lane < LANES)) | (
        (lane >= LANES + MLA_ROPE_DIM) & (lane < LANES + 2 * MLA_ROPE_DIM))
    qs = jnp.concatenate([jnp.where(first, t, zero), jnp.where(second, t, zero)], axis=0)
    o = _attn_step(qs, k_ref, v_ref, *scratch, n_chunks, False)
    lo = _lane_iota((tq, LANES)) < MLA_V_DIM
    o_ref[0] = jnp.where(lo, o[0:tq], o[tq:2 * tq]).astype(o_ref.dtype)


def _attention(body, q, k, v, extra, *, n_groups, wq, wk, wv, n_s, tq, q_tiles, q_tile0, key0, n_keys):
    b = q.shape[0]
    wo = wq if body is _gqa_body else LANES
    n_chunks = n_keys // KEY_CHUNK
    key_blk = key0 // n_keys
    assert key_blk * n_keys == key0
    m_rows = n_s * tq
    n_tiles = b * n_groups * q_tiles

    def split(t):
        return t // (n_groups * q_tiles), (t // q_tiles) % n_groups, t % q_tiles

    cur = lambda j: split(jnp.minimum(j, n_tiles - 1))
    prev = lambda j: split(jnp.clip(j - 1, 0, n_tiles - 1))
    done = lambda j: split(jnp.maximum(j - 2, 0))
    extra_specs = [pl.BlockSpec(e.shape, lambda j: (0, 0)) for e in extra]
    return pl.pallas_call(
        functools.partial(body, n_chunks=n_chunks),
        grid=(n_tiles + 2,),
        in_specs=[pl.BlockSpec((1, tq, wq), lambda j: (cur(j)[0], cur(j)[2] + q_tile0, cur(j)[1])),
                  pl.BlockSpec((1, n_keys, wk), lambda j: (cur(j)[0], key_blk, cur(j)[1])),
                  pl.BlockSpec((1, n_keys, wv), lambda j: (prev(j)[0], key_blk, prev(j)[1]))] + extra_specs,
        out_specs=pl.BlockSpec((1, tq, wo), lambda j: (done(j)[0], done(j)[2], done(j)[1])),
        out_shape=jax.ShapeDtypeStruct((b, q_tiles * tq, n_groups * wo), BF16),
        scratch_shapes=[pltpu.VMEM((n_chunks, m_rows, KEY_CHUNK), F32),
                        pltpu.VMEM((2, m_rows, LANES), F32),
                        pltpu.VMEM((m_rows, LANES), F32),
                        pltpu.VMEM((m_rows, LANES), F32),
                        pltpu.VMEM((m_rows, LANES), F32)],
        compiler_params=_params("arbitrary"),
        name=body.__name__.strip("_"),
    )(q, k, v, *extra)


def _router_gates(logits):
    lane = _lane_iota(logits.shape).astype(F32)
    neg = jnp.float32(-jnp.inf)
    z = jnp.where(lane < N_EXPERTS, logits, neg)
    m1 = jnp.max(z, axis=-1, keepdims=True)
    i1 = jnp.min(jnp.where(z == m1, lane, float(LANES)), axis=-1, keepdims=True)
    z2 = jnp.where(lane == i1, neg, z)
    m2 = jnp.max(z2, axis=-1, keepdims=True)
    i2 = jnp.min(jnp.where(z2 == m2, lane, float(LANES)), axis=-1, keepdims=True)
    e2 = jnp.exp(m2 - m1)
    den = 1.0 + e2
    return jnp.where(lane == i1, 1.0 / den, 0.0) + jnp.where(lane == i2, e2 / den, 0.0)


def _outproj_body(x_ref, oa_ref, ob_ref, om_ref, w_ref, mod_ref, g_ref, *rest, routed):
    if routed:
        wr_ref, xo_ref, h_ref, gates_ref, gates_t_ref = rest
    else:
        xo_ref, h_ref = rest
    mod = mod_ref[0]
    wa, wb = oa_ref.shape[2], ob_ref.shape[2]
    y = jnp.dot(oa_ref[0], w_ref[0:wa, :], preferred_element_type=F32)
    y += jnp.dot(ob_ref[0], w_ref[wa:wa + wb, :], preferred_element_type=F32)
    y += jnp.dot(om_ref[0], w_ref[wa + wb:, :], preferred_element_type=F32)
    x = x_ref[0] + mod[2:3] * y
    xo_ref[0] = x
    n = x * _rms_rows(x, x.shape[-1]) * g_ref[...]
    h = n * (1.0 + mod[4:5]) + mod[3:4]
    h_ref[0] = h.astype(BF16)
    if routed:
        logits = jnp.dot(h, wr_ref[...], precision=HIGHEST, preferred_element_type=F32)
        gates = _router_gates(logits)
        gates_ref[0] = gates
        gates_t_ref[0] = gates.T[:gates_t_ref.shape[1]]


def _out_project(x, oa, ob, om, w, mod3, g, mod_row, w_router=None):
    b, s, d = x.shape
    tm = ROW_TILE
    routed = w_router is not None
    row = lambda bi, i: (bi, i, 0)
    in_specs = [pl.BlockSpec((1, tm, d), row),
                pl.BlockSpec((1, tm, oa.shape[2]), row),
                pl.BlockSpec((1, tm, ob.shape[2]), row),
                pl.BlockSpec((1, tm, om.shape[2]), row),
                pl.BlockSpec(w.shape, lambda bi, i: (0, 0)),
                pl.BlockSpec((1, N_MOD, d), lambda bi, i: (mod_row(bi), 0, 0)),
                pl.BlockSpec((1, d), lambda bi, i: (0, 0))]
    out_specs = [pl.BlockSpec((1, tm, d), row), pl.BlockSpec((1, tm, d), row)]
    out_shape = [jax.ShapeDtypeStruct((b, s, d), F32), jax.ShapeDtypeStruct((b, s, d), BF16)]
    args = [x, oa, ob, om, w, mod3, g]
    if routed:
        in_specs.append(pl.BlockSpec(w_router.shape, lambda bi, i: (0, 0)))
        out_specs.append(pl.BlockSpec((1, tm, LANES), row))
        out_shape.append(jax.ShapeDtypeStruct((b, s, LANES), F32))
        out_specs.append(pl.BlockSpec((1, GATE_ROWS, tm), lambda bi, i: (bi, 0, i)))
        out_shape.append(jax.ShapeDtypeStruct((b, GATE_ROWS, s), F32))
        args.append(w_router)
    return pl.pallas_call(
        functools.partial(_outproj_body, routed=routed),
        grid=(b, s // tm),
        in_specs=in_specs, out_specs=out_specs, out_shape=out_shape,
        compiler_params=_params("parallel", "arbitrary"),
        name="out_project_routed" if routed else "out_project",
    )(*args)


def _swiglu_chunks(h, wg_ref, wu_ref, wo_ref, lead):
    acc = None
    for j in range(wg_ref.shape[len(lead)]):
        gate = jnp.dot(h, wg_ref[lead + (j,)], preferred_element_type=F32)
        up = jnp.dot(h, wu_ref[lead + (j,)], preferred_element_type=F32)
        a = (gate * jax.nn.sigmoid(gate) * up).astype(BF16)
        part = jnp.dot(a, wo_ref[lead + (j,)], preferred_element_type=F32)
        acc = part if acc is None else acc + part
    return acc


def _residual_out(x, gate, y, gf_ref, final_norm):
    x = x + gate * y
    if final_norm:
        x = x * _rms_rows(x, x.shape[-1]) * gf_ref[...]
    return x


def _ffn_body(x_ref, h_ref, wg_ref, wu_ref, wo_ref, mod_ref, gf_ref, o_ref, *, final_norm):
    y = _swiglu_chunks(h_ref[0], wg_ref, wu_ref, wo_ref, ())
    o_ref[0] = _residual_out(x_ref[0], mod_ref[0][5:6], y, gf_ref, final_norm)


def _dense_ffn(x, h, wg, wu, wo, mod3, mod_row, g_final, final_norm):
    b, s, d = x.shape
    tm = min(FFN_ROWS, s)
    row = lambda bi, i: (bi, i, 0)
    resident = lambda a: pl.BlockSpec(a.shape, lambda bi, i: (0, 0, 0),
                                      pipeline_mode=pl.Buffered(1))
    return pl.pallas_call(
        functools.partial(_ffn_body, final_norm=final_norm),
        grid=(b, s // tm),
        in_specs=[pl.BlockSpec((1, tm, d), row), pl.BlockSpec((1, tm, d), row),
                  resident(wg), resident(wu), resident(wo),
                  pl.BlockSpec((1, N_MOD, d), lambda bi, i: (mod_row(bi), 0, 0)),
                  pl.BlockSpec((1, d), lambda bi, i: (0, 0))],
        out_specs=pl.BlockSpec((1, tm, d), row),
        out_shape=jax.ShapeDtypeStruct((b, s, d), F32),
        compiler_params=_params("parallel", "arbitrary"),
        name="dense_ffn",
    )(x, h, wg, wu, wo, mod3, g_final)


def _moe_body(x_ref, h_ref, gates_ref, gates_t_ref, tri_ref, tri_t_ref, wg_ref, wu_ref, wo_ref,
              mod_ref, gf_ref, o_ref, tot_ref, rank_ref, rank_t_ref, *, final_norm):
    e = pl.program_id(2)
    tm = h_ref.shape[1]
    gates = gates_ref[0]
    lane = _lane_iota(gates.shape)

    @pl.when(e == 0)
    def _():
        tot_ref[...] = jnp.zeros(tot_ref.shape, F32)
        rank_ref[...] = jnp.dot(tri_ref[...], (gates > 0.0).astype(BF16), preferred_element_type=F32)
        rank_t_ref[...] = jnp.dot((gates_t_ref[0] > 0.0).astype(BF16), tri_t_ref[...],
                                  preferred_element_type=F32)

    pick = lambda a: jnp.sum(jnp.where(lane == e, a, 0.0), axis=-1, keepdims=True)
    gate_e = pick(gates)
    rank_e = pick(rank_ref[...])
    h = h_ref[0]

    def compact():
        slots = MOE_SLOTS
        rank_row = rank_t_ref[pl.ds(e, 1), :]
        live_row = gates_t_ref[0, pl.ds(e, 1), :] > 0.0
        slot_sub = lax.broadcasted_iota(jnp.int32, (slots, tm), 0).astype(F32) + 1.0
        gather = jnp.where((rank_row == slot_sub) & live_row, 1.0, 0.0).astype(BF16)
        rows = jnp.dot(gather, h, preferred_element_type=F32).astype(BF16)
        y = _swiglu_chunks(rows, wg_ref, wu_ref, wo_ref, (0,))
        y_hi = y.astype(BF16)
        y_lo = (y - y_hi.astype(F32)).astype(BF16)
        pad = (-slots) % LANES
        if pad:
            zeros = jnp.zeros((pad, y.shape[1]), BF16)
            y_hi = jnp.concatenate([y_hi, zeros], axis=0)
            y_lo = jnp.concatenate([y_lo, zeros], axis=0)
        slot_lane = _lane_iota((tm, slots + pad)).astype(F32) + 1.0
        scatter = jnp.where((rank_e == slot_lane) & (gate_e > 0.0), 1.0, 0.0).astype(BF16)
        return (jnp.dot(scatter, y_hi, preferred_element_type=F32)
                + jnp.dot(scatter, y_lo, preferred_element_type=F32))

    def dense():
        return _swiglu_chunks(h, wg_ref, wu_ref, wo_ref, (0,))

    y = lax.cond(jnp.max(rank_e) <= float(MOE_SLOTS), compact, dense)
    tot_ref[...] += gate_e * y

    @pl.when(e == pl.num_programs(2) - 1)
    def _():
        o_ref[0] = _residual_out(x_ref[0], mod_ref[0][5:6], tot_ref[...], gf_ref, final_norm)


def _moe_ffn(x, h, gates, gates_t, wg, wu, wo, mod3, mod_row, g_final, final_norm):
    b, s, d = x.shape
    tm = min(MOE_ROWS, s)
    tri = jnp.asarray(np.tril(np.ones((tm, tm), np.float32)), BF16)
    row = lambda bi, i, e: (bi, i, 0)
    const = lambda bi, i, e: (0, 0)
    expert = lambda a: pl.BlockSpec((1,) + a.shape[1:], lambda bi, i, e: (e, 0, 0, 0))
    return pl.pallas_call(
        functools.partial(_moe_body, final_norm=final_norm),
        grid=(b, s // tm, wg.shape[0]),
        in_specs=[pl.BlockSpec((1, tm, d), row, pipeline_mode=pl.Buffered(1)),
                  pl.BlockSpec((1, tm, d), row),
                  pl.BlockSpec((1, tm, LANES), row),
                  pl.BlockSpec((1, GATE_ROWS, tm), lambda bi, i, e: (bi, 0, i)),
                  pl.BlockSpec((tm, tm), const), pl.BlockSpec((tm, tm), const),
                  expert(wg), expert(wu), expert(wo),
                  pl.BlockSpec((1, N_MOD, d), lambda bi, i, e: (mod_row(bi), 0, 0)),
                  pl.BlockSpec((1, d), const)],
        out_specs=pl.BlockSpec((1, tm, d), row),
        out_shape=jax.ShapeDtypeStruct((b, s, d), F32),
        scratch_shapes=[pltpu.VMEM((tm, d), F32), pltpu.VMEM((tm, LANES), F32),
                        pltpu.VMEM((GATE_ROWS, tm), F32)],
        compiler_params=_params("parallel", "parallel", "arbitrary"),
        name="moe_ffn",
    )(x, h, gates, gates_t, tri, tri.T, wg, wu, wo, mod3, g_final)


def _deinterleave(n):
    return np.concatenate([np.arange(0, n, 2), np.arange(1, n, 2)])


def _in_proj_columns(pad):
    pads = lambda n: np.full((n,), pad)
    cols = []
    for hd in range(GQA_HEADS):
        cols.append(64 * hd + _deinterleave(64))
    for g in range(GQA_KV_HEADS):
        cols += [512 + 64 * g + _deinterleave(64), pads(64)]
    for g in range(GQA_KV_HEADS):
        cols += [640 + 64 * g + np.arange(64), pads(64)]
    for u in range(2 * DIFF_HEADS):
        cols.append(768 + 32 * u + _deinterleave(32))
    for u in range(2 * DIFF_HEADS):
        cols.append(1024 + 32 * u + _deinterleave(32))
    cols.append(1280 + np.arange(256))
    cols += [1536 + np.arange(MLA_Q_RANK), pads(64)]
    cols.append(1728 + np.arange(MLA_KV_RANK))
    cols += [1856 + _deinterleave(32), 1856 + _deinterleave(32), pads(64)]
    return np.concatenate(cols)


def _uq_columns(pad):
    per = MLA_NOPE_DIM + MLA_ROPE_DIM
    cols = []
    for pr in range(2):
        h0, h1 = 2 * pr, 2 * pr + 1
        cols += [per * h0 + np.arange(64), per * h1 + np.arange(64),
                 per * h0 + 64 + _deinterleave(32), per * h1 + 64 + _deinterleave(32),
                 np.full((64,), pad)]
    return np.concatenate(cols)


def _ukv_columns():
    per = MLA_NOPE_DIM + MLA_V_DIM
    k = [per * hd + np.arange(64) for hd in range(MLA_HEADS)]
    v = [per * hd + 64 + np.arange(64) for hd in range(MLA_HEADS)]
    return np.concatenate(k + v)


def _take_cols(w, cols):
    w_ext = jnp.concatenate([w, jnp.zeros((w.shape[0], 1), w.dtype)], axis=1)
    return jnp.take(w_ext, jnp.asarray(cols), axis=1)


def _rope_tables(s, c):
    t = jnp.arange(s)
    rows = (t // GRID_W).astype(F32)
    cols = (t % GRID_W).astype(F32)
    out = []
    for dim in (HEAD_DIM, DIFF_QK_DIM):
        quarter = dim // 4
        half = dim // 2
        inv_freq = ROPE_THETA ** (-jnp.arange(quarter, dtype=F32) / quarter)
        ang = jnp.concatenate([rows[:, None] * inv_freq, cols[:, None] * inv_freq], axis=-1)
        lane = np.arange(LANES)
        idx = (lane % dim) % half
        sign = np.where((lane % dim) < half, -1.0, 1.0).astype(np.float32)
        cos = jnp.cos(ang)[:, idx]
        sin = jnp.sin(ang)[:, idx] * sign
        out.append(jnp.concatenate([cos, jnp.ones((c, LANES), F32)], axis=0))
        out.append(jnp.concatenate([sin, jnp.zeros((c, LANES), F32)], axis=0))
    return out


def _chunked_in(w, n_hidden):
    d = w.shape[0]
    nc = n_hidden // FFN_CHUNK
    g = w[:, :n_hidden].reshape(d, nc, FFN_CHUNK).transpose(1, 0, 2).astype(BF16)
    u = w[:, n_hidden:].reshape(d, nc, FFN_CHUNK).transpose(1, 0, 2).astype(BF16)
    return g, u


def kernel(x, c, ctx, c_ctx, w_mod, b_mod, g_attn, g_ffn, w_in, w_out, gqa_gq, gqa_gk,
           diff_lq1, diff_lk1, diff_lq2, diff_lk2, diff_gsub, mla_gcq, mla_gckv, mla_wuq, mla_wukv,
           ffn_w_in, ffn_w_out, moe_router, moe_w_in, moe_w_out, g_final):
    b, s, d = x.shape
    n_ctx = ctx.shape[1]
    depth = w_mod.shape[0]
    t_all = n_ctx + s
    tables = _rope_tables(s, n_ctx)
    in_cols = _in_proj_columns(w_in.shape[2])
    uq_cols = _uq_columns(mla_wuq.shape[2])
    ukv_cols = _ukv_columns()
    perm64 = _deinterleave(HEAD_DIM)
    bd = jnp.asarray(np.kron(np.eye(LANES // HEAD_DIM), np.full((HEAD_DIM, HEAD_DIM), 1.0 / HEAD_DIM)), F32)

    mod_rows = 16
    c_all = jnp.zeros((mod_rows, d), F32).at[:b].set(c).at[b].set(c_ctx)
    pad_lanes = lambda v: jnp.zeros((1, LANES), F32).at[0, :v.shape[0]].set(v)

    xc = ctx
    for l in range(depth):
        need_ctx = l < depth - 1
        lam_init = 0.8 - 0.6 * math.exp(-0.3 * l)
        mod3 = _modulation(c_all, w_mod[l], b_mod[l]).reshape(mod_rows, N_MOD, d)

        w_in_p = _take_cols(w_in[l], in_cols).astype(BF16)
        wuq_p = jnp.zeros((2 * LANES, 512), F32).at[:MLA_Q_RANK].set(
            _take_cols(mla_wuq[l], uq_cols)).astype(BF16)
        wukv_p = jnp.take(mla_wukv[l], jnp.asarray(ukv_cols), axis=1).astype(BF16)
        gq2 = jnp.tile(gqa_gq[l][perm64], 2).reshape(1, LANES)
        gk2 = jnp.tile(gqa_gk[l][perm64], 2).reshape(1, LANES)
        gcq = jnp.zeros((1, 2 * LANES), F32).at[0, :MLA_Q_RANK].set(mla_gcq[l])
        gckv = mla_gckv[l].reshape(1, MLA_KV_RANK)

        qa, ka, va, qb, kb, vb, qm, km, vm = _project(
            x, xc, mod3, g_attn[l].reshape(1, d), w_in_p, tables, gq2, gk2, bd, gcq, gckv, wuq_p, wukv_p)

        diff_extra = [pad_lanes(diff_lq1[l]), pad_lanes(diff_lk1[l]), pad_lanes(diff_lq2[l]),
                      pad_lanes(diff_lk2[l]), jnp.tile(diff_gsub[l], 2).reshape(1, LANES)]
        diff_body = functools.partial(_diff_body, lam_init=lam_init)
        diff_body.__name__ = "_diff_body"

        def attend(q_rows, q_row0, key0, n_keys, tq_mla):
            kw = dict(key0=key0, n_keys=n_keys)
            tq = ROW_TILE
            oa = _attention(_gqa_body, qa, ka, va, [], n_groups=2, wq=256, wk=LANES, wv=LANES, n_s=4,
                            tq=tq, q_tiles=q_rows // tq, q_tile0=q_row0 // tq, **kw)
            ob = _attention(diff_body, qb, kb, vb, diff_extra, n_groups=2, wq=LANES, wk=LANES, wv=LANES,
                            n_s=4, tq=tq, q_tiles=q_rows // tq, q_tile0=q_row0 // tq, **kw)
            om = _attention(_mla_body, qm, km, vm, [], n_groups=2, wq=256, wk=256, wv=LANES, n_s=2,
                            tq=tq_mla, q_tiles=q_rows // tq_mla, q_tile0=q_row0 // tq_mla, **kw)
            return oa, ob, om

        w_out_b = w_out[l].astype(BF16)
        g2 = g_ffn[l].reshape(1, d)
        dense = l % 2 == 0
        if dense:
            wg, wu = _chunked_in(ffn_w_in[l // 2], ffn_w_out.shape[1])
            wo = ffn_w_out[l // 2].reshape(-1, FFN_CHUNK, d).astype(BF16)
            w_router = None
        else:
            n_hidden = moe_w_out.shape[2]
            wi = moe_w_in[l // 2]
            wg = wi[:, :, :n_hidden].reshape(N_EXPERTS, d, -1, FFN_CHUNK).transpose(0, 2, 1, 3).astype(BF16)
            wu = wi[:, :, n_hidden:].reshape(N_EXPERTS, d, -1, FFN_CHUNK).transpose(0, 2, 1, 3).astype(BF16)
            wo = moe_w_out[l // 2].reshape(N_EXPERTS, -1, FFN_CHUNK, d).astype(BF16)
            w_router = jnp.zeros((d, LANES), F32).at[:, :N_EXPERTS].set(moe_router[l // 2])
        last = l == depth - 1

        def channel_mix(xs, attn_out, mod_row, final_norm):
            res = _out_project(xs, *attn_out, w_out_b, mod3, g2, mod_row, w_router)
            gf = g_final.reshape(1, d)
            if dense:
                return _dense_ffn(res[0], res[1], wg, wu, wo, mod3, mod_row, gf, final_norm)
            return _moe_ffn(res[0], res[1], res[2], res[3], wg, wu, wo, mod3, mod_row, gf, final_norm)

        x_new = channel_mix(x, attend(s, 0, 0, t_all, 2 * ROW_TILE), lambda bi: bi, last)
        if need_ctx:
            xc = channel_mix(xc, attend(n_ctx, s, s, n_ctx, ROW_TILE), lambda bi: b, False)
        x = x_new
    return x
```

```python
import functools
import math

import numpy as np
import jax
import jax.numpy as jnp
from jax import lax
from jax.experimental import pallas as pl
from jax.experimental.pallas import tpu as pltpu

LANES = 128
MXU_TILE = 256
VMEM_LIMIT = 60 * 1024 * 1024

NORM_EPS = 1e-6
ROPE_THETA = 10000.0
GRID_W = 64
N_MOD = 6
HEAD_DIM = 64
GQA_HEADS, GQA_KV_HEADS = 8, 2
DIFF_HEADS, DIFF_QK_DIM, DIFF_V_DIM = 4, 32, 64
MLA_HEADS, MLA_Q_RANK, MLA_KV_RANK = 4, 192, 128
MLA_NOPE_DIM, MLA_ROPE_DIM, MLA_V_DIM = 64, 32, 64
N_EXPERTS, TOP_K = 8, 2
GATE_ROWS = 16

ROW_TILE = 256
PROJ_ROWS = 512
KEY_CHUNK = MXU_TILE
FFN_CHUNK = MXU_TILE
FFN_ROWS = 512
MOE_ROWS = 1024
MOE_SLOTS = 320

F32 = jnp.float32
BF16 = jnp.bfloat16
HIGHEST = lax.Precision.HIGHEST


def _params(*sem):
    return pltpu.CompilerParams(dimension_semantics=sem, vmem_limit_bytes=VMEM_LIMIT)


def _lane_iota(shape):
    return lax.broadcasted_iota(jnp.int32, shape, len(shape) - 1)


def _mod_body(c_ref, w_ref, b_ref, o_ref):
    c = c_ref[...]
    sc = c * jax.nn.sigmoid(c)
    o_ref[...] = jnp.dot(sc, w_ref[...], precision=HIGHEST,
                         preferred_element_type=F32) + b_ref[...]


def _modulation(c_all, w, b):
    rows, d = c_all.shape
    n = w.shape[1]
    tn = 1536
    return pl.pallas_call(
        _mod_body,
        grid=(n // tn,),
        in_specs=[pl.BlockSpec((rows, d), lambda j: (0, 0)),
                  pl.BlockSpec((d, tn), lambda j: (0, j)),
                  pl.BlockSpec((1, tn), lambda j: (0, j))],
        out_specs=pl.BlockSpec((rows, tn), lambda j: (0, j)),
        out_shape=jax.ShapeDtypeStruct((rows, n), F32),
        compiler_params=_params("arbitrary"),
        name="modulation",
    )(c_all, w, b.reshape(1, n))


def _rms_rows(x, width):
    return lax.rsqrt(jnp.sum(x * x, axis=-1, keepdims=True) * (1.0 / width) + NORM_EPS)


def _rope(t, cos, sin_signed, half):
    lane = _lane_iota(t.shape)
    partner = jnp.where((lane & half) == 0,
                        pltpu.roll(t, LANES - half, 1), pltpu.roll(t, half, 1))
    return t * cos + partner * sin_signed


def _with_ones(t):
    return jnp.where(_lane_iota(t.shape) == HEAD_DIM, 1.0, t)


def _proj_body(x_ref, mod_ref, g_ref, w_ref, ca_ref, sa_ref, cb_ref, sb_ref,
               gq_ref, gk_ref, bd_ref, gcq_ref, gckv_ref, wuq_ref, wukv_ref,
               qa_ref, ka_ref, va_ref, qb_ref, kb_ref, vb_ref, qm_ref, km_ref, vm_ref,
               *, scale_a, scale_b, scale_m):
    x = x_ref[0]
    d = x.shape[-1]
    mod = mod_ref[0]
    y = x * _rms_rows(x, d) * g_ref[...]
    h = (y * (1.0 + mod[1:2]) + mod[0:1]).astype(BF16)
    p = jnp.dot(h, w_ref[...], preferred_element_type=F32)

    ca, sa, cb, sb = ca_ref[...], sa_ref[...], cb_ref[...], sb_ref[...]
    bd = bd_ref[...]
    tile = lambda ref, j: (0, slice(None), slice(j * LANES, (j + 1) * LANES))
    cols = lambda a, base, j: a[:, base + j * LANES:base + (j + 1) * LANES]

    def head_norm(t, g):
        sq = t * t
        hi = sq.astype(BF16)
        lo = (sq - hi.astype(F32)).astype(BF16)
        ms = (jnp.dot(hi, bd, preferred_element_type=F32) + jnp.dot(lo, bd, preferred_element_type=F32))
        return t * lax.rsqrt(ms + NORM_EPS) * g

    for j in range(4):
        t = head_norm(cols(p, 0, j), gq_ref[...])
        qa_ref[tile(qa_ref, j)] = (_rope(t, ca, sa, 32) * scale_a).astype(BF16)
    for j in range(2):
        t = head_norm(cols(p, 512, j), gk_ref[...])
        ka_ref[tile(ka_ref, j)] = _rope(t, ca, sa, 32).astype(BF16)
        va_ref[tile(va_ref, j)] = _with_ones(cols(p, 768, j)).astype(BF16)

    for j in range(2):
        qb_ref[tile(qb_ref, j)] = (_rope(cols(p, 1024, j), cb, sb, 16) * scale_b).astype(BF16)
        kb_ref[tile(kb_ref, j)] = _rope(cols(p, 1280, j), cb, sb, 16).astype(BF16)
    vb_ref[0] = p[:, 1536:1792].astype(BF16)

    cq = p[:, 1792:2048]
    cqn = (cq * _rms_rows(cq, MLA_Q_RANK) * gcq_ref[...]).astype(BF16)
    uq = jnp.dot(cqn, wuq_ref[...], preferred_element_type=F32)
    ckv = p[:, 2048:2176]
    ckvn = (ckv * _rms_rows(ckv, MLA_KV_RANK) * gckv_ref[...]).astype(BF16)
    ukv = jnp.dot(ckvn, wukv_ref[...], preferred_element_type=F32)
    kr = _rope(p[:, 2176:2304], cb, sb, 16).astype(BF16)
    for pr in range(2):
        qm_ref[tile(qm_ref, 2 * pr)] = (cols(uq, 0, 2 * pr) * scale_m).astype(BF16)
        qr = _rope(cols(uq, 0, 2 * pr + 1), cb, sb, 16)
        qm_ref[tile(qm_ref, 2 * pr + 1)] = (qr * scale_m).astype(BF16)
        km_ref[tile(km_ref, 2 * pr)] = cols(ukv, 0, pr).astype(BF16)
        km_ref[tile(km_ref, 2 * pr + 1)] = kr
    vm_ref[0] = ukv[:, 256:512].astype(BF16)


def _project(x, mod3, mod_row, g, w, tables, gq2, gk2, bd, gcq, gckv, wuq, wukv):
    b, s, d = x.shape
    tm = min(PROJ_ROWS, s)
    widths = (512, 256, 256, 256, 256, 256, 512, 512, 256)
    log2e = math.log2(math.e)
    body = functools.partial(_proj_body, scale_a=HEAD_DIM ** -0.5 * log2e,
                             scale_b=DIFF_QK_DIM ** -0.5 * log2e,
                             scale_m=(MLA_NOPE_DIM + MLA_ROPE_DIM) ** -0.5 * log2e)
    const = lambda shape: pl.BlockSpec(shape, lambda bi, i: (0,) * len(shape))
    tab = pl.BlockSpec((tm, LANES), lambda bi, i: (i, 0))
    return pl.pallas_call(
        body,
        grid=(b, s // tm),
        in_specs=[pl.BlockSpec((1, tm, d), lambda bi, i: (bi, i, 0)),
                  pl.BlockSpec((1, N_MOD, d), lambda bi, i: (mod_row(bi), 0, 0)),
                  const((1, d)), const(w.shape), tab, tab, tab, tab,
                  const((1, LANES)), const((1, LANES)), const((LANES, LANES)),
                  const(gcq.shape), const(gckv.shape), const(wuq.shape), const(wukv.shape)],
        out_specs=[pl.BlockSpec((1, tm, wd), lambda bi, i: (bi, i, 0)) for wd in widths],
        out_shape=[jax.ShapeDtypeStruct((b, s, wd), BF16) for wd in widths],
        compiler_params=_params("parallel", "arbitrary"),
        name="project",
    )(x, mod3, g, w, *tables, gq2, gk2, bd, gcq, gckv, wuq, wukv)


def _attn_step(qs, kv_refs, s_ref, mp_ref, mb_ref, acc_ref, l_ref, ones_lane):
    step = pl.program_id(0)

    @pl.when(step == 0)
    def _():
        s_ref[...] = jnp.zeros(s_ref.shape, F32)
        mp_ref[...] = jnp.zeros(mp_ref.shape, F32)
        acc_ref[...] = jnp.ones(acc_ref.shape, F32)
        l_ref[...] = jnp.ones(l_ref.shape, F32)

    raw = acc_ref[...]
    if ones_lane:
        row_sum = raw[:, HEAD_DIM:HEAD_DIM + 1]
    else:
        row_sum = jnp.sum(l_ref[...], axis=-1, keepdims=True)
    done = raw * (1.0 / row_sum)

    cur = step % 2
    mb_ref[...] = jnp.broadcast_to(jnp.max(mp_ref[1 - cur], axis=-1, keepdims=True), mb_ref.shape)
    acc = None
    l_acc = None
    chunks = [(k_ref, v_ref, slice(i * KEY_CHUNK, (i + 1) * KEY_CHUNK))
              for k_ref, v_ref in kv_refs for i in range(k_ref.shape[1] // KEY_CHUNK)]
    for c, (k_ref, v_ref, keys) in enumerate(chunks):
        s_new = lax.dot_general(qs, k_ref[0, keys, :], (((1,), (1,)), ((), ())),
                                preferred_element_type=F32)
        s_old = s_ref[c]
        m_old = mb_ref[...]
        p0 = jnp.exp2(s_old[:, :LANES] - m_old)
        p1 = jnp.exp2(s_old[:, LANES:] - m_old)
        if not ones_lane:
            l_acc = p0 + p1 if l_acc is None else l_acc + (p0 + p1)
        part = jnp.dot(jnp.concatenate([p0, p1], axis=1).astype(BF16), v_ref[0, keys, :],
                       preferred_element_type=F32)
        acc = part if acc is None else acc + part
        s_ref[c] = s_new
        mc = jnp.maximum(s_new[:, :LANES], s_new[:, LANES:])
        mp_ref[cur] = mc if c == 0 else jnp.maximum(mp_ref[cur], mc)
    acc_ref[...] = acc
    if not ones_lane:
        l_ref[...] = l_acc
    return done


def _split_kv(refs, n_kv):
    return [(refs[2 * i], refs[2 * i + 1]) for i in range(n_kv)], refs[2 * n_kv:]


def _pack_heads(even, odd):
    lo = _lane_iota(even.shape) < HEAD_DIM
    return jnp.where(lo, even, pltpu.roll(odd, HEAD_DIM, 1))


def _gqa_body(q_ref, *rest, n_kv):
    tq = q_ref.shape[1]
    lo = _lane_iota((tq, LANES)) < HEAD_DIM
    heads = []
    for j in range(2):
        t = q_ref[0, :, j * LANES:(j + 1) * LANES].astype(F32)
        heads.append(jnp.where(lo, t, 0.0))
        heads.append(jnp.where(lo, pltpu.roll(t, HEAD_DIM, 1), 0.0))
    qs = jnp.concatenate(heads, axis=0).astype(BF16)
    kv, (o_ref, *scratch) = _split_kv(rest, n_kv)
    o = _attn_step(qs, kv, *scratch, True)
    for j in range(2):
        o_ref[0, :, j * LANES:(j + 1) * LANES] = _pack_heads(
            o[(2 * j) * tq:(2 * j + 1) * tq], o[(2 * j + 1) * tq:(2 * j + 2) * tq]).astype(o_ref.dtype)


def _diff_body(q_ref, *rest, n_kv, lam_init):
    kv, (lq1_ref, lk1_ref, lq2_ref, lk2_ref, gsub_ref, o_ref, *scratch) = _split_kv(rest, n_kv)
    tq = q_ref.shape[1]
    lane = _lane_iota((tq, LANES))
    t = q_ref[0]
    zero = jnp.zeros_like(t)
    qs = jnp.concatenate([jnp.where((lane // DIFF_QK_DIM) == j, t, zero) for j in range(4)], axis=0)
    o = _attn_step(qs, kv, *scratch, False)
    lam = (jnp.exp(jnp.sum(lq1_ref[...] * lk1_ref[...], axis=-1, keepdims=True))
           - jnp.exp(jnp.sum(lq2_ref[...] * lk2_ref[...], axis=-1, keepdims=True)) + lam_init)
    lo = lane < DIFF_V_DIM
    d = jnp.where(lo, o[0:tq] - lam * o[tq:2 * tq], o[2 * tq:3 * tq] - lam * o[3 * tq:4 * tq])
    sq = d * d
    ms_lo = jnp.sum(jnp.where(lo, sq, 0.0), axis=-1, keepdims=True) * (1.0 / DIFF_V_DIM)
    ms_hi = jnp.sum(jnp.where(lo, 0.0, sq), axis=-1, keepdims=True) * (1.0 / DIFF_V_DIM)
    r = jnp.where(lo, lax.rsqrt(ms_lo + NORM_EPS), lax.rsqrt(ms_hi + NORM_EPS))
    o_ref[0] = ((d * r * gsub_ref[...]) * (1.0 - lam_init)).astype(o_ref.dtype)


def _mla_body(q_ref, *rest, n_kv):
    tq = q_ref.shape[1]
    wq = q_ref.shape[2]
    lane = _lane_iota((tq, wq))
    t = q_ref[0]
    zero = jnp.zeros_like(t)
    first = (lane < MLA_NOPE_DIM) | ((lane >= LANES) & (lane < LANES + MLA_ROPE_DIM))
    second = ((lane >= MLA_NOPE_DIM) & (lane < LANES)) | (
        (lane >= LANES + MLA_ROPE_DIM) & (lane < LANES + 2 * MLA_ROPE_DIM))
    qs = jnp.concatenate([jnp.where(first, t, zero), jnp.where(second, t, zero)], axis=0)
    kv, (o_ref, *scratch) = _split_kv(rest, n_kv)
    o = _attn_step(qs, kv, *scratch, False)
    lo = _lane_iota((tq, LANES)) < MLA_V_DIM
    o_ref[0] = jnp.where(lo, o[0:tq], o[tq:2 * tq]).astype(o_ref.dtype)


def _attention(body, q, kvs, extra, *, n_groups, wq, wk, wv, n_s, tq):
    b, q_rows, _ = q.shape
    q_tiles = q_rows // tq
    wo = wq if body is _gqa_body else LANES
    n_chunks = sum(k.shape[1] // KEY_CHUNK for k, _ in kvs)
    m_rows = n_s * tq
    n_tiles = b * n_groups * q_tiles

    def split(t):
        return t // (n_groups * q_tiles), (t // q_tiles) % n_groups, t % q_tiles

    cur = lambda j: split(jnp.minimum(j, n_tiles - 1))
    prev = lambda j: split(jnp.clip(j - 1, 0, n_tiles - 1))
    done = lambda j: split(jnp.maximum(j - 2, 0))
    kv_specs, kv_args = [], []
    for k, v in kvs:
        kv_specs += [pl.BlockSpec((1, k.shape[1], wk), lambda j: (cur(j)[0], 0, cur(j)[1])),
                     pl.BlockSpec((1, v.shape[1], wv), lambda j: (prev(j)[0], 0, prev(j)[1]))]
        kv_args += [k, v]
    extra_specs = [pl.BlockSpec(e.shape, lambda j: (0, 0)) for e in extra]
    return pl.pallas_call(
        functools.partial(body, n_kv=len(kvs)),
        grid=(n_tiles + 2,),
        in_specs=[pl.BlockSpec((1, tq, wq), lambda j: (cur(j)[0], cur(j)[2], cur(j)[1]))]
        + kv_specs + extra_specs,
        out_specs=pl.BlockSpec((1, tq, wo), lambda j: (done(j)[0], done(j)[2], done(j)[1])),
        out_shape=jax.ShapeDtypeStruct((b, q_rows, n_groups * wo), BF16),
        scratch_shapes=[pltpu.VMEM((n_chunks, m_rows, KEY_CHUNK), F32),
                        pltpu.VMEM((2, m_rows, LANES), F32),
                        pltpu.VMEM((m_rows, LANES), F32),
                        pltpu.VMEM((m_rows, LANES), F32),
                        pltpu.VMEM((m_rows, LANES), F32)],
        compiler_params=_params("arbitrary"),
        name=body.__name__.strip("_"),
    )(q, *kv_args, *extra)


def _router_gates(logits):
    lane = _lane_iota(logits.shape).astype(F32)
    neg = jnp.float32(-jnp.inf)
    z = jnp.where(lane < N_EXPERTS, logits, neg)
    m1 = jnp.max(z, axis=-1, keepdims=True)
    i1 = jnp.min(jnp.where(z == m1, lane, float(LANES)), axis=-1, keepdims=True)
    z2 = jnp.where(lane == i1, neg, z)
    m2 = jnp.max(z2, axis=-1, keepdims=True)
    i2 = jnp.min(jnp.where(z2 == m2, lane, float(LANES)), axis=-1, keepdims=True)
    e2 = jnp.exp(m2 - m1)
    den = 1.0 + e2
    return jnp.where(lane == i1, 1.0 / den, 0.0) + jnp.where(lane == i2, e2 / den, 0.0)


def _outproj_body(x_ref, oa_ref, ob_ref, om_ref, w_ref, mod_ref, g_ref, *rest, routed):
    if routed:
        wr_ref, xo_ref, h_ref, gates_ref, gates_t_ref = rest
    else:
        xo_ref, h_ref = rest
    mod = mod_ref[0]
    wa, wb = oa_ref.shape[2], ob_ref.shape[2]
    y = jnp.dot(oa_ref[0], w_ref[0:wa, :], preferred_element_type=F32)
    y += jnp.dot(ob_ref[0], w_ref[wa:wa + wb, :], preferred_element_type=F32)
    y += jnp.dot(om_ref[0], w_ref[wa + wb:, :], preferred_element_type=F32)
    x = x_ref[0] + mod[2:3] * y
    xo_ref[0] = x
    n = x * _rms_rows(x, x.shape[-1]) * g_ref[...]
    h = n * (1.0 + mod[4:5]) + mod[3:4]
    h_ref[0] = h.astype(BF16)
    if routed:
        h_hi = h.astype(BF16)
        h_lo = (h - h_hi.astype(F32)).astype(BF16)
        logits = (jnp.dot(h_hi, wr_ref[0], preferred_element_type=F32)
                  + jnp.dot(h_lo, wr_ref[0], preferred_element_type=F32)
                  + jnp.dot(h_hi, wr_ref[1], preferred_element_type=F32))
        gates = _router_gates(logits)
        gates_ref[0] = gates
        gates_t_ref[0] = gates.T[:gates_t_ref.shape[1]]


def _out_project(x, oa, ob, om, w, mod3, g, mod_row, w_router=None):
    b, s, d = x.shape
    tm = ROW_TILE
    routed = w_router is not None
    row = lambda bi, i: (bi, i, 0)
    in_specs = [pl.BlockSpec((1, tm, d), row),
                pl.BlockSpec((1, tm, oa.shape[2]), row),
                pl.BlockSpec((1, tm, ob.shape[2]), row),
                pl.BlockSpec((1, tm, om.shape[2]), row),
                pl.BlockSpec(w.shape, lambda bi, i: (0, 0)),
                pl.BlockSpec((1, N_MOD, d), lambda bi, i: (mod_row(bi), 0, 0)),
                pl.BlockSpec((1, d), lambda bi, i: (0, 0))]
    out_specs = [pl.BlockSpec((1, tm, d), row), pl.BlockSpec((1, tm, d), row)]
    out_shape = [jax.ShapeDtypeStruct((b, s, d), F32), jax.ShapeDtypeStruct((b, s, d), BF16)]
    args = [x, oa, ob, om, w, mod3, g]
    if routed:
        in_specs.append(pl.BlockSpec(w_router.shape, lambda bi, i: (0, 0, 0)))
        out_specs.append(pl.BlockSpec((1, tm, LANES), row))
        out_shape.append(jax.ShapeDtypeStruct((b, s, LANES), F32))
        out_specs.append(pl.BlockSpec((1, GATE_ROWS, tm), lambda bi, i: (bi, 0, i)))
        out_shape.append(jax.ShapeDtypeStruct((b, GATE_ROWS, s), F32))
        args.append(w_router)
    return pl.pallas_call(
        functools.partial(_outproj_body, routed=routed),
        grid=(b, s // tm),
        in_specs=in_specs, out_specs=out_specs, out_shape=out_shape,
        compiler_params=_params("parallel", "arbitrary"),
        name="out_project_routed" if routed else "out_project",
    )(*args)


def _swiglu_chunks(h, wg_ref, wu_ref, wo_ref, lead):
    acc = None
    for j in range(wo_ref.shape[len(lead)] // FFN_CHUNK):
        cols = slice(j * FFN_CHUNK, (j + 1) * FFN_CHUNK)
        gate = jnp.dot(h, wg_ref[lead + (slice(None), cols)], preferred_element_type=F32)
        up = jnp.dot(h, wu_ref[lead + (slice(None), cols)], preferred_element_type=F32)
        a = (gate * jax.nn.sigmoid(gate) * up).astype(BF16)
        part = jnp.dot(a, wo_ref[lead + (cols, slice(None))], preferred_element_type=F32)
        acc = part if acc is None else acc + part
    return acc


def _residual_out(x, gate, y, gf_ref, final_norm):
    x = x + gate * y
    if final_norm:
        x = x * _rms_rows(x, x.shape[-1]) * gf_ref[...]
    return x


def _ffn_body(x_ref, h_ref, wg_ref, wu_ref, wo_ref, mod_ref, gf_ref, o_ref, *, final_norm):
    y = _swiglu_chunks(h_ref[0], wg_ref, wu_ref, wo_ref, ())
    o_ref[0] = _residual_out(x_ref[0], mod_ref[0][5:6], y, gf_ref, final_norm)


def _dense_ffn(x, h, wg, wu, wo, mod3, mod_row, g_final, final_norm):
    b, s, d = x.shape
    tm = min(FFN_ROWS, s)
    row = lambda bi, i: (bi, i, 0)
    resident = lambda a: pl.BlockSpec(a.shape, lambda bi, i: (0, 0), pipeline_mode=pl.Buffered(1))
    return pl.pallas_call(
        functools.partial(_ffn_body, final_norm=final_norm),
        grid=(b, s // tm),
        in_specs=[pl.BlockSpec((1, tm, d), row), pl.BlockSpec((1, tm, d), row),
                  resident(wg), resident(wu), resident(wo),
                  pl.BlockSpec((1, N_MOD, d), lambda bi, i: (mod_row(bi), 0, 0)),
                  pl.BlockSpec((1, d), lambda bi, i: (0, 0))],
        out_specs=pl.BlockSpec((1, tm, d), row),
        out_shape=jax.ShapeDtypeStruct((b, s, d), F32),
        compiler_params=_params("parallel", "arbitrary"),
        name="dense_ffn",
    )(x, h, wg, wu, wo, mod3, g_final)


def _moe_body(x_ref, h_ref, gates_ref, gates_t_ref, tri_ref, tri_t_ref, wg_ref, wu_ref, wo_ref,
              mod_ref, gf_ref, o_ref, tot_ref, rank_ref, rank_t_ref, *, final_norm):
    e = pl.program_id(2)
    tm = h_ref.shape[1]
    gates = gates_ref[0]
    lane = _lane_iota(gates.shape)

    @pl.when(e == 0)
    def _():
        tot_ref[...] = jnp.zeros(tot_ref.shape, F32)
        rank_ref[...] = jnp.dot(tri_ref[...], (gates > 0.0).astype(BF16), preferred_element_type=F32)
        rank_t_ref[...] = jnp.dot((gates_t_ref[0] > 0.0).astype(BF16), tri_t_ref[...],
                                  preferred_element_type=F32)

    pick = lambda a: jnp.sum(jnp.where(lane == e, a, 0.0), axis=-1, keepdims=True)
    gate_e = pick(gates)
    rank_e = pick(rank_ref[...])
    h = h_ref[0]

    def compact():
        slots = MOE_SLOTS
        rank_row = rank_t_ref[pl.ds(e, 1), :]
        live_row = gates_t_ref[0, pl.ds(e, 1), :] > 0.0
        slot_sub = lax.broadcasted_iota(jnp.int32, (slots, tm), 0).astype(F32) + 1.0
        gather = jnp.where((rank_row == slot_sub) & live_row, 1.0, 0.0).astype(BF16)
        rows = jnp.dot(gather, h, preferred_element_type=F32).astype(BF16)
        y = _swiglu_chunks(rows, wg_ref, wu_ref, wo_ref, (0,))
        y_hi = y.astype(BF16)
        y_lo = (y - y_hi.astype(F32)).astype(BF16)
        pad = (-slots) % LANES
        if pad:
            zeros = jnp.zeros((pad, y.shape[1]), BF16)
            y_hi = jnp.concatenate([y_hi, zeros], axis=0)
            y_lo = jnp.concatenate([y_lo, zeros], axis=0)
        slot_lane = _lane_iota((tm, slots + pad)).astype(F32) + 1.0
        scatter = jnp.where((rank_e == slot_lane) & (gate_e > 0.0), 1.0, 0.0).astype(BF16)
        return (jnp.dot(scatter, y_hi, preferred_element_type=F32)
                + jnp.dot(scatter, y_lo, preferred_element_type=F32))

    def dense():
        return _swiglu_chunks(h, wg_ref, wu_ref, wo_ref, (0,))

    y = lax.cond(jnp.max(rank_e) <= float(MOE_SLOTS), compact, dense)
    tot_ref[...] += gate_e * y

    @pl.when(e == pl.num_programs(2) - 1)
    def _():
        o_ref[0] = _residual_out(x_ref[0], mod_ref[0][5:6], tot_ref[...], gf_ref, final_norm)


def _moe_ffn(x, h, gates, gates_t, wg, wu, wo, mod3, mod_row, g_final, final_norm):
    b, s, d = x.shape
    tm = min(MOE_ROWS, s)
    tri = jnp.asarray(np.tril(np.ones((tm, tm), np.float32)), BF16)
    row = lambda bi, i, e: (bi, i, 0)
    const = lambda bi, i, e: (0, 0)
    expert = lambda a: pl.BlockSpec((1,) + a.shape[1:], lambda bi, i, e: (e, 0, 0))
    return pl.pallas_call(
        functools.partial(_moe_body, final_norm=final_norm),
        grid=(b, s // tm, wg.shape[0]),
        in_specs=[pl.BlockSpec((1, tm, d), row, pipeline_mode=pl.Buffered(1)),
                  pl.BlockSpec((1, tm, d), row),
                  pl.BlockSpec((1, tm, LANES), row),
                  pl.BlockSpec((1, GATE_ROWS, tm), lambda bi, i, e: (bi, 0, i)),
                  pl.BlockSpec((tm, tm), const), pl.BlockSpec((tm, tm), const),
                  expert(wg), expert(wu), expert(wo),
                  pl.BlockSpec((1, N_MOD, d), lambda bi, i, e: (mod_row(bi), 0, 0)),
                  pl.BlockSpec((1, d), const)],
        out_specs=pl.BlockSpec((1, tm, d), row),
        out_shape=jax.ShapeDtypeStruct((b, s, d), F32),
        scratch_shapes=[pltpu.VMEM((tm, d), F32), pltpu.VMEM((tm, LANES), F32),
                        pltpu.VMEM((GATE_ROWS, tm), F32)],
        compiler_params=_params("parallel", "parallel", "arbitrary"),
        name="moe_ffn",
    )(x, h, gates, gates_t, tri, tri.T, wg, wu, wo, mod3, g_final)


def _deinterleave(n):
    return np.concatenate([np.arange(0, n, 2), np.arange(1, n, 2)])


def _in_proj_columns(pad):
    pads = lambda n: np.full((n,), pad)
    cols = []
    for hd in range(GQA_HEADS):
        cols.append(64 * hd + _deinterleave(64))
    for g in range(GQA_KV_HEADS):
        cols += [512 + 64 * g + _deinterleave(64), pads(64)]
    for g in range(GQA_KV_HEADS):
        cols += [640 + 64 * g + np.arange(64), pads(64)]
    for u in range(2 * DIFF_HEADS):
        cols.append(768 + 32 * u + _deinterleave(32))
    for u in range(2 * DIFF_HEADS):
        cols.append(1024 + 32 * u + _deinterleave(32))
    cols.append(1280 + np.arange(256))
    cols += [1536 + np.arange(MLA_Q_RANK), pads(64)]
    cols.append(1728 + np.arange(MLA_KV_RANK))
    cols += [1856 + _deinterleave(32), 1856 + _deinterleave(32), pads(64)]
    return np.concatenate(cols)


def _uq_columns(pad):
    per = MLA_NOPE_DIM + MLA_ROPE_DIM
    cols = []
    for pr in range(2):
        h0, h1 = 2 * pr, 2 * pr + 1
        cols += [per * h0 + np.arange(64), per * h1 + np.arange(64),
                 per * h0 + 64 + _deinterleave(32), per * h1 + 64 + _deinterleave(32),
                 np.full((64,), pad)]
    return np.concatenate(cols)


def _ukv_columns():
    per = MLA_NOPE_DIM + MLA_V_DIM
    k = [per * hd + np.arange(64) for hd in range(MLA_HEADS)]
    v = [per * hd + 64 + np.arange(64) for hd in range(MLA_HEADS)]
    return np.concatenate(k + v)


def _take_cols(w, cols):
    w_ext = jnp.concatenate([w, jnp.zeros((w.shape[0], 1), w.dtype)], axis=1)
    return jnp.take(w_ext, jnp.asarray(cols), axis=1)


def _rope_tables(s):
    t = jnp.arange(s)
    rows = (t // GRID_W).astype(F32)
    cols = (t % GRID_W).astype(F32)
    out = []
    for dim in (HEAD_DIM, DIFF_QK_DIM):
        quarter = dim // 4
        half = dim // 2
        inv_freq = ROPE_THETA ** (-jnp.arange(quarter, dtype=F32) / quarter)
        ang = jnp.concatenate([rows[:, None] * inv_freq, cols[:, None] * inv_freq], axis=-1)
        lane = np.arange(LANES)
        idx = (lane % dim) % half
        sign = np.where((lane % dim) < half, -1.0, 1.0).astype(np.float32)
        out += [jnp.cos(ang)[:, idx], jnp.sin(ang)[:, idx] * sign]
    return out


def kernel(x, c, ctx, c_ctx, w_mod, b_mod, g_attn, g_ffn, w_in, w_out, gqa_gq, gqa_gk,
           diff_lq1, diff_lk1, diff_lq2, diff_lk2, diff_gsub, mla_gcq, mla_gckv, mla_wuq, mla_wukv,
           ffn_w_in, ffn_w_out, moe_router, moe_w_in, moe_w_out, g_final):
    b, s, d = x.shape
    n_ctx = ctx.shape[1]
    depth = w_mod.shape[0]
    tables = _rope_tables(s)
    no_rotation = [jnp.ones((n_ctx, LANES), F32), jnp.zeros((n_ctx, LANES), F32)] * 2
    in_cols = _in_proj_columns(w_in.shape[2])
    uq_cols = _uq_columns(mla_wuq.shape[2])
    ukv_cols = _ukv_columns()
    perm64 = _deinterleave(HEAD_DIM)
    bd = jnp.asarray(np.kron(np.eye(LANES // HEAD_DIM), np.full((HEAD_DIM, HEAD_DIM), 1.0 / HEAD_DIM)), F32)

    mod_rows = 16
    c_all = jnp.zeros((mod_rows, d), F32).at[:b].set(c).at[b].set(c_ctx)
    pad_lanes = lambda v: jnp.zeros((1, LANES), F32).at[0, :v.shape[0]].set(v)

    xc = ctx
    for l in range(depth):
        need_ctx = l < depth - 1
        lam_init = 0.8 - 0.6 * math.exp(-0.3 * l)
        mod3 = _modulation(c_all, w_mod[l], b_mod[l]).reshape(mod_rows, N_MOD, d)

        w_in_p = _take_cols(w_in[l], in_cols).astype(BF16)
        wuq_p = jnp.zeros((2 * LANES, 512), F32).at[:MLA_Q_RANK].set(
            _take_cols(mla_wuq[l], uq_cols)).astype(BF16)
        wukv_p = jnp.take(mla_wukv[l], jnp.asarray(ukv_cols), axis=1).astype(BF16)
        gq2 = jnp.tile(gqa_gq[l][perm64], 2).reshape(1, LANES)
        gk2 = jnp.tile(gqa_gk[l][perm64], 2).reshape(1, LANES)
        gcq = jnp.zeros((1, 2 * LANES), F32).at[0, :MLA_Q_RANK].set(mla_gcq[l])
        gckv = mla_gckv[l].reshape(1, MLA_KV_RANK)

        proj_w = (g_attn[l].reshape(1, d), w_in_p)
        proj_aux = (gq2, gk2, bd, gcq, gckv, wuq_p, wukv_p)
        lat = _project(x, mod3, lambda bi: bi, *proj_w, tables, *proj_aux)
        ctxp = _project(xc, mod3, lambda bi: b, *proj_w, no_rotation, *proj_aux)

        diff_extra = [pad_lanes(diff_lq1[l]), pad_lanes(diff_lk1[l]), pad_lanes(diff_lq2[l]),
                      pad_lanes(diff_lk2[l]), jnp.tile(diff_gsub[l], 2).reshape(1, LANES)]
        diff_body = functools.partial(_diff_body, lam_init=lam_init)
        diff_body.__name__ = "_diff_body"

        def attend(q, streams, tq_mla):
            kv = lambda ik, iv: [(p[ik], p[iv]) for p in streams]
            oa = _attention(_gqa_body, q[0], kv(1, 2), [], n_groups=2, wq=256, wk=LANES, wv=LANES,
                            n_s=4, tq=ROW_TILE)
            ob = _attention(diff_body, q[3], kv(4, 5), diff_extra, n_groups=2, wq=LANES, wk=LANES,
                            wv=LANES, n_s=4, tq=ROW_TILE)
            om = _attention(_mla_body, q[6], kv(7, 8), [], n_groups=2, wq=256, wk=256, wv=LANES,
                            n_s=2, tq=tq_mla)
            return oa, ob, om

        w_out_b = w_out[l].astype(BF16)
        g2 = g_ffn[l].reshape(1, d)
        dense = l % 2 == 0
        if dense:
            n_hidden = ffn_w_out.shape[1]
            wg = ffn_w_in[l // 2][:, :n_hidden].astype(BF16)
            wu = ffn_w_in[l // 2][:, n_hidden:].astype(BF16)
            wo = ffn_w_out[l // 2].astype(BF16)
            w_router = None
        else:
            n_hidden = moe_w_out.shape[2]
            wi = moe_w_in[l // 2]
            wg = wi[:, :, :n_hidden].astype(BF16)
            wu = wi[:, :, n_hidden:].astype(BF16)
            wo = moe_w_out[l // 2].astype(BF16)
            wr = jnp.zeros((d, LANES), F32).at[:, :N_EXPERTS].set(moe_router[l // 2])
            wr_hi = wr.astype(BF16)
            w_router = jnp.stack([wr_hi, (wr - wr_hi.astype(F32)).astype(BF16)])
        last = l == depth - 1

        def channel_mix(xs, attn_out, mod_row, final_norm):
            res = _out_project(xs, *attn_out, w_out_b, mod3, g2, mod_row, w_router)
            gf = g_final.reshape(1, d)
            if dense:
                return _dense_ffn(res[0], res[1], wg, wu, wo, mod3, mod_row, gf, final_norm)
            return _moe_ffn(res[0], res[1], res[2], res[3], wg, wu, wo, mod3, mod_row, gf, final_norm)

        x_new = channel_mix(x, attend(lat, [lat, ctxp], 2 * ROW_TILE), lambda bi: bi, last)
        if need_ctx:
            xc = channel_mix(xc, attend(ctxp, [ctxp], ROW_TILE), lambda bi: b, False)
        x = x_new
    return x
```

```python
import functools
import math

import numpy as np
import jax
import jax.numpy as jnp
from jax import lax
from jax.experimental import pallas as pl
from jax.experimental.pallas import tpu as pltpu

LANES = 128
MXU_TILE = 256
VMEM_LIMIT = 60 * 1024 * 1024

NORM_EPS = 1e-6
ROPE_THETA = 10000.0
GRID_W = 64
N_MOD = 6
HEAD_DIM = 64
GQA_HEADS, GQA_KV_HEADS = 8, 2
DIFF_HEADS, DIFF_QK_DIM, DIFF_V_DIM = 4, 32, 64
MLA_HEADS, MLA_Q_RANK, MLA_KV_RANK = 4, 192, 128
MLA_NOPE_DIM, MLA_ROPE_DIM, MLA_V_DIM = 64, 32, 64
N_EXPERTS, TOP_K = 8, 2
GATE_ROWS = 16

ROW_TILE = 256
PROJ_ROWS = 512
KEY_CHUNK = MXU_TILE
FFN_CHUNK = MXU_TILE
FFN_ROWS = 512
MOE_ROWS = 1024
MOE_PART = 512
MOE_SLOTS = 256

F32 = jnp.float32
BF16 = jnp.bfloat16
HIGHEST = lax.Precision.HIGHEST


def _params(*sem):
    return pltpu.CompilerParams(dimension_semantics=sem, vmem_limit_bytes=VMEM_LIMIT)


def _lane_iota(shape):
    return lax.broadcasted_iota(jnp.int32, shape, len(shape) - 1)


def _mod_body(c_ref, w_ref, b_ref, o_ref):
    c = c_ref[...]
    sc = c * jax.nn.sigmoid(c)
    o_ref[...] = jnp.dot(sc, w_ref[...], precision=HIGHEST,
                         preferred_element_type=F32) + b_ref[...]


def _modulation(c_all, w, b):
    rows, d = c_all.shape
    n = w.shape[1]
    tn = 1536
    return pl.pallas_call(
        _mod_body,
        grid=(n // tn,),
        in_specs=[pl.BlockSpec((rows, d), lambda j: (0, 0)),
                  pl.BlockSpec((d, tn), lambda j: (0, j)),
                  pl.BlockSpec((1, tn), lambda j: (0, j))],
        out_specs=pl.BlockSpec((rows, tn), lambda j: (0, j)),
        out_shape=jax.ShapeDtypeStruct((rows, n), F32),
        compiler_params=_params("arbitrary"),
        name="modulation",
    )(c_all, w, b.reshape(1, n))


def _rms_rows(x, width):
    return lax.rsqrt(jnp.sum(x * x, axis=-1, keepdims=True) * (1.0 / width) + NORM_EPS)


def _rope(t, cos, sin_signed, half):
    lane = _lane_iota(t.shape)
    partner = jnp.where((lane & half) == 0,
                        pltpu.roll(t, LANES - half, 1), pltpu.roll(t, half, 1))
    return t * cos + partner * sin_signed


def _with_ones(t):
    return jnp.where(_lane_iota(t.shape) == HEAD_DIM, 1.0, t)


def _proj_body(x_ref, mod_ref, g_ref, w_ref, ca_ref, sa_ref, cb_ref, sb_ref,
               gq_ref, gk_ref, bd_ref, gcq_ref, gckv_ref, wuq_ref, wukv_ref,
               qa_ref, ka_ref, va_ref, qb_ref, kb_ref, vb_ref, qm_ref, km_ref, vm_ref,
               *, scale_a, scale_b, scale_m):
    x = x_ref[0]
    d = x.shape[-1]
    mod = mod_ref[0]
    y = x * _rms_rows(x, d) * g_ref[...]
    h = (y * (1.0 + mod[1:2]) + mod[0:1]).astype(BF16)
    p = jnp.dot(h, w_ref[...], preferred_element_type=F32)

    ca, sa, cb, sb = ca_ref[...], sa_ref[...], cb_ref[...], sb_ref[...]
    bd = bd_ref[...]
    tile = lambda ref, j: (0, slice(None), slice(j * LANES, (j + 1) * LANES))
    cols = lambda a, base, j: a[:, base + j * LANES:base + (j + 1) * LANES]

    def head_norm(t, g):
        sq = t * t
        hi = sq.astype(BF16)
        lo = (sq - hi.astype(F32)).astype(BF16)
        ms = (jnp.dot(hi, bd, preferred_element_type=F32) + jnp.dot(lo, bd, preferred_element_type=F32))
        return t * lax.rsqrt(ms + NORM_EPS) * g

    for j in range(4):
        t = head_norm(cols(p, 0, j), gq_ref[...])
        qa_ref[tile(qa_ref, j)] = (_rope(t, ca, sa, 32) * scale_a).astype(BF16)
    for j in range(2):
        t = head_norm(cols(p, 512, j), gk_ref[...])
        ka_ref[tile(ka_ref, j)] = _rope(t, ca, sa, 32).astype(BF16)
        va_ref[tile(va_ref, j)] = _with_ones(cols(p, 768, j)).astype(BF16)

    for j in range(2):
        qb_ref[tile(qb_ref, j)] = (_rope(cols(p, 1024, j), cb, sb, 16) * scale_b).astype(BF16)
        kb_ref[tile(kb_ref, j)] = _rope(cols(p, 1280, j), cb, sb, 16).astype(BF16)
    vb_ref[0] = p[:, 1536:1792].astype(BF16)

    cq = p[:, 1792:2048]
    cqn = (cq * _rms_rows(cq, MLA_Q_RANK) * gcq_ref[...]).astype(BF16)
    uq = jnp.dot(cqn, wuq_ref[...], preferred_element_type=F32)
    ckv = p[:, 2048:2176]
    ckvn = (ckv * _rms_rows(ckv, MLA_KV_RANK) * gckv_ref[...]).astype(BF16)
    ukv = jnp.dot(ckvn, wukv_ref[...], preferred_element_type=F32)
    kr = _rope(p[:, 2176:2304], cb, sb, 16).astype(BF16)
    for pr in range(2):
        qm_ref[tile(qm_ref, 2 * pr)] = (cols(uq, 0, 2 * pr) * scale_m).astype(BF16)
        qr = _rope(cols(uq, 0, 2 * pr + 1), cb, sb, 16)
        qm_ref[tile(qm_ref, 2 * pr + 1)] = (qr * scale_m).astype(BF16)
        km_ref[tile(km_ref, 2 * pr)] = cols(ukv, 0, pr).astype(BF16)
        km_ref[tile(km_ref, 2 * pr + 1)] = kr
    vm_ref[0] = ukv[:, 256:512].astype(BF16)


def _project(x, mod3, mod_row, g, w, tables, gq2, gk2, bd, gcq, gckv, wuq, wukv):
    b, s, d = x.shape
    tm = min(PROJ_ROWS, s)
    widths = (512, 256, 256, 256, 256, 256, 512, 512, 256)
    log2e = math.log2(math.e)
    body = functools.partial(_proj_body, scale_a=HEAD_DIM ** -0.5 * log2e,
                             scale_b=DIFF_QK_DIM ** -0.5 * log2e,
                             scale_m=(MLA_NOPE_DIM + MLA_ROPE_DIM) ** -0.5 * log2e)
    const = lambda shape: pl.BlockSpec(shape, lambda bi, i: (0,) * len(shape))
    tab = pl.BlockSpec((tm, LANES), lambda bi, i: (i, 0))
    return pl.pallas_call(
        body,
        grid=(b, s // tm),
        in_specs=[pl.BlockSpec((1, tm, d), lambda bi, i: (bi, i, 0)),
                  pl.BlockSpec((1, N_MOD, d), lambda bi, i: (mod_row(bi), 0, 0)),
                  const((1, d)), const(w.shape), tab, tab, tab, tab,
                  const((1, LANES)), const((1, LANES)), const((LANES, LANES)),
                  const(gcq.shape), const(gckv.shape), const(wuq.shape), const(wukv.shape)],
        out_specs=[pl.BlockSpec((1, tm, wd), lambda bi, i: (bi, i, 0)) for wd in widths],
        out_shape=[jax.ShapeDtypeStruct((b, s, wd), BF16) for wd in widths],
        compiler_params=_params("parallel", "arbitrary"),
        name="project",
    )(x, mod3, g, w, *tables, gq2, gk2, bd, gcq, gckv, wuq, wukv)


def _attn_step(qs, kv_refs, s_ref, mp_ref, mb_ref, acc_ref, l_ref, ones_lane):
    step = pl.program_id(0)

    @pl.when(step == 0)
    def _():
        s_ref[...] = jnp.zeros(s_ref.shape, F32)
        mp_ref[...] = jnp.zeros(mp_ref.shape, F32)
        acc_ref[...] = jnp.ones(acc_ref.shape, F32)
        l_ref[...] = jnp.ones(l_ref.shape, F32)

    raw = acc_ref[...]
    if ones_lane:
        row_sum = raw[:, HEAD_DIM:HEAD_DIM + 1]
    else:
        row_sum = jnp.sum(l_ref[...], axis=-1, keepdims=True)
    done = raw * (1.0 / row_sum)

    cur = step % 2
    mb_ref[...] = jnp.broadcast_to(jnp.max(mp_ref[1 - cur], axis=-1, keepdims=True), mb_ref.shape)
    acc = None
    l_acc = None
    chunks = [(k_ref, v_ref, slice(i * KEY_CHUNK, (i + 1) * KEY_CHUNK))
              for k_ref, v_ref in kv_refs for i in range(k_ref.shape[1] // KEY_CHUNK)]
    for c, (k_ref, v_ref, keys) in enumerate(chunks):
        s_new = lax.dot_general(qs, k_ref[0, keys, :], (((1,), (1,)), ((), ())),
                                preferred_element_type=F32)
        s_old = s_ref[c]
        m_old = mb_ref[...]
        p0 = jnp.exp2(s_old[:, :LANES] - m_old)
        p1 = jnp.exp2(s_old[:, LANES:] - m_old)
        if not ones_lane:
            l_acc = p0 + p1 if l_acc is None else l_acc + (p0 + p1)
        part = jnp.dot(jnp.concatenate([p0, p1], axis=1).astype(BF16), v_ref[0, keys, :],
                       preferred_element_type=F32)
        acc = part if acc is None else acc + part
        s_ref[c] = s_new
        mc = jnp.maximum(s_new[:, :LANES], s_new[:, LANES:])
        mp_ref[cur] = mc if c == 0 else jnp.maximum(mp_ref[cur], mc)
    acc_ref[...] = acc
    if not ones_lane:
        l_ref[...] = l_acc
    return done


def _split_kv(refs, n_kv):
    return [(refs[2 * i], refs[2 * i + 1]) for i in range(n_kv)], refs[2 * n_kv:]


def _pack_heads(even, odd):
    lo = _lane_iota(even.shape) < HEAD_DIM
    return jnp.where(lo, even, pltpu.roll(odd, HEAD_DIM, 1))


def _gqa_body(q_ref, *rest, n_kv):
    tq = q_ref.shape[1]
    lo = _lane_iota((tq, LANES)) < HEAD_DIM
    heads = []
    for j in range(2):
        t = q_ref[0, :, j * LANES:(j + 1) * LANES].astype(F32)
        heads.append(jnp.where(lo, t, 0.0))
        heads.append(jnp.where(lo, pltpu.roll(t, HEAD_DIM, 1), 0.0))
    qs = jnp.concatenate(heads, axis=0).astype(BF16)
    kv, (o_ref, *scratch) = _split_kv(rest, n_kv)
    o = _attn_step(qs, kv, *scratch, True)
    for j in range(2):
        o_ref[0, :, j * LANES:(j + 1) * LANES] = _pack_heads(
            o[(2 * j) * tq:(2 * j + 1) * tq], o[(2 * j + 1) * tq:(2 * j + 2) * tq]).astype(o_ref.dtype)


def _diff_body(q_ref, *rest, n_kv, lam_init):
    kv, (lq1_ref, lk1_ref, lq2_ref, lk2_ref, gsub_ref, o_ref, *scratch) = _split_kv(rest, n_kv)
    tq = q_ref.shape[1]
    lane = _lane_iota((tq, LANES))
    t = q_ref[0]
    zero = jnp.zeros_like(t)
    qs = jnp.concatenate([jnp.where((lane // DIFF_QK_DIM) == j, t, zero) for j in range(4)], axis=0)
    o = _attn_step(qs, kv, *scratch, False)
    lam = (jnp.exp(jnp.sum(lq1_ref[...] * lk1_ref[...], axis=-1, keepdims=True))
           - jnp.exp(jnp.sum(lq2_ref[...] * lk2_ref[...], axis=-1, keepdims=True)) + lam_init)
    lo = lane < DIFF_V_DIM
    d = jnp.where(lo, o[0:tq] - lam * o[tq:2 * tq], o[2 * tq:3 * tq] - lam * o[3 * tq:4 * tq])
    sq = d * d
    ms_lo = jnp.sum(jnp.where(lo, sq, 0.0), axis=-1, keepdims=True) * (1.0 / DIFF_V_DIM)
    ms_hi = jnp.sum(jnp.where(lo, 0.0, sq), axis=-1, keepdims=True) * (1.0 / DIFF_V_DIM)
    r = jnp.where(lo, lax.rsqrt(ms_lo + NORM_EPS), lax.rsqrt(ms_hi + NORM_EPS))
    o_ref[0] = ((d * r * gsub_ref[...]) * (1.0 - lam_init)).astype(o_ref.dtype)


def _mla_body(q_ref, *rest, n_kv):
    tq = q_ref.shape[1]
    wq = q_ref.shape[2]
    lane = _lane_iota((tq, wq))
    t = q_ref[0]
    zero = jnp.zeros_like(t)
    first = (lane < MLA_NOPE_DIM) | ((lane >= LANES) & (lane < LANES + MLA_ROPE_DIM))
    second = ((lane >= MLA_NOPE_DIM) & (lane < LANES)) | (
        (lane >= LANES + MLA_ROPE_DIM) & (lane < LANES + 2 * MLA_ROPE_DIM))
    qs = jnp.concatenate([jnp.where(first, t, zero), jnp.where(second, t, zero)], axis=0)
    kv, (o_ref, *scratch) = _split_kv(rest, n_kv)
    o = _attn_step(qs, kv, *scratch, False)
    lo = _lane_iota((tq, LANES)) < MLA_V_DIM
    o_ref[0] = jnp.where(lo, o[0:tq], o[tq:2 * tq]).astype(o_ref.dtype)


def _attention(body, q, kvs, extra, *, n_groups, wq, wk, wv, n_s, tq):
    b, q_rows, _ = q.shape
    q_tiles = q_rows // tq
    wo = wq if body is _gqa_body else LANES
    n_chunks = sum(k.shape[1] // KEY_CHUNK for k, _ in kvs)
    m_rows = n_s * tq
    n_tiles = b * n_groups * q_tiles

    def split(t):
        return t // (n_groups * q_tiles), (t // q_tiles) % n_groups, t % q_tiles

    cur = lambda j: split(jnp.minimum(j, n_tiles - 1))
    prev = lambda j: split(jnp.clip(j - 1, 0, n_tiles - 1))
    done = lambda j: split(jnp.maximum(j - 2, 0))
    kv_specs, kv_args = [], []
    for k, v in kvs:
        kv_specs += [pl.BlockSpec((1, k.shape[1], wk), lambda j: (cur(j)[0], 0, cur(j)[1])),
                     pl.BlockSpec((1, v.shape[1], wv), lambda j: (prev(j)[0], 0, prev(j)[1]))]
        kv_args += [k, v]
    extra_specs = [pl.BlockSpec(e.shape, lambda j: (0, 0)) for e in extra]
    return pl.pallas_call(
        functools.partial(body, n_kv=len(kvs)),
        grid=(n_tiles + 2,),
        in_specs=[pl.BlockSpec((1, tq, wq), lambda j: (cur(j)[0], cur(j)[2], cur(j)[1]))]
        + kv_specs + extra_specs,
        out_specs=pl.BlockSpec((1, tq, wo), lambda j: (done(j)[0], done(j)[2], done(j)[1])),
        out_shape=jax.ShapeDtypeStruct((b, q_rows, n_groups * wo), BF16),
        scratch_shapes=[pltpu.VMEM((n_chunks, m_rows, KEY_CHUNK), F32),
                        pltpu.VMEM((2, m_rows, LANES), F32),
                        pltpu.VMEM((m_rows, LANES), F32),
                        pltpu.VMEM((m_rows, LANES), F32),
                        pltpu.VMEM((m_rows, LANES), F32)],
        compiler_params=_params("arbitrary"),
        name=body.__name__.strip("_"),
    )(q, *kv_args, *extra)


def _router_gates(logits):
    lane = _lane_iota(logits.shape).astype(F32)
    neg = jnp.float32(-jnp.inf)
    z = jnp.where(lane < N_EXPERTS, logits, neg)
    m1 = jnp.max(z, axis=-1, keepdims=True)
    i1 = jnp.min(jnp.where(z == m1, lane, float(LANES)), axis=-1, keepdims=True)
    z2 = jnp.where(lane == i1, neg, z)
    m2 = jnp.max(z2, axis=-1, keepdims=True)
    i2 = jnp.min(jnp.where(z2 == m2, lane, float(LANES)), axis=-1, keepdims=True)
    e2 = jnp.exp(m2 - m1)
    den = 1.0 + e2
    return jnp.where(lane == i1, 1.0 / den, 0.0) + jnp.where(lane == i2, e2 / den, 0.0)


def _outproj_body(x_ref, oa_ref, ob_ref, om_ref, w_ref, mod_ref, g_ref, *rest, routed):
    if routed:
        wr_ref, xo_ref, h_ref, gates_ref, gates_t_ref = rest
    else:
        xo_ref, h_ref = rest
    mod = mod_ref[0]
    wa, wb = oa_ref.shape[2], ob_ref.shape[2]
    y = jnp.dot(oa_ref[0], w_ref[0:wa, :], preferred_element_type=F32)
    y += jnp.dot(ob_ref[0], w_ref[wa:wa + wb, :], preferred_element_type=F32)
    y += jnp.dot(om_ref[0], w_ref[wa + wb:, :], preferred_element_type=F32)
    x = x_ref[0] + mod[2:3] * y
    xo_ref[0] = x
    n = x * _rms_rows(x, x.shape[-1]) * g_ref[...]
    h = n * (1.0 + mod[4:5]) + mod[3:4]
    h_ref[0] = h.astype(BF16)
    if routed:
        h_hi = h.astype(BF16)
        h_lo = (h - h_hi.astype(F32)).astype(BF16)
        logits = (jnp.dot(h_hi, wr_ref[0], preferred_element_type=F32)
                  + jnp.dot(h_lo, wr_ref[0], preferred_element_type=F32)
                  + jnp.dot(h_hi, wr_ref[1], preferred_element_type=F32))
        gates = _router_gates(logits)
        gates_ref[0] = gates
        gates_t_ref[0] = gates.T[:gates_t_ref.shape[1]]


def _out_project(x, oa, ob, om, w, mod3, g, mod_row, w_router=None):
    b, s, d = x.shape
    tm = ROW_TILE
    routed = w_router is not None
    row = lambda bi, i: (bi, i, 0)
    in_specs = [pl.BlockSpec((1, tm, d), row),
                pl.BlockSpec((1, tm, oa.shape[2]), row),
                pl.BlockSpec((1, tm, ob.shape[2]), row),
                pl.BlockSpec((1, tm, om.shape[2]), row),
                pl.BlockSpec(w.shape, lambda bi, i: (0, 0)),
                pl.BlockSpec((1, N_MOD, d), lambda bi, i: (mod_row(bi), 0, 0)),
                pl.BlockSpec((1, d), lambda bi, i: (0, 0))]
    out_specs = [pl.BlockSpec((1, tm, d), row), pl.BlockSpec((1, tm, d), row)]
    out_shape = [jax.ShapeDtypeStruct((b, s, d), F32), jax.ShapeDtypeStruct((b, s, d), BF16)]
    args = [x, oa, ob, om, w, mod3, g]
    if routed:
        in_specs.append(pl.BlockSpec(w_router.shape, lambda bi, i: (0, 0, 0)))
        out_specs.append(pl.BlockSpec((1, tm, LANES), row))
        out_shape.append(jax.ShapeDtypeStruct((b, s, LANES), F32))
        out_specs.append(pl.BlockSpec((1, GATE_ROWS, tm), lambda bi, i: (bi, 0, i)))
        out_shape.append(jax.ShapeDtypeStruct((b, GATE_ROWS, s), F32))
        args.append(w_router)
    return pl.pallas_call(
        functools.partial(_outproj_body, routed=routed),
        grid=(b, s // tm),
        in_specs=in_specs, out_specs=out_specs, out_shape=out_shape,
        compiler_params=_params("parallel", "arbitrary"),
        name="out_project_routed" if routed else "out_project",
    )(*args)


def _swiglu_chunks(h, wg_ref, wu_ref, wo_ref, lead):
    acc = None
    for j in range(wo_ref.shape[len(lead)] // FFN_CHUNK):
        cols = slice(j * FFN_CHUNK, (j + 1) * FFN_CHUNK)
        gate = jnp.dot(h, wg_ref[lead + (slice(None), cols)], preferred_element_type=F32)
        up = jnp.dot(h, wu_ref[lead + (slice(None), cols)], preferred_element_type=F32)
        a = (gate * jax.nn.sigmoid(gate) * up).astype(BF16)
        part = jnp.dot(a, wo_ref[lead + (cols, slice(None))], preferred_element_type=F32)
        acc = part if acc is None else acc + part
    return acc


def _residual_out(x, gate, y, gf_ref, final_norm):
    x = x + gate * y
    if final_norm:
        x = x * _rms_rows(x, x.shape[-1]) * gf_ref[...]
    return x


def _ffn_body(x_ref, h_ref, wg_ref, wu_ref, wo_ref, mod_ref, gf_ref, o_ref, *, final_norm):
    y = _swiglu_chunks(h_ref[0], wg_ref, wu_ref, wo_ref, ())
    o_ref[0] = _residual_out(x_ref[0], mod_ref[0][5:6], y, gf_ref, final_norm)


def _dense_ffn(x, h, wg, wu, wo, mod3, mod_row, g_final, final_norm):
    b, s, d = x.shape
    tm = min(FFN_ROWS, s)
    row = lambda bi, i: (bi, i, 0)
    resident = lambda a: pl.BlockSpec(a.shape, lambda bi, i: (0, 0), pipeline_mode=pl.Buffered(1))
    return pl.pallas_call(
        functools.partial(_ffn_body, final_norm=final_norm),
        grid=(b, s // tm),
        in_specs=[pl.BlockSpec((1, tm, d), row), pl.BlockSpec((1, tm, d), row),
                  resident(wg), resident(wu), resident(wo),
                  pl.BlockSpec((1, N_MOD, d), lambda bi, i: (mod_row(bi), 0, 0)),
                  pl.BlockSpec((1, d), lambda bi, i: (0, 0))],
        out_specs=pl.BlockSpec((1, tm, d), row),
        out_shape=jax.ShapeDtypeStruct((b, s, d), F32),
        compiler_params=_params("parallel", "arbitrary"),
        name="dense_ffn",
    )(x, h, wg, wu, wo, mod3, g_final)


def _moe_body(x_ref, h_ref, gates_ref, gates_t_ref, tri_ref, tri_t_ref, wg_ref, wu_ref, wo_ref,
              mod_ref, gf_ref, o_ref, tot_ref, rank_ref, rank_t_ref, *, final_norm):
    e = pl.program_id(2)
    tm = h_ref.shape[1]
    parts = [slice(k * MOE_PART, (k + 1) * MOE_PART) for k in range(tm // MOE_PART)]
    gates = gates_ref[0]
    lane = _lane_iota(gates.shape)

    @pl.when(e == 0)
    def _():
        tot_ref[...] = jnp.zeros(tot_ref.shape, F32)
        live = (gates > 0.0).astype(BF16)
        live_t = (gates_t_ref[0] > 0.0).astype(BF16)
        for part in parts:
            rank_ref[part, :] = jnp.dot(tri_ref[...], live[part], preferred_element_type=F32)
            rank_t_ref[:, part] = jnp.dot(live_t[:, part], tri_t_ref[...], preferred_element_type=F32)

    pick = lambda a: jnp.sum(jnp.where(lane == e, a, 0.0), axis=-1, keepdims=True)
    gate_e = pick(gates)
    rank_e = pick(rank_ref[...])

    @pl.when(jnp.max(rank_e) <= float(MOE_SLOTS))
    def _():
        rank_row = rank_t_ref[pl.ds(e, 1), :]
        live_row = gates_t_ref[0, pl.ds(e, 1), :] > 0.0
        slot_sub = lax.broadcasted_iota(jnp.int32, (MOE_SLOTS, MOE_PART), 0).astype(F32) + 1.0
        rows = []
        for part in parts:
            gather = jnp.where((rank_row[:, part] == slot_sub) & live_row[:, part], 1.0, 0.0)
            rows.append(jnp.dot(gather.astype(BF16), h_ref[0, part, :],
                                preferred_element_type=F32).astype(BF16))
        y = _swiglu_chunks(jnp.concatenate(rows, axis=0), wg_ref, wu_ref, wo_ref, (0,))
        y_hi = y.astype(BF16)
        y_lo = (y - y_hi.astype(F32)).astype(BF16)
        slot_lane = _lane_iota((MOE_PART, MOE_SLOTS)).astype(F32) + 1.0
        for k, part in enumerate(parts):
            slots = slice(k * MOE_SLOTS, (k + 1) * MOE_SLOTS)
            scatter = jnp.where((rank_e[part] == slot_lane) & (gate_e[part] > 0.0), 1.0, 0.0).astype(BF16)
            back = (jnp.dot(scatter, y_hi[slots], preferred_element_type=F32)
                    + jnp.dot(scatter, y_lo[slots], preferred_element_type=F32))
            tot_ref[part, :] += gate_e[part] * back

    @pl.when(jnp.max(rank_e) > float(MOE_SLOTS))
    def _():
        tot_ref[...] += gate_e * _swiglu_chunks(h_ref[0], wg_ref, wu_ref, wo_ref, (0,))

    @pl.when(e == pl.num_programs(2) - 1)
    def _():
        o_ref[0] = _residual_out(x_ref[0], mod_ref[0][5:6], tot_ref[...], gf_ref, final_norm)


def _moe_ffn(x, h, gates, gates_t, wg, wu, wo, mod3, mod_row, g_final, final_norm):
    b, s, d = x.shape
    tm = min(MOE_ROWS, s)
    tp = MOE_PART
    tri = jnp.asarray(np.tril(np.ones((tp, tp), np.float32)), BF16)
    row = lambda bi, i, e: (bi, i, 0)
    const = lambda bi, i, e: (0, 0)
    expert = lambda a: pl.BlockSpec((1,) + a.shape[1:], lambda bi, i, e: (e, 0, 0))
    return pl.pallas_call(
        functools.partial(_moe_body, final_norm=final_norm),
        grid=(b, s // tm, wg.shape[0]),
        in_specs=[pl.BlockSpec((1, tm, d), row, pipeline_mode=pl.Buffered(1)),
                  pl.BlockSpec((1, tm, d), row),
                  pl.BlockSpec((1, tm, LANES), row),
                  pl.BlockSpec((1, GATE_ROWS, tm), lambda bi, i, e: (bi, 0, i)),
                  pl.BlockSpec((tp, tp), const), pl.BlockSpec((tp, tp), const),
                  expert(wg), expert(wu), expert(wo),
                  pl.BlockSpec((1, N_MOD, d), lambda bi, i, e: (mod_row(bi), 0, 0)),
                  pl.BlockSpec((1, d), const)],
        out_specs=pl.BlockSpec((1, tm, d), row),
        out_shape=jax.ShapeDtypeStruct((b, s, d), F32),
        scratch_shapes=[pltpu.VMEM((tm, d), F32), pltpu.VMEM((tm, LANES), F32),
                        pltpu.VMEM((GATE_ROWS, tm), F32)],
        compiler_params=_params("parallel", "parallel", "arbitrary"),
        name="moe_ffn",
    )(x, h, gates, gates_t, tri, tri.T, wg, wu, wo, mod3, g_final)


def _deinterleave(n):
    return np.concatenate([np.arange(0, n, 2), np.arange(1, n, 2)])


def _in_proj_columns(pad):
    pads = lambda n: np.full((n,), pad)
    cols = []
    for hd in range(GQA_HEADS):
        cols.append(64 * hd + _deinterleave(64))
    for g in range(GQA_KV_HEADS):
        cols += [512 + 64 * g + _deinterleave(64), pads(64)]
    for g in range(GQA_KV_HEADS):
        cols += [640 + 64 * g + np.arange(64), pads(64)]
    for u in range(2 * DIFF_HEADS):
        cols.append(768 + 32 * u + _deinterleave(32))
    for u in range(2 * DIFF_HEADS):
        cols.append(1024 + 32 * u + _deinterleave(32))
    cols.append(1280 + np.arange(256))
    cols += [1536 + np.arange(MLA_Q_RANK), pads(64)]
    cols.append(1728 + np.arange(MLA_KV_RANK))
    cols += [1856 + _deinterleave(32), 1856 + _deinterleave(32), pads(64)]
    return np.concatenate(cols)


def _uq_columns(pad):
    per = MLA_NOPE_DIM + MLA_ROPE_DIM
    cols = []
    for pr in range(2):
        h0, h1 = 2 * pr, 2 * pr + 1
        cols += [per * h0 + np.arange(64), per * h1 + np.arange(64),
                 per * h0 + 64 + _deinterleave(32), per * h1 + 64 + _deinterleave(32),
                 np.full((64,), pad)]
    return np.concatenate(cols)


def _ukv_columns():
    per = MLA_NOPE_DIM + MLA_V_DIM
    k = [per * hd + np.arange(64) for hd in range(MLA_HEADS)]
    v = [per * hd + 64 + np.arange(64) for hd in range(MLA_HEADS)]
    return np.concatenate(k + v)


def _take_cols(w, cols):
    w_ext = jnp.concatenate([w, jnp.zeros((w.shape[0], 1), w.dtype)], axis=1)
    return jnp.take(w_ext, jnp.asarray(cols), axis=1)


def _rope_tables(s):
    t = jnp.arange(s)
    rows = (t // GRID_W).astype(F32)
    cols = (t % GRID_W).astype(F32)
    out = []
    for dim in (HEAD_DIM, DIFF_QK_DIM):
        quarter = dim // 4
        half = dim // 2
        inv_freq = ROPE_THETA ** (-jnp.arange(quarter, dtype=F32) / quarter)
        ang = jnp.concatenate([rows[:, None] * inv_freq, cols[:, None] * inv_freq], axis=-1)
        lane = np.arange(LANES)
        idx = (lane % dim) % half
        sign = np.where((lane % dim) < half, -1.0, 1.0).astype(np.float32)
        out += [jnp.cos(ang)[:, idx], jnp.sin(ang)[:, idx] * sign]
    return out


def kernel(x, c, ctx, c_ctx, w_mod, b_mod, g_attn, g_ffn, w_in, w_out, gqa_gq, gqa_gk,
           diff_lq1, diff_lk1, diff_lq2, diff_lk2, diff_gsub, mla_gcq, mla_gckv, mla_wuq, mla_wukv,
           ffn_w_in, ffn_w_out, moe_router, moe_w_in, moe_w_out, g_final):
    b, s, d = x.shape
    n_ctx = ctx.shape[1]
    depth = w_mod.shape[0]
    tables = _rope_tables(s)
    no_rotation = [jnp.ones((n_ctx, LANES), F32), jnp.zeros((n_ctx, LANES), F32)] * 2
    in_cols = _in_proj_columns(w_in.shape[2])
    uq_cols = _uq_columns(mla_wuq.shape[2])
    ukv_cols = _ukv_columns()
    perm64 = _deinterleave(HEAD_DIM)
    bd = jnp.asarray(np.kron(np.eye(LANES // HEAD_DIM), np.full((HEAD_DIM, HEAD_DIM), 1.0 / HEAD_DIM)), F32)

    mod_rows = 16
    c_all = jnp.zeros((mod_rows, d), F32).at[:b].set(c).at[b].set(c_ctx)
    pad_lanes = lambda v: jnp.zeros((1, LANES), F32).at[0, :v.shape[0]].set(v)

    xc = ctx
    for l in range(depth):
        need_ctx = l < depth - 1
        lam_init = 0.8 - 0.6 * math.exp(-0.3 * l)
        mod3 = _modulation(c_all, w_mod[l], b_mod[l]).reshape(mod_rows, N_MOD, d)

        w_in_p = _take_cols(w_in[l], in_cols).astype(BF16)
        wuq_p = jnp.zeros((2 * LANES, 512), F32).at[:MLA_Q_RANK].set(
            _take_cols(mla_wuq[l], uq_cols)).astype(BF16)
        wukv_p = jnp.take(mla_wukv[l], jnp.asarray(ukv_cols), axis=1).astype(BF16)
        gq2 = jnp.tile(gqa_gq[l][perm64], 2).reshape(1, LANES)
        gk2 = jnp.tile(gqa_gk[l][perm64], 2).reshape(1, LANES)
        gcq = jnp.zeros((1, 2 * LANES), F32).at[0, :MLA_Q_RANK].set(mla_gcq[l])
        gckv = mla_gckv[l].reshape(1, MLA_KV_RANK)

        proj_w = (g_attn[l].reshape(1, d), w_in_p)
        proj_aux = (gq2, gk2, bd, gcq, gckv, wuq_p, wukv_p)
        lat = _project(x, mod3, lambda bi: bi, *proj_w, tables, *proj_aux)
        ctxp = _project(xc, mod3, lambda bi: b, *proj_w, no_rotation, *proj_aux)

        diff_extra = [pad_lanes(diff_lq1[l]), pad_lanes(diff_lk1[l]), pad_lanes(diff_lq2[l]),
                      pad_lanes(diff_lk2[l]), jnp.tile(diff_gsub[l], 2).reshape(1, LANES)]
        diff_body = functools.partial(_diff_body, lam_init=lam_init)
        diff_body.__name__ = "_diff_body"

        def attend(q, streams, tq_mla):
            kv = lambda ik, iv: [(p[ik], p[iv]) for p in streams]
            oa = _attention(_gqa_body, q[0], kv(1, 2), [], n_groups=2, wq=256, wk=LANES, wv=LANES,
                            n_s=4, tq=ROW_TILE)
            ob = _attention(diff_body, q[3], kv(4, 5), diff_extra, n_groups=2, wq=LANES, wk=LANES,
                            wv=LANES, n_s=4, tq=ROW_TILE)
            om = _attention(_mla_body, q[6], kv(7, 8), [], n_groups=2, wq=256, wk=256, wv=LANES,
                            n_s=2, tq=tq_mla)
            return oa, ob, om

        w_out_b = w_out[l].astype(BF16)
        g2 = g_ffn[l].reshape(1, d)
        dense = l % 2 == 0
        if dense:
            n_hidden = ffn_w_out.shape[1]
            wg = ffn_w_in[l // 2][:, :n_hidden].astype(BF16)
            wu = ffn_w_in[l // 2][:, n_hidden:].astype(BF16)
            wo = ffn_w_out[l // 2].astype(BF16)
            w_router = None
        else:
            n_hidden = moe_w_out.shape[2]
            wi = moe_w_in[l // 2]
            wg = wi[:, :, :n_hidden].astype(BF16)
            wu = wi[:, :, n_hidden:].astype(BF16)
            wo = moe_w_out[l // 2].astype(BF16)
            wr = jnp.zeros((d, LANES), F32).at[:, :N_EXPERTS].set(moe_router[l // 2])
            wr_hi = wr.astype(BF16)
            w_router = jnp.stack([wr_hi, (wr - wr_hi.astype(F32)).astype(BF16)])
        last = l == depth - 1

        def channel_mix(xs, attn_out, mod_row, final_norm):
            res = _out_project(xs, *attn_out, w_out_b, mod3, g2, mod_row, w_router)
            gf = g_final.reshape(1, d)
            if dense:
                return _dense_ffn(res[0], res[1], wg, wu, wo, mod3, mod_row, gf, final_norm)
            return _moe_ffn(res[0], res[1], res[2], res[3], wg, wu, wo, mod3, mod_row, gf, final_norm)

        x_new = channel_mix(x, attend(lat, [lat, ctxp], 2 * ROW_TILE), lambda bi: bi, last)
        if need_ctx:
            xc = channel_mix(xc, attend(ctxp, [ctxp], ROW_TILE), lambda bi: b, False)
        x = x_new
    return x
```

```python
import functools
import math

import numpy as np
import jax
import jax.numpy as jnp
from jax import lax
from jax.experimental import pallas as pl
from jax.experimental.pallas import tpu as pltpu

LANES = 128
MXU_TILE = 256
VMEM_LIMIT = 60 * 1024 * 1024

NORM_EPS = 1e-6
ROPE_THETA = 10000.0
GRID_W = 64
N_MOD = 6
HEAD_DIM = 64
GQA_HEADS, GQA_KV_HEADS = 8, 2
DIFF_HEADS, DIFF_QK_DIM, DIFF_V_DIM = 4, 32, 64
MLA_HEADS, MLA_Q_RANK, MLA_KV_RANK = 4, 192, 128
MLA_NOPE_DIM, MLA_ROPE_DIM, MLA_V_DIM = 64, 32, 64
N_EXPERTS, TOP_K = 8, 2
GATE_ROWS = 16

ROW_TILE = 256
PROJ_ROWS = 512
KEY_CHUNK = MXU_TILE
FFN_CHUNK = MXU_TILE
FFN_ROWS = 512
MOE_ROWS = 1024
MOE_PART = 512
MOE_SLOTS = 256

F32 = jnp.float32
BF16 = jnp.bfloat16
HIGHEST = lax.Precision.HIGHEST


def _params(*sem):
    return pltpu.CompilerParams(dimension_semantics=sem, vmem_limit_bytes=VMEM_LIMIT)


def _lane_iota(shape):
    return lax.broadcasted_iota(jnp.int32, shape, len(shape) - 1)


def _mod_body(c_ref, w_ref, b_ref, o_ref):
    c = c_ref[...]
    sc = c * jax.nn.sigmoid(c)
    o_ref[...] = jnp.dot(sc, w_ref[...], precision=HIGHEST,
                         preferred_element_type=F32) + b_ref[...]


def _modulation(c_all, w, b):
    rows, d = c_all.shape
    n = w.shape[1]
    tn = 1536
    return pl.pallas_call(
        _mod_body,
        grid=(n // tn,),
        in_specs=[pl.BlockSpec((rows, d), lambda j: (0, 0)),
                  pl.BlockSpec((d, tn), lambda j: (0, j)),
                  pl.BlockSpec((1, tn), lambda j: (0, j))],
        out_specs=pl.BlockSpec((rows, tn), lambda j: (0, j)),
        out_shape=jax.ShapeDtypeStruct((rows, n), F32),
        compiler_params=_params("arbitrary"),
        name="modulation",
    )(c_all, w, b.reshape(1, n))


def _rms_rows(x, width):
    return lax.rsqrt(jnp.sum(x * x, axis=-1, keepdims=True) * (1.0 / width) + NORM_EPS)


def _rope(t, cos, sin_signed, half):
    lane = _lane_iota(t.shape)
    partner = jnp.where((lane & half) == 0,
                        pltpu.roll(t, LANES - half, 1), pltpu.roll(t, half, 1))
    return t * cos + partner * sin_signed


def _with_ones(t):
    return jnp.where(_lane_iota(t.shape) == HEAD_DIM, 1.0, t)


def _proj_body(x_ref, mod_ref, g_ref, w_ref, ca_ref, sa_ref, cb_ref, sb_ref,
               gq_ref, gk_ref, bd_ref, gcq_ref, gckv_ref, wuq_ref, wukv_ref,
               qa_ref, ka_ref, va_ref, qb_ref, kb_ref, vb_ref, qm_ref, km_ref, vm_ref,
               *, scale_a, scale_b, scale_m):
    x = x_ref[0]
    d = x.shape[-1]
    mod = mod_ref[0]
    y = x * _rms_rows(x, d) * g_ref[...]
    h = (y * (1.0 + mod[1:2]) + mod[0:1]).astype(BF16)
    p = jnp.dot(h, w_ref[...], preferred_element_type=F32)

    ca, sa, cb, sb = ca_ref[...], sa_ref[...], cb_ref[...], sb_ref[...]
    bd = bd_ref[...]
    tile = lambda ref, j: (0, slice(None), slice(j * LANES, (j + 1) * LANES))
    cols = lambda a, base, j: a[:, base + j * LANES:base + (j + 1) * LANES]

    def head_norm(t, g):
        sq = t * t
        hi = sq.astype(BF16)
        lo = (sq - hi.astype(F32)).astype(BF16)
        ms = (jnp.dot(hi, bd, preferred_element_type=F32) + jnp.dot(lo, bd, preferred_element_type=F32))
        return t * lax.rsqrt(ms + NORM_EPS) * g

    for j in range(4):
        t = head_norm(cols(p, 0, j), gq_ref[...])
        qa_ref[tile(qa_ref, j)] = (_rope(t, ca, sa, 32) * scale_a).astype(BF16)
    for j in range(2):
        t = head_norm(cols(p, 512, j), gk_ref[...])
        ka_ref[tile(ka_ref, j)] = _rope(t, ca, sa, 32).astype(BF16)
        va_ref[tile(va_ref, j)] = _with_ones(cols(p, 768, j)).astype(BF16)

    for j in range(2):
        qb_ref[tile(qb_ref, j)] = (_rope(cols(p, 1024, j), cb, sb, 16) * scale_b).astype(BF16)
        kb_ref[tile(kb_ref, j)] = _rope(cols(p, 1280, j), cb, sb, 16).astype(BF16)
    for j in range(4):
        vb_ref[tile(vb_ref, j)] = _with_ones(cols(p, 1536, j)).astype(BF16)

    cq = p[:, 2048:2304]
    cqn = (cq * _rms_rows(cq, MLA_Q_RANK) * gcq_ref[...]).astype(BF16)
    uq = jnp.dot(cqn, wuq_ref[...], preferred_element_type=F32)
    ckv = p[:, 2304:2432]
    ckvn = (ckv * _rms_rows(ckv, MLA_KV_RANK) * gckv_ref[...]).astype(BF16)
    ukv = jnp.dot(ckvn, wukv_ref[...], preferred_element_type=F32)
    kr = _rope(p[:, 2432:2560], cb, sb, 16).astype(BF16)
    for pr in range(2):
        qm_ref[tile(qm_ref, 2 * pr)] = (cols(uq, 0, 2 * pr) * scale_m).astype(BF16)
        qr = _rope(cols(uq, 0, 2 * pr + 1), cb, sb, 16)
        qm_ref[tile(qm_ref, 2 * pr + 1)] = (qr * scale_m).astype(BF16)
        km_ref[tile(km_ref, 2 * pr)] = cols(ukv, 0, pr).astype(BF16)
        km_ref[tile(km_ref, 2 * pr + 1)] = kr
    for j in range(4):
        vm_ref[tile(vm_ref, j)] = _with_ones(cols(ukv, 256, j)).astype(BF16)


def _project(x, mod3, mod_row, g, w, tables, gq2, gk2, bd, gcq, gckv, wuq, wukv):
    b, s, d = x.shape
    tm = min(PROJ_ROWS, s)
    widths = (512, 256, 256, 256, 256, 512, 512, 512, 512)
    log2e = math.log2(math.e)
    body = functools.partial(_proj_body, scale_a=HEAD_DIM ** -0.5 * log2e,
                             scale_b=DIFF_QK_DIM ** -0.5 * log2e,
                             scale_m=(MLA_NOPE_DIM + MLA_ROPE_DIM) ** -0.5 * log2e)
    const = lambda shape: pl.BlockSpec(shape, lambda bi, i: (0,) * len(shape))
    tab = pl.BlockSpec((tm, LANES), lambda bi, i: (i, 0))
    return pl.pallas_call(
        body,
        grid=(b, s // tm),
        in_specs=[pl.BlockSpec((1, tm, d), lambda bi, i: (bi, i, 0)),
                  pl.BlockSpec((1, N_MOD, d), lambda bi, i: (mod_row(bi), 0, 0)),
                  const((1, d)), const(w.shape), tab, tab, tab, tab,
                  const((1, LANES)), const((1, LANES)), const((LANES, LANES)),
                  const(gcq.shape), const(gckv.shape), const(wuq.shape), const(wukv.shape)],
        out_specs=[pl.BlockSpec((1, tm, wd), lambda bi, i: (bi, i, 0)) for wd in widths],
        out_shape=[jax.ShapeDtypeStruct((b, s, wd), BF16) for wd in widths],
        compiler_params=_params("parallel", "arbitrary"),
        name="project",
    )(x, mod3, g, w, *tables, gq2, gk2, bd, gcq, gckv, wuq, wukv)


def _attn_step(qs, kv_refs, s_ref, mp_ref, mb_ref, acc_ref):
    step = pl.program_id(0)

    @pl.when(step == 0)
    def _():
        s_ref[...] = jnp.zeros(s_ref.shape, F32)
        mp_ref[...] = jnp.zeros(mp_ref.shape, F32)
        acc_ref[...] = jnp.ones(acc_ref.shape, F32)

    raw = acc_ref[...]
    done = raw * (1.0 / raw[:, HEAD_DIM:HEAD_DIM + 1])

    cur = step % 2
    mb_ref[...] = jnp.broadcast_to(jnp.max(mp_ref[1 - cur], axis=-1, keepdims=True), mb_ref.shape)
    acc = None
    chunks = [(k_ref, v_ref, slice(i * KEY_CHUNK, (i + 1) * KEY_CHUNK))
              for k_ref, v_ref in kv_refs for i in range(k_ref.shape[1] // KEY_CHUNK)]
    for c, (k_ref, v_ref, keys) in enumerate(chunks):
        s_new = lax.dot_general(qs, k_ref[0, keys, :], (((1,), (1,)), ((), ())),
                                preferred_element_type=F32)
        s_old = s_ref[c]
        m_old = mb_ref[...]
        p0 = jnp.exp2(s_old[:, :LANES] - m_old)
        p1 = jnp.exp2(s_old[:, LANES:] - m_old)
        part = jnp.dot(jnp.concatenate([p0, p1], axis=1).astype(BF16), v_ref[0, keys, :],
                       preferred_element_type=F32)
        acc = part if acc is None else acc + part
        s_ref[c] = s_new
        mc = jnp.maximum(s_new[:, :LANES], s_new[:, LANES:])
        mp_ref[cur] = mc if c == 0 else jnp.maximum(mp_ref[cur], mc)
    acc_ref[...] = acc
    return done


def _split_kv(refs, n_kv):
    return [(refs[2 * i], refs[2 * i + 1]) for i in range(n_kv)], refs[2 * n_kv:]


def _pack_heads(even, odd):
    lo = _lane_iota(even.shape) < HEAD_DIM
    return jnp.where(lo, even, pltpu.roll(odd, HEAD_DIM, 1))


def _gqa_body(q_ref, *rest, n_kv, n_tiles):
    tq = q_ref.shape[1]
    lo = _lane_iota((tq, LANES)) < HEAD_DIM
    heads = []
    for j in range(2):
        t = q_ref[0, :, j * LANES:(j + 1) * LANES].astype(F32)
        heads.append(jnp.where(lo, t, 0.0))
        heads.append(jnp.where(lo, pltpu.roll(t, HEAD_DIM, 1), 0.0))
    qs = jnp.concatenate(heads, axis=0).astype(BF16)
    kv, (o_ref, *scratch) = _split_kv(rest, n_kv)
    o = _attn_step(qs, kv, *scratch)
    for j in range(2):
        o_ref[0, :, j * LANES:(j + 1) * LANES] = _pack_heads(
            o[(2 * j) * tq:(2 * j + 1) * tq], o[(2 * j + 1) * tq:(2 * j + 2) * tq]).astype(o_ref.dtype)


def _head_parity(n_tiles):
    step = pl.program_id(0)
    return jnp.minimum(step, n_tiles - 1) % 2, jnp.maximum(step - 2, 0) % 2


def _store_head(o_ref, res, parity):
    @pl.when(parity == 0)
    def _():
        o_ref[0, :, 0:HEAD_DIM] = res[:, 0:HEAD_DIM].astype(o_ref.dtype)

    @pl.when(parity == 1)
    def _():
        o_ref[0, :, HEAD_DIM:LANES] = pltpu.roll(res, HEAD_DIM, 1)[:, HEAD_DIM:LANES].astype(o_ref.dtype)


def _diff_body(q_ref, *rest, n_kv, n_tiles, lam_init):
    kv, (lq1_ref, lk1_ref, lq2_ref, lk2_ref, gsub_ref, o_ref, *scratch) = _split_kv(rest, n_kv)
    tq = q_ref.shape[1]
    par, par_done = _head_parity(n_tiles)
    lane = _lane_iota((tq, LANES))
    t = q_ref[0]
    zero = jnp.zeros_like(t)
    qs = jnp.concatenate([jnp.where((lane // DIFF_QK_DIM) == 2 * par + j, t, zero) for j in range(2)],
                         axis=0)
    o = _attn_step(qs, kv, *scratch)
    lam = (jnp.exp(jnp.sum(lq1_ref[...] * lk1_ref[...], axis=-1, keepdims=True))
           - jnp.exp(jnp.sum(lq2_ref[...] * lk2_ref[...], axis=-1, keepdims=True)) + lam_init)
    d = o[0:tq] - lam * o[tq:2 * tq]
    ms = jnp.sum(jnp.where(lane < DIFF_V_DIM, d * d, 0.0), axis=-1, keepdims=True) * (1.0 / DIFF_V_DIM)
    _store_head(o_ref, (d * lax.rsqrt(ms + NORM_EPS) * gsub_ref[...]) * (1.0 - lam_init), par_done)


def _mla_body(q_ref, *rest, n_kv, n_tiles):
    kv, (o_ref, *scratch) = _split_kv(rest, n_kv)
    par, par_done = _head_parity(n_tiles)
    t = q_ref[0]
    lane = _lane_iota(t.shape)
    nope0 = MLA_NOPE_DIM * par
    rope0 = LANES + MLA_ROPE_DIM * par
    mine = ((lane >= nope0) & (lane < nope0 + MLA_NOPE_DIM)) | (
        (lane >= rope0) & (lane < rope0 + MLA_ROPE_DIM))
    qs = jnp.where(mine, t, jnp.zeros_like(t))
    _store_head(o_ref, _attn_step(qs, kv, *scratch), par_done)


def _attention(body, q, kvs, extra, *, n_groups, wq, wk, n_s, tq, per_head):
    b, q_rows, _ = q.shape
    q_tiles = q_rows // tq
    wo = LANES if per_head else wq
    heads = 2 if per_head else 1
    n_chunks = sum(k.shape[1] // KEY_CHUNK for k, _ in kvs)
    m_rows = n_s * tq
    n_tiles = b * n_groups * q_tiles * heads

    def split(t):
        t, par = t // heads, t % heads
        return t // (n_groups * q_tiles), (t // q_tiles) % n_groups, t % q_tiles, par

    cur = lambda j: split(jnp.minimum(j, n_tiles - 1))
    prev = lambda j: split(jnp.clip(j - 1, 0, n_tiles - 1))
    done = lambda j: split(jnp.maximum(j - 2, 0))
    kv_specs, kv_args = [], []
    for k, v in kvs:
        kv_specs += [pl.BlockSpec((1, k.shape[1], wk), lambda j: (cur(j)[0], 0, cur(j)[1])),
                     pl.BlockSpec((1, v.shape[1], LANES),
                                  lambda j: (prev(j)[0], 0, prev(j)[1] * heads + prev(j)[3]))]
        kv_args += [k, v]
    extra_specs = [pl.BlockSpec(e.shape, lambda j: (0, 0)) for e in extra]
    return pl.pallas_call(
        functools.partial(body, n_kv=len(kvs), n_tiles=n_tiles),
        grid=(n_tiles + 2,),
        in_specs=[pl.BlockSpec((1, tq, wq), lambda j: (cur(j)[0], cur(j)[2], cur(j)[1]))]
        + kv_specs + extra_specs,
        out_specs=pl.BlockSpec((1, tq, wo), lambda j: (done(j)[0], done(j)[2], done(j)[1])),
        out_shape=jax.ShapeDtypeStruct((b, q_rows, n_groups * wo), BF16),
        scratch_shapes=[pltpu.VMEM((n_chunks, m_rows, KEY_CHUNK), F32),
                        pltpu.VMEM((2, m_rows, LANES), F32),
                        pltpu.VMEM((m_rows, LANES), F32),
                        pltpu.VMEM((m_rows, LANES), F32)],
        compiler_params=_params("arbitrary"),
        name=body.__name__.strip("_"),
    )(q, *kv_args, *extra)


def _router_gates(logits):
    lane = _lane_iota(logits.shape).astype(F32)
    neg = jnp.float32(-jnp.inf)
    z = jnp.where(lane < N_EXPERTS, logits, neg)
    m1 = jnp.max(z, axis=-1, keepdims=True)
    i1 = jnp.min(jnp.where(z == m1, lane, float(LANES)), axis=-1, keepdims=True)
    z2 = jnp.where(lane == i1, neg, z)
    m2 = jnp.max(z2, axis=-1, keepdims=True)
    i2 = jnp.min(jnp.where(z2 == m2, lane, float(LANES)), axis=-1, keepdims=True)
    e2 = jnp.exp(m2 - m1)
    den = 1.0 + e2
    return jnp.where(lane == i1, 1.0 / den, 0.0) + jnp.where(lane == i2, e2 / den, 0.0)


def _outproj_body(x_ref, oa_ref, ob_ref, om_ref, w_ref, mod_ref, g_ref, *rest, routed):
    if routed:
        wr_ref, xo_ref, h_ref, gates_ref, gates_t_ref = rest
    else:
        xo_ref, h_ref = rest
    mod = mod_ref[0]
    wa, wb = oa_ref.shape[2], ob_ref.shape[2]
    y = jnp.dot(oa_ref[0], w_ref[0:wa, :], preferred_element_type=F32)
    y += jnp.dot(ob_ref[0], w_ref[wa:wa + wb, :], preferred_element_type=F32)
    y += jnp.dot(om_ref[0], w_ref[wa + wb:, :], preferred_element_type=F32)
    x = x_ref[0] + mod[2:3] * y
    xo_ref[0] = x
    n = x * _rms_rows(x, x.shape[-1]) * g_ref[...]
    h = n * (1.0 + mod[4:5]) + mod[3:4]
    h_ref[0] = h.astype(BF16)
    if routed:
        h_hi = h.astype(BF16)
        h_lo = (h - h_hi.astype(F32)).astype(BF16)
        logits = (jnp.dot(h_hi, wr_ref[0], preferred_element_type=F32)
                  + jnp.dot(h_lo, wr_ref[0], preferred_element_type=F32)
                  + jnp.dot(h_hi, wr_ref[1], preferred_element_type=F32))
        gates = _router_gates(logits)
        gates_ref[0] = gates
        gates_t_ref[0] = gates.T[:gates_t_ref.shape[1]]


def _out_project(x, oa, ob, om, w, mod3, g, mod_row, w_router=None):
    b, s, d = x.shape
    tm = ROW_TILE
    routed = w_router is not None
    row = lambda bi, i: (bi, i, 0)
    in_specs = [pl.BlockSpec((1, tm, d), row),
                pl.BlockSpec((1, tm, oa.shape[2]), row),
                pl.BlockSpec((1, tm, ob.shape[2]), row),
                pl.BlockSpec((1, tm, om.shape[2]), row),
                pl.BlockSpec(w.shape, lambda bi, i: (0, 0)),
                pl.BlockSpec((1, N_MOD, d), lambda bi, i: (mod_row(bi), 0, 0)),
                pl.BlockSpec((1, d), lambda bi, i: (0, 0))]
    out_specs = [pl.BlockSpec((1, tm, d), row), pl.BlockSpec((1, tm, d), row)]
    out_shape = [jax.ShapeDtypeStruct((b, s, d), F32), jax.ShapeDtypeStruct((b, s, d), BF16)]
    args = [x, oa, ob, om, w, mod3, g]
    if routed:
        in_specs.append(pl.BlockSpec(w_router.shape, lambda bi, i: (0, 0, 0)))
        out_specs.append(pl.BlockSpec((1, tm, LANES), row))
        out_shape.append(jax.ShapeDtypeStruct((b, s, LANES), F32))
        out_specs.append(pl.BlockSpec((1, GATE_ROWS, tm), lambda bi, i: (bi, 0, i)))
        out_shape.append(jax.ShapeDtypeStruct((b, GATE_ROWS, s), F32))
        args.append(w_router)
    return pl.pallas_call(
        functools.partial(_outproj_body, routed=routed),
        grid=(b, s // tm),
        in_specs=in_specs, out_specs=out_specs, out_shape=out_shape,
        compiler_params=_params("parallel", "arbitrary"),
        name="out_project_routed" if routed else "out_project",
    )(*args)


def _swiglu_chunks(h, wg_ref, wu_ref, wo_ref, lead):
    acc = None
    for j in range(wo_ref.shape[len(lead)] // FFN_CHUNK):
        cols = slice(j * FFN_CHUNK, (j + 1) * FFN_CHUNK)
        gate = jnp.dot(h, wg_ref[lead + (slice(None), cols)], preferred_element_type=F32)
        up = jnp.dot(h, wu_ref[lead + (slice(None), cols)], preferred_element_type=F32)
        a = (gate * jax.nn.sigmoid(gate) * up).astype(BF16)
        part = jnp.dot(a, wo_ref[lead + (cols, slice(None))], preferred_element_type=F32)
        acc = part if acc is None else acc + part
    return acc


def _residual_out(x, gate, y, gf_ref, final_norm):
    x = x + gate * y
    if final_norm:
        x = x * _rms_rows(x, x.shape[-1]) * gf_ref[...]
    return x


def _ffn_body(x_ref, h_ref, wg_ref, wu_ref, wo_ref, mod_ref, gf_ref, o_ref, *, final_norm):
    y = _swiglu_chunks(h_ref[0], wg_ref, wu_ref, wo_ref, ())
    o_ref[0] = _residual_out(x_ref[0], mod_ref[0][5:6], y, gf_ref, final_norm)


def _dense_ffn(x, h, wg, wu, wo, mod3, mod_row, g_final, final_norm):
    b, s, d = x.shape
    tm = min(FFN_ROWS, s)
    row = lambda bi, i: (bi, i, 0)
    resident = lambda a: pl.BlockSpec(a.shape, lambda bi, i: (0, 0), pipeline_mode=pl.Buffered(1))
    return pl.pallas_call(
        functools.partial(_ffn_body, final_norm=final_norm),
        grid=(b, s // tm),
        in_specs=[pl.BlockSpec((1, tm, d), row), pl.BlockSpec((1, tm, d), row),
                  resident(wg), resident(wu), resident(wo),
                  pl.BlockSpec((1, N_MOD, d), lambda bi, i: (mod_row(bi), 0, 0)),
                  pl.BlockSpec((1, d), lambda bi, i: (0, 0))],
        out_specs=pl.BlockSpec((1, tm, d), row),
        out_shape=jax.ShapeDtypeStruct((b, s, d), F32),
        compiler_params=_params("parallel", "arbitrary"),
        name="dense_ffn",
    )(x, h, wg, wu, wo, mod3, g_final)


def _moe_body(x_ref, h_ref, gates_ref, gates_t_ref, tri_ref, tri_t_ref, wg_ref, wu_ref, wo_ref,
              mod_ref, gf_ref, o_ref, tot_ref, rank_ref, rank_t_ref, *, final_norm):
    e = pl.program_id(2)
    tm = h_ref.shape[1]
    parts = [slice(k * MOE_PART, (k + 1) * MOE_PART) for k in range(tm // MOE_PART)]
    gates = gates_ref[0]
    lane = _lane_iota(gates.shape)

    @pl.when(e == 0)
    def _():
        tot_ref[...] = jnp.zeros(tot_ref.shape, F32)
        live = (gates > 0.0).astype(BF16)
        live_t = (gates_t_ref[0] > 0.0).astype(BF16)
        for part in parts:
            rank_ref[part, :] = jnp.dot(tri_ref[...], live[part], preferred_element_type=F32)
            rank_t_ref[:, part] = jnp.dot(live_t[:, part], tri_t_ref[...], preferred_element_type=F32)

    pick = lambda a: jnp.sum(jnp.where(lane == e, a, 0.0), axis=-1, keepdims=True)
    gate_e = pick(gates)
    rank_e = pick(rank_ref[...])

    @pl.when(jnp.max(rank_e) <= float(MOE_SLOTS))
    def _():
        rank_row = rank_t_ref[pl.ds(e, 1), :]
        live_row = gates_t_ref[0, pl.ds(e, 1), :] > 0.0
        slot_sub = lax.broadcasted_iota(jnp.int32, (MOE_SLOTS, MOE_PART), 0).astype(F32) + 1.0
        rows = []
        for part in parts:
            gather = jnp.where((rank_row[:, part] == slot_sub) & live_row[:, part], 1.0, 0.0)
            rows.append(jnp.dot(gather.astype(BF16), h_ref[0, part, :],
                                preferred_element_type=F32).astype(BF16))
        y = _swiglu_chunks(jnp.concatenate(rows, axis=0), wg_ref, wu_ref, wo_ref, (0,))
        y_hi = y.astype(BF16)
        y_lo = (y - y_hi.astype(F32)).astype(BF16)
        slot_lane = _lane_iota((MOE_PART, MOE_SLOTS)).astype(F32) + 1.0
        for k, part in enumerate(parts):
            slots = slice(k * MOE_SLOTS, (k + 1) * MOE_SLOTS)
            scatter = jnp.where((rank_e[part] == slot_lane) & (gate_e[part] > 0.0), 1.0, 0.0).astype(BF16)
            back = (jnp.dot(scatter, y_hi[slots], preferred_element_type=F32)
                    + jnp.dot(scatter, y_lo[slots], preferred_element_type=F32))
            tot_ref[part, :] += gate_e[part] * back

    @pl.when(jnp.max(rank_e) > float(MOE_SLOTS))
    def _():
        tot_ref[...] += gate_e * _swiglu_chunks(h_ref[0], wg_ref, wu_ref, wo_ref, (0,))

    @pl.when(e == pl.num_programs(2) - 1)
    def _():
        o_ref[0] = _residual_out(x_ref[0], mod_ref[0][5:6], tot_ref[...], gf_ref, final_norm)


def _moe_ffn(x, h, gates, gates_t, wg, wu, wo, mod3, mod_row, g_final, final_norm):
    b, s, d = x.shape
    tm = min(MOE_ROWS, s)
    tp = MOE_PART
    tri = jnp.asarray(np.tril(np.ones((tp, tp), np.float32)), BF16)
    row = lambda bi, i, e: (bi, i, 0)
    const = lambda bi, i, e: (0, 0)
    expert = lambda a: pl.BlockSpec((1,) + a.shape[1:], lambda bi, i, e: (e, 0, 0))
    return pl.pallas_call(
        functools.partial(_moe_body, final_norm=final_norm),
        grid=(b, s // tm, wg.shape[0]),
        in_specs=[pl.BlockSpec((1, tm, d), row, pipeline_mode=pl.Buffered(1)),
                  pl.BlockSpec((1, tm, d), row),
                  pl.BlockSpec((1, tm, LANES), row),
                  pl.BlockSpec((1, GATE_ROWS, tm), lambda bi, i, e: (bi, 0, i)),
                  pl.BlockSpec((tp, tp), const), pl.BlockSpec((tp, tp), const),
                  expert(wg), expert(wu), expert(wo),
                  pl.BlockSpec((1, N_MOD, d), lambda bi, i, e: (mod_row(bi), 0, 0)),
                  pl.BlockSpec((1, d), const)],
        out_specs=pl.BlockSpec((1, tm, d), row),
        out_shape=jax.ShapeDtypeStruct((b, s, d), F32),
        scratch_shapes=[pltpu.VMEM((tm, d), F32), pltpu.VMEM((tm, LANES), F32),
                        pltpu.VMEM((GATE_ROWS, tm), F32)],
        compiler_params=_params("parallel", "parallel", "arbitrary"),
        name="moe_ffn",
    )(x, h, gates, gates_t, tri, tri.T, wg, wu, wo, mod3, g_final)


def _deinterleave(n):
    return np.concatenate([np.arange(0, n, 2), np.arange(1, n, 2)])


def _in_proj_columns(pad):
    pads = lambda n: np.full((n,), pad)
    cols = []
    for hd in range(GQA_HEADS):
        cols.append(64 * hd + _deinterleave(64))
    for g in range(GQA_KV_HEADS):
        cols += [512 + 64 * g + _deinterleave(64), pads(64)]
    for g in range(GQA_KV_HEADS):
        cols += [640 + 64 * g + np.arange(64), pads(64)]
    for u in range(2 * DIFF_HEADS):
        cols.append(768 + 32 * u + _deinterleave(32))
    for u in range(2 * DIFF_HEADS):
        cols.append(1024 + 32 * u + _deinterleave(32))
    for hd in range(DIFF_HEADS):
        cols += [1280 + 64 * hd + np.arange(64), pads(64)]
    cols += [1536 + np.arange(MLA_Q_RANK), pads(64)]
    cols.append(1728 + np.arange(MLA_KV_RANK))
    cols += [1856 + _deinterleave(32), 1856 + _deinterleave(32), pads(64)]
    return np.concatenate(cols)


def _uq_columns(pad):
    per = MLA_NOPE_DIM + MLA_ROPE_DIM
    cols = []
    for pr in range(2):
        h0, h1 = 2 * pr, 2 * pr + 1
        cols += [per * h0 + np.arange(64), per * h1 + np.arange(64),
                 per * h0 + 64 + _deinterleave(32), per * h1 + 64 + _deinterleave(32),
                 np.full((64,), pad)]
    return np.concatenate(cols)


def _ukv_columns(pad):
    per = MLA_NOPE_DIM + MLA_V_DIM
    k = [per * hd + np.arange(64) for hd in range(MLA_HEADS)]
    v = []
    for hd in range(MLA_HEADS):
        v += [per * hd + 64 + np.arange(64), np.full((64,), pad)]
    return np.concatenate(k + v)


def _take_cols(w, cols):
    w_ext = jnp.concatenate([w, jnp.zeros((w.shape[0], 1), w.dtype)], axis=1)
    return jnp.take(w_ext, jnp.asarray(cols), axis=1)


def _rope_tables(s):
    t = jnp.arange(s)
    rows = (t // GRID_W).astype(F32)
    cols = (t % GRID_W).astype(F32)
    out = []
    for dim in (HEAD_DIM, DIFF_QK_DIM):
        quarter = dim // 4
        half = dim // 2
        inv_freq = ROPE_THETA ** (-jnp.arange(quarter, dtype=F32) / quarter)
        ang = jnp.concatenate([rows[:, None] * inv_freq, cols[:, None] * inv_freq], axis=-1)
        lane = np.arange(LANES)
        idx = (lane % dim) % half
        sign = np.where((lane % dim) < half, -1.0, 1.0).astype(np.float32)
        out += [jnp.cos(ang)[:, idx], jnp.sin(ang)[:, idx] * sign]
    return out


def kernel(x, c, ctx, c_ctx, w_mod, b_mod, g_attn, g_ffn, w_in, w_out, gqa_gq, gqa_gk,
           diff_lq1, diff_lk1, diff_lq2, diff_lk2, diff_gsub, mla_gcq, mla_gckv, mla_wuq, mla_wukv,
           ffn_w_in, ffn_w_out, moe_router, moe_w_in, moe_w_out, g_final):
    b, s, d = x.shape
    n_ctx = ctx.shape[1]
    depth = w_mod.shape[0]
    tables = _rope_tables(s)
    no_rotation = [jnp.ones((n_ctx, LANES), F32), jnp.zeros((n_ctx, LANES), F32)] * 2
    in_cols = _in_proj_columns(w_in.shape[2])
    uq_cols = _uq_columns(mla_wuq.shape[2])
    ukv_cols = _ukv_columns(mla_wukv.shape[2])
    perm64 = _deinterleave(HEAD_DIM)
    bd = jnp.asarray(np.kron(np.eye(LANES // HEAD_DIM), np.full((HEAD_DIM, HEAD_DIM), 1.0 / HEAD_DIM)), F32)

    mod_rows = 16
    c_all = jnp.zeros((mod_rows, d), F32).at[:b].set(c).at[b].set(c_ctx)
    pad_lanes = lambda v: jnp.zeros((1, LANES), F32).at[0, :v.shape[0]].set(v)

    xc = ctx
    for l in range(depth):
        need_ctx = l < depth - 1
        lam_init = 0.8 - 0.6 * math.exp(-0.3 * l)
        mod3 = _modulation(c_all, w_mod[l], b_mod[l]).reshape(mod_rows, N_MOD, d)

        w_in_p = _take_cols(w_in[l], in_cols).astype(BF16)
        wuq_p = jnp.zeros((2 * LANES, 512), F32).at[:MLA_Q_RANK].set(
            _take_cols(mla_wuq[l], uq_cols)).astype(BF16)
        wukv_p = _take_cols(mla_wukv[l], ukv_cols).astype(BF16)
        gq2 = jnp.tile(gqa_gq[l][perm64], 2).reshape(1, LANES)
        gk2 = jnp.tile(gqa_gk[l][perm64], 2).reshape(1, LANES)
        gcq = jnp.zeros((1, 2 * LANES), F32).at[0, :MLA_Q_RANK].set(mla_gcq[l])
        gckv = mla_gckv[l].reshape(1, MLA_KV_RANK)

        proj_w = (g_attn[l].reshape(1, d), w_in_p)
        proj_aux = (gq2, gk2, bd, gcq, gckv, wuq_p, wukv_p)
        lat = _project(x, mod3, lambda bi: bi, *proj_w, tables, *proj_aux)
        ctxp = _project(xc, mod3, lambda bi: b, *proj_w, no_rotation, *proj_aux)

        diff_extra = [pad_lanes(diff_lq1[l]), pad_lanes(diff_lk1[l]), pad_lanes(diff_lq2[l]),
                      pad_lanes(diff_lk2[l]), pad_lanes(diff_gsub[l])]
        diff_body = functools.partial(_diff_body, lam_init=lam_init)
        diff_body.__name__ = "_diff_body"

        def attend(q, streams):
            kv = lambda ik, iv: [(p[ik], p[iv]) for p in streams]
            rows = q[0].shape[1]
            stack = 4 * ROW_TILE
            oa = _attention(_gqa_body, q[0], kv(1, 2), [], n_groups=2, wq=256, wk=LANES,
                            n_s=4, tq=min(stack // 4, rows), per_head=False)
            ob = _attention(diff_body, q[3], kv(4, 5), diff_extra, n_groups=2, wq=LANES, wk=LANES,
                            n_s=2, tq=min(stack // 2, rows), per_head=True)
            om = _attention(_mla_body, q[6], kv(7, 8), [], n_groups=2, wq=256, wk=256,
                            n_s=1, tq=min(stack, rows), per_head=True)
            return oa, ob, om

        w_out_b = w_out[l].astype(BF16)
        g2 = g_ffn[l].reshape(1, d)
        dense = l % 2 == 0
        if dense:
            n_hidden = ffn_w_out.shape[1]
            wg = ffn_w_in[l // 2][:, :n_hidden].astype(BF16)
            wu = ffn_w_in[l // 2][:, n_hidden:].astype(BF16)
            wo = ffn_w_out[l // 2].astype(BF16)
            w_router = None
        else:
            n_hidden = moe_w_out.shape[2]
            wi = moe_w_in[l // 2]
            wg = wi[:, :, :n_hidden].astype(BF16)
            wu = wi[:, :, n_hidden:].astype(BF16)
            wo = moe_w_out[l // 2].astype(BF16)
            wr = jnp.zeros((d, LANES), F32).at[:, :N_EXPERTS].set(moe_router[l // 2])
            wr_hi = wr.astype(BF16)
            w_router = jnp.stack([wr_hi, (wr - wr_hi.astype(F32)).astype(BF16)])
        last = l == depth - 1

        def channel_mix(xs, attn_out, mod_row, final_norm):
            res = _out_project(xs, *attn_out, w_out_b, mod3, g2, mod_row, w_router)
            gf = g_final.reshape(1, d)
            if dense:
                return _dense_ffn(res[0], res[1], wg, wu, wo, mod3, mod_row, gf, final_norm)
            return _moe_ffn(res[0], res[1], res[2], res[3], wg, wu, wo, mod3, mod_row, gf, final_norm)

        x_new = channel_mix(x, attend(lat, [lat, ctxp]), lambda bi: bi, last)
        if need_ctx:
            xc = channel_mix(xc, attend(ctxp, [ctxp]), lambda bi: b, False)
        x = x_new
    return x
```

```python
import functools
import math

import numpy as np
import jax
import jax.numpy as jnp
from jax import lax
from jax.experimental import pallas as pl
from jax.experimental.pallas import tpu as pltpu

LANES = 128
MXU_TILE = 256
VMEM_LIMIT = 60 * 1024 * 1024

NORM_EPS = 1e-6
ROPE_THETA = 10000.0
GRID_W = 64
N_MOD = 6
HEAD_DIM = 64
GQA_HEADS, GQA_KV_HEADS = 8, 2
DIFF_HEADS, DIFF_QK_DIM, DIFF_V_DIM = 4, 32, 64
MLA_HEADS, MLA_Q_RANK, MLA_KV_RANK = 4, 192, 128
MLA_NOPE_DIM, MLA_ROPE_DIM, MLA_V_DIM = 64, 32, 64
N_EXPERTS, TOP_K = 8, 2
GATE_ROWS = 16

ROW_TILE = 256
PROJ_ROWS = 512
KEY_CHUNK = MXU_TILE
FFN_CHUNK = MXU_TILE
FFN_ROWS = 512
MOE_ROWS = 1024
MOE_PART = 512
MOE_SLOTS = 256

F32 = jnp.float32
BF16 = jnp.bfloat16
HIGHEST = lax.Precision.HIGHEST


def _params(*sem):
    return pltpu.CompilerParams(dimension_semantics=sem, vmem_limit_bytes=VMEM_LIMIT)


def _lane_iota(shape):
    return lax.broadcasted_iota(jnp.int32, shape, len(shape) - 1)


def _mod_body(c_ref, w_ref, b_ref, o_ref):
    c = c_ref[...]
    sc = c * jax.nn.sigmoid(c)
    o_ref[...] = jnp.dot(sc, w_ref[...], precision=HIGHEST,
                         preferred_element_type=F32) + b_ref[...]


def _modulation(c_all, w, b):
    rows, d = c_all.shape
    n = w.shape[1]
    tn = 1536
    return pl.pallas_call(
        _mod_body,
        grid=(n // tn,),
        in_specs=[pl.BlockSpec((rows, d), lambda j: (0, 0)),
                  pl.BlockSpec((d, tn), lambda j: (0, j)),
                  pl.BlockSpec((1, tn), lambda j: (0, j))],
        out_specs=pl.BlockSpec((rows, tn), lambda j: (0, j)),
        out_shape=jax.ShapeDtypeStruct((rows, n), F32),
        compiler_params=_params("arbitrary"),
        name="modulation",
    )(c_all, w, b.reshape(1, n))


def _rms_rows(x, width):
    return lax.rsqrt(jnp.sum(x * x, axis=-1, keepdims=True) * (1.0 / width) + NORM_EPS)


def _rope(t, cos, sin_signed, half):
    lane = _lane_iota(t.shape)
    partner = jnp.where((lane & half) == 0,
                        pltpu.roll(t, LANES - half, 1), pltpu.roll(t, half, 1))
    return t * cos + partner * sin_signed


def _with_ones(t):
    return jnp.where(_lane_iota(t.shape) == HEAD_DIM, 1.0, t)


def _proj_body(x_ref, mod_ref, g_ref, w_ref, ca_ref, sa_ref, cb_ref, sb_ref,
               gq_ref, gk_ref, bd_ref, gcq_ref, gckv_ref, wuq_ref, wukv_ref,
               qa_ref, ka_ref, va_ref, qb_ref, kb_ref, vb_ref, qm_ref, km_ref, vm_ref, p_ref,
               *, scale_a, scale_b, scale_m):
    @pl.when(pl.program_id(0) == 0)
    def _():
        p_ref[...] = jnp.zeros(p_ref.shape, F32)

    x = x_ref[0]
    d = x.shape[-1]
    mod = mod_ref[0]
    y = x * _rms_rows(x, d) * g_ref[...]
    h = (y * (1.0 + mod[1:2]) + mod[0:1]).astype(BF16)
    p_new = jnp.dot(h, w_ref[...], preferred_element_type=F32)
    p = p_ref

    ca, sa, cb, sb = ca_ref[...], sa_ref[...], cb_ref[...], sb_ref[...]
    bd = bd_ref[...]
    tile = lambda ref, j: (0, slice(None), slice(j * LANES, (j + 1) * LANES))
    cols = lambda a, base, j: a[:, base + j * LANES:base + (j + 1) * LANES]

    def head_norm(t, g):
        sq = t * t
        hi = sq.astype(BF16)
        lo = (sq - hi.astype(F32)).astype(BF16)
        ms = (jnp.dot(hi, bd, preferred_element_type=F32) + jnp.dot(lo, bd, preferred_element_type=F32))
        return t * lax.rsqrt(ms + NORM_EPS) * g

    for j in range(4):
        t = head_norm(cols(p, 0, j), gq_ref[...])
        qa_ref[tile(qa_ref, j)] = (_rope(t, ca, sa, 32) * scale_a).astype(BF16)
    for j in range(2):
        t = head_norm(cols(p, 512, j), gk_ref[...])
        ka_ref[tile(ka_ref, j)] = _rope(t, ca, sa, 32).astype(BF16)
        va_ref[tile(va_ref, j)] = _with_ones(cols(p, 768, j)).astype(BF16)

    for j in range(2):
        qb_ref[tile(qb_ref, j)] = (_rope(cols(p, 1024, j), cb, sb, 16) * scale_b).astype(BF16)
        kb_ref[tile(kb_ref, j)] = _rope(cols(p, 1280, j), cb, sb, 16).astype(BF16)
    for j in range(4):
        vb_ref[tile(vb_ref, j)] = _with_ones(cols(p, 1536, j)).astype(BF16)

    cq = p[:, 2048:2304]
    cqn = (cq * _rms_rows(cq, MLA_Q_RANK) * gcq_ref[...]).astype(BF16)
    uq = jnp.dot(cqn, wuq_ref[...], preferred_element_type=F32)
    ckv = p[:, 2304:2432]
    ckvn = (ckv * _rms_rows(ckv, MLA_KV_RANK) * gckv_ref[...]).astype(BF16)
    ukv = jnp.dot(ckvn, wukv_ref[...], preferred_element_type=F32)
    kr = _rope(p[:, 2432:2560], cb, sb, 16).astype(BF16)
    for pr in range(2):
        qm_ref[tile(qm_ref, 2 * pr)] = (cols(uq, 0, 2 * pr) * scale_m).astype(BF16)
        qr = _rope(cols(uq, 0, 2 * pr + 1), cb, sb, 16)
        qm_ref[tile(qm_ref, 2 * pr + 1)] = (qr * scale_m).astype(BF16)
        km_ref[tile(km_ref, 2 * pr)] = cols(ukv, 0, pr).astype(BF16)
        km_ref[tile(km_ref, 2 * pr + 1)] = kr
    for j in range(4):
        vm_ref[tile(vm_ref, j)] = _with_ones(cols(ukv, 256, j)).astype(BF16)
    p_ref[...] = p_new


def _project(x, mod3, mod_row, g, w, tables, gq2, gk2, bd, gcq, gckv, wuq, wukv):
    b, s, d = x.shape
    tm = min(PROJ_ROWS, s)
    nt = s // tm
    n_tiles = b * nt
    widths = (512, 256, 256, 256, 256, 512, 512, 512, 512)
    log2e = math.log2(math.e)
    body = functools.partial(_proj_body, scale_a=HEAD_DIM ** -0.5 * log2e,
                             scale_b=DIFF_QK_DIM ** -0.5 * log2e,
                             scale_m=(MLA_NOPE_DIM + MLA_ROPE_DIM) ** -0.5 * log2e)
    cur = lambda j: (jnp.minimum(j, n_tiles - 1) // nt, jnp.minimum(j, n_tiles - 1) % nt)
    prev = lambda j: (jnp.maximum(j - 1, 0) // nt, jnp.maximum(j - 1, 0) % nt)
    const = lambda shape: pl.BlockSpec(shape, lambda j: (0,) * len(shape))
    tab = pl.BlockSpec((tm, LANES), lambda j: (prev(j)[1], 0))
    return pl.pallas_call(
        body,
        grid=(n_tiles + 1,),
        in_specs=[pl.BlockSpec((1, tm, d), lambda j: (cur(j)[0], cur(j)[1], 0)),
                  pl.BlockSpec((1, N_MOD, d), lambda j: (mod_row(cur(j)[0]), 0, 0)),
                  const((1, d)), const(w.shape), tab, tab, tab, tab,
                  const((1, LANES)), const((1, LANES)), const((LANES, LANES)),
                  const(gcq.shape), const(gckv.shape), const(wuq.shape), const(wukv.shape)],
        out_specs=[pl.BlockSpec((1, tm, wd), lambda j: (prev(j)[0], prev(j)[1], 0)) for wd in widths],
        out_shape=[jax.ShapeDtypeStruct((b, s, wd), BF16) for wd in widths],
        scratch_shapes=[pltpu.VMEM((tm, w.shape[1]), F32)],
        compiler_params=_params("arbitrary"),
        name="project",
    )(x, mod3, g, w, *tables, gq2, gk2, bd, gcq, gckv, wuq, wukv)


def _attn_step(qs, kv_refs, s_ref, mp_ref, mb_ref, acc_ref):
    step = pl.program_id(0)

    @pl.when(step == 0)
    def _():
        s_ref[...] = jnp.zeros(s_ref.shape, F32)
        mp_ref[...] = jnp.zeros(mp_ref.shape, F32)
        acc_ref[...] = jnp.ones(acc_ref.shape, F32)

    raw = acc_ref[...]
    done = raw * (1.0 / raw[:, HEAD_DIM:HEAD_DIM + 1])

    cur = step % 2
    mb_ref[...] = jnp.broadcast_to(jnp.max(mp_ref[1 - cur], axis=-1, keepdims=True), mb_ref.shape)
    acc = None
    chunks = [(k_ref, v_ref, slice(i * KEY_CHUNK, (i + 1) * KEY_CHUNK))
              for k_ref, v_ref in kv_refs for i in range(k_ref.shape[1] // KEY_CHUNK)]
    for c, (k_ref, v_ref, keys) in enumerate(chunks):
        s_new = lax.dot_general(qs, k_ref[0, keys, :], (((1,), (1,)), ((), ())),
                                preferred_element_type=F32)
        s_old = s_ref[c]
        m_old = mb_ref[...]
        p0 = jnp.exp2(s_old[:, :LANES] - m_old)
        p1 = jnp.exp2(s_old[:, LANES:] - m_old)
        part = jnp.dot(jnp.concatenate([p0, p1], axis=1).astype(BF16), v_ref[0, keys, :],
                       preferred_element_type=F32)
        acc = part if acc is None else acc + part
        s_ref[c] = s_new
        mc = jnp.maximum(s_new[:, :LANES], s_new[:, LANES:])
        mp_ref[cur] = mc if c == 0 else jnp.maximum(mp_ref[cur], mc)
    acc_ref[...] = acc
    return done


def _split_kv(refs, n_kv):
    return [(refs[2 * i], refs[2 * i + 1]) for i in range(n_kv)], refs[2 * n_kv:]


def _pack_heads(even, odd):
    lo = _lane_iota(even.shape) < HEAD_DIM
    return jnp.where(lo, even, pltpu.roll(odd, HEAD_DIM, 1))


def _gqa_body(q_ref, *rest, n_kv, n_tiles):
    tq = q_ref.shape[1]
    lo = _lane_iota((tq, LANES)) < HEAD_DIM
    heads = []
    for j in range(2):
        t = q_ref[0, :, j * LANES:(j + 1) * LANES].astype(F32)
        heads.append(jnp.where(lo, t, 0.0))
        heads.append(jnp.where(lo, pltpu.roll(t, HEAD_DIM, 1), 0.0))
    qs = jnp.concatenate(heads, axis=0).astype(BF16)
    kv, (o_ref, *scratch) = _split_kv(rest, n_kv)
    o = _attn_step(qs, kv, *scratch)
    for j in range(2):
        o_ref[0, :, j * LANES:(j + 1) * LANES] = _pack_heads(
            o[(2 * j) * tq:(2 * j + 1) * tq], o[(2 * j + 1) * tq:(2 * j + 2) * tq]).astype(o_ref.dtype)


def _head_parity(n_tiles):
    step = pl.program_id(0)
    return jnp.minimum(step, n_tiles - 1) % 2, jnp.maximum(step - 2, 0) % 2


def _store_head(o_ref, res, parity):
    @pl.when(parity == 0)
    def _():
        o_ref[0, :, 0:HEAD_DIM] = res[:, 0:HEAD_DIM].astype(o_ref.dtype)

    @pl.when(parity == 1)
    def _():
        o_ref[0, :, HEAD_DIM:LANES] = pltpu.roll(res, HEAD_DIM, 1)[:, HEAD_DIM:LANES].astype(o_ref.dtype)


def _diff_body(q_ref, *rest, n_kv, n_tiles, lam_init):
    kv, (lq1_ref, lk1_ref, lq2_ref, lk2_ref, gsub_ref, o_ref, *scratch) = _split_kv(rest, n_kv)
    tq = q_ref.shape[1]
    par, par_done = _head_parity(n_tiles)
    lane = _lane_iota((tq, LANES))
    t = q_ref[0]
    zero = jnp.zeros_like(t)
    qs = jnp.concatenate([jnp.where((lane // DIFF_QK_DIM) == 2 * par + j, t, zero) for j in range(2)],
                         axis=0)
    o = _attn_step(qs, kv, *scratch)
    lam = (jnp.exp(jnp.sum(lq1_ref[...] * lk1_ref[...], axis=-1, keepdims=True))
           - jnp.exp(jnp.sum(lq2_ref[...] * lk2_ref[...], axis=-1, keepdims=True)) + lam_init)
    d = o[0:tq] - lam * o[tq:2 * tq]
    ms = jnp.sum(jnp.where(lane < DIFF_V_DIM, d * d, 0.0), axis=-1, keepdims=True) * (1.0 / DIFF_V_DIM)
    _store_head(o_ref, (d * lax.rsqrt(ms + NORM_EPS) * gsub_ref[...]) * (1.0 - lam_init), par_done)


def _mla_body(q_ref, *rest, n_kv, n_tiles):
    kv, (o_ref, *scratch) = _split_kv(rest, n_kv)
    par, par_done = _head_parity(n_tiles)
    t = q_ref[0]
    lane = _lane_iota(t.shape)
    nope0 = MLA_NOPE_DIM * par
    rope0 = LANES + MLA_ROPE_DIM * par
    mine = ((lane >= nope0) & (lane < nope0 + MLA_NOPE_DIM)) | (
        (lane >= rope0) & (lane < rope0 + MLA_ROPE_DIM))
    qs = jnp.where(mine, t, jnp.zeros_like(t))
    _store_head(o_ref, _attn_step(qs, kv, *scratch), par_done)


def _attention(body, q, kvs, extra, *, n_groups, wq, wk, n_s, tq, per_head):
    b, q_rows, _ = q.shape
    q_tiles = q_rows // tq
    wo = LANES if per_head else wq
    heads = 2 if per_head else 1
    n_chunks = sum(k.shape[1] // KEY_CHUNK for k, _ in kvs)
    m_rows = n_s * tq
    n_tiles = b * n_groups * q_tiles * heads

    def split(t):
        t, par = t // heads, t % heads
        return t // (n_groups * q_tiles), (t // q_tiles) % n_groups, t % q_tiles, par

    cur = lambda j: split(jnp.minimum(j, n_tiles - 1))
    prev = lambda j: split(jnp.clip(j - 1, 0, n_tiles - 1))
    done = lambda j: split(jnp.maximum(j - 2, 0))
    kv_specs, kv_args = [], []
    for k, v in kvs:
        kv_specs += [pl.BlockSpec((1, k.shape[1], wk), lambda j: (cur(j)[0], 0, cur(j)[1])),
                     pl.BlockSpec((1, v.shape[1], LANES),
                                  lambda j: (prev(j)[0], 0, prev(j)[1] * heads + prev(j)[3]))]
        kv_args += [k, v]
    extra_specs = [pl.BlockSpec(e.shape, lambda j: (0, 0)) for e in extra]
    return pl.pallas_call(
        functools.partial(body, n_kv=len(kvs), n_tiles=n_tiles),
        grid=(n_tiles + 2,),
        in_specs=[pl.BlockSpec((1, tq, wq), lambda j: (cur(j)[0], cur(j)[2], cur(j)[1]))]
        + kv_specs + extra_specs,
        out_specs=pl.BlockSpec((1, tq, wo), lambda j: (done(j)[0], done(j)[2], done(j)[1])),
        out_shape=jax.ShapeDtypeStruct((b, q_rows, n_groups * wo), BF16),
        scratch_shapes=[pltpu.VMEM((n_chunks, m_rows, KEY_CHUNK), F32),
                        pltpu.VMEM((2, m_rows, LANES), F32),
                        pltpu.VMEM((m_rows, LANES), F32),
                        pltpu.VMEM((m_rows, LANES), F32)],
        compiler_params=_params("arbitrary"),
        name=body.__name__.strip("_"),
    )(q, *kv_args, *extra)


def _router_gates(logits):
    lane = _lane_iota(logits.shape).astype(F32)
    neg = jnp.float32(-jnp.inf)
    z = jnp.where(lane < N_EXPERTS, logits, neg)
    m1 = jnp.max(z, axis=-1, keepdims=True)
    i1 = jnp.min(jnp.where(z == m1, lane, float(LANES)), axis=-1, keepdims=True)
    z2 = jnp.where(lane == i1, neg, z)
    m2 = jnp.max(z2, axis=-1, keepdims=True)
    i2 = jnp.min(jnp.where(z2 == m2, lane, float(LANES)), axis=-1, keepdims=True)
    e2 = jnp.exp(m2 - m1)
    den = 1.0 + e2
    return jnp.where(lane == i1, 1.0 / den, 0.0) + jnp.where(lane == i2, e2 / den, 0.0)


def _outproj_body(x_ref, oa_ref, ob_ref, om_ref, w_ref, mod_ref, g_ref, *rest, routed):
    if routed:
        wr_ref, xo_ref, h_ref, gates_ref, gates_t_ref = rest
    else:
        xo_ref, h_ref = rest
    mod = mod_ref[0]
    wa, wb = oa_ref.shape[2], ob_ref.shape[2]
    y = jnp.dot(oa_ref[0], w_ref[0:wa, :], preferred_element_type=F32)
    y += jnp.dot(ob_ref[0], w_ref[wa:wa + wb, :], preferred_element_type=F32)
    y += jnp.dot(om_ref[0], w_ref[wa + wb:, :], preferred_element_type=F32)
    x = x_ref[0] + mod[2:3] * y
    xo_ref[0] = x
    n = x * _rms_rows(x, x.shape[-1]) * g_ref[...]
    h = n * (1.0 + mod[4:5]) + mod[3:4]
    h_ref[0] = h.astype(BF16)
    if routed:
        h_hi = h.astype(BF16)
        h_lo = (h - h_hi.astype(F32)).astype(BF16)
        logits = (jnp.dot(h_hi, wr_ref[0], preferred_element_type=F32)
                  + jnp.dot(h_lo, wr_ref[0], preferred_element_type=F32)
                  + jnp.dot(h_hi, wr_ref[1], preferred_element_type=F32))
        gates = _router_gates(logits)
        gates_ref[0] = gates
        gates_t_ref[0] = gates.T[:gates_t_ref.shape[1]]


def _out_project(x, oa, ob, om, w, mod3, g, mod_row, w_router=None):
    b, s, d = x.shape
    tm = ROW_TILE
    routed = w_router is not None
    row = lambda bi, i: (bi, i, 0)
    in_specs = [pl.BlockSpec((1, tm, d), row),
                pl.BlockSpec((1, tm, oa.shape[2]), row),
                pl.BlockSpec((1, tm, ob.shape[2]), row),
                pl.BlockSpec((1, tm, om.shape[2]), row),
                pl.BlockSpec(w.shape, lambda bi, i: (0, 0)),
                pl.BlockSpec((1, N_MOD, d), lambda bi, i: (mod_row(bi), 0, 0)),
                pl.BlockSpec((1, d), lambda bi, i: (0, 0))]
    out_specs = [pl.BlockSpec((1, tm, d), row), pl.BlockSpec((1, tm, d), row)]
    out_shape = [jax.ShapeDtypeStruct((b, s, d), F32), jax.ShapeDtypeStruct((b, s, d), BF16)]
    args = [x, oa, ob, om, w, mod3, g]
    if routed:
        in_specs.append(pl.BlockSpec(w_router.shape, lambda bi, i: (0, 0, 0)))
        out_specs.append(pl.BlockSpec((1, tm, LANES), row))
        out_shape.append(jax.ShapeDtypeStruct((b, s, LANES), F32))
        out_specs.append(pl.BlockSpec((1, GATE_ROWS, tm), lambda bi, i: (bi, 0, i)))
        out_shape.append(jax.ShapeDtypeStruct((b, GATE_ROWS, s), F32))
        args.append(w_router)
    return pl.pallas_call(
        functools.partial(_outproj_body, routed=routed),
        grid=(b, s // tm),
        in_specs=in_specs, out_specs=out_specs, out_shape=out_shape,
        compiler_params=_params("parallel", "arbitrary"),
        name="out_project_routed" if routed else "out_project",
    )(*args)


def _swiglu_chunks(h, wg_ref, wu_ref, wo_ref, lead):
    acc = None
    for j in range(wo_ref.shape[len(lead)] // FFN_CHUNK):
        cols = slice(j * FFN_CHUNK, (j + 1) * FFN_CHUNK)
        gate = jnp.dot(h, wg_ref[lead + (slice(None), cols)], preferred_element_type=F32)
        up = jnp.dot(h, wu_ref[lead + (slice(None), cols)], preferred_element_type=F32)
        a = (gate * jax.nn.sigmoid(gate) * up).astype(BF16)
        part = jnp.dot(a, wo_ref[lead + (cols, slice(None))], preferred_element_type=F32)
        acc = part if acc is None else acc + part
    return acc


def _residual_out(x, gate, y, gf_ref, final_norm):
    x = x + gate * y
    if final_norm:
        x = x * _rms_rows(x, x.shape[-1]) * gf_ref[...]
    return x


def _ffn_body(x_ref, h_ref, wg_ref, wu_ref, wo_ref, mod_ref, gf_ref, o_ref, *, final_norm):
    y = _swiglu_chunks(h_ref[0], wg_ref, wu_ref, wo_ref, ())
    o_ref[0] = _residual_out(x_ref[0], mod_ref[0][5:6], y, gf_ref, final_norm)


def _dense_ffn(x, h, wg, wu, wo, mod3, mod_row, g_final, final_norm):
    b, s, d = x.shape
    tm = min(FFN_ROWS, s)
    row = lambda bi, i: (bi, i, 0)
    resident = lambda a: pl.BlockSpec(a.shape, lambda bi, i: (0, 0), pipeline_mode=pl.Buffered(1))
    return pl.pallas_call(
        functools.partial(_ffn_body, final_norm=final_norm),
        grid=(b, s // tm),
        in_specs=[pl.BlockSpec((1, tm, d), row), pl.BlockSpec((1, tm, d), row),
                  resident(wg), resident(wu), resident(wo),
                  pl.BlockSpec((1, N_MOD, d), lambda bi, i: (mod_row(bi), 0, 0)),
                  pl.BlockSpec((1, d), lambda bi, i: (0, 0))],
        out_specs=pl.BlockSpec((1, tm, d), row),
        out_shape=jax.ShapeDtypeStruct((b, s, d), F32),
        compiler_params=_params("parallel", "arbitrary"),
        name="dense_ffn",
    )(x, h, wg, wu, wo, mod3, g_final)


def _moe_body(x_ref, h_ref, gates_ref, gates_t_ref, tri_ref, tri_t_ref, wg_ref, wu_ref, wo_ref,
              mod_ref, gf_ref, o_ref, tot_ref, rank_ref, rank_t_ref, *, final_norm):
    e = pl.program_id(2)
    tm = h_ref.shape[1]
    parts = [slice(k * MOE_PART, (k + 1) * MOE_PART) for k in range(tm // MOE_PART)]
    gates = gates_ref[0]
    lane = _lane_iota(gates.shape)

    @pl.when(e == 0)
    def _():
        tot_ref[...] = jnp.zeros(tot_ref.shape, F32)
        live = (gates > 0.0).astype(BF16)
        live_t = (gates_t_ref[0] > 0.0).astype(BF16)
        for part in parts:
            rank_ref[part, :] = jnp.dot(tri_ref[...], live[part], preferred_element_type=F32)
            rank_t_ref[:, part] = jnp.dot(live_t[:, part], tri_t_ref[...], preferred_element_type=F32)

    pick = lambda a: jnp.sum(jnp.where(lane == e, a, 0.0), axis=-1, keepdims=True)
    gate_e = pick(gates)
    rank_e = pick(rank_ref[...])

    @pl.when(jnp.max(rank_e) <= float(MOE_SLOTS))
    def _():
        rank_row = rank_t_ref[pl.ds(e, 1), :]
        live_row = gates_t_ref[0, pl.ds(e, 1), :] > 0.0
        slot_sub = lax.broadcasted_iota(jnp.int32, (MOE_SLOTS, MOE_PART), 0).astype(F32) + 1.0
        rows = []
        for part in parts:
            gather = jnp.where((rank_row[:, part] == slot_sub) & live_row[:, part], 1.0, 0.0)
            rows.append(jnp.dot(gather.astype(BF16), h_ref[0, part, :],
                                preferred_element_type=F32).astype(BF16))
        y = _swiglu_chunks(jnp.concatenate(rows, axis=0), wg_ref, wu_ref, wo_ref, (0,))
        y_hi = y.astype(BF16)
        y_lo = (y - y_hi.astype(F32)).astype(BF16)
        slot_lane = _lane_iota((MOE_PART, MOE_SLOTS)).astype(F32) + 1.0
        for k, part in enumerate(parts):
            slots = slice(k * MOE_SLOTS, (k + 1) * MOE_SLOTS)
            scatter = jnp.where((rank_e[part] == slot_lane) & (gate_e[part] > 0.0), 1.0, 0.0).astype(BF16)
            back = (jnp.dot(scatter, y_hi[slots], preferred_element_type=F32)
                    + jnp.dot(scatter, y_lo[slots], preferred_element_type=F32))
            tot_ref[part, :] += gate_e[part] * back

    @pl.when(jnp.max(rank_e) > float(MOE_SLOTS))
    def _():
        tot_ref[...] += gate_e * _swiglu_chunks(h_ref[0], wg_ref, wu_ref, wo_ref, (0,))

    @pl.when(e == pl.num_programs(2) - 1)
    def _():
        o_ref[0] = _residual_out(x_ref[0], mod_ref[0][5:6], tot_ref[...], gf_ref, final_norm)


def _moe_ffn(x, h, gates, gates_t, wg, wu, wo, mod3, mod_row, g_final, final_norm):
    b, s, d = x.shape
    tm = min(MOE_ROWS, s)
    tp = MOE_PART
    tri = jnp.asarray(np.tril(np.ones((tp, tp), np.float32)), BF16)
    row = lambda bi, i, e: (bi, i, 0)
    const = lambda bi, i, e: (0, 0)
    expert = lambda a: pl.BlockSpec((1,) + a.shape[1:], lambda bi, i, e: (e, 0, 0))
    return pl.pallas_call(
        functools.partial(_moe_body, final_norm=final_norm),
        grid=(b, s // tm, wg.shape[0]),
        in_specs=[pl.BlockSpec((1, tm, d), row, pipeline_mode=pl.Buffered(1)),
                  pl.BlockSpec((1, tm, d), row),
                  pl.BlockSpec((1, tm, LANES), row),
                  pl.BlockSpec((1, GATE_ROWS, tm), lambda bi, i, e: (bi, 0, i)),
                  pl.BlockSpec((tp, tp), const), pl.BlockSpec((tp, tp), const),
                  expert(wg), expert(wu), expert(wo),
                  pl.BlockSpec((1, N_MOD, d), lambda bi, i, e: (mod_row(bi), 0, 0)),
                  pl.BlockSpec((1, d), const)],
        out_specs=pl.BlockSpec((1, tm, d), row),
        out_shape=jax.ShapeDtypeStruct((b, s, d), F32),
        scratch_shapes=[pltpu.VMEM((tm, d), F32), pltpu.VMEM((tm, LANES), F32),
                        pltpu.VMEM((GATE_ROWS, tm), F32)],
        compiler_params=_params("parallel", "parallel", "arbitrary"),
        name="moe_ffn",
    )(x, h, gates, gates_t, tri, tri.T, wg, wu, wo, mod3, g_final)


def _deinterleave(n):
    return np.concatenate([np.arange(0, n, 2), np.arange(1, n, 2)])


def _in_proj_columns(pad):
    pads = lambda n: np.full((n,), pad)
    cols = []
    for hd in range(GQA_HEADS):
        cols.append(64 * hd + _deinterleave(64))
    for g in range(GQA_KV_HEADS):
        cols += [512 + 64 * g + _deinterleave(64), pads(64)]
    for g in range(GQA_KV_HEADS):
        cols += [640 + 64 * g + np.arange(64), pads(64)]
    for u in range(2 * DIFF_HEADS):
        cols.append(768 + 32 * u + _deinterleave(32))
    for u in range(2 * DIFF_HEADS):
        cols.append(1024 + 32 * u + _deinterleave(32))
    for hd in range(DIFF_HEADS):
        cols += [1280 + 64 * hd + np.arange(64), pads(64)]
    cols += [1536 + np.arange(MLA_Q_RANK), pads(64)]
    cols.append(1728 + np.arange(MLA_KV_RANK))
    cols += [1856 + _deinterleave(32), 1856 + _deinterleave(32), pads(64)]
    return np.concatenate(cols)


def _uq_columns(pad):
    per = MLA_NOPE_DIM + MLA_ROPE_DIM
    cols = []
    for pr in range(2):
        h0, h1 = 2 * pr, 2 * pr + 1
        cols += [per * h0 + np.arange(64), per * h1 + np.arange(64),
                 per * h0 + 64 + _deinterleave(32), per * h1 + 64 + _deinterleave(32),
                 np.full((64,), pad)]
    return np.concatenate(cols)


def _ukv_columns(pad):
    per = MLA_NOPE_DIM + MLA_V_DIM
    k = [per * hd + np.arange(64) for hd in range(MLA_HEADS)]
    v = []
    for hd in range(MLA_HEADS):
        v += [per * hd + 64 + np.arange(64), np.full((64,), pad)]
    return np.concatenate(k + v)


def _take_cols(w, cols):
    w_ext = jnp.concatenate([w, jnp.zeros((w.shape[0], 1), w.dtype)], axis=1)
    return jnp.take(w_ext, jnp.asarray(cols), axis=1)


def _rope_tables(s):
    t = jnp.arange(s)
    rows = (t // GRID_W).astype(F32)
    cols = (t % GRID_W).astype(F32)
    out = []
    for dim in (HEAD_DIM, DIFF_QK_DIM):
        quarter = dim // 4
        half = dim // 2
        inv_freq = ROPE_THETA ** (-jnp.arange(quarter, dtype=F32) / quarter)
        ang = jnp.concatenate([rows[:, None] * inv_freq, cols[:, None] * inv_freq], axis=-1)
        lane = np.arange(LANES)
        idx = (lane % dim) % half
        sign = np.where((lane % dim) < half, -1.0, 1.0).astype(np.float32)
        out += [jnp.cos(ang)[:, idx], jnp.sin(ang)[:, idx] * sign]
    return out


def kernel(x, c, ctx, c_ctx, w_mod, b_mod, g_attn, g_ffn, w_in, w_out, gqa_gq, gqa_gk,
           diff_lq1, diff_lk1, diff_lq2, diff_lk2, diff_gsub, mla_gcq, mla_gckv, mla_wuq, mla_wukv,
           ffn_w_in, ffn_w_out, moe_router, moe_w_in, moe_w_out, g_final):
    b, s, d = x.shape
    n_ctx = ctx.shape[1]
    depth = w_mod.shape[0]
    tables = _rope_tables(s)
    no_rotation = [jnp.ones((n_ctx, LANES), F32), jnp.zeros((n_ctx, LANES), F32)] * 2
    in_cols = _in_proj_columns(w_in.shape[2])
    uq_cols = _uq_columns(mla_wuq.shape[2])
    ukv_cols = _ukv_columns(mla_wukv.shape[2])
    perm64 = _deinterleave(HEAD_DIM)
    bd = jnp.asarray(np.kron(np.eye(LANES // HEAD_DIM), np.full((HEAD_DIM, HEAD_DIM), 1.0 / HEAD_DIM)), F32)

    mod_rows = 16
    c_all = jnp.zeros((mod_rows, d), F32).at[:b].set(c).at[b].set(c_ctx)
    pad_lanes = lambda v: jnp.zeros((1, LANES), F32).at[0, :v.shape[0]].set(v)

    xc = ctx
    for l in range(depth):
        need_ctx = l < depth - 1
        lam_init = 0.8 - 0.6 * math.exp(-0.3 * l)
        mod3 = _modulation(c_all, w_mod[l], b_mod[l]).reshape(mod_rows, N_MOD, d)

        w_in_p = _take_cols(w_in[l], in_cols).astype(BF16)
        wuq_p = jnp.zeros((2 * LANES, 512), F32).at[:MLA_Q_RANK].set(
            _take_cols(mla_wuq[l], uq_cols)).astype(BF16)
        wukv_p = _take_cols(mla_wukv[l], ukv_cols).astype(BF16)
        gq2 = jnp.tile(gqa_gq[l][perm64], 2).reshape(1, LANES)
        gk2 = jnp.tile(gqa_gk[l][perm64], 2).reshape(1, LANES)
        gcq = jnp.zeros((1, 2 * LANES), F32).at[0, :MLA_Q_RANK].set(mla_gcq[l])
        gckv = mla_gckv[l].reshape(1, MLA_KV_RANK)

        proj_w = (g_attn[l].reshape(1, d), w_in_p)
        proj_aux = (gq2, gk2, bd, gcq, gckv, wuq_p, wukv_p)
        lat = _project(x, mod3, lambda bi: bi, *proj_w, tables, *proj_aux)
        ctxp = _project(xc, mod3, lambda bi: b, *proj_w, no_rotation, *proj_aux)

        diff_extra = [pad_lanes(diff_lq1[l]), pad_lanes(diff_lk1[l]), pad_lanes(diff_lq2[l]),
                      pad_lanes(diff_lk2[l]), pad_lanes(diff_gsub[l])]
        diff_body = functools.partial(_diff_body, lam_init=lam_init)
        diff_body.__name__ = "_diff_body"

        def attend(q, streams):
            kv = lambda ik, iv: [(p[ik], p[iv]) for p in streams]
            rows = q[0].shape[1]
            stack = 4 * ROW_TILE
            oa = _attention(_gqa_body, q[0], kv(1, 2), [], n_groups=2, wq=256, wk=LANES,
                            n_s=4, tq=min(stack // 4, rows), per_head=False)
            ob = _attention(diff_body, q[3], kv(4, 5), diff_extra, n_groups=2, wq=LANES, wk=LANES,
                            n_s=2, tq=min(stack // 2, rows), per_head=True)
            om = _attention(_mla_body, q[6], kv(7, 8), [], n_groups=2, wq=256, wk=256,
                            n_s=1, tq=min(stack, rows), per_head=True)
            return oa, ob, om

        w_out_b = w_out[l].astype(BF16)
        g2 = g_ffn[l].reshape(1, d)
        dense = l % 2 == 0
        if dense:
            n_hidden = ffn_w_out.shape[1]
            wg = ffn_w_in[l // 2][:, :n_hidden].astype(BF16)
            wu = ffn_w_in[l // 2][:, n_hidden:].astype(BF16)
            wo = ffn_w_out[l // 2].astype(BF16)
            w_router = None
        else:
            n_hidden = moe_w_out.shape[2]
            wi = moe_w_in[l // 2]
            wg = wi[:, :, :n_hidden].astype(BF16)
            wu = wi[:, :, n_hidden:].astype(BF16)
            wo = moe_w_out[l // 2].astype(BF16)
            wr = jnp.zeros((d, LANES), F32).at[:, :N_EXPERTS].set(moe_router[l // 2])
            wr_hi = wr.astype(BF16)
            w_router = jnp.stack([wr_hi, (wr - wr_hi.astype(F32)).astype(BF16)])
        last = l == depth - 1

        def channel_mix(xs, attn_out, mod_row, final_norm):
            res = _out_project(xs, *attn_out, w_out_b, mod3, g2, mod_row, w_router)
            gf = g_final.reshape(1, d)
            if dense:
                return _dense_ffn(res[0], res[1], wg, wu, wo, mod3, mod_row, gf, final_norm)
            return _moe_ffn(res[0], res[1], res[2], res[3], wg, wu, wo, mod3, mod_row, gf, final_norm)

        x_new = channel_mix(x, attend(lat, [lat, ctxp]), lambda bi: bi, last)
        if need_ctx:
            xc = channel_mix(xc, attend(ctxp, [ctxp]), lambda bi: b, False)
        x = x_new
    return x
```

```python
import functools
import math

import numpy as np
import jax
import jax.numpy as jnp
from jax import lax
from jax.experimental import pallas as pl
from jax.experimental.pallas import tpu as pltpu

LANES = 128
MXU_TILE = 256
VMEM_LIMIT = 60 * 1024 * 1024

NORM_EPS = 1e-6
ROPE_THETA = 10000.0
GRID_W = 64
N_MOD = 6
HEAD_DIM = 64
GQA_HEADS, GQA_KV_HEADS = 8, 2
DIFF_HEADS, DIFF_QK_DIM, DIFF_V_DIM = 4, 32, 64
MLA_HEADS, MLA_Q_RANK, MLA_KV_RANK = 4, 192, 128
MLA_NOPE_DIM, MLA_ROPE_DIM, MLA_V_DIM = 64, 32, 64
N_EXPERTS, TOP_K = 8, 2
GATE_ROWS = 16

ROW_TILE = 256
PROJ_ROWS = 512
KEY_CHUNK = MXU_TILE
FFN_CHUNK = MXU_TILE
FFN_ROWS = 512
MOE_ROWS = 1024
MOE_PART = 512
MOE_SLOTS = 256

F32 = jnp.float32
BF16 = jnp.bfloat16
HIGHEST = lax.Precision.HIGHEST


def _params(*sem):
    return pltpu.CompilerParams(dimension_semantics=sem, vmem_limit_bytes=VMEM_LIMIT)


def _lane_iota(shape):
    return lax.broadcasted_iota(jnp.int32, shape, len(shape) - 1)


def _mod_body(c_ref, w_ref, b_ref, o_ref):
    c = c_ref[...]
    sc = c * jax.nn.sigmoid(c)
    o_ref[...] = jnp.dot(sc, w_ref[...], precision=HIGHEST,
                         preferred_element_type=F32) + b_ref[...]


def _modulation(c_all, w, b):
    rows, d = c_all.shape
    n = w.shape[1]
    tn = 1536
    return pl.pallas_call(
        _mod_body,
        grid=(n // tn,),
        in_specs=[pl.BlockSpec((rows, d), lambda j: (0, 0)),
                  pl.BlockSpec((d, tn), lambda j: (0, j)),
                  pl.BlockSpec((1, tn), lambda j: (0, j))],
        out_specs=pl.BlockSpec((rows, tn), lambda j: (0, j)),
        out_shape=jax.ShapeDtypeStruct((rows, n), F32),
        compiler_params=_params("arbitrary"),
        name="modulation",
    )(c_all, w, b.reshape(1, n))


def _rms_rows(x, width):
    return lax.rsqrt(jnp.sum(x * x, axis=-1, keepdims=True) * (1.0 / width) + NORM_EPS)


def _rope(t, cos, sin_signed, half):
    lane = _lane_iota(t.shape)
    partner = jnp.where((lane & half) == 0,
                        pltpu.roll(t, LANES - half, 1), pltpu.roll(t, half, 1))
    return t * cos + partner * sin_signed


def _with_ones(t):
    return jnp.where(_lane_iota(t.shape) == HEAD_DIM, 1.0, t)


def _proj_body(x_ref, mod_ref, g_ref, w_ref, ca_ref, sa_ref, cb_ref, sb_ref,
               gq_ref, gk_ref, bd_ref, gcq_ref, gckv_ref, wuq_ref, wukv_ref,
               qa_ref, ka_ref, va_ref, qb_ref, kb_ref, vb_ref, qm_ref, km_ref, vm_ref, p_ref,
               *, scale_a, scale_b, scale_m):
    @pl.when(pl.program_id(0) == 0)
    def _():
        p_ref[...] = jnp.zeros(p_ref.shape, F32)

    x = x_ref[0]
    d = x.shape[-1]
    mod = mod_ref[0]
    y = x * _rms_rows(x, d) * g_ref[...]
    h = (y * (1.0 + mod[1:2]) + mod[0:1]).astype(BF16)
    p_new = jnp.dot(h, w_ref[...], preferred_element_type=F32)
    p = p_ref

    ca, sa, cb, sb = ca_ref[...], sa_ref[...], cb_ref[...], sb_ref[...]
    bd = bd_ref[...]
    tile = lambda ref, j: (0, slice(None), slice(j * LANES, (j + 1) * LANES))
    cols = lambda a, base, j: a[:, base + j * LANES:base + (j + 1) * LANES]

    def head_norm(t, g):
        sq = t * t
        hi = sq.astype(BF16)
        lo = (sq - hi.astype(F32)).astype(BF16)
        ms = (jnp.dot(hi, bd, preferred_element_type=F32) + jnp.dot(lo, bd, preferred_element_type=F32))
        return t * lax.rsqrt(ms + NORM_EPS) * g

    for j in range(4):
        t = head_norm(cols(p, 0, j), gq_ref[...])
        qa_ref[tile(qa_ref, j)] = (_rope(t, ca, sa, 32) * scale_a).astype(BF16)
    for j in range(2):
        t = head_norm(cols(p, 512, j), gk_ref[...])
        ka_ref[tile(ka_ref, j)] = _rope(t, ca, sa, 32).astype(BF16)
        va_ref[tile(va_ref, j)] = _with_ones(cols(p, 768, j)).astype(BF16)

    for j in range(2):
        qb_ref[tile(qb_ref, j)] = (_rope(cols(p, 1024, j), cb, sb, 16) * scale_b).astype(BF16)
        kb_ref[tile(kb_ref, j)] = _rope(cols(p, 1280, j), cb, sb, 16).astype(BF16)
    for j in range(4):
        vb_ref[tile(vb_ref, j)] = _with_ones(cols(p, 1536, j)).astype(BF16)

    cq = p[:, 2048:2304]
    cqn = (cq * _rms_rows(cq, MLA_Q_RANK) * gcq_ref[...]).astype(BF16)
    uq = jnp.dot(cqn, wuq_ref[...], preferred_element_type=F32)
    ckv = p[:, 2304:2432]
    ckvn = (ckv * _rms_rows(ckv, MLA_KV_RANK) * gckv_ref[...]).astype(BF16)
    ukv = jnp.dot(ckvn, wukv_ref[...], preferred_element_type=F32)
    kr = _rope(p[:, 2432:2560], cb, sb, 16).astype(BF16)
    for pr in range(2):
        qm_ref[tile(qm_ref, 2 * pr)] = (cols(uq, 0, 2 * pr) * scale_m).astype(BF16)
        qr = _rope(cols(uq, 0, 2 * pr + 1), cb, sb, 16)
        qm_ref[tile(qm_ref, 2 * pr + 1)] = (qr * scale_m).astype(BF16)
        km_ref[tile(km_ref, 2 * pr)] = cols(ukv, 0, pr).astype(BF16)
        km_ref[tile(km_ref, 2 * pr + 1)] = kr
    for j in range(4):
        vm_ref[tile(vm_ref, j)] = _with_ones(cols(ukv, 256, j)).astype(BF16)
    p_ref[...] = p_new


def _project(x, mod3, mod_row, g, w, tables, gq2, gk2, bd, gcq, gckv, wuq, wukv):
    b, s, d = x.shape
    tm = min(PROJ_ROWS, s)
    nt = s // tm
    n_tiles = b * nt
    widths = (512, 256, 256, 256, 256, 512, 512, 512, 512)
    log2e = math.log2(math.e)
    body = functools.partial(_proj_body, scale_a=HEAD_DIM ** -0.5 * log2e,
                             scale_b=DIFF_QK_DIM ** -0.5 * log2e,
                             scale_m=(MLA_NOPE_DIM + MLA_ROPE_DIM) ** -0.5 * log2e)
    cur = lambda j: (jnp.minimum(j, n_tiles - 1) // nt, jnp.minimum(j, n_tiles - 1) % nt)
    prev = lambda j: (jnp.maximum(j - 1, 0) // nt, jnp.maximum(j - 1, 0) % nt)
    const = lambda shape: pl.BlockSpec(shape, lambda j: (0,) * len(shape))
    tab = pl.BlockSpec((tm, LANES), lambda j: (prev(j)[1], 0))
    return pl.pallas_call(
        body,
        grid=(n_tiles + 1,),
        in_specs=[pl.BlockSpec((1, tm, d), lambda j: (cur(j)[0], cur(j)[1], 0)),
                  pl.BlockSpec((1, N_MOD, d), lambda j: (mod_row(cur(j)[0]), 0, 0)),
                  const((1, d)), const(w.shape), tab, tab, tab, tab,
                  const((1, LANES)), const((1, LANES)), const((LANES, LANES)),
                  const(gcq.shape), const(gckv.shape), const(wuq.shape), const(wukv.shape)],
        out_specs=[pl.BlockSpec((1, tm, wd), lambda j: (prev(j)[0], prev(j)[1], 0)) for wd in widths],
        out_shape=[jax.ShapeDtypeStruct((b, s, wd), BF16) for wd in widths],
        scratch_shapes=[pltpu.VMEM((tm, w.shape[1]), F32)],
        compiler_params=_params("arbitrary"),
        name="project",
    )(x, mod3, g, w, *tables, gq2, gk2, bd, gcq, gckv, wuq, wukv)


def _attn_step(qs, kv_refs, s_ref, mp_ref, mb_ref, acc_ref):
    step = pl.program_id(0)

    @pl.when(step == 0)
    def _():
        s_ref[...] = jnp.zeros(s_ref.shape, F32)
        mp_ref[...] = jnp.zeros(mp_ref.shape, F32)
        acc_ref[...] = jnp.ones(acc_ref.shape, F32)

    raw = acc_ref[...]
    done = raw * (1.0 / raw[:, HEAD_DIM:HEAD_DIM + 1])

    cur = step % 2
    mb_ref[...] = jnp.broadcast_to(jnp.max(mp_ref[1 - cur], axis=-1, keepdims=True), mb_ref.shape)
    acc = None
    chunks = [(k_ref, v_ref, slice(i * KEY_CHUNK, (i + 1) * KEY_CHUNK))
              for k_ref, v_ref in kv_refs for i in range(k_ref.shape[1] // KEY_CHUNK)]
    for c, (k_ref, v_ref, keys) in enumerate(chunks):
        s_new = lax.dot_general(qs, k_ref[0, keys, :], (((1,), (1,)), ((), ())),
                                preferred_element_type=F32)
        s_old = s_ref[c]
        m_old = mb_ref[...]
        p0 = jnp.exp2(s_old[:, :LANES] - m_old)
        p1 = jnp.exp2(s_old[:, LANES:] - m_old)
        part = jnp.dot(jnp.concatenate([p0, p1], axis=1).astype(BF16), v_ref[0, keys, :],
                       preferred_element_type=F32)
        acc = part if acc is None else acc + part
        s_ref[c] = s_new
        mc = jnp.maximum(s_new[:, :LANES], s_new[:, LANES:])
        mp_ref[cur] = mc if c == 0 else jnp.maximum(mp_ref[cur], mc)
    acc_ref[...] = acc
    return done


def _split_kv(refs, n_kv):
    return [(refs[2 * i], refs[2 * i + 1]) for i in range(n_kv)], refs[2 * n_kv:]


def _pack_heads(even, odd):
    lo = _lane_iota(even.shape) < HEAD_DIM
    return jnp.where(lo, even, pltpu.roll(odd, HEAD_DIM, 1))


def _gqa_body(q_ref, *rest, n_kv, n_tiles):
    tq = q_ref.shape[1]
    lo = _lane_iota((tq, LANES)) < HEAD_DIM
    heads = []
    for j in range(2):
        t = q_ref[0, :, j * LANES:(j + 1) * LANES].astype(F32)
        heads.append(jnp.where(lo, t, 0.0))
        heads.append(jnp.where(lo, pltpu.roll(t, HEAD_DIM, 1), 0.0))
    qs = jnp.concatenate(heads, axis=0).astype(BF16)
    kv, (o_ref, *scratch) = _split_kv(rest, n_kv)
    o = _attn_step(qs, kv, *scratch)
    for j in range(2):
        o_ref[0, :, j * LANES:(j + 1) * LANES] = _pack_heads(
            o[(2 * j) * tq:(2 * j + 1) * tq], o[(2 * j + 1) * tq:(2 * j + 2) * tq]).astype(o_ref.dtype)


def _head_parity(n_tiles):
    step = pl.program_id(0)
    return jnp.minimum(step, n_tiles - 1) % 2, jnp.maximum(step - 2, 0) % 2


def _store_head(o_ref, res, parity):
    @pl.when(parity == 0)
    def _():
        o_ref[0, :, 0:HEAD_DIM] = res[:, 0:HEAD_DIM].astype(o_ref.dtype)

    @pl.when(parity == 1)
    def _():
        o_ref[0, :, HEAD_DIM:LANES] = pltpu.roll(res, HEAD_DIM, 1)[:, HEAD_DIM:LANES].astype(o_ref.dtype)


def _diff_body(q_ref, *rest, n_kv, n_tiles, lam_init):
    kv, (lq1_ref, lk1_ref, lq2_ref, lk2_ref, gsub_ref, o_ref, *scratch) = _split_kv(rest, n_kv)
    tq = q_ref.shape[1]
    par, par_done = _head_parity(n_tiles)
    lane = _lane_iota((tq, LANES))
    t = q_ref[0]
    zero = jnp.zeros_like(t)
    qs = jnp.concatenate([jnp.where((lane // DIFF_QK_DIM) == 2 * par + j, t, zero) for j in range(2)],
                         axis=0)
    o = _attn_step(qs, kv, *scratch)
    lam = (jnp.exp(jnp.sum(lq1_ref[...] * lk1_ref[...], axis=-1, keepdims=True))
           - jnp.exp(jnp.sum(lq2_ref[...] * lk2_ref[...], axis=-1, keepdims=True)) + lam_init)
    d = o[0:tq] - lam * o[tq:2 * tq]
    ms = jnp.sum(jnp.where(lane < DIFF_V_DIM, d * d, 0.0), axis=-1, keepdims=True) * (1.0 / DIFF_V_DIM)
    _store_head(o_ref, (d * lax.rsqrt(ms + NORM_EPS) * gsub_ref[...]) * (1.0 - lam_init), par_done)


def _mla_body(q_ref, *rest, n_kv, n_tiles):
    kv, (o_ref, *scratch) = _split_kv(rest, n_kv)
    par, par_done = _head_parity(n_tiles)
    t = q_ref[0]
    lane = _lane_iota(t.shape)
    nope0 = MLA_NOPE_DIM * par
    rope0 = LANES + MLA_ROPE_DIM * par
    mine = ((lane >= nope0) & (lane < nope0 + MLA_NOPE_DIM)) | (
        (lane >= rope0) & (lane < rope0 + MLA_ROPE_DIM))
    qs = jnp.where(mine, t, jnp.zeros_like(t))
    _store_head(o_ref, _attn_step(qs, kv, *scratch), par_done)


def _attention(body, q, kvs, extra, *, n_groups, wq, wk, n_s, tq, per_head):
    b, q_rows, _ = q.shape
    q_tiles = q_rows // tq
    wo = LANES if per_head else wq
    heads = 2 if per_head else 1
    n_chunks = sum(k.shape[1] // KEY_CHUNK for k, _ in kvs)
    m_rows = n_s * tq
    n_tiles = b * n_groups * q_tiles * heads

    def split(t):
        t, par = t // heads, t % heads
        return t // (n_groups * q_tiles), (t // q_tiles) % n_groups, t % q_tiles, par

    cur = lambda j: split(jnp.minimum(j, n_tiles - 1))
    prev = lambda j: split(jnp.clip(j - 1, 0, n_tiles - 1))
    done = lambda j: split(jnp.maximum(j - 2, 0))
    kv_specs, kv_args = [], []
    for k, v in kvs:
        kv_specs += [pl.BlockSpec((1, k.shape[1], wk), lambda j: (cur(j)[0], 0, cur(j)[1])),
                     pl.BlockSpec((1, v.shape[1], LANES),
                                  lambda j: (prev(j)[0], 0, prev(j)[1] * heads + prev(j)[3]))]
        kv_args += [k, v]
    extra_specs = [pl.BlockSpec(e.shape, lambda j: (0, 0)) for e in extra]
    return pl.pallas_call(
        functools.partial(body, n_kv=len(kvs), n_tiles=n_tiles),
        grid=(n_tiles + 2,),
        in_specs=[pl.BlockSpec((1, tq, wq), lambda j: (cur(j)[0], cur(j)[2], cur(j)[1]))]
        + kv_specs + extra_specs,
        out_specs=pl.BlockSpec((1, tq, wo), lambda j: (done(j)[0], done(j)[2], done(j)[1])),
        out_shape=jax.ShapeDtypeStruct((b, q_rows, n_groups * wo), BF16),
        scratch_shapes=[pltpu.VMEM((n_chunks, m_rows, KEY_CHUNK), F32),
                        pltpu.VMEM((2, m_rows, LANES), F32),
                        pltpu.VMEM((m_rows, LANES), F32),
                        pltpu.VMEM((m_rows, LANES), F32)],
        compiler_params=_params("arbitrary"),
        name=body.__name__.strip("_"),
    )(q, *kv_args, *extra)


def _router_gates(logits):
    lane = _lane_iota(logits.shape).astype(F32)
    neg = jnp.float32(-jnp.inf)
    z = jnp.where(lane < N_EXPERTS, logits, neg)
    m1 = jnp.max(z, axis=-1, keepdims=True)
    i1 = jnp.min(jnp.where(z == m1, lane, float(LANES)), axis=-1, keepdims=True)
    z2 = jnp.where(lane == i1, neg, z)
    m2 = jnp.max(z2, axis=-1, keepdims=True)
    i2 = jnp.min(jnp.where(z2 == m2, lane, float(LANES)), axis=-1, keepdims=True)
    e2 = jnp.exp(m2 - m1)
    den = 1.0 + e2
    return jnp.where(lane == i1, 1.0 / den, 0.0) + jnp.where(lane == i2, e2 / den, 0.0)


def _outproj_body(x_ref, oa_ref, ob_ref, om_ref, w_ref, mod_ref, g_ref, *rest, routed):
    if routed:
        wr_ref, xo_ref, h_ref, gates_ref, gates_t_ref = rest
    else:
        xo_ref, h_ref = rest
    mod = mod_ref[0]
    wa, wb = oa_ref.shape[2], ob_ref.shape[2]
    y = jnp.dot(oa_ref[0], w_ref[0:wa, :], preferred_element_type=F32)
    y += jnp.dot(ob_ref[0], w_ref[wa:wa + wb, :], preferred_element_type=F32)
    y += jnp.dot(om_ref[0], w_ref[wa + wb:, :], preferred_element_type=F32)
    x = x_ref[0] + mod[2:3] * y
    xo_ref[0] = x
    n = x * _rms_rows(x, x.shape[-1]) * g_ref[...]
    h = n * (1.0 + mod[4:5]) + mod[3:4]
    h_ref[0] = h.astype(BF16)
    if routed:
        h_hi = h.astype(BF16)
        h_lo = (h - h_hi.astype(F32)).astype(BF16)
        logits = (jnp.dot(h_hi, wr_ref[0], preferred_element_type=F32)
                  + jnp.dot(h_lo, wr_ref[0], preferred_element_type=F32)
                  + jnp.dot(h_hi, wr_ref[1], preferred_element_type=F32))
        gates = _router_gates(logits)
        gates_ref[0] = gates
        gates_t_ref[0] = gates.T[:gates_t_ref.shape[1]]


def _out_project(x, oa, ob, om, w, mod3, g, mod_row, w_router=None):
    b, s, d = x.shape
    routed = w_router is not None
    tm = min(ROW_TILE if routed else PROJ_ROWS, s)
    row = lambda bi, i: (bi, i, 0)
    in_specs = [pl.BlockSpec((1, tm, d), row),
                pl.BlockSpec((1, tm, oa.shape[2]), row),
                pl.BlockSpec((1, tm, ob.shape[2]), row),
                pl.BlockSpec((1, tm, om.shape[2]), row),
                pl.BlockSpec(w.shape, lambda bi, i: (0, 0)),
                pl.BlockSpec((1, N_MOD, d), lambda bi, i: (mod_row(bi), 0, 0)),
                pl.BlockSpec((1, d), lambda bi, i: (0, 0))]
    out_specs = [pl.BlockSpec((1, tm, d), row), pl.BlockSpec((1, tm, d), row)]
    out_shape = [jax.ShapeDtypeStruct((b, s, d), F32), jax.ShapeDtypeStruct((b, s, d), BF16)]
    args = [x, oa, ob, om, w, mod3, g]
    if routed:
        in_specs.append(pl.BlockSpec(w_router.shape, lambda bi, i: (0, 0, 0)))
        out_specs.append(pl.BlockSpec((1, tm, LANES), row))
        out_shape.append(jax.ShapeDtypeStruct((b, s, LANES), F32))
        out_specs.append(pl.BlockSpec((1, GATE_ROWS, tm), lambda bi, i: (bi, 0, i)))
        out_shape.append(jax.ShapeDtypeStruct((b, GATE_ROWS, s), F32))
        args.append(w_router)
    return pl.pallas_call(
        functools.partial(_outproj_body, routed=routed),
        grid=(b, s // tm),
        in_specs=in_specs, out_specs=out_specs, out_shape=out_shape,
        compiler_params=_params("parallel", "arbitrary"),
        name="out_project_routed" if routed else "out_project",
    )(*args)


def _swiglu_chunks(h, wg_ref, wu_ref, wo_ref, lead):
    acc = None
    for j in range(wo_ref.shape[len(lead)] // FFN_CHUNK):
        cols = slice(j * FFN_CHUNK, (j + 1) * FFN_CHUNK)
        gate = jnp.dot(h, wg_ref[lead + (slice(None), cols)], preferred_element_type=F32)
        up = jnp.dot(h, wu_ref[lead + (slice(None), cols)], preferred_element_type=F32)
        a = (gate * jax.nn.sigmoid(gate) * up).astype(BF16)
        part = jnp.dot(a, wo_ref[lead + (cols, slice(None))], preferred_element_type=F32)
        acc = part if acc is None else acc + part
    return acc


def _residual_out(x, gate, y, gf_ref, final_norm):
    x = x + gate * y
    if final_norm:
        x = x * _rms_rows(x, x.shape[-1]) * gf_ref[...]
    return x


def _ffn_body(x_ref, h_ref, wg_ref, wu_ref, wo_ref, mod_ref, gf_ref, o_ref, *, final_norm):
    y = _swiglu_chunks(h_ref[0], wg_ref, wu_ref, wo_ref, ())
    o_ref[0] = _residual_out(x_ref[0], mod_ref[0][5:6], y, gf_ref, final_norm)


def _dense_ffn(x, h, wg, wu, wo, mod3, mod_row, g_final, final_norm):
    b, s, d = x.shape
    tm = min(FFN_ROWS, s)
    row = lambda bi, i: (bi, i, 0)
    resident = lambda a: pl.BlockSpec(a.shape, lambda bi, i: (0, 0), pipeline_mode=pl.Buffered(1))
    return pl.pallas_call(
        functools.partial(_ffn_body, final_norm=final_norm),
        grid=(b, s // tm),
        in_specs=[pl.BlockSpec((1, tm, d), row), pl.BlockSpec((1, tm, d), row),
                  resident(wg), resident(wu), resident(wo),
                  pl.BlockSpec((1, N_MOD, d), lambda bi, i: (mod_row(bi), 0, 0)),
                  pl.BlockSpec((1, d), lambda bi, i: (0, 0))],
        out_specs=pl.BlockSpec((1, tm, d), row),
        out_shape=jax.ShapeDtypeStruct((b, s, d), F32),
        compiler_params=_params("parallel", "arbitrary"),
        name="dense_ffn",
    )(x, h, wg, wu, wo, mod3, g_final)


def _moe_body(x_ref, h_ref, gates_ref, gates_t_ref, tri_ref, tri_t_ref, wg_ref, wu_ref, wo_ref,
              mod_ref, gf_ref, o_ref, tot_ref, rank_ref, rank_t_ref, *, final_norm):
    e = pl.program_id(2)
    tm = h_ref.shape[1]
    parts = [slice(k * MOE_PART, (k + 1) * MOE_PART) for k in range(tm // MOE_PART)]
    gates = gates_ref[0]
    lane = _lane_iota(gates.shape)

    @pl.when(e == 0)
    def _():
        tot_ref[...] = jnp.zeros(tot_ref.shape, F32)
        live = (gates > 0.0).astype(BF16)
        live_t = (gates_t_ref[0] > 0.0).astype(BF16)
        for part in parts:
            rank_ref[part, :] = jnp.dot(tri_ref[...], live[part], preferred_element_type=F32)
            rank_t_ref[:, part] = jnp.dot(live_t[:, part], tri_t_ref[...], preferred_element_type=F32)

    pick = lambda a: jnp.sum(jnp.where(lane == e, a, 0.0), axis=-1, keepdims=True)
    gate_e = pick(gates)
    rank_e = pick(rank_ref[...])

    @pl.when(jnp.max(rank_e) <= float(MOE_SLOTS))
    def _():
        rank_row = rank_t_ref[pl.ds(e, 1), :]
        live_row = gates_t_ref[0, pl.ds(e, 1), :] > 0.0
        slot_sub = lax.broadcasted_iota(jnp.int32, (MOE_SLOTS, MOE_PART), 0).astype(F32) + 1.0
        rows = []
        for part in parts:
            gather = jnp.where((rank_row[:, part] == slot_sub) & live_row[:, part], 1.0, 0.0)
            rows.append(jnp.dot(gather.astype(BF16), h_ref[0, part, :],
                                preferred_element_type=F32).astype(BF16))
        y = _swiglu_chunks(jnp.concatenate(rows, axis=0), wg_ref, wu_ref, wo_ref, (0,))
        y_hi = y.astype(BF16)
        y_lo = (y - y_hi.astype(F32)).astype(BF16)
        slot_lane = _lane_iota((MOE_PART, MOE_SLOTS)).astype(F32) + 1.0
        for k, part in enumerate(parts):
            slots = slice(k * MOE_SLOTS, (k + 1) * MOE_SLOTS)
            scatter = jnp.where((rank_e[part] == slot_lane) & (gate_e[part] > 0.0), 1.0, 0.0).astype(BF16)
            back = (jnp.dot(scatter, y_hi[slots], preferred_element_type=F32)
                    + jnp.dot(scatter, y_lo[slots], preferred_element_type=F32))
            tot_ref[part, :] += gate_e[part] * back

    @pl.when(jnp.max(rank_e) > float(MOE_SLOTS))
    def _():
        tot_ref[...] += gate_e * _swiglu_chunks(h_ref[0], wg_ref, wu_ref, wo_ref, (0,))

    @pl.when(e == pl.num_programs(2) - 1)
    def _():
        o_ref[0] = _residual_out(x_ref[0], mod_ref[0][5:6], tot_ref[...], gf_ref, final_norm)


def _moe_ffn(x, h, gates, gates_t, wg, wu, wo, mod3, mod_row, g_final, final_norm):
    b, s, d = x.shape
    tm = min(MOE_ROWS, s)
    tp = MOE_PART
    tri = jnp.asarray(np.tril(np.ones((tp, tp), np.float32)), BF16)
    row = lambda bi, i, e: (bi, i, 0)
    const = lambda bi, i, e: (0, 0)
    expert = lambda a: pl.BlockSpec((1,) + a.shape[1:], lambda bi, i, e: (e, 0, 0))
    return pl.pallas_call(
        functools.partial(_moe_body, final_norm=final_norm),
        grid=(b, s // tm, wg.shape[0]),
        in_specs=[pl.BlockSpec((1, tm, d), row, pipeline_mode=pl.Buffered(1)),
                  pl.BlockSpec((1, tm, d), row),
                  pl.BlockSpec((1, tm, LANES), row),
                  pl.BlockSpec((1, GATE_ROWS, tm), lambda bi, i, e: (bi, 0, i)),
                  pl.BlockSpec((tp, tp), const), pl.BlockSpec((tp, tp), const),
                  expert(wg), expert(wu), expert(wo),
                  pl.BlockSpec((1, N_MOD, d), lambda bi, i, e: (mod_row(bi), 0, 0)),
                  pl.BlockSpec((1, d), const)],
        out_specs=pl.BlockSpec((1, tm, d), row),
        out_shape=jax.ShapeDtypeStruct((b, s, d), F32),
        scratch_shapes=[pltpu.VMEM((tm, d), F32), pltpu.VMEM((tm, LANES), F32),
                        pltpu.VMEM((GATE_ROWS, tm), F32)],
        compiler_params=_params("parallel", "parallel", "arbitrary"),
        name="moe_ffn",
    )(x, h, gates, gates_t, tri, tri.T, wg, wu, wo, mod3, g_final)


def _deinterleave(n):
    return np.concatenate([np.arange(0, n, 2), np.arange(1, n, 2)])


def _in_proj_columns(pad):
    pads = lambda n: np.full((n,), pad)
    cols = []
    for hd in range(GQA_HEADS):
        cols.append(64 * hd + _deinterleave(64))
    for g in range(GQA_KV_HEADS):
        cols += [512 + 64 * g + _deinterleave(64), pads(64)]
    for g in range(GQA_KV_HEADS):
        cols += [640 + 64 * g + np.arange(64), pads(64)]
    for u in range(2 * DIFF_HEADS):
        cols.append(768 + 32 * u + _deinterleave(32))
    for u in range(2 * DIFF_HEADS):
        cols.append(1024 + 32 * u + _deinterleave(32))
    for hd in range(DIFF_HEADS):
        cols += [1280 + 64 * hd + np.arange(64), pads(64)]
    cols += [1536 + np.arange(MLA_Q_RANK), pads(64)]
    cols.append(1728 + np.arange(MLA_KV_RANK))
    cols += [1856 + _deinterleave(32), 1856 + _deinterleave(32), pads(64)]
    return np.concatenate(cols)


def _uq_columns(pad):
    per = MLA_NOPE_DIM + MLA_ROPE_DIM
    cols = []
    for pr in range(2):
        h0, h1 = 2 * pr, 2 * pr + 1
        cols += [per * h0 + np.arange(64), per * h1 + np.arange(64),
                 per * h0 + 64 + _deinterleave(32), per * h1 + 64 + _deinterleave(32),
                 np.full((64,), pad)]
    return np.concatenate(cols)


def _ukv_columns(pad):
    per = MLA_NOPE_DIM + MLA_V_DIM
    k = [per * hd + np.arange(64) for hd in range(MLA_HEADS)]
    v = []
    for hd in range(MLA_HEADS):
        v += [per * hd + 64 + np.arange(64), np.full((64,), pad)]
    return np.concatenate(k + v)


def _take_cols(w, cols):
    w_ext = jnp.concatenate([w, jnp.zeros((w.shape[0], 1), w.dtype)], axis=1)
    return jnp.take(w_ext, jnp.asarray(cols), axis=1)


def _rope_tables(s):
    t = np.arange(s)
    rows = (t // GRID_W).astype(np.float32)
    cols = (t % GRID_W).astype(np.float32)
    out = []
    for dim in (HEAD_DIM, DIFF_QK_DIM):
        quarter = dim // 4
        half = dim // 2
        inv_freq = (ROPE_THETA ** (-np.arange(quarter, dtype=np.float32) / quarter)).astype(np.float32)
        ang = np.concatenate([rows[:, None] * inv_freq, cols[:, None] * inv_freq], axis=-1)
        lane = np.arange(LANES)
        idx = (lane % dim) % half
        sign = np.where((lane % dim) < half, -1.0, 1.0).astype(np.float32)
        out += [jnp.asarray(np.cos(ang)[:, idx], F32), jnp.asarray(np.sin(ang)[:, idx] * sign, F32)]
    return out


def kernel(x, c, ctx, c_ctx, w_mod, b_mod, g_attn, g_ffn, w_in, w_out, gqa_gq, gqa_gk,
           diff_lq1, diff_lk1, diff_lq2, diff_lk2, diff_gsub, mla_gcq, mla_gckv, mla_wuq, mla_wukv,
           ffn_w_in, ffn_w_out, moe_router, moe_w_in, moe_w_out, g_final):
    b, s, d = x.shape
    n_ctx = ctx.shape[1]
    depth = w_mod.shape[0]
    tables = _rope_tables(s)
    no_rotation = [jnp.ones((n_ctx, LANES), F32), jnp.zeros((n_ctx, LANES), F32)] * 2
    in_cols = _in_proj_columns(w_in.shape[2])
    uq_cols = _uq_columns(mla_wuq.shape[2])
    ukv_cols = _ukv_columns(mla_wukv.shape[2])
    perm64 = _deinterleave(HEAD_DIM)
    bd = jnp.asarray(np.kron(np.eye(LANES // HEAD_DIM), np.full((HEAD_DIM, HEAD_DIM), 1.0 / HEAD_DIM)), F32)

    mod_rows = 16
    c_all = jnp.zeros((mod_rows, d), F32).at[:b].set(c).at[b].set(c_ctx)
    pad_lanes = lambda v: jnp.zeros((1, LANES), F32).at[0, :v.shape[0]].set(v)

    xc = ctx
    for l in range(depth):
        need_ctx = l < depth - 1
        lam_init = 0.8 - 0.6 * math.exp(-0.3 * l)
        mod3 = _modulation(c_all, w_mod[l], b_mod[l]).reshape(mod_rows, N_MOD, d)

        w_in_p = _take_cols(w_in[l], in_cols).astype(BF16)
        wuq_p = jnp.zeros((2 * LANES, 512), F32).at[:MLA_Q_RANK].set(
            _take_cols(mla_wuq[l], uq_cols)).astype(BF16)
        wukv_p = _take_cols(mla_wukv[l], ukv_cols).astype(BF16)
        gq2 = jnp.tile(gqa_gq[l][perm64], 2).reshape(1, LANES)
        gk2 = jnp.tile(gqa_gk[l][perm64], 2).reshape(1, LANES)
        gcq = jnp.zeros((1, 2 * LANES), F32).at[0, :MLA_Q_RANK].set(mla_gcq[l])
        gckv = mla_gckv[l].reshape(1, MLA_KV_RANK)

        proj_w = (g_attn[l].reshape(1, d), w_in_p)
        proj_aux = (gq2, gk2, bd, gcq, gckv, wuq_p, wukv_p)
        lat = _project(x, mod3, lambda bi: bi, *proj_w, tables, *proj_aux)
        ctxp = _project(xc, mod3, lambda bi: b, *proj_w, no_rotation, *proj_aux)

        diff_extra = [pad_lanes(diff_lq1[l]), pad_lanes(diff_lk1[l]), pad_lanes(diff_lq2[l]),
                      pad_lanes(diff_lk2[l]), pad_lanes(diff_gsub[l])]
        diff_body = functools.partial(_diff_body, lam_init=lam_init)
        diff_body.__name__ = "_diff_body"

        def attend(q, streams):
            kv = lambda ik, iv: [(p[ik], p[iv]) for p in streams]
            rows = q[0].shape[1]
            stack = 4 * ROW_TILE
            oa = _attention(_gqa_body, q[0], kv(1, 2), [], n_groups=2, wq=256, wk=LANES,
                            n_s=4, tq=min(stack // 4, rows), per_head=False)
            ob = _attention(diff_body, q[3], kv(4, 5), diff_extra, n_groups=2, wq=LANES, wk=LANES,
                            n_s=2, tq=min(stack // 2, rows), per_head=True)
            om = _attention(_mla_body, q[6], kv(7, 8), [], n_groups=2, wq=256, wk=256,
                            n_s=1, tq=min(stack, rows), per_head=True)
            return oa, ob, om

        w_out_b = w_out[l].astype(BF16)
        g2 = g_ffn[l].reshape(1, d)
        dense = l % 2 == 0
        if dense:
            n_hidden = ffn_w_out.shape[1]
            wg = ffn_w_in[l // 2][:, :n_hidden].astype(BF16)
            wu = ffn_w_in[l // 2][:, n_hidden:].astype(BF16)
            wo = ffn_w_out[l // 2].astype(BF16)
            w_router = None
        else:
            n_hidden = moe_w_out.shape[2]
            wi = moe_w_in[l // 2]
            wg = wi[:, :, :n_hidden].astype(BF16)
            wu = wi[:, :, n_hidden:].astype(BF16)
            wo = moe_w_out[l // 2].astype(BF16)
            wr = jnp.zeros((d, LANES), F32).at[:, :N_EXPERTS].set(moe_router[l // 2])
            wr_hi = wr.astype(BF16)
            w_router = jnp.stack([wr_hi, (wr - wr_hi.astype(F32)).astype(BF16)])
        last = l == depth - 1

        def channel_mix(xs, attn_out, mod_row, final_norm):
            res = _out_project(xs, *attn_out, w_out_b, mod3, g2, mod_row, w_router)
            gf = g_final.reshape(1, d)
            if dense:
                return _dense_ffn(res[0], res[1], wg, wu, wo, mod3, mod_row, gf, final_norm)
            return _moe_ffn(res[0], res[1], res[2], res[3], wg, wu, wo, mod3, mod_row, gf, final_norm)

        x_new = channel_mix(x, attend(lat, [lat, ctxp]), lambda bi: bi, last)
        if need_ctx:
            xc = channel_mix(xc, attend(ctxp, [ctxp]), lambda bi: b, False)
        x = x_new
    return x
```

```python
import functools
import math

import numpy as np
import jax
import jax.numpy as jnp
from jax import lax
from jax.experimental import pallas as pl
from jax.experimental.pallas import tpu as pltpu

LANES = 128
MXU_TILE = 256
VMEM_LIMIT = 60 * 1024 * 1024

NORM_EPS = 1e-6
ROPE_THETA = 10000.0
GRID_W = 64
N_MOD = 6
HEAD_DIM = 64
GQA_HEADS, GQA_KV_HEADS = 8, 2
DIFF_HEADS, DIFF_QK_DIM, DIFF_V_DIM = 4, 32, 64
MLA_HEADS, MLA_Q_RANK, MLA_KV_RANK = 4, 192, 128
MLA_NOPE_DIM, MLA_ROPE_DIM, MLA_V_DIM = 64, 32, 64
N_EXPERTS, TOP_K = 8, 2
GATE_ROWS = 16

ROW_TILE = 256
PROJ_ROWS = 512
KEY_CHUNK = MXU_TILE
FFN_CHUNK = MXU_TILE
FFN_ROWS = 512
MOE_ROWS = 1024
MOE_PART = 512
MOE_SLOTS = 256

F32 = jnp.float32
BF16 = jnp.bfloat16
HIGHEST = lax.Precision.HIGHEST


def _params(*sem):
    return pltpu.CompilerParams(dimension_semantics=sem, vmem_limit_bytes=VMEM_LIMIT)


def _lane_iota(shape):
    return lax.broadcasted_iota(jnp.int32, shape, len(shape) - 1)


def _mod_body(c_ref, w_ref, b_ref, o_ref):
    c = c_ref[...]
    sc = c * jax.nn.sigmoid(c)
    o_ref[...] = jnp.dot(sc, w_ref[...], precision=HIGHEST,
                         preferred_element_type=F32) + b_ref[...]


def _modulation(c_all, w, b):
    rows, d = c_all.shape
    n = w.shape[1]
    tn = 1536
    return pl.pallas_call(
        _mod_body,
        grid=(n // tn,),
        in_specs=[pl.BlockSpec((rows, d), lambda j: (0, 0)),
                  pl.BlockSpec((d, tn), lambda j: (0, j)),
                  pl.BlockSpec((1, tn), lambda j: (0, j))],
        out_specs=pl.BlockSpec((rows, tn), lambda j: (0, j)),
        out_shape=jax.ShapeDtypeStruct((rows, n), F32),
        compiler_params=_params("arbitrary"),
        name="modulation",
    )(c_all, w, b.reshape(1, n))


def _rms_rows(x, width):
    return lax.rsqrt(jnp.sum(x * x, axis=-1, keepdims=True) * (1.0 / width) + NORM_EPS)


def _rope(t, cos, sin_signed, half):
    lane = _lane_iota(t.shape)
    partner = jnp.where((lane & half) == 0,
                        pltpu.roll(t, LANES - half, 1), pltpu.roll(t, half, 1))
    return t * cos + partner * sin_signed


def _with_ones(t):
    return jnp.where(_lane_iota(t.shape) == HEAD_DIM, 1.0, t)


def _proj_body(x_ref, mod_ref, g_ref, w_ref, ca_ref, sa_ref, cb_ref, sb_ref,
               gq_ref, gk_ref, bd_ref, gcq_ref, gckv_ref, wuq_ref, wukv_ref,
               qa_ref, ka_ref, va_ref, qb_ref, kb_ref, vb_ref, qm_ref, km_ref, vm_ref, p_ref,
               *, scale_a, scale_b, scale_m):
    @pl.when(pl.program_id(0) == 0)
    def _():
        p_ref[...] = jnp.zeros(p_ref.shape, F32)

    x = x_ref[0]
    d = x.shape[-1]
    mod = mod_ref[0]
    y = x * _rms_rows(x, d) * g_ref[...]
    h = (y * (1.0 + mod[1:2]) + mod[0:1]).astype(BF16)
    p_new = jnp.dot(h, w_ref[...], preferred_element_type=F32)
    p = p_ref

    ca, sa, cb, sb = ca_ref[...], sa_ref[...], cb_ref[...], sb_ref[...]
    bd = bd_ref[...]
    tile = lambda ref, j: (0, slice(None), slice(j * LANES, (j + 1) * LANES))
    cols = lambda a, base, j: a[:, base + j * LANES:base + (j + 1) * LANES]

    def head_norm(t, g):
        sq = t * t
        hi = sq.astype(BF16)
        lo = (sq - hi.astype(F32)).astype(BF16)
        ms = (jnp.dot(hi, bd, preferred_element_type=F32) + jnp.dot(lo, bd, preferred_element_type=F32))
        return t * lax.rsqrt(ms + NORM_EPS) * g

    for j in range(4):
        t = head_norm(cols(p, 0, j), gq_ref[...])
        qa_ref[tile(qa_ref, j)] = (_rope(t, ca, sa, 32) * scale_a).astype(BF16)
    for j in range(2):
        t = head_norm(cols(p, 512, j), gk_ref[...])
        ka_ref[tile(ka_ref, j)] = _rope(t, ca, sa, 32).astype(BF16)
        va_ref[tile(va_ref, j)] = _with_ones(cols(p, 768, j)).astype(BF16)

    for j in range(2):
        qb_ref[tile(qb_ref, j)] = (_rope(cols(p, 1024, j), cb, sb, 16) * scale_b).astype(BF16)
        kb_ref[tile(kb_ref, j)] = _rope(cols(p, 1280, j), cb, sb, 16).astype(BF16)
    for j in range(4):
        vb_ref[tile(vb_ref, j)] = _with_ones(cols(p, 1536, j)).astype(BF16)

    cq = p[:, 2048:2304]
    cqn = (cq * _rms_rows(cq, MLA_Q_RANK) * gcq_ref[...]).astype(BF16)
    uq = jnp.dot(cqn, wuq_ref[...], preferred_element_type=F32)
    ckv = p[:, 2304:2432]
    ckvn = (ckv * _rms_rows(ckv, MLA_KV_RANK) * gckv_ref[...]).astype(BF16)
    ukv = jnp.dot(ckvn, wukv_ref[...], preferred_element_type=F32)
    kr = _rope(p[:, 2432:2560], cb, sb, 16).astype(BF16)
    for pr in range(2):
        qm_ref[tile(qm_ref, 2 * pr)] = (cols(uq, 0, 2 * pr) * scale_m).astype(BF16)
        qr = _rope(cols(uq, 0, 2 * pr + 1), cb, sb, 16)
        qm_ref[tile(qm_ref, 2 * pr + 1)] = (qr * scale_m).astype(BF16)
        km_ref[tile(km_ref, 2 * pr)] = cols(ukv, 0, pr).astype(BF16)
        km_ref[tile(km_ref, 2 * pr + 1)] = kr
    for j in range(4):
        vm_ref[tile(vm_ref, j)] = _with_ones(cols(ukv, 256, j)).astype(BF16)
    p_ref[...] = p_new


def _project(x, mod3, mod_row, g, w, tables, gq2, gk2, bd, gcq, gckv, wuq, wukv):
    b, s, d = x.shape
    tm = min(PROJ_ROWS, s)
    nt = s // tm
    n_tiles = b * nt
    widths = (512, 256, 256, 256, 256, 512, 512, 512, 512)
    log2e = math.log2(math.e)
    body = functools.partial(_proj_body, scale_a=HEAD_DIM ** -0.5 * log2e,
                             scale_b=DIFF_QK_DIM ** -0.5 * log2e,
                             scale_m=(MLA_NOPE_DIM + MLA_ROPE_DIM) ** -0.5 * log2e)
    cur = lambda j: (jnp.minimum(j, n_tiles - 1) // nt, jnp.minimum(j, n_tiles - 1) % nt)
    prev = lambda j: (jnp.maximum(j - 1, 0) // nt, jnp.maximum(j - 1, 0) % nt)
    const = lambda shape: pl.BlockSpec(shape, lambda j: (0,) * len(shape))
    tab = pl.BlockSpec((tm, LANES), lambda j: (prev(j)[1], 0))
    return pl.pallas_call(
        body,
        grid=(n_tiles + 1,),
        in_specs=[pl.BlockSpec((1, tm, d), lambda j: (cur(j)[0], cur(j)[1], 0)),
                  pl.BlockSpec((1, N_MOD, d), lambda j: (mod_row(cur(j)[0]), 0, 0)),
                  const((1, d)), const(w.shape), tab, tab, tab, tab,
                  const((1, LANES)), const((1, LANES)), const((LANES, LANES)),
                  const(gcq.shape), const(gckv.shape), const(wuq.shape), const(wukv.shape)],
        out_specs=[pl.BlockSpec((1, tm, wd), lambda j: (prev(j)[0], prev(j)[1], 0)) for wd in widths],
        out_shape=[jax.ShapeDtypeStruct((b, s, wd), BF16) for wd in widths],
        scratch_shapes=[pltpu.VMEM((tm, w.shape[1]), F32)],
        compiler_params=_params("arbitrary"),
        name="project",
    )(x, mod3, g, w, *tables, gq2, gk2, bd, gcq, gckv, wuq, wukv)


def _attn_step(qs, kv_refs, s_ref, mp_ref, mb_ref, acc_ref):
    step = pl.program_id(0)

    @pl.when(step == 0)
    def _():
        s_ref[...] = jnp.zeros(s_ref.shape, F32)
        mp_ref[...] = jnp.zeros(mp_ref.shape, F32)
        acc_ref[...] = jnp.ones(acc_ref.shape, F32)

    raw = acc_ref[...]
    done = raw * (1.0 / raw[:, HEAD_DIM:HEAD_DIM + 1])

    cur = step % 2
    mb_ref[...] = jnp.broadcast_to(jnp.max(mp_ref[1 - cur], axis=-1, keepdims=True), mb_ref.shape)
    acc = None
    chunks = [(k_ref, v_ref, slice(i * KEY_CHUNK, (i + 1) * KEY_CHUNK))
              for k_ref, v_ref in kv_refs for i in range(k_ref.shape[1] // KEY_CHUNK)]
    for c, (k_ref, v_ref, keys) in enumerate(chunks):
        s_new = lax.dot_general(qs, k_ref[0, keys, :], (((1,), (1,)), ((), ())),
                                preferred_element_type=F32)
        s_old = s_ref[c]
        m_old = mb_ref[...]
        p0 = jnp.exp2(s_old[:, :LANES] - m_old)
        p1 = jnp.exp2(s_old[:, LANES:] - m_old)
        part = jnp.dot(jnp.concatenate([p0, p1], axis=1).astype(BF16), v_ref[0, keys, :],
                       preferred_element_type=F32)
        acc = part if acc is None else acc + part
        s_ref[c] = s_new
        mc = jnp.maximum(s_new[:, :LANES], s_new[:, LANES:])
        mp_ref[cur] = mc if c == 0 else jnp.maximum(mp_ref[cur], mc)
    acc_ref[...] = acc
    return done


def _split_kv(refs, n_kv):
    return [(refs[2 * i], refs[2 * i + 1]) for i in range(n_kv)], refs[2 * n_kv:]


def _pack_heads(even, odd):
    lo = _lane_iota(even.shape) < HEAD_DIM
    return jnp.where(lo, even, pltpu.roll(odd, HEAD_DIM, 1))


def _gqa_body(q_ref, *rest, n_kv, n_tiles):
    tq = q_ref.shape[1]
    lo = _lane_iota((tq, LANES)) < HEAD_DIM
    heads = []
    for j in range(2):
        t = q_ref[0, :, j * LANES:(j + 1) * LANES].astype(F32)
        heads.append(jnp.where(lo, t, 0.0))
        heads.append(jnp.where(lo, pltpu.roll(t, HEAD_DIM, 1), 0.0))
    qs = jnp.concatenate(heads, axis=0).astype(BF16)
    kv, (o_ref, *scratch) = _split_kv(rest, n_kv)
    o = _attn_step(qs, kv, *scratch)
    for j in range(2):
        o_ref[0, :, j * LANES:(j + 1) * LANES] = _pack_heads(
            o[(2 * j) * tq:(2 * j + 1) * tq], o[(2 * j + 1) * tq:(2 * j + 2) * tq]).astype(o_ref.dtype)


def _head_parity(n_tiles):
    step = pl.program_id(0)
    return jnp.minimum(step, n_tiles - 1) % 2, jnp.maximum(step - 2, 0) % 2


def _store_head(o_ref, res, parity):
    @pl.when(parity == 0)
    def _():
        o_ref[0, :, 0:HEAD_DIM] = res[:, 0:HEAD_DIM].astype(o_ref.dtype)

    @pl.when(parity == 1)
    def _():
        o_ref[0, :, HEAD_DIM:LANES] = pltpu.roll(res, HEAD_DIM, 1)[:, HEAD_DIM:LANES].astype(o_ref.dtype)


def _diff_body(q_ref, *rest, n_kv, n_tiles, lam_init):
    kv, (lq1_ref, lk1_ref, lq2_ref, lk2_ref, gsub_ref, o_ref, *scratch) = _split_kv(rest, n_kv)
    tq = q_ref.shape[1]
    par, par_done = _head_parity(n_tiles)
    lane = _lane_iota((tq, LANES))
    t = q_ref[0]
    zero = jnp.zeros_like(t)
    qs = jnp.concatenate([jnp.where((lane // DIFF_QK_DIM) == 2 * par + j, t, zero) for j in range(2)],
                         axis=0)
    o = _attn_step(qs, kv, *scratch)
    lam = (jnp.exp(jnp.sum(lq1_ref[...] * lk1_ref[...], axis=-1, keepdims=True))
           - jnp.exp(jnp.sum(lq2_ref[...] * lk2_ref[...], axis=-1, keepdims=True)) + lam_init)
    d = o[0:tq] - lam * o[tq:2 * tq]
    ms = jnp.sum(jnp.where(lane < DIFF_V_DIM, d * d, 0.0), axis=-1, keepdims=True) * (1.0 / DIFF_V_DIM)
    _store_head(o_ref, (d * lax.rsqrt(ms + NORM_EPS) * gsub_ref[...]) * (1.0 - lam_init), par_done)


def _mla_body(q_ref, *rest, n_kv, n_tiles):
    kv, (o_ref, *scratch) = _split_kv(rest, n_kv)
    par, par_done = _head_parity(n_tiles)
    t = q_ref[0]
    lane = _lane_iota(t.shape)
    nope0 = MLA_NOPE_DIM * par
    rope0 = LANES + MLA_ROPE_DIM * par
    mine = ((lane >= nope0) & (lane < nope0 + MLA_NOPE_DIM)) | (
        (lane >= rope0) & (lane < rope0 + MLA_ROPE_DIM))
    qs = jnp.where(mine, t, jnp.zeros_like(t))
    _store_head(o_ref, _attn_step(qs, kv, *scratch), par_done)


def _attention(body, q, kvs, extra, *, n_groups, wq, wk, n_s, tq, per_head):
    b, q_rows, _ = q.shape
    q_tiles = q_rows // tq
    wo = LANES if per_head else wq
    heads = 2 if per_head else 1
    n_chunks = sum(k.shape[1] // KEY_CHUNK for k, _ in kvs)
    m_rows = n_s * tq
    n_tiles = b * n_groups * q_tiles * heads

    def split(t):
        t, par = t // heads, t % heads
        return t // (n_groups * q_tiles), (t // q_tiles) % n_groups, t % q_tiles, par

    cur = lambda j: split(jnp.minimum(j, n_tiles - 1))
    prev = lambda j: split(jnp.clip(j - 1, 0, n_tiles - 1))
    done = lambda j: split(jnp.maximum(j - 2, 0))
    kv_specs, kv_args = [], []
    for k, v in kvs:
        kv_specs += [pl.BlockSpec((1, k.shape[1], wk), lambda j: (cur(j)[0], 0, cur(j)[1])),
                     pl.BlockSpec((1, v.shape[1], LANES),
                                  lambda j: (prev(j)[0], 0, prev(j)[1] * heads + prev(j)[3]))]
        kv_args += [k, v]
    extra_specs = [pl.BlockSpec(e.shape, lambda j: (0, 0)) for e in extra]
    return pl.pallas_call(
        functools.partial(body, n_kv=len(kvs), n_tiles=n_tiles),
        grid=(n_tiles + 2,),
        in_specs=[pl.BlockSpec((1, tq, wq), lambda j: (cur(j)[0], cur(j)[2], cur(j)[1]))]
        + kv_specs + extra_specs,
        out_specs=pl.BlockSpec((1, tq, wo), lambda j: (done(j)[0], done(j)[2], done(j)[1])),
        out_shape=jax.ShapeDtypeStruct((b, q_rows, n_groups * wo), BF16),
        scratch_shapes=[pltpu.VMEM((n_chunks, m_rows, KEY_CHUNK), F32),
                        pltpu.VMEM((2, m_rows, LANES), F32),
                        pltpu.VMEM((m_rows, LANES), F32),
                        pltpu.VMEM((m_rows, LANES), F32)],
        compiler_params=_params("arbitrary"),
        name=body.__name__.strip("_"),
    )(q, *kv_args, *extra)


def _router_gates(logits):
    lane = _lane_iota(logits.shape).astype(F32)
    neg = jnp.float32(-jnp.inf)
    z = jnp.where(lane < N_EXPERTS, logits, neg)
    m1 = jnp.max(z, axis=-1, keepdims=True)
    i1 = jnp.min(jnp.where(z == m1, lane, float(LANES)), axis=-1, keepdims=True)
    z2 = jnp.where(lane == i1, neg, z)
    m2 = jnp.max(z2, axis=-1, keepdims=True)
    i2 = jnp.min(jnp.where(z2 == m2, lane, float(LANES)), axis=-1, keepdims=True)
    e2 = jnp.exp(m2 - m1)
    den = 1.0 + e2
    return jnp.where(lane == i1, 1.0 / den, 0.0) + jnp.where(lane == i2, e2 / den, 0.0)


def _outproj_body(x_ref, oa_ref, ob_ref, om_ref, w_ref, mod_ref, g_ref, *rest, routed):
    if routed:
        wr_ref, xo_ref, h_ref, gates_ref, gates_t_ref = rest
    else:
        xo_ref, h_ref = rest
    mod = mod_ref[0]
    wa, wb = oa_ref.shape[2], ob_ref.shape[2]
    y = jnp.dot(oa_ref[0], w_ref[0:wa, :], preferred_element_type=F32)
    y += jnp.dot(ob_ref[0], w_ref[wa:wa + wb, :], preferred_element_type=F32)
    y += jnp.dot(om_ref[0], w_ref[wa + wb:, :], preferred_element_type=F32)
    x = x_ref[0] + mod[2:3] * y
    xo_ref[0] = x
    n = x * _rms_rows(x, x.shape[-1]) * g_ref[...]
    h = n * (1.0 + mod[4:5]) + mod[3:4]
    h_ref[0] = h.astype(BF16)
    if routed:
        h_hi = h.astype(BF16)
        h_lo = (h - h_hi.astype(F32)).astype(BF16)
        logits = (jnp.dot(h_hi, wr_ref[0], preferred_element_type=F32)
                  + jnp.dot(h_lo, wr_ref[0], preferred_element_type=F32)
                  + jnp.dot(h_hi, wr_ref[1], preferred_element_type=F32))
        gates = _router_gates(logits)
        gates_ref[0] = gates
        gates_t_ref[0] = gates.T[:gates_t_ref.shape[1]]


def _out_project(x, oa, ob, om, w, mod3, g, mod_row, w_router=None):
    b, s, d = x.shape
    routed = w_router is not None
    tm = min(ROW_TILE if routed else PROJ_ROWS, s)
    row = lambda bi, i: (bi, i, 0)
    in_specs = [pl.BlockSpec((1, tm, d), row),
                pl.BlockSpec((1, tm, oa.shape[2]), row),
                pl.BlockSpec((1, tm, ob.shape[2]), row),
                pl.BlockSpec((1, tm, om.shape[2]), row),
                pl.BlockSpec(w.shape, lambda bi, i: (0, 0)),
                pl.BlockSpec((1, N_MOD, d), lambda bi, i: (mod_row(bi), 0, 0)),
                pl.BlockSpec((1, d), lambda bi, i: (0, 0))]
    out_specs = [pl.BlockSpec((1, tm, d), row), pl.BlockSpec((1, tm, d), row)]
    out_shape = [jax.ShapeDtypeStruct((b, s, d), F32), jax.ShapeDtypeStruct((b, s, d), BF16)]
    args = [x, oa, ob, om, w, mod3, g]
    if routed:
        in_specs.append(pl.BlockSpec(w_router.shape, lambda bi, i: (0, 0, 0)))
        out_specs.append(pl.BlockSpec((1, tm, LANES), row))
        out_shape.append(jax.ShapeDtypeStruct((b, s, LANES), F32))
        out_specs.append(pl.BlockSpec((1, GATE_ROWS, tm), lambda bi, i: (bi, 0, i)))
        out_shape.append(jax.ShapeDtypeStruct((b, GATE_ROWS, s), F32))
        args.append(w_router)
    return pl.pallas_call(
        functools.partial(_outproj_body, routed=routed),
        grid=(b, s // tm),
        in_specs=in_specs, out_specs=out_specs, out_shape=out_shape,
        compiler_params=_params("parallel", "arbitrary"),
        name="out_project_routed" if routed else "out_project",
    )(*args)


def _swiglu_chunks(h, wg_ref, wu_ref, wo_ref, lead):
    acc = None
    for j in range(wo_ref.shape[len(lead)] // FFN_CHUNK):
        cols = slice(j * FFN_CHUNK, (j + 1) * FFN_CHUNK)
        gate = jnp.dot(h, wg_ref[lead + (slice(None), cols)], preferred_element_type=F32)
        up = jnp.dot(h, wu_ref[lead + (slice(None), cols)], preferred_element_type=F32)
        a = (gate * jax.nn.sigmoid(gate) * up).astype(BF16)
        part = jnp.dot(a, wo_ref[lead + (cols, slice(None))], preferred_element_type=F32)
        acc = part if acc is None else acc + part
    return acc


def _residual_out(x, gate, y, gf_ref, final_norm):
    x = x + gate * y
    if final_norm:
        x = x * _rms_rows(x, x.shape[-1]) * gf_ref[...]
    return x


def _ffn_body(x_ref, h_ref, wg_ref, wu_ref, wo_ref, mod_ref, gf_ref, o_ref, *, final_norm):
    y = _swiglu_chunks(h_ref[0], wg_ref, wu_ref, wo_ref, ())
    o_ref[0] = _residual_out(x_ref[0], mod_ref[0][5:6], y, gf_ref, final_norm)


def _dense_ffn(x, h, wg, wu, wo, mod3, mod_row, g_final, final_norm):
    b, s, d = x.shape
    tm = min(FFN_ROWS, s)
    row = lambda bi, i: (bi, i, 0)
    resident = lambda a: pl.BlockSpec(a.shape, lambda bi, i: (0, 0), pipeline_mode=pl.Buffered(1))
    return pl.pallas_call(
        functools.partial(_ffn_body, final_norm=final_norm),
        grid=(b, s // tm),
        in_specs=[pl.BlockSpec((1, tm, d), row), pl.BlockSpec((1, tm, d), row),
                  resident(wg), resident(wu), resident(wo),
                  pl.BlockSpec((1, N_MOD, d), lambda bi, i: (mod_row(bi), 0, 0)),
                  pl.BlockSpec((1, d), lambda bi, i: (0, 0))],
        out_specs=pl.BlockSpec((1, tm, d), row),
        out_shape=jax.ShapeDtypeStruct((b, s, d), F32),
        compiler_params=_params("parallel", "arbitrary"),
        name="dense_ffn",
    )(x, h, wg, wu, wo, mod3, g_final)


def _moe_body(x_ref, h_ref, gates_ref, gates_t_ref, tri_ref, tri_t_ref, wg_ref, wu_ref, wo_ref,
              mod_ref, gf_ref, o_ref, tot_ref, rank_ref, rank_t_ref, *, final_norm):
    e = pl.program_id(2)
    tm = h_ref.shape[1]
    parts = [slice(k * MOE_PART, (k + 1) * MOE_PART) for k in range(tm // MOE_PART)]
    gates = gates_ref[0]
    lane = _lane_iota(gates.shape)

    @pl.when(e == 0)
    def _():
        tot_ref[...] = jnp.zeros(tot_ref.shape, F32)
        live = (gates > 0.0).astype(BF16)
        live_t = (gates_t_ref[0] > 0.0).astype(BF16)
        for part in parts:
            rank_ref[part, :] = jnp.dot(tri_ref[...], live[part], preferred_element_type=F32)
            rank_t_ref[:, part] = jnp.dot(live_t[:, part], tri_t_ref[...], preferred_element_type=F32)

    pick = lambda a: jnp.sum(jnp.where(lane == e, a, 0.0), axis=-1, keepdims=True)
    gate_e = pick(gates)
    rank_e = pick(rank_ref[...])

    @pl.when(jnp.max(rank_e) <= float(MOE_SLOTS))
    def _():
        rank_row = rank_t_ref[pl.ds(e, 1), :]
        live_row = gates_t_ref[0, pl.ds(e, 1), :] > 0.0
        slot_sub = lax.broadcasted_iota(jnp.int32, (MOE_SLOTS, MOE_PART), 0).astype(F32) + 1.0
        rows = []
        for part in parts:
            gather = jnp.where((rank_row[:, part] == slot_sub) & live_row[:, part], 1.0, 0.0)
            rows.append(jnp.dot(gather.astype(BF16), h_ref[0, part, :],
                                preferred_element_type=F32).astype(BF16))
        y = _swiglu_chunks(jnp.concatenate(rows, axis=0), wg_ref, wu_ref, wo_ref, (0,))
        y_hi = y.astype(BF16)
        y_lo = (y - y_hi.astype(F32)).astype(BF16)
        slot_lane = _lane_iota((MOE_PART, MOE_SLOTS)).astype(F32) + 1.0
        for k, part in enumerate(parts):
            slots = slice(k * MOE_SLOTS, (k + 1) * MOE_SLOTS)
            scatter = jnp.where((rank_e[part] == slot_lane) & (gate_e[part] > 0.0), 1.0, 0.0).astype(BF16)
            back = (jnp.dot(scatter, y_hi[slots], preferred_element_type=F32)
                    + jnp.dot(scatter, y_lo[slots], preferred_element_type=F32))
            tot_ref[part, :] += gate_e[part] * back

    @pl.when(jnp.max(rank_e) > float(MOE_SLOTS))
    def _():
        tot_ref[...] += gate_e * _swiglu_chunks(h_ref[0], wg_ref, wu_ref, wo_ref, (0,))

    @pl.when(e == pl.num_programs(2) - 1)
    def _():
        o_ref[0] = _residual_out(x_ref[0], mod_ref[0][5:6], tot_ref[...], gf_ref, final_norm)


def _moe_ffn(x, h, gates, gates_t, wg, wu, wo, mod3, mod_row, g_final, final_norm):
    b, s, d = x.shape
    tm = min(MOE_ROWS, s)
    tp = MOE_PART
    tri = jnp.asarray(np.tril(np.ones((tp, tp), np.float32)), BF16)
    row = lambda bi, i, e: (bi, i, 0)
    const = lambda bi, i, e: (0, 0)
    expert = lambda a: pl.BlockSpec((1,) + a.shape[1:], lambda bi, i, e: (e, 0, 0))
    return pl.pallas_call(
        functools.partial(_moe_body, final_norm=final_norm),
        grid=(b, s // tm, wg.shape[0]),
        in_specs=[pl.BlockSpec((1, tm, d), row, pipeline_mode=pl.Buffered(1)),
                  pl.BlockSpec((1, tm, d), row),
                  pl.BlockSpec((1, tm, LANES), row),
                  pl.BlockSpec((1, GATE_ROWS, tm), lambda bi, i, e: (bi, 0, i)),
                  pl.BlockSpec((tp, tp), const), pl.BlockSpec((tp, tp), const),
                  expert(wg), expert(wu), expert(wo),
                  pl.BlockSpec((1, N_MOD, d), lambda bi, i, e: (mod_row(bi), 0, 0)),
                  pl.BlockSpec((1, d), const)],
        out_specs=pl.BlockSpec((1, tm, d), row),
        out_shape=jax.ShapeDtypeStruct((b, s, d), F32),
        scratch_shapes=[pltpu.VMEM((tm, d), F32), pltpu.VMEM((tm, LANES), F32),
                        pltpu.VMEM((GATE_ROWS, tm), F32)],
        compiler_params=_params("parallel", "parallel", "arbitrary"),
        name="moe_ffn",
    )(x, h, gates, gates_t, tri, tri.T, wg, wu, wo, mod3, g_final)


def _deinterleave(n):
    return np.concatenate([np.arange(0, n, 2), np.arange(1, n, 2)])


def _in_proj_columns(pad):
    pads = lambda n: np.full((n,), pad)
    cols = []
    for hd in range(GQA_HEADS):
        cols.append(64 * hd + _deinterleave(64))
    for g in range(GQA_KV_HEADS):
        cols += [512 + 64 * g + _deinterleave(64), pads(64)]
    for g in range(GQA_KV_HEADS):
        cols += [640 + 64 * g + np.arange(64), pads(64)]
    for u in range(2 * DIFF_HEADS):
        cols.append(768 + 32 * u + _deinterleave(32))
    for u in range(2 * DIFF_HEADS):
        cols.append(1024 + 32 * u + _deinterleave(32))
    for hd in range(DIFF_HEADS):
        cols += [1280 + 64 * hd + np.arange(64), pads(64)]
    cols += [1536 + np.arange(MLA_Q_RANK), pads(64)]
    cols.append(1728 + np.arange(MLA_KV_RANK))
    cols += [1856 + _deinterleave(32), 1856 + _deinterleave(32), pads(64)]
    return np.concatenate(cols)


def _uq_columns(pad):
    per = MLA_NOPE_DIM + MLA_ROPE_DIM
    cols = []
    for pr in range(2):
        h0, h1 = 2 * pr, 2 * pr + 1
        cols += [per * h0 + np.arange(64), per * h1 + np.arange(64),
                 per * h0 + 64 + _deinterleave(32), per * h1 + 64 + _deinterleave(32),
                 np.full((64,), pad)]
    return np.concatenate(cols)


def _ukv_columns(pad):
    per = MLA_NOPE_DIM + MLA_V_DIM
    k = [per * hd + np.arange(64) for hd in range(MLA_HEADS)]
    v = []
    for hd in range(MLA_HEADS):
        v += [per * hd + 64 + np.arange(64), np.full((64,), pad)]
    return np.concatenate(k + v)


def _take_cols(w, cols):
    w_ext = jnp.concatenate([w, jnp.zeros((w.shape[0], 1), w.dtype)], axis=1)
    return jnp.take(w_ext, jnp.asarray(cols), axis=1)


def _rope_tables(s):
    t = np.arange(s)
    rows = (t // GRID_W).astype(np.float32)
    cols = (t % GRID_W).astype(np.float32)
    out = []
    for dim in (HEAD_DIM, DIFF_QK_DIM):
        quarter = dim // 4
        half = dim // 2
        inv_freq = (ROPE_THETA ** (-np.arange(quarter, dtype=np.float32) / quarter)).astype(np.float32)
        ang = np.concatenate([rows[:, None] * inv_freq, cols[:, None] * inv_freq], axis=-1)
        lane = np.arange(LANES)
        idx = (lane % dim) % half
        sign = np.where((lane % dim) < half, -1.0, 1.0).astype(np.float32)
        out += [jnp.asarray(np.cos(ang)[:, idx], F32), jnp.asarray(np.sin(ang)[:, idx] * sign, F32)]
    return out


def kernel(x, c, ctx, c_ctx, w_mod, b_mod, g_attn, g_ffn, w_in, w_out, gqa_gq, gqa_gk,
           diff_lq1, diff_lk1, diff_lq2, diff_lk2, diff_gsub, mla_gcq, mla_gckv, mla_wuq, mla_wukv,
           ffn_w_in, ffn_w_out, moe_router, moe_w_in, moe_w_out, g_final):
    b, s, d = x.shape
    n_ctx = ctx.shape[1]
    depth = w_mod.shape[0]
    tables = _rope_tables(s)
    no_rotation = [jnp.ones((b * n_ctx, LANES), F32), jnp.zeros((b * n_ctx, LANES), F32)] * 2
    flat = lambda a: a.reshape(1, -1, a.shape[-1])
    unflat = lambda a: a.reshape(b, n_ctx, a.shape[-1])
    in_cols = _in_proj_columns(w_in.shape[2])
    uq_cols = _uq_columns(mla_wuq.shape[2])
    ukv_cols = _ukv_columns(mla_wukv.shape[2])
    perm64 = _deinterleave(HEAD_DIM)
    bd = jnp.asarray(np.kron(np.eye(LANES // HEAD_DIM), np.full((HEAD_DIM, HEAD_DIM), 1.0 / HEAD_DIM)), F32)

    mod_rows = 16
    c_all = jnp.zeros((mod_rows, d), F32).at[:b].set(c).at[b].set(c_ctx)
    pad_lanes = lambda v: jnp.zeros((1, LANES), F32).at[0, :v.shape[0]].set(v)

    xc = ctx
    for l in range(depth):
        need_ctx = l < depth - 1
        lam_init = 0.8 - 0.6 * math.exp(-0.3 * l)
        mod3 = _modulation(c_all, w_mod[l], b_mod[l]).reshape(mod_rows, N_MOD, d)

        w_in_p = _take_cols(w_in[l], in_cols).astype(BF16)
        wuq_p = jnp.zeros((2 * LANES, 512), F32).at[:MLA_Q_RANK].set(
            _take_cols(mla_wuq[l], uq_cols)).astype(BF16)
        wukv_p = _take_cols(mla_wukv[l], ukv_cols).astype(BF16)
        gq2 = jnp.tile(gqa_gq[l][perm64], 2).reshape(1, LANES)
        gk2 = jnp.tile(gqa_gk[l][perm64], 2).reshape(1, LANES)
        gcq = jnp.zeros((1, 2 * LANES), F32).at[0, :MLA_Q_RANK].set(mla_gcq[l])
        gckv = mla_gckv[l].reshape(1, MLA_KV_RANK)

        proj_w = (g_attn[l].reshape(1, d), w_in_p)
        proj_aux = (gq2, gk2, bd, gcq, gckv, wuq_p, wukv_p)
        lat = _project(x, mod3, lambda bi: bi, *proj_w, tables, *proj_aux)
        ctxp = [unflat(t) for t in _project(flat(xc), mod3, lambda bi: b, *proj_w, no_rotation, *proj_aux)]

        diff_extra = [pad_lanes(diff_lq1[l]), pad_lanes(diff_lk1[l]), pad_lanes(diff_lq2[l]),
                      pad_lanes(diff_lk2[l]), pad_lanes(diff_gsub[l])]
        diff_body = functools.partial(_diff_body, lam_init=lam_init)
        diff_body.__name__ = "_diff_body"

        def attend(q, streams):
            kv = lambda ik, iv: [(p[ik], p[iv]) for p in streams]
            rows = q[0].shape[1]
            stack = 4 * ROW_TILE
            oa = _attention(_gqa_body, q[0], kv(1, 2), [], n_groups=2, wq=256, wk=LANES,
                            n_s=4, tq=min(stack // 4, rows), per_head=False)
            ob = _attention(diff_body, q[3], kv(4, 5), diff_extra, n_groups=2, wq=LANES, wk=LANES,
                            n_s=2, tq=min(stack // 2, rows), per_head=True)
            om = _attention(_mla_body, q[6], kv(7, 8), [], n_groups=2, wq=256, wk=256,
                            n_s=1, tq=min(stack, rows), per_head=True)
            return oa, ob, om

        w_out_b = w_out[l].astype(BF16)
        g2 = g_ffn[l].reshape(1, d)
        dense = l % 2 == 0
        if dense:
            n_hidden = ffn_w_out.shape[1]
            wg = ffn_w_in[l // 2][:, :n_hidden].astype(BF16)
            wu = ffn_w_in[l // 2][:, n_hidden:].astype(BF16)
            wo = ffn_w_out[l // 2].astype(BF16)
            w_router = None
        else:
            n_hidden = moe_w_out.shape[2]
            wi = moe_w_in[l // 2]
            wg = wi[:, :, :n_hidden].astype(BF16)
            wu = wi[:, :, n_hidden:].astype(BF16)
            wo = moe_w_out[l // 2].astype(BF16)
            wr = jnp.zeros((d, LANES), F32).at[:, :N_EXPERTS].set(moe_router[l // 2])
            wr_hi = wr.astype(BF16)
            w_router = jnp.stack([wr_hi, (wr - wr_hi.astype(F32)).astype(BF16)])
        last = l == depth - 1

        def channel_mix(xs, attn_out, mod_row, final_norm):
            res = _out_project(xs, *attn_out, w_out_b, mod3, g2, mod_row, w_router)
            gf = g_final.reshape(1, d)
            if dense:
                return _dense_ffn(res[0], res[1], wg, wu, wo, mod3, mod_row, gf, final_norm)
            return _moe_ffn(res[0], res[1], res[2], res[3], wg, wu, wo, mod3, mod_row, gf, final_norm)

        x_new = channel_mix(x, attend(lat, [lat, ctxp]), lambda bi: bi, last)
        if need_ctx:
            xc = unflat(channel_mix(flat(xc), [flat(o) for o in attend(ctxp, [ctxp])], lambda bi: b, False))
        x = x_new
    return x
```

```python
import functools
import math

import numpy as np
import jax
import jax.numpy as jnp
from jax import lax
from jax.experimental import pallas as pl
from jax.experimental.pallas import tpu as pltpu

LANES = 128
MXU_TILE = 256
VMEM_LIMIT = 60 * 1024 * 1024

NORM_EPS = 1e-6
ROPE_THETA = 10000.0
GRID_W = 64
N_MOD = 6
HEAD_DIM = 64
GQA_HEADS, GQA_KV_HEADS = 8, 2
DIFF_HEADS, DIFF_QK_DIM, DIFF_V_DIM = 4, 32, 64
MLA_HEADS, MLA_Q_RANK, MLA_KV_RANK = 4, 192, 128
MLA_NOPE_DIM, MLA_ROPE_DIM, MLA_V_DIM = 64, 32, 64
N_EXPERTS, TOP_K = 8, 2
GATE_ROWS = 16

ROW_TILE = 256
PROJ_ROWS = 512
KEY_CHUNK = MXU_TILE
FFN_CHUNK = MXU_TILE
FFN_ROWS = 512
MOE_ROWS = 1024
MOE_PART = 512
MOE_SLOTS = 256

F32 = jnp.float32
BF16 = jnp.bfloat16
HIGHEST = lax.Precision.HIGHEST


def _params(*sem):
    return pltpu.CompilerParams(dimension_semantics=sem, vmem_limit_bytes=VMEM_LIMIT)


def _lane_iota(shape):
    return lax.broadcasted_iota(jnp.int32, shape, len(shape) - 1)


def _mod_body(c_ref, w_ref, b_ref, o_ref):
    c = c_ref[...]
    sc = c * jax.nn.sigmoid(c)
    o_ref[...] = jnp.dot(sc, w_ref[...], precision=HIGHEST,
                         preferred_element_type=F32) + b_ref[...]


def _modulation(c_all, w, b):
    rows, d = c_all.shape
    n = w.shape[1]
    tn = 1536
    return pl.pallas_call(
        _mod_body,
        grid=(n // tn,),
        in_specs=[pl.BlockSpec((rows, d), lambda j: (0, 0)),
                  pl.BlockSpec((d, tn), lambda j: (0, j)),
                  pl.BlockSpec((1, tn), lambda j: (0, j))],
        out_specs=pl.BlockSpec((rows, tn), lambda j: (0, j)),
        out_shape=jax.ShapeDtypeStruct((rows, n), F32),
        compiler_params=_params("arbitrary"),
        name="modulation",
    )(c_all, w, b.reshape(1, n))


def _rms_rows(x, width):
    return lax.rsqrt(jnp.sum(x * x, axis=-1, keepdims=True) * (1.0 / width) + NORM_EPS)


def _rope(t, cos, sin_signed, half):
    lane = _lane_iota(t.shape)
    partner = jnp.where((lane & half) == 0,
                        pltpu.roll(t, LANES - half, 1), pltpu.roll(t, half, 1))
    return t * cos + partner * sin_signed


def _with_ones(t):
    return jnp.where(_lane_iota(t.shape) == HEAD_DIM, 1.0, t)


def _proj_body(x_ref, mod_ref, g_ref, w_ref, ca_ref, sa_ref, cb_ref, sb_ref,
               gq_ref, gk_ref, bd_ref, gcq_ref, gckv_ref, wuq_ref, wukv_ref,
               qa_ref, ka_ref, va_ref, qb_ref, kb_ref, vb_ref, qm_ref, km_ref, vm_ref, p_ref,
               *, scale_a, scale_b, scale_m):
    @pl.when(pl.program_id(0) == 0)
    def _():
        p_ref[...] = jnp.zeros(p_ref.shape, F32)

    x = x_ref[0]
    d = x.shape[-1]
    mod = mod_ref[0]
    y = x * _rms_rows(x, d) * g_ref[...]
    h = (y * (1.0 + mod[1:2]) + mod[0:1]).astype(BF16)
    p_new = jnp.dot(h, w_ref[...], preferred_element_type=F32)
    p = p_ref

    ca, sa, cb, sb = ca_ref[...], sa_ref[...], cb_ref[...], sb_ref[...]
    bd = bd_ref[...]
    tile = lambda ref, j: (0, slice(None), slice(j * LANES, (j + 1) * LANES))
    cols = lambda a, base, j: a[:, base + j * LANES:base + (j + 1) * LANES]

    def head_norm(t, g):
        sq = t * t
        hi = sq.astype(BF16)
        lo = (sq - hi.astype(F32)).astype(BF16)
        ms = (jnp.dot(hi, bd, preferred_element_type=F32) + jnp.dot(lo, bd, preferred_element_type=F32))
        return t * lax.rsqrt(ms + NORM_EPS) * g

    for j in range(4):
        t = head_norm(cols(p, 0, j), gq_ref[...])
        qa_ref[tile(qa_ref, j)] = (_rope(t, ca, sa, 32) * scale_a).astype(BF16)
    for j in range(2):
        t = head_norm(cols(p, 512, j), gk_ref[...])
        ka_ref[tile(ka_ref, j)] = _rope(t, ca, sa, 32).astype(BF16)
        va_ref[tile(va_ref, j)] = _with_ones(cols(p, 768, j)).astype(BF16)

    for j in range(2):
        qb_ref[tile(qb_ref, j)] = (_rope(cols(p, 1024, j), cb, sb, 16) * scale_b).astype(BF16)
        kb_ref[tile(kb_ref, j)] = _rope(cols(p, 1280, j), cb, sb, 16).astype(BF16)
    for j in range(4):
        vb_ref[tile(vb_ref, j)] = _with_ones(cols(p, 1536, j)).astype(BF16)

    cq = p[:, 2048:2304]
    cqn = (cq * _rms_rows(cq, MLA_Q_RANK) * gcq_ref[...]).astype(BF16)
    uq = jnp.dot(cqn, wuq_ref[...], preferred_element_type=F32)
    ckv = p[:, 2304:2432]
    ckvn = (ckv * _rms_rows(ckv, MLA_KV_RANK) * gckv_ref[...]).astype(BF16)
    ukv = jnp.dot(ckvn, wukv_ref[...], preferred_element_type=F32)
    kr = _rope(p[:, 2432:2560], cb, sb, 16).astype(BF16)
    for pr in range(2):
        qm_ref[tile(qm_ref, 2 * pr)] = (cols(uq, 0, 2 * pr) * scale_m).astype(BF16)
        qr = _rope(cols(uq, 0, 2 * pr + 1), cb, sb, 16)
        qm_ref[tile(qm_ref, 2 * pr + 1)] = (qr * scale_m).astype(BF16)
        km_ref[tile(km_ref, 2 * pr)] = cols(ukv, 0, pr).astype(BF16)
        km_ref[tile(km_ref, 2 * pr + 1)] = kr
    for j in range(4):
        vm_ref[tile(vm_ref, j)] = _with_ones(cols(ukv, 256, j)).astype(BF16)
    p_ref[...] = p_new


def _project(x, mod3, mod_row, g, w, tables, gq2, gk2, bd, gcq, gckv, wuq, wukv):
    b, s, d = x.shape
    tm = min(PROJ_ROWS, s)
    nt = s // tm
    n_tiles = b * nt
    widths = (512, 256, 256, 256, 256, 512, 512, 512, 512)
    log2e = math.log2(math.e)
    body = functools.partial(_proj_body, scale_a=HEAD_DIM ** -0.5 * log2e,
                             scale_b=DIFF_QK_DIM ** -0.5 * log2e,
                             scale_m=(MLA_NOPE_DIM + MLA_ROPE_DIM) ** -0.5 * log2e)
    cur = lambda j: (jnp.minimum(j, n_tiles - 1) // nt, jnp.minimum(j, n_tiles - 1) % nt)
    prev = lambda j: (jnp.maximum(j - 1, 0) // nt, jnp.maximum(j - 1, 0) % nt)
    const = lambda shape: pl.BlockSpec(shape, lambda j: (0,) * len(shape))
    tab = pl.BlockSpec((tm, LANES), lambda j: (prev(j)[1], 0))
    return pl.pallas_call(
        body,
        grid=(n_tiles + 1,),
        in_specs=[pl.BlockSpec((1, tm, d), lambda j: (cur(j)[0], cur(j)[1], 0)),
                  pl.BlockSpec((1, N_MOD, d), lambda j: (mod_row(cur(j)[0]), 0, 0)),
                  const((1, d)), const(w.shape), tab, tab, tab, tab,
                  const((1, LANES)), const((1, LANES)), const((LANES, LANES)),
                  const(gcq.shape), const(gckv.shape), const(wuq.shape), const(wukv.shape)],
        out_specs=[pl.BlockSpec((1, tm, wd), lambda j: (prev(j)[0], prev(j)[1], 0)) for wd in widths],
        out_shape=[jax.ShapeDtypeStruct((b, s, wd), BF16) for wd in widths],
        scratch_shapes=[pltpu.VMEM((tm, w.shape[1]), F32)],
        compiler_params=_params("arbitrary"),
        name="project",
    )(x, mod3, g, w, *tables, gq2, gk2, bd, gcq, gckv, wuq, wukv)


def _attn_step(qs, kv_refs, s_ref, mp_ref, mb_ref, acc_ref):
    step = pl.program_id(0)

    @pl.when(step == 0)
    def _():
        s_ref[...] = jnp.zeros(s_ref.shape, F32)
        mb_ref[...] = jnp.zeros(mb_ref.shape, F32)
        acc_ref[...] = jnp.ones(acc_ref.shape, F32)

    raw = acc_ref[...]
    done = raw * (1.0 / raw[:, HEAD_DIM:HEAD_DIM + 1])

    acc = None
    chunks = [(k_ref, v_ref, slice(i * KEY_CHUNK, (i + 1) * KEY_CHUNK))
              for k_ref, v_ref in kv_refs for i in range(k_ref.shape[1] // KEY_CHUNK)]
    for c, (k_ref, v_ref, keys) in enumerate(chunks):
        s_new = lax.dot_general(qs, k_ref[0, keys, :], (((1,), (1,)), ((), ())),
                                preferred_element_type=F32)
        s_old = s_ref[c]
        m_old = mb_ref[...]
        p0 = jnp.exp2(s_old[:, :LANES] - m_old)
        p1 = jnp.exp2(s_old[:, LANES:] - m_old)
        part = jnp.dot(jnp.concatenate([p0, p1], axis=1).astype(BF16), v_ref[0, keys, :],
                       preferred_element_type=F32)
        acc = part if acc is None else acc + part
        s_ref[c] = s_new
        mc = jnp.maximum(s_new[:, :LANES], s_new[:, LANES:])
        mp_ref[...] = mc if c == 0 else jnp.maximum(mp_ref[...], mc)
    acc_ref[...] = acc
    mb_ref[...] = jnp.broadcast_to(jnp.max(mp_ref[...], axis=-1, keepdims=True), mb_ref.shape)
    return done


def _split_kv(refs, n_kv):
    return [(refs[2 * i], refs[2 * i + 1]) for i in range(n_kv)], refs[2 * n_kv:]


def _pack_heads(even, odd):
    lo = _lane_iota(even.shape) < HEAD_DIM
    return jnp.where(lo, even, pltpu.roll(odd, HEAD_DIM, 1))


def _gqa_body(q_ref, *rest, n_kv, n_tiles):
    tq = q_ref.shape[1]
    lo = _lane_iota((tq, LANES)) < HEAD_DIM
    heads = []
    for j in range(2):
        t = q_ref[0, :, j * LANES:(j + 1) * LANES].astype(F32)
        heads.append(jnp.where(lo, t, 0.0))
        heads.append(jnp.where(lo, pltpu.roll(t, HEAD_DIM, 1), 0.0))
    qs = jnp.concatenate(heads, axis=0).astype(BF16)
    kv, (o_ref, *scratch) = _split_kv(rest, n_kv)
    o = _attn_step(qs, kv, *scratch)
    for j in range(2):
        o_ref[0, :, j * LANES:(j + 1) * LANES] = _pack_heads(
            o[(2 * j) * tq:(2 * j + 1) * tq], o[(2 * j + 1) * tq:(2 * j + 2) * tq]).astype(o_ref.dtype)


def _head_parity(n_tiles):
    step = pl.program_id(0)
    return jnp.minimum(step, n_tiles - 1) % 2, jnp.maximum(step - 2, 0) % 2


def _store_head(o_ref, res, parity):
    @pl.when(parity == 0)
    def _():
        o_ref[0, :, 0:HEAD_DIM] = res[:, 0:HEAD_DIM].astype(o_ref.dtype)

    @pl.when(parity == 1)
    def _():
        o_ref[0, :, HEAD_DIM:LANES] = pltpu.roll(res, HEAD_DIM, 1)[:, HEAD_DIM:LANES].astype(o_ref.dtype)


def _diff_body(q_ref, *rest, n_kv, n_tiles, lam_init):
    kv, (lq1_ref, lk1_ref, lq2_ref, lk2_ref, gsub_ref, o_ref, *scratch) = _split_kv(rest, n_kv)
    tq = q_ref.shape[1]
    par, par_done = _head_parity(n_tiles)
    lane = _lane_iota((tq, LANES))
    t = q_ref[0]
    zero = jnp.zeros_like(t)
    qs = jnp.concatenate([jnp.where((lane // DIFF_QK_DIM) == 2 * par + j, t, zero) for j in range(2)],
                         axis=0)
    o = _attn_step(qs, kv, *scratch)
    lam = (jnp.exp(jnp.sum(lq1_ref[...] * lk1_ref[...], axis=-1, keepdims=True))
           - jnp.exp(jnp.sum(lq2_ref[...] * lk2_ref[...], axis=-1, keepdims=True)) + lam_init)
    d = o[0:tq] - lam * o[tq:2 * tq]
    ms = jnp.sum(jnp.where(lane < DIFF_V_DIM, d * d, 0.0), axis=-1, keepdims=True) * (1.0 / DIFF_V_DIM)
    _store_head(o_ref, (d * lax.rsqrt(ms + NORM_EPS) * gsub_ref[...]) * (1.0 - lam_init), par_done)


def _mla_body(q_ref, *rest, n_kv, n_tiles):
    kv, (o_ref, *scratch) = _split_kv(rest, n_kv)
    par, par_done = _head_parity(n_tiles)
    t = q_ref[0]
    lane = _lane_iota(t.shape)
    nope0 = MLA_NOPE_DIM * par
    rope0 = LANES + MLA_ROPE_DIM * par
    mine = ((lane >= nope0) & (lane < nope0 + MLA_NOPE_DIM)) | (
        (lane >= rope0) & (lane < rope0 + MLA_ROPE_DIM))
    qs = jnp.where(mine, t, jnp.zeros_like(t))
    _store_head(o_ref, _attn_step(qs, kv, *scratch), par_done)


def _attention(body, q, kvs, extra, *, n_groups, wq, wk, n_s, tq, per_head):
    b, q_rows, _ = q.shape
    q_tiles = q_rows // tq
    wo = LANES if per_head else wq
    heads = 2 if per_head else 1
    n_chunks = sum(k.shape[1] // KEY_CHUNK for k, _ in kvs)
    m_rows = n_s * tq
    n_tiles = b * n_groups * q_tiles * heads

    def split(t):
        t, par = t // heads, t % heads
        return t // (n_groups * q_tiles), (t // q_tiles) % n_groups, t % q_tiles, par

    cur = lambda j: split(jnp.minimum(j, n_tiles - 1))
    prev = lambda j: split(jnp.clip(j - 1, 0, n_tiles - 1))
    done = lambda j: split(jnp.maximum(j - 2, 0))
    kv_specs, kv_args = [], []
    for k, v in kvs:
        kv_specs += [pl.BlockSpec((1, k.shape[1], wk), lambda j: (cur(j)[0], 0, cur(j)[1])),
                     pl.BlockSpec((1, v.shape[1], LANES),
                                  lambda j: (prev(j)[0], 0, prev(j)[1] * heads + prev(j)[3]))]
        kv_args += [k, v]
    extra_specs = [pl.BlockSpec(e.shape, lambda j: (0, 0)) for e in extra]
    return pl.pallas_call(
        functools.partial(body, n_kv=len(kvs), n_tiles=n_tiles),
        grid=(n_tiles + 2,),
        in_specs=[pl.BlockSpec((1, tq, wq), lambda j: (cur(j)[0], cur(j)[2], cur(j)[1]))]
        + kv_specs + extra_specs,
        out_specs=pl.BlockSpec((1, tq, wo), lambda j: (done(j)[0], done(j)[2], done(j)[1])),
        out_shape=jax.ShapeDtypeStruct((b, q_rows, n_groups * wo), BF16),
        scratch_shapes=[pltpu.VMEM((n_chunks, m_rows, KEY_CHUNK), F32),
                        pltpu.VMEM((m_rows, LANES), F32),
                        pltpu.VMEM((m_rows, LANES), F32),
                        pltpu.VMEM((m_rows, LANES), F32)],
        compiler_params=_params("arbitrary"),
        name=body.__name__.strip("_"),
    )(q, *kv_args, *extra)


def _router_gates(logits):
    lane = _lane_iota(logits.shape).astype(F32)
    neg = jnp.float32(-jnp.inf)
    z = jnp.where(lane < N_EXPERTS, logits, neg)
    m1 = jnp.max(z, axis=-1, keepdims=True)
    i1 = jnp.min(jnp.where(z == m1, lane, float(LANES)), axis=-1, keepdims=True)
    z2 = jnp.where(lane == i1, neg, z)
    m2 = jnp.max(z2, axis=-1, keepdims=True)
    i2 = jnp.min(jnp.where(z2 == m2, lane, float(LANES)), axis=-1, keepdims=True)
    e2 = jnp.exp(m2 - m1)
    den = 1.0 + e2
    return jnp.where(lane == i1, 1.0 / den, 0.0) + jnp.where(lane == i2, e2 / den, 0.0)


def _outproj_body(x_ref, oa_ref, ob_ref, om_ref, w_ref, mod_ref, g_ref, *rest, routed):
    if routed:
        wr_ref, xo_ref, h_ref, gates_ref, gates_t_ref = rest
    else:
        xo_ref, h_ref = rest
    mod = mod_ref[0]
    wa, wb = oa_ref.shape[2], ob_ref.shape[2]
    y = jnp.dot(oa_ref[0], w_ref[0:wa, :], preferred_element_type=F32)
    y += jnp.dot(ob_ref[0], w_ref[wa:wa + wb, :], preferred_element_type=F32)
    y += jnp.dot(om_ref[0], w_ref[wa + wb:, :], preferred_element_type=F32)
    x = x_ref[0] + mod[2:3] * y
    xo_ref[0] = x
    n = x * _rms_rows(x, x.shape[-1]) * g_ref[...]
    h = n * (1.0 + mod[4:5]) + mod[3:4]
    h_ref[0] = h.astype(BF16)
    if routed:
        h_hi = h.astype(BF16)
        h_lo = (h - h_hi.astype(F32)).astype(BF16)
        logits = (jnp.dot(h_hi, wr_ref[0], preferred_element_type=F32)
                  + jnp.dot(h_lo, wr_ref[0], preferred_element_type=F32)
                  + jnp.dot(h_hi, wr_ref[1], preferred_element_type=F32))
        gates = _router_gates(logits)
        gates_ref[0] = gates
        gates_t_ref[0] = gates.T[:gates_t_ref.shape[1]]


def _out_project(x, oa, ob, om, w, mod3, g, mod_row, w_router=None):
    b, s, d = x.shape
    routed = w_router is not None
    tm = min(ROW_TILE if routed else PROJ_ROWS, s)
    row = lambda bi, i: (bi, i, 0)
    in_specs = [pl.BlockSpec((1, tm, d), row),
                pl.BlockSpec((1, tm, oa.shape[2]), row),
                pl.BlockSpec((1, tm, ob.shape[2]), row),
                pl.BlockSpec((1, tm, om.shape[2]), row),
                pl.BlockSpec(w.shape, lambda bi, i: (0, 0)),
                pl.BlockSpec((1, N_MOD, d), lambda bi, i: (mod_row(bi), 0, 0)),
                pl.BlockSpec((1, d), lambda bi, i: (0, 0))]
    out_specs = [pl.BlockSpec((1, tm, d), row), pl.BlockSpec((1, tm, d), row)]
    out_shape = [jax.ShapeDtypeStruct((b, s, d), F32), jax.ShapeDtypeStruct((b, s, d), BF16)]
    args = [x, oa, ob, om, w, mod3, g]
    if routed:
        in_specs.append(pl.BlockSpec(w_router.shape, lambda bi, i: (0, 0, 0)))
        out_specs.append(pl.BlockSpec((1, tm, LANES), row))
        out_shape.append(jax.ShapeDtypeStruct((b, s, LANES), F32))
        out_specs.append(pl.BlockSpec((1, GATE_ROWS, tm), lambda bi, i: (bi, 0, i)))
        out_shape.append(jax.ShapeDtypeStruct((b, GATE_ROWS, s), F32))
        args.append(w_router)
    return pl.pallas_call(
        functools.partial(_outproj_body, routed=routed),
        grid=(b, s // tm),
        in_specs=in_specs, out_specs=out_specs, out_shape=out_shape,
        compiler_params=_params("parallel", "arbitrary"),
        name="out_project_routed" if routed else "out_project",
    )(*args)


def _swiglu_chunks(h, wg_ref, wu_ref, wo_ref, lead):
    acc = None
    for j in range(wo_ref.shape[len(lead)] // FFN_CHUNK):
        cols = slice(j * FFN_CHUNK, (j + 1) * FFN_CHUNK)
        gate = jnp.dot(h, wg_ref[lead + (slice(None), cols)], preferred_element_type=F32)
        up = jnp.dot(h, wu_ref[lead + (slice(None), cols)], preferred_element_type=F32)
        a = (gate * jax.nn.sigmoid(gate) * up).astype(BF16)
        part = jnp.dot(a, wo_ref[lead + (cols, slice(None))], preferred_element_type=F32)
        acc = part if acc is None else acc + part
    return acc


def _residual_out(x, gate, y, gf_ref, final_norm):
    x = x + gate * y
    if final_norm:
        x = x * _rms_rows(x, x.shape[-1]) * gf_ref[...]
    return x


def _ffn_body(x_ref, h_ref, wg_ref, wu_ref, wo_ref, mod_ref, gf_ref, o_ref, *, final_norm):
    y = _swiglu_chunks(h_ref[0], wg_ref, wu_ref, wo_ref, ())
    o_ref[0] = _residual_out(x_ref[0], mod_ref[0][5:6], y, gf_ref, final_norm)


def _dense_ffn(x, h, wg, wu, wo, mod3, mod_row, g_final, final_norm):
    b, s, d = x.shape
    tm = min(FFN_ROWS, s)
    row = lambda bi, i: (bi, i, 0)
    resident = lambda a: pl.BlockSpec(a.shape, lambda bi, i: (0, 0), pipeline_mode=pl.Buffered(1))
    return pl.pallas_call(
        functools.partial(_ffn_body, final_norm=final_norm),
        grid=(b, s // tm),
        in_specs=[pl.BlockSpec((1, tm, d), row), pl.BlockSpec((1, tm, d), row),
                  resident(wg), resident(wu), resident(wo),
                  pl.BlockSpec((1, N_MOD, d), lambda bi, i: (mod_row(bi), 0, 0)),
                  pl.BlockSpec((1, d), lambda bi, i: (0, 0))],
        out_specs=pl.BlockSpec((1, tm, d), row),
        out_shape=jax.ShapeDtypeStruct((b, s, d), F32),
        compiler_params=_params("parallel", "arbitrary"),
        name="dense_ffn",
    )(x, h, wg, wu, wo, mod3, g_final)


def _moe_body(x_ref, h_ref, gates_ref, gates_t_ref, tri_ref, tri_t_ref, wg_ref, wu_ref, wo_ref,
              mod_ref, gf_ref, o_ref, tot_ref, rank_ref, rank_t_ref, *, final_norm):
    e = pl.program_id(2)
    tm = h_ref.shape[1]
    parts = [slice(k * MOE_PART, (k + 1) * MOE_PART) for k in range(tm // MOE_PART)]
    gates = gates_ref[0]
    lane = _lane_iota(gates.shape)

    @pl.when(e == 0)
    def _():
        tot_ref[...] = jnp.zeros(tot_ref.shape, F32)
        live = (gates > 0.0).astype(BF16)
        live_t = (gates_t_ref[0] > 0.0).astype(BF16)
        for part in parts:
            rank_ref[part, :] = jnp.dot(tri_ref[...], live[part], preferred_element_type=F32)
            rank_t_ref[:, part] = jnp.dot(live_t[:, part], tri_t_ref[...], preferred_element_type=F32)

    pick = lambda a: jnp.sum(jnp.where(lane == e, a, 0.0), axis=-1, keepdims=True)
    gate_e = pick(gates)
    rank_e = pick(rank_ref[...])

    @pl.when(jnp.max(rank_e) <= float(MOE_SLOTS))
    def _():
        rank_row = rank_t_ref[pl.ds(e, 1), :]
        live_row = gates_t_ref[0, pl.ds(e, 1), :] > 0.0
        slot_sub = lax.broadcasted_iota(jnp.int32, (MOE_SLOTS, MOE_PART), 0).astype(F32) + 1.0
        rows = []
        for part in parts:
            gather = jnp.where((rank_row[:, part] == slot_sub) & live_row[:, part], 1.0, 0.0)
            rows.append(jnp.dot(gather.astype(BF16), h_ref[0, part, :],
                                preferred_element_type=F32).astype(BF16))
        y = _swiglu_chunks(jnp.concatenate(rows, axis=0), wg_ref, wu_ref, wo_ref, (0,))
        y_hi = y.astype(BF16)
        y_lo = (y - y_hi.astype(F32)).astype(BF16)
        slot_lane = _lane_iota((MOE_PART, MOE_SLOTS)).astype(F32) + 1.0
        for k, part in enumerate(parts):
            slots = slice(k * MOE_SLOTS, (k + 1) * MOE_SLOTS)
            scatter = jnp.where((rank_e[part] == slot_lane) & (gate_e[part] > 0.0), 1.0, 0.0).astype(BF16)
            back = (jnp.dot(scatter, y_hi[slots], preferred_element_type=F32)
                    + jnp.dot(scatter, y_lo[slots], preferred_element_type=F32))
            tot_ref[part, :] += gate_e[part] * back

    @pl.when(jnp.max(rank_e) > float(MOE_SLOTS))
    def _():
        tot_ref[...] += gate_e * _swiglu_chunks(h_ref[0], wg_ref, wu_ref, wo_ref, (0,))

    @pl.when(e == pl.num_programs(2) - 1)
    def _():
        o_ref[0] = _residual_out(x_ref[0], mod_ref[0][5:6], tot_ref[...], gf_ref, final_norm)


def _moe_ffn(x, h, gates, gates_t, wg, wu, wo, mod3, mod_row, g_final, final_norm):
    b, s, d = x.shape
    tm = min(MOE_ROWS, s)
    tp = MOE_PART
    tri = jnp.asarray(np.tril(np.ones((tp, tp), np.float32)), BF16)
    row = lambda bi, i, e: (bi, i, 0)
    const = lambda bi, i, e: (0, 0)
    expert = lambda a: pl.BlockSpec((1,) + a.shape[1:], lambda bi, i, e: (e, 0, 0))
    return pl.pallas_call(
        functools.partial(_moe_body, final_norm=final_norm),
        grid=(b, s // tm, wg.shape[0]),
        in_specs=[pl.BlockSpec((1, tm, d), row, pipeline_mode=pl.Buffered(1)),
                  pl.BlockSpec((1, tm, d), row),
                  pl.BlockSpec((1, tm, LANES), row),
                  pl.BlockSpec((1, GATE_ROWS, tm), lambda bi, i, e: (bi, 0, i)),
                  pl.BlockSpec((tp, tp), const), pl.BlockSpec((tp, tp), const),
                  expert(wg), expert(wu), expert(wo),
                  pl.BlockSpec((1, N_MOD, d), lambda bi, i, e: (mod_row(bi), 0, 0)),
                  pl.BlockSpec((1, d), const)],
        out_specs=pl.BlockSpec((1, tm, d), row),
        out_shape=jax.ShapeDtypeStruct((b, s, d), F32),
        scratch_shapes=[pltpu.VMEM((tm, d), F32), pltpu.VMEM((tm, LANES), F32),
                        pltpu.VMEM((GATE_ROWS, tm), F32)],
        compiler_params=_params("parallel", "parallel", "arbitrary"),
        name="moe_ffn",
    )(x, h, gates, gates_t, tri, tri.T, wg, wu, wo, mod3, g_final)


def _deinterleave(n):
    return np.concatenate([np.arange(0, n, 2), np.arange(1, n, 2)])


def _in_proj_columns(pad):
    pads = lambda n: np.full((n,), pad)
    cols = []
    for hd in range(GQA_HEADS):
        cols.append(64 * hd + _deinterleave(64))
    for g in range(GQA_KV_HEADS):
        cols += [512 + 64 * g + _deinterleave(64), pads(64)]
    for g in range(GQA_KV_HEADS):
        cols += [640 + 64 * g + np.arange(64), pads(64)]
    for u in range(2 * DIFF_HEADS):
        cols.append(768 + 32 * u + _deinterleave(32))
    for u in range(2 * DIFF_HEADS):
        cols.append(1024 + 32 * u + _deinterleave(32))
    for hd in range(DIFF_HEADS):
        cols += [1280 + 64 * hd + np.arange(64), pads(64)]
    cols += [1536 + np.arange(MLA_Q_RANK), pads(64)]
    cols.append(1728 + np.arange(MLA_KV_RANK))
    cols += [1856 + _deinterleave(32), 1856 + _deinterleave(32), pads(64)]
    return np.concatenate(cols)


def _uq_columns(pad):
    per = MLA_NOPE_DIM + MLA_ROPE_DIM
    cols = []
    for pr in range(2):
        h0, h1 = 2 * pr, 2 * pr + 1
        cols += [per * h0 + np.arange(64), per * h1 + np.arange(64),
                 per * h0 + 64 + _deinterleave(32), per * h1 + 64 + _deinterleave(32),
                 np.full((64,), pad)]
    return np.concatenate(cols)


def _ukv_columns(pad):
    per = MLA_NOPE_DIM + MLA_V_DIM
    k = [per * hd + np.arange(64) for hd in range(MLA_HEADS)]
    v = []
    for hd in range(MLA_HEADS):
        v += [per * hd + 64 + np.arange(64), np.full((64,), pad)]
    return np.concatenate(k + v)


def _take_cols(w, cols):
    w_ext = jnp.concatenate([w, jnp.zeros((w.shape[0], 1), w.dtype)], axis=1)
    return jnp.take(w_ext, jnp.asarray(cols), axis=1)


def _rope_tables(s):
    t = np.arange(s)
    rows = (t // GRID_W).astype(np.float32)
    cols = (t % GRID_W).astype(np.float32)
    out = []
    for dim in (HEAD_DIM, DIFF_QK_DIM):
        quarter = dim // 4
        half = dim // 2
        inv_freq = (ROPE_THETA ** (-np.arange(quarter, dtype=np.float32) / quarter)).astype(np.float32)
        ang = np.concatenate([rows[:, None] * inv_freq, cols[:, None] * inv_freq], axis=-1)
        lane = np.arange(LANES)
        idx = (lane % dim) % half
        sign = np.where((lane % dim) < half, -1.0, 1.0).astype(np.float32)
        out += [jnp.asarray(np.cos(ang)[:, idx], F32), jnp.asarray(np.sin(ang)[:, idx] * sign, F32)]
    return out


def kernel(x, c, ctx, c_ctx, w_mod, b_mod, g_attn, g_ffn, w_in, w_out, gqa_gq, gqa_gk,
           diff_lq1, diff_lk1, diff_lq2, diff_lk2, diff_gsub, mla_gcq, mla_gckv, mla_wuq, mla_wukv,
           ffn_w_in, ffn_w_out, moe_router, moe_w_in, moe_w_out, g_final):
    b, s, d = x.shape
    n_ctx = ctx.shape[1]
    depth = w_mod.shape[0]
    tables = _rope_tables(s)
    no_rotation = [jnp.ones((n_ctx, LANES), F32), jnp.zeros((n_ctx, LANES), F32)] * 2
    in_cols = _in_proj_columns(w_in.shape[2])
    uq_cols = _uq_columns(mla_wuq.shape[2])
    ukv_cols = _ukv_columns(mla_wukv.shape[2])
    perm64 = _deinterleave(HEAD_DIM)
    bd = jnp.asarray(np.kron(np.eye(LANES // HEAD_DIM), np.full((HEAD_DIM, HEAD_DIM), 1.0 / HEAD_DIM)), F32)

    mod_rows = 16
    c_all = jnp.zeros((mod_rows, d), F32).at[:b].set(c).at[b].set(c_ctx)
    pad_lanes = lambda v: jnp.zeros((1, LANES), F32).at[0, :v.shape[0]].set(v)

    xc = ctx
    for l in range(depth):
        need_ctx = l < depth - 1
        lam_init = 0.8 - 0.6 * math.exp(-0.3 * l)
        mod3 = _modulation(c_all, w_mod[l], b_mod[l]).reshape(mod_rows, N_MOD, d)

        w_in_p = _take_cols(w_in[l], in_cols).astype(BF16)
        wuq_p = jnp.zeros((2 * LANES, 512), F32).at[:MLA_Q_RANK].set(
            _take_cols(mla_wuq[l], uq_cols)).astype(BF16)
        wukv_p = _take_cols(mla_wukv[l], ukv_cols).astype(BF16)
        gq2 = jnp.tile(gqa_gq[l][perm64], 2).reshape(1, LANES)
        gk2 = jnp.tile(gqa_gk[l][perm64], 2).reshape(1, LANES)
        gcq = jnp.zeros((1, 2 * LANES), F32).at[0, :MLA_Q_RANK].set(mla_gcq[l])
        gckv = mla_gckv[l].reshape(1, MLA_KV_RANK)

        proj_w = (g_attn[l].reshape(1, d), w_in_p)
        proj_aux = (gq2, gk2, bd, gcq, gckv, wuq_p, wukv_p)
        lat = _project(x, mod3, lambda bi: bi, *proj_w, tables, *proj_aux)
        ctxp = _project(xc, mod3, lambda bi: b, *proj_w, no_rotation, *proj_aux)

        diff_extra = [pad_lanes(diff_lq1[l]), pad_lanes(diff_lk1[l]), pad_lanes(diff_lq2[l]),
                      pad_lanes(diff_lk2[l]), pad_lanes(diff_gsub[l])]
        diff_body = functools.partial(_diff_body, lam_init=lam_init)
        diff_body.__name__ = "_diff_body"

        def attend(q, streams):
            kv = lambda ik, iv: [(p[ik], p[iv]) for p in streams]
            rows = q[0].shape[1]
            stack = 8 * ROW_TILE
            oa = _attention(_gqa_body, q[0], kv(1, 2), [], n_groups=2, wq=256, wk=LANES,
                            n_s=4, tq=min(stack // 4, rows), per_head=False)
            ob = _attention(diff_body, q[3], kv(4, 5), diff_extra, n_groups=2, wq=LANES, wk=LANES,
                            n_s=2, tq=min(stack // 2, rows), per_head=True)
            om = _attention(_mla_body, q[6], kv(7, 8), [], n_groups=2, wq=256, wk=256,
                            n_s=1, tq=min(stack, rows), per_head=True)
            return oa, ob, om

        w_out_b = w_out[l].astype(BF16)
        g2 = g_ffn[l].reshape(1, d)
        dense = l % 2 == 0
        if dense:
            n_hidden = ffn_w_out.shape[1]
            wg = ffn_w_in[l // 2][:, :n_hidden].astype(BF16)
            wu = ffn_w_in[l // 2][:, n_hidden:].astype(BF16)
            wo = ffn_w_out[l // 2].astype(BF16)
            w_router = None
        else:
            n_hidden = moe_w_out.shape[2]
            wi = moe_w_in[l // 2]
            wg = wi[:, :, :n_hidden].astype(BF16)
            wu = wi[:, :, n_hidden:].astype(BF16)
            wo = moe_w_out[l // 2].astype(BF16)
            wr = jnp.zeros((d, LANES), F32).at[:, :N_EXPERTS].set(moe_router[l // 2])
            wr_hi = wr.astype(BF16)
            w_router = jnp.stack([wr_hi, (wr - wr_hi.astype(F32)).astype(BF16)])
        last = l == depth - 1

        def channel_mix(xs, attn_out, mod_row, final_norm):
            res = _out_project(xs, *attn_out, w_out_b, mod3, g2, mod_row, w_router)
            gf = g_final.reshape(1, d)
            if dense:
                return _dense_ffn(res[0], res[1], wg, wu, wo, mod3, mod_row, gf, final_norm)
            return _moe_ffn(res[0], res[1], res[2], res[3], wg, wu, wo, mod3, mod_row, gf, final_norm)

        x_new = channel_mix(x, attend(lat, [lat, ctxp]), lambda bi: bi, last)
        if need_ctx:
            xc = channel_mix(xc, attend(ctxp, [ctxp]), lambda bi: b, False)
        x = x_new
    return x
```

```python
import functools
import math

import numpy as np
import jax
import jax.numpy as jnp
from jax import lax
from jax.experimental import pallas as pl
from jax.experimental.pallas import tpu as pltpu

LANES = 128
MXU_TILE = 256
VMEM_LIMIT = 60 * 1024 * 1024

NORM_EPS = 1e-6
ROPE_THETA = 10000.0
GRID_W = 64
N_MOD = 6
HEAD_DIM = 64
GQA_HEADS, GQA_KV_HEADS = 8, 2
DIFF_HEADS, DIFF_QK_DIM, DIFF_V_DIM = 4, 32, 64
MLA_HEADS, MLA_Q_RANK, MLA_KV_RANK = 4, 192, 128
MLA_NOPE_DIM, MLA_ROPE_DIM, MLA_V_DIM = 64, 32, 64
N_EXPERTS, TOP_K = 8, 2
GATE_ROWS = 16

ROW_TILE = 256
PROJ_ROWS = 512
KEY_CHUNK = MXU_TILE
FFN_CHUNK = MXU_TILE
FFN_ROWS = 512
MOE_ROWS = 1024
MOE_PART = 512
MOE_SLOTS = 256

F32 = jnp.float32
BF16 = jnp.bfloat16
HIGHEST = lax.Precision.HIGHEST


def _params(*sem):
    return pltpu.CompilerParams(dimension_semantics=sem, vmem_limit_bytes=VMEM_LIMIT)


def _lane_iota(shape):
    return lax.broadcasted_iota(jnp.int32, shape, len(shape) - 1)


def _mod_body(c_ref, w_ref, b_ref, o_ref):
    c = c_ref[...]
    sc = c * jax.nn.sigmoid(c)
    o_ref[...] = jnp.dot(sc, w_ref[...], precision=HIGHEST,
                         preferred_element_type=F32) + b_ref[...]


def _modulation(c_all, w, b):
    rows, d = c_all.shape
    n = w.shape[1]
    tn = 1536
    return pl.pallas_call(
        _mod_body,
        grid=(n // tn,),
        in_specs=[pl.BlockSpec((rows, d), lambda j: (0, 0)),
                  pl.BlockSpec((d, tn), lambda j: (0, j)),
                  pl.BlockSpec((1, tn), lambda j: (0, j))],
        out_specs=pl.BlockSpec((rows, tn), lambda j: (0, j)),
        out_shape=jax.ShapeDtypeStruct((rows, n), F32),
        compiler_params=_params("arbitrary"),
        name="modulation",
    )(c_all, w, b.reshape(1, n))


def _rms_rows(x, width):
    return lax.rsqrt(jnp.sum(x * x, axis=-1, keepdims=True) * (1.0 / width) + NORM_EPS)


def _rope(t, cos, sin_signed, half):
    lane = _lane_iota(t.shape)
    partner = jnp.where((lane & half) == 0,
                        pltpu.roll(t, LANES - half, 1), pltpu.roll(t, half, 1))
    return t * cos + partner * sin_signed


def _with_ones(t):
    return jnp.where(_lane_iota(t.shape) == HEAD_DIM, 1.0, t)


def _proj_body(x_ref, mod_ref, g_ref, w_ref, ca_ref, sa_ref, cb_ref, sb_ref,
               gq_ref, gk_ref, bd_ref, gcq_ref, gckv_ref, wuq_ref, wukv_ref,
               qa_ref, ka_ref, va_ref, qb_ref, kb_ref, vb_ref, qm_ref, km_ref, vm_ref, p_ref,
               *, scale_a, scale_b, scale_m):
    @pl.when(pl.program_id(0) == 0)
    def _():
        p_ref[...] = jnp.zeros(p_ref.shape, F32)

    x = x_ref[0]
    d = x.shape[-1]
    mod = mod_ref[0]
    y = x * _rms_rows(x, d) * g_ref[...]
    h = (y * (1.0 + mod[1:2]) + mod[0:1]).astype(BF16)
    p_new = jnp.dot(h, w_ref[...], preferred_element_type=F32)
    p = p_ref

    ca, sa, cb, sb = ca_ref[...], sa_ref[...], cb_ref[...], sb_ref[...]
    bd = bd_ref[...]
    tile = lambda ref, j: (0, slice(None), slice(j * LANES, (j + 1) * LANES))
    cols = lambda a, base, j: a[:, base + j * LANES:base + (j + 1) * LANES]

    def head_norm(t, g):
        sq = t * t
        hi = sq.astype(BF16)
        lo = (sq - hi.astype(F32)).astype(BF16)
        ms = (jnp.dot(hi, bd, preferred_element_type=F32) + jnp.dot(lo, bd, preferred_element_type=F32))
        return t * lax.rsqrt(ms + NORM_EPS) * g

    for j in range(4):
        t = head_norm(cols(p, 0, j), gq_ref[...])
        qa_ref[tile(qa_ref, j)] = (_rope(t, ca, sa, 32) * scale_a).astype(BF16)
    for j in range(2):
        t = head_norm(cols(p, 512, j), gk_ref[...])
        ka_ref[tile(ka_ref, j)] = _rope(t, ca, sa, 32).astype(BF16)
        va_ref[tile(va_ref, j)] = _with_ones(cols(p, 768, j)).astype(BF16)

    for j in range(2):
        qb_ref[tile(qb_ref, j)] = (_rope(cols(p, 1024, j), cb, sb, 16) * scale_b).astype(BF16)
        kb_ref[tile(kb_ref, j)] = _rope(cols(p, 1280, j), cb, sb, 16).astype(BF16)
    for j in range(4):
        vb_ref[tile(vb_ref, j)] = _with_ones(cols(p, 1536, j)).astype(BF16)

    cq = p[:, 2048:2304]
    cqn = (cq * _rms_rows(cq, MLA_Q_RANK) * gcq_ref[...]).astype(BF16)
    uq = jnp.dot(cqn, wuq_ref[...], preferred_element_type=F32)
    ckv = p[:, 2304:2432]
    ckvn = (ckv * _rms_rows(ckv, MLA_KV_RANK) * gckv_ref[...]).astype(BF16)
    ukv = jnp.dot(ckvn, wukv_ref[...], preferred_element_type=F32)
    kr = _rope(p[:, 2432:2560], cb, sb, 16).astype(BF16)
    for pr in range(2):
        qm_ref[tile(qm_ref, 2 * pr)] = (cols(uq, 0, 2 * pr) * scale_m).astype(BF16)
        qr = _rope(cols(uq, 0, 2 * pr + 1), cb, sb, 16)
        qm_ref[tile(qm_ref, 2 * pr + 1)] = (qr * scale_m).astype(BF16)
        km_ref[tile(km_ref, 2 * pr)] = cols(ukv, 0, pr).astype(BF16)
        km_ref[tile(km_ref, 2 * pr + 1)] = kr
    for j in range(4):
        vm_ref[tile(vm_ref, j)] = _with_ones(cols(ukv, 256, j)).astype(BF16)
    p_ref[...] = p_new


def _project(x, mod3, mod_row, g, w, tables, gq2, gk2, bd, gcq, gckv, wuq, wukv):
    b, s, d = x.shape
    tm = min(PROJ_ROWS, s)
    nt = s // tm
    n_tiles = b * nt
    widths = (512, 256, 256, 256, 256, 512, 512, 512, 512)
    log2e = math.log2(math.e)
    body = functools.partial(_proj_body, scale_a=HEAD_DIM ** -0.5 * log2e,
                             scale_b=DIFF_QK_DIM ** -0.5 * log2e,
                             scale_m=(MLA_NOPE_DIM + MLA_ROPE_DIM) ** -0.5 * log2e)
    cur = lambda j: (jnp.minimum(j, n_tiles - 1) // nt, jnp.minimum(j, n_tiles - 1) % nt)
    prev = lambda j: (jnp.maximum(j - 1, 0) // nt, jnp.maximum(j - 1, 0) % nt)
    const = lambda shape: pl.BlockSpec(shape, lambda j: (0,) * len(shape))
    tab = pl.BlockSpec((tm, LANES), lambda j: (prev(j)[1], 0))
    return pl.pallas_call(
        body,
        grid=(n_tiles + 1,),
        in_specs=[pl.BlockSpec((1, tm, d), lambda j: (cur(j)[0], cur(j)[1], 0)),
                  pl.BlockSpec((1, N_MOD, d), lambda j: (mod_row(cur(j)[0]), 0, 0)),
                  const((1, d)), const(w.shape), tab, tab, tab, tab,
                  const((1, LANES)), const((1, LANES)), const((LANES, LANES)),
                  const(gcq.shape), const(gckv.shape), const(wuq.shape), const(wukv.shape)],
        out_specs=[pl.BlockSpec((1, tm, wd), lambda j: (prev(j)[0], prev(j)[1], 0)) for wd in widths],
        out_shape=[jax.ShapeDtypeStruct((b, s, wd), BF16) for wd in widths],
        scratch_shapes=[pltpu.VMEM((tm, w.shape[1]), F32)],
        compiler_params=_params("arbitrary"),
        name="project",
    )(x, mod3, g, w, *tables, gq2, gk2, bd, gcq, gckv, wuq, wukv)


def _attn_step(qs, kv_refs, s_ref, mp_ref, mb_ref, acc_ref):
    step = pl.program_id(0)

    @pl.when(step == 0)
    def _():
        s_ref[...] = jnp.zeros(s_ref.shape, F32)
        mb_ref[...] = jnp.zeros(mb_ref.shape, F32)
        acc_ref[...] = jnp.ones(acc_ref.shape, F32)

    raw = acc_ref[...]
    done = raw * (1.0 / raw[:, HEAD_DIM:HEAD_DIM + 1])

    acc = None
    chunks = [(k_ref, v_ref, slice(i * KEY_CHUNK, (i + 1) * KEY_CHUNK))
              for k_ref, v_ref in kv_refs for i in range(k_ref.shape[1] // KEY_CHUNK)]
    for c, (k_ref, v_ref, keys) in enumerate(chunks):
        s_new = lax.dot_general(qs, k_ref[0, keys, :], (((1,), (1,)), ((), ())),
                                preferred_element_type=F32)
        s_old = s_ref[c]
        m_old = mb_ref[...]
        p0 = jnp.exp2(s_old[:, :LANES] - m_old)
        p1 = jnp.exp2(s_old[:, LANES:] - m_old)
        part = jnp.dot(jnp.concatenate([p0, p1], axis=1).astype(BF16), v_ref[0, keys, :],
                       preferred_element_type=F32)
        acc = part if acc is None else acc + part
        s_ref[c] = s_new
        mc = jnp.maximum(s_new[:, :LANES], s_new[:, LANES:])
        mp_ref[...] = mc if c == 0 else jnp.maximum(mp_ref[...], mc)
    acc_ref[...] = acc
    mb_ref[...] = jnp.broadcast_to(jnp.max(mp_ref[...], axis=-1, keepdims=True), mb_ref.shape)
    return done


def _split_kv(refs, n_kv):
    return [(refs[2 * i], refs[2 * i + 1]) for i in range(n_kv)], refs[2 * n_kv:]


def _pack_heads(even, odd):
    lo = _lane_iota(even.shape) < HEAD_DIM
    return jnp.where(lo, even, pltpu.roll(odd, HEAD_DIM, 1))


def _gqa_body(q_ref, *rest, n_kv, n_tiles):
    tq = q_ref.shape[1]
    lo = _lane_iota((tq, LANES)) < HEAD_DIM
    heads = []
    for j in range(2):
        t = q_ref[0, :, j * LANES:(j + 1) * LANES].astype(F32)
        heads.append(jnp.where(lo, t, 0.0))
        heads.append(jnp.where(lo, pltpu.roll(t, HEAD_DIM, 1), 0.0))
    qs = jnp.concatenate(heads, axis=0).astype(BF16)
    kv, (o_ref, *scratch) = _split_kv(rest, n_kv)
    o = _attn_step(qs, kv, *scratch)
    for j in range(2):
        o_ref[0, :, j * LANES:(j + 1) * LANES] = _pack_heads(
            o[(2 * j) * tq:(2 * j + 1) * tq], o[(2 * j + 1) * tq:(2 * j + 2) * tq]).astype(o_ref.dtype)


def _head_parity(n_tiles):
    step = pl.program_id(0)
    return jnp.minimum(step, n_tiles - 1) % 2, jnp.maximum(step - 2, 0) % 2


def _store_head(o_ref, res, parity):
    @pl.when(parity == 0)
    def _():
        o_ref[0, :, 0:HEAD_DIM] = res[:, 0:HEAD_DIM].astype(o_ref.dtype)

    @pl.when(parity == 1)
    def _():
        o_ref[0, :, HEAD_DIM:LANES] = pltpu.roll(res, HEAD_DIM, 1)[:, HEAD_DIM:LANES].astype(o_ref.dtype)


def _diff_body(q_ref, *rest, n_kv, n_tiles, lam_init):
    kv, (lq1_ref, lk1_ref, lq2_ref, lk2_ref, gsub_ref, o_ref, *scratch) = _split_kv(rest, n_kv)
    tq = q_ref.shape[1]
    par, par_done = _head_parity(n_tiles)
    lane = _lane_iota((tq, LANES))
    t = q_ref[0]
    zero = jnp.zeros_like(t)
    qs = jnp.concatenate([jnp.where((lane // DIFF_QK_DIM) == 2 * par + j, t, zero) for j in range(2)],
                         axis=0)
    o = _attn_step(qs, kv, *scratch)
    lam = (jnp.exp(jnp.sum(lq1_ref[...] * lk1_ref[...], axis=-1, keepdims=True))
           - jnp.exp(jnp.sum(lq2_ref[...] * lk2_ref[...], axis=-1, keepdims=True)) + lam_init)
    d = o[0:tq] - lam * o[tq:2 * tq]
    ms = jnp.sum(jnp.where(lane < DIFF_V_DIM, d * d, 0.0), axis=-1, keepdims=True) * (1.0 / DIFF_V_DIM)
    _store_head(o_ref, (d * lax.rsqrt(ms + NORM_EPS) * gsub_ref[...]) * (1.0 - lam_init), par_done)


def _mla_body(q_ref, *rest, n_kv, n_tiles):
    kv, (o_ref, *scratch) = _split_kv(rest, n_kv)
    par, par_done = _head_parity(n_tiles)
    t = q_ref[0]
    lane = _lane_iota(t.shape)
    nope0 = MLA_NOPE_DIM * par
    rope0 = LANES + MLA_ROPE_DIM * par
    mine = ((lane >= nope0) & (lane < nope0 + MLA_NOPE_DIM)) | (
        (lane >= rope0) & (lane < rope0 + MLA_ROPE_DIM))
    qs = jnp.where(mine, t, jnp.zeros_like(t))
    _store_head(o_ref, _attn_step(qs, kv, *scratch), par_done)


def _attention(body, q, kvs, extra, *, n_groups, wq, wk, n_s, tq, per_head):
    b, q_rows, _ = q.shape
    q_tiles = q_rows // tq
    wo = LANES if per_head else wq
    heads = 2 if per_head else 1
    n_chunks = sum(k.shape[1] // KEY_CHUNK for k, _ in kvs)
    m_rows = n_s * tq
    n_tiles = b * n_groups * q_tiles * heads

    def split(t):
        t, par = t // heads, t % heads
        return t // (n_groups * q_tiles), (t // q_tiles) % n_groups, t % q_tiles, par

    cur = lambda j: split(jnp.minimum(j, n_tiles - 1))
    prev = lambda j: split(jnp.clip(j - 1, 0, n_tiles - 1))
    done = lambda j: split(jnp.maximum(j - 2, 0))
    kv_specs, kv_args = [], []
    for k, v in kvs:
        kv_specs += [pl.BlockSpec((1, k.shape[1], wk), lambda j: (cur(j)[0], 0, cur(j)[1])),
                     pl.BlockSpec((1, v.shape[1], LANES),
                                  lambda j: (prev(j)[0], 0, prev(j)[1] * heads + prev(j)[3]))]
        kv_args += [k, v]
    extra_specs = [pl.BlockSpec(e.shape, lambda j: (0, 0)) for e in extra]
    return pl.pallas_call(
        functools.partial(body, n_kv=len(kvs), n_tiles=n_tiles),
        grid=(n_tiles + 2,),
        in_specs=[pl.BlockSpec((1, tq, wq), lambda j: (cur(j)[0], cur(j)[2], cur(j)[1]))]
        + kv_specs + extra_specs,
        out_specs=pl.BlockSpec((1, tq, wo), lambda j: (done(j)[0], done(j)[2], done(j)[1])),
        out_shape=jax.ShapeDtypeStruct((b, q_rows, n_groups * wo), BF16),
        scratch_shapes=[pltpu.VMEM((n_chunks, m_rows, KEY_CHUNK), F32),
                        pltpu.VMEM((m_rows, LANES), F32),
                        pltpu.VMEM((m_rows, LANES), F32),
                        pltpu.VMEM((m_rows, LANES), F32)],
        compiler_params=_params("arbitrary"),
        name=body.__name__.strip("_"),
    )(q, *kv_args, *extra)


def _router_gates(logits):
    lane = _lane_iota(logits.shape).astype(F32)
    neg = jnp.float32(-jnp.inf)
    z = jnp.where(lane < N_EXPERTS, logits, neg)
    m1 = jnp.max(z, axis=-1, keepdims=True)
    i1 = jnp.min(jnp.where(z == m1, lane, float(LANES)), axis=-1, keepdims=True)
    z2 = jnp.where(lane == i1, neg, z)
    m2 = jnp.max(z2, axis=-1, keepdims=True)
    i2 = jnp.min(jnp.where(z2 == m2, lane, float(LANES)), axis=-1, keepdims=True)
    e2 = jnp.exp(m2 - m1)
    den = 1.0 + e2
    return jnp.where(lane == i1, 1.0 / den, 0.0) + jnp.where(lane == i2, e2 / den, 0.0)


def _mixer_residual(x_ref, oa_ref, ob_ref, om_ref, w_ref, mod, g_ref):
    wa, wb = oa_ref.shape[2], ob_ref.shape[2]
    y = jnp.dot(oa_ref[0], w_ref[0:wa, :], preferred_element_type=F32)
    y += jnp.dot(ob_ref[0], w_ref[wa:wa + wb, :], preferred_element_type=F32)
    y += jnp.dot(om_ref[0], w_ref[wa + wb:, :], preferred_element_type=F32)
    x = x_ref[0] + mod[2:3] * y
    n = x * _rms_rows(x, x.shape[-1]) * g_ref[...]
    return x, n * (1.0 + mod[4:5]) + mod[3:4]


def _mixer_specs(x, oa, ob, om, w, mod_row, tm):
    d = x.shape[2]
    row = lambda bi, i: (bi, i, 0)
    return [pl.BlockSpec((1, tm, d), row),
            pl.BlockSpec((1, tm, oa.shape[2]), row),
            pl.BlockSpec((1, tm, ob.shape[2]), row),
            pl.BlockSpec((1, tm, om.shape[2]), row),
            pl.BlockSpec(w.shape, lambda bi, i: (0, 0)),
            pl.BlockSpec((1, N_MOD, d), lambda bi, i: (mod_row(bi), 0, 0)),
            pl.BlockSpec((1, d), lambda bi, i: (0, 0))]


def _outproj_body(x_ref, oa_ref, ob_ref, om_ref, w_ref, mod_ref, g_ref, wr_ref,
                  xo_ref, h_ref, gates_ref, gates_t_ref):
    x, h = _mixer_residual(x_ref, oa_ref, ob_ref, om_ref, w_ref, mod_ref[0], g_ref)
    xo_ref[0] = x
    h_hi = h.astype(BF16)
    h_ref[0] = h_hi
    h_lo = (h - h_hi.astype(F32)).astype(BF16)
    logits = (jnp.dot(h_hi, wr_ref[0], preferred_element_type=F32)
              + jnp.dot(h_lo, wr_ref[0], preferred_element_type=F32)
              + jnp.dot(h_hi, wr_ref[1], preferred_element_type=F32))
    gates = _router_gates(logits)
    gates_ref[0] = gates
    gates_t_ref[0] = gates.T[:gates_t_ref.shape[1]]


def _out_project_routed(x, oa, ob, om, w, mod3, g, mod_row, w_router):
    b, s, d = x.shape
    tm = min(ROW_TILE, s)
    row = lambda bi, i: (bi, i, 0)
    return pl.pallas_call(
        _outproj_body,
        grid=(b, s // tm),
        in_specs=_mixer_specs(x, oa, ob, om, w, mod_row, tm)
        + [pl.BlockSpec(w_router.shape, lambda bi, i: (0, 0, 0))],
        out_specs=[pl.BlockSpec((1, tm, d), row), pl.BlockSpec((1, tm, d), row),
                   pl.BlockSpec((1, tm, LANES), row),
                   pl.BlockSpec((1, GATE_ROWS, tm), lambda bi, i: (bi, 0, i))],
        out_shape=[jax.ShapeDtypeStruct((b, s, d), F32), jax.ShapeDtypeStruct((b, s, d), BF16),
                   jax.ShapeDtypeStruct((b, s, LANES), F32), jax.ShapeDtypeStruct((b, GATE_ROWS, s), F32)],
        compiler_params=_params("parallel", "arbitrary"),
        name="out_project_routed",
    )(x, oa, ob, om, w, mod3, g, w_router)


def _swiglu_chunks(h, wg_ref, wu_ref, wo_ref, lead):
    acc = None
    for j in range(wo_ref.shape[len(lead)] // FFN_CHUNK):
        cols = slice(j * FFN_CHUNK, (j + 1) * FFN_CHUNK)
        gate = jnp.dot(h, wg_ref[lead + (slice(None), cols)], preferred_element_type=F32)
        up = jnp.dot(h, wu_ref[lead + (slice(None), cols)], preferred_element_type=F32)
        a = (gate * jax.nn.sigmoid(gate) * up).astype(BF16)
        part = jnp.dot(a, wo_ref[lead + (cols, slice(None))], preferred_element_type=F32)
        acc = part if acc is None else acc + part
    return acc


def _residual_out(x, gate, y, gf_ref, final_norm):
    x = x + gate * y
    if final_norm:
        x = x * _rms_rows(x, x.shape[-1]) * gf_ref[...]
    return x


def _ffn_body(x_ref, oa_ref, ob_ref, om_ref, w_ref, mod_ref, g_ref, wg_ref, wu_ref, wo_ref, gf_ref,
              o_ref, *, final_norm):
    mod = mod_ref[0]
    x, h = _mixer_residual(x_ref, oa_ref, ob_ref, om_ref, w_ref, mod, g_ref)
    y = _swiglu_chunks(h.astype(BF16), wg_ref, wu_ref, wo_ref, ())
    o_ref[0] = _residual_out(x, mod[5:6], y, gf_ref, final_norm)


def _mix_dense_ffn(x, oa, ob, om, w, mod3, g, mod_row, wg, wu, wo, g_final, final_norm):
    b, s, d = x.shape
    tm = min(FFN_ROWS, s)
    resident = lambda a: pl.BlockSpec(a.shape, lambda bi, i: (0, 0), pipeline_mode=pl.Buffered(1))
    return pl.pallas_call(
        functools.partial(_ffn_body, final_norm=final_norm),
        grid=(b, s // tm),
        in_specs=_mixer_specs(x, oa, ob, om, w, mod_row, tm)
        + [resident(wg), resident(wu), resident(wo), pl.BlockSpec((1, d), lambda bi, i: (0, 0))],
        out_specs=pl.BlockSpec((1, tm, d), lambda bi, i: (bi, i, 0)),
        out_shape=jax.ShapeDtypeStruct((b, s, d), F32),
        compiler_params=_params("parallel", "arbitrary"),
        name="dense_ffn",
    )(x, oa, ob, om, w, mod3, g, wg, wu, wo, g_final)


def _moe_body(x_ref, h_ref, gates_ref, gates_t_ref, tri_ref, tri_t_ref, wg_ref, wu_ref, wo_ref,
              mod_ref, gf_ref, o_ref, tot_ref, rank_ref, rank_t_ref, *, final_norm):
    e = pl.program_id(2)
    tm = h_ref.shape[1]
    parts = [slice(k * MOE_PART, (k + 1) * MOE_PART) for k in range(tm // MOE_PART)]
    gates = gates_ref[0]
    lane = _lane_iota(gates.shape)

    @pl.when(e == 0)
    def _():
        tot_ref[...] = jnp.zeros(tot_ref.shape, F32)
        live = (gates > 0.0).astype(BF16)
        live_t = (gates_t_ref[0] > 0.0).astype(BF16)
        for part in parts:
            rank_ref[part, :] = jnp.dot(tri_ref[...], live[part], preferred_element_type=F32)
            rank_t_ref[:, part] = jnp.dot(live_t[:, part], tri_t_ref[...], preferred_element_type=F32)

    pick = lambda a: jnp.sum(jnp.where(lane == e, a, 0.0), axis=-1, keepdims=True)
    gate_e = pick(gates)
    rank_e = pick(rank_ref[...])

    @pl.when(jnp.max(rank_e) <= float(MOE_SLOTS))
    def _():
        rank_row = rank_t_ref[pl.ds(e, 1), :]
        live_row = gates_t_ref[0, pl.ds(e, 1), :] > 0.0
        slot_sub = lax.broadcasted_iota(jnp.int32, (MOE_SLOTS, MOE_PART), 0).astype(F32) + 1.0
        rows = []
        for part in parts:
            gather = jnp.where((rank_row[:, part] == slot_sub) & live_row[:, part], 1.0, 0.0)
            rows.append(jnp.dot(gather.astype(BF16), h_ref[0, part, :],
                                preferred_element_type=F32).astype(BF16))
        y = _swiglu_chunks(jnp.concatenate(rows, axis=0), wg_ref, wu_ref, wo_ref, (0,))
        y_hi = y.astype(BF16)
        y_lo = (y - y_hi.astype(F32)).astype(BF16)
        slot_lane = _lane_iota((MOE_PART, MOE_SLOTS)).astype(F32) + 1.0
        for k, part in enumerate(parts):
            slots = slice(k * MOE_SLOTS, (k + 1) * MOE_SLOTS)
            scatter = jnp.where((rank_e[part] == slot_lane) & (gate_e[part] > 0.0), 1.0, 0.0).astype(BF16)
            back = (jnp.dot(scatter, y_hi[slots], preferred_element_type=F32)
                    + jnp.dot(scatter, y_lo[slots], preferred_element_type=F32))
            tot_ref[part, :] += gate_e[part] * back

    @pl.when(jnp.max(rank_e) > float(MOE_SLOTS))
    def _():
        tot_ref[...] += gate_e * _swiglu_chunks(h_ref[0], wg_ref, wu_ref, wo_ref, (0,))

    @pl.when(e == pl.num_programs(2) - 1)
    def _():
        o_ref[0] = _residual_out(x_ref[0], mod_ref[0][5:6], tot_ref[...], gf_ref, final_norm)


def _moe_ffn(x, h, gates, gates_t, wg, wu, wo, mod3, mod_row, g_final, final_norm):
    b, s, d = x.shape
    tm = min(MOE_ROWS, s)
    tp = MOE_PART
    tri = jnp.asarray(np.tril(np.ones((tp, tp), np.float32)), BF16)
    row = lambda bi, i, e: (bi, i, 0)
    const = lambda bi, i, e: (0, 0)
    expert = lambda a: pl.BlockSpec((1,) + a.shape[1:], lambda bi, i, e: (e, 0, 0))
    return pl.pallas_call(
        functools.partial(_moe_body, final_norm=final_norm),
        grid=(b, s // tm, wg.shape[0]),
        in_specs=[pl.BlockSpec((1, tm, d), row, pipeline_mode=pl.Buffered(1)),
                  pl.BlockSpec((1, tm, d), row),
                  pl.BlockSpec((1, tm, LANES), row),
                  pl.BlockSpec((1, GATE_ROWS, tm), lambda bi, i, e: (bi, 0, i)),
                  pl.BlockSpec((tp, tp), const), pl.BlockSpec((tp, tp), const),
                  expert(wg), expert(wu), expert(wo),
                  pl.BlockSpec((1, N_MOD, d), lambda bi, i, e: (mod_row(bi), 0, 0)),
                  pl.BlockSpec((1, d), const)],
        out_specs=pl.BlockSpec((1, tm, d), row),
        out_shape=jax.ShapeDtypeStruct((b, s, d), F32),
        scratch_shapes=[pltpu.VMEM((tm, d), F32), pltpu.VMEM((tm, LANES), F32),
                        pltpu.VMEM((GATE_ROWS, tm), F32)],
        compiler_params=_params("parallel", "parallel", "arbitrary"),
        name="moe_ffn",
    )(x, h, gates, gates_t, tri, tri.T, wg, wu, wo, mod3, g_final)


def _deinterleave(n):
    return np.concatenate([np.arange(0, n, 2), np.arange(1, n, 2)])


def _in_proj_columns(pad):
    pads = lambda n: np.full((n,), pad)
    cols = []
    for hd in range(GQA_HEADS):
        cols.append(64 * hd + _deinterleave(64))
    for g in range(GQA_KV_HEADS):
        cols += [512 + 64 * g + _deinterleave(64), pads(64)]
    for g in range(GQA_KV_HEADS):
        cols += [640 + 64 * g + np.arange(64), pads(64)]
    for u in range(2 * DIFF_HEADS):
        cols.append(768 + 32 * u + _deinterleave(32))
    for u in range(2 * DIFF_HEADS):
        cols.append(1024 + 32 * u + _deinterleave(32))
    for hd in range(DIFF_HEADS):
        cols += [1280 + 64 * hd + np.arange(64), pads(64)]
    cols += [1536 + np.arange(MLA_Q_RANK), pads(64)]
    cols.append(1728 + np.arange(MLA_KV_RANK))
    cols += [1856 + _deinterleave(32), 1856 + _deinterleave(32), pads(64)]
    return np.concatenate(cols)


def _uq_columns(pad):
    per = MLA_NOPE_DIM + MLA_ROPE_DIM
    cols = []
    for pr in range(2):
        h0, h1 = 2 * pr, 2 * pr + 1
        cols += [per * h0 + np.arange(64), per * h1 + np.arange(64),
                 per * h0 + 64 + _deinterleave(32), per * h1 + 64 + _deinterleave(32),
                 np.full((64,), pad)]
    return np.concatenate(cols)


def _ukv_columns(pad):
    per = MLA_NOPE_DIM + MLA_V_DIM
    k = [per * hd + np.arange(64) for hd in range(MLA_HEADS)]
    v = []
    for hd in range(MLA_HEADS):
        v += [per * hd + 64 + np.arange(64), np.full((64,), pad)]
    return np.concatenate(k + v)


def _take_cols(w, cols):
    w_ext = jnp.concatenate([w, jnp.zeros((w.shape[0], 1), w.dtype)], axis=1)
    return jnp.take(w_ext, jnp.asarray(cols), axis=1)


def _rope_tables(s):
    t = np.arange(s)
    rows = (t // GRID_W).astype(np.float32)
    cols = (t % GRID_W).astype(np.float32)
    out = []
    for dim in (HEAD_DIM, DIFF_QK_DIM):
        quarter = dim // 4
        half = dim // 2
        inv_freq = (ROPE_THETA ** (-np.arange(quarter, dtype=np.float32) / quarter)).astype(np.float32)
        ang = np.concatenate([rows[:, None] * inv_freq, cols[:, None] * inv_freq], axis=-1)
        lane = np.arange(LANES)
        idx = (lane % dim) % half
        sign = np.where((lane % dim) < half, -1.0, 1.0).astype(np.float32)
        out += [jnp.asarray(np.cos(ang)[:, idx], F32), jnp.asarray(np.sin(ang)[:, idx] * sign, F32)]
    return out


def kernel(x, c, ctx, c_ctx, w_mod, b_mod, g_attn, g_ffn, w_in, w_out, gqa_gq, gqa_gk,
           diff_lq1, diff_lk1, diff_lq2, diff_lk2, diff_gsub, mla_gcq, mla_gckv, mla_wuq, mla_wukv,
           ffn_w_in, ffn_w_out, moe_router, moe_w_in, moe_w_out, g_final):
    b, s, d = x.shape
    n_ctx = ctx.shape[1]
    depth = w_mod.shape[0]
    tables = _rope_tables(s)
    no_rotation = [jnp.ones((n_ctx, LANES), F32), jnp.zeros((n_ctx, LANES), F32)] * 2
    in_cols = _in_proj_columns(w_in.shape[2])
    uq_cols = _uq_columns(mla_wuq.shape[2])
    ukv_cols = _ukv_columns(mla_wukv.shape[2])
    perm64 = _deinterleave(HEAD_DIM)
    bd = jnp.asarray(np.kron(np.eye(LANES // HEAD_DIM), np.full((HEAD_DIM, HEAD_DIM), 1.0 / HEAD_DIM)), F32)

    mod_rows = 16
    c_all = jnp.zeros((mod_rows, d), F32).at[:b].set(c).at[b].set(c_ctx)
    pad_lanes = lambda v: jnp.zeros((1, LANES), F32).at[0, :v.shape[0]].set(v)

    xc = ctx
    for l in range(depth):
        need_ctx = l < depth - 1
        lam_init = 0.8 - 0.6 * math.exp(-0.3 * l)
        mod3 = _modulation(c_all, w_mod[l], b_mod[l]).reshape(mod_rows, N_MOD, d)

        w_in_p = _take_cols(w_in[l], in_cols).astype(BF16)
        wuq_p = jnp.zeros((2 * LANES, 512), F32).at[:MLA_Q_RANK].set(
            _take_cols(mla_wuq[l], uq_cols)).astype(BF16)
        wukv_p = _take_cols(mla_wukv[l], ukv_cols).astype(BF16)
        gq2 = jnp.tile(gqa_gq[l][perm64], 2).reshape(1, LANES)
        gk2 = jnp.tile(gqa_gk[l][perm64], 2).reshape(1, LANES)
        gcq = jnp.zeros((1, 2 * LANES), F32).at[0, :MLA_Q_RANK].set(mla_gcq[l])
        gckv = mla_gckv[l].reshape(1, MLA_KV_RANK)

        proj_w = (g_attn[l].reshape(1, d), w_in_p)
        proj_aux = (gq2, gk2, bd, gcq, gckv, wuq_p, wukv_p)
        lat = _project(x, mod3, lambda bi: bi, *proj_w, tables, *proj_aux)
        ctxp = _project(xc, mod3, lambda bi: b, *proj_w, no_rotation, *proj_aux)

        diff_extra = [pad_lanes(diff_lq1[l]), pad_lanes(diff_lk1[l]), pad_lanes(diff_lq2[l]),
                      pad_lanes(diff_lk2[l]), pad_lanes(diff_gsub[l])]
        diff_body = functools.partial(_diff_body, lam_init=lam_init)
        diff_body.__name__ = "_diff_body"

        def attend(q, streams):
            kv = lambda ik, iv: [(p[ik], p[iv]) for p in streams]
            rows = q[0].shape[1]
            stack = 8 * ROW_TILE
            oa = _attention(_gqa_body, q[0], kv(1, 2), [], n_groups=2, wq=256, wk=LANES,
                            n_s=4, tq=min(stack // 4, rows), per_head=False)
            ob = _attention(diff_body, q[3], kv(4, 5), diff_extra, n_groups=2, wq=LANES, wk=LANES,
                            n_s=2, tq=min(stack // 2, rows), per_head=True)
            om = _attention(_mla_body, q[6], kv(7, 8), [], n_groups=2, wq=256, wk=256,
                            n_s=1, tq=min(stack, rows), per_head=True)
            return oa, ob, om

        w_out_b = w_out[l].astype(BF16)
        g2 = g_ffn[l].reshape(1, d)
        dense = l % 2 == 0
        if dense:
            n_hidden = ffn_w_out.shape[1]
            wg = ffn_w_in[l // 2][:, :n_hidden].astype(BF16)
            wu = ffn_w_in[l // 2][:, n_hidden:].astype(BF16)
            wo = ffn_w_out[l // 2].astype(BF16)
            w_router = None
        else:
            n_hidden = moe_w_out.shape[2]
            wi = moe_w_in[l // 2]
            wg = wi[:, :, :n_hidden].astype(BF16)
            wu = wi[:, :, n_hidden:].astype(BF16)
            wo = moe_w_out[l // 2].astype(BF16)
            wr = jnp.zeros((d, LANES), F32).at[:, :N_EXPERTS].set(moe_router[l // 2])
            wr_hi = wr.astype(BF16)
            w_router = jnp.stack([wr_hi, (wr - wr_hi.astype(F32)).astype(BF16)])
        last = l == depth - 1

        def channel_mix(xs, attn_out, mod_row, final_norm):
            gf = g_final.reshape(1, d)
            if dense:
                return _mix_dense_ffn(xs, *attn_out, w_out_b, mod3, g2, mod_row, wg, wu, wo, gf, final_norm)
            res = _out_project_routed(xs, *attn_out, w_out_b, mod3, g2, mod_row, w_router)
            return _moe_ffn(*res, wg, wu, wo, mod3, mod_row, gf, final_norm)

        x_new = channel_mix(x, attend(lat, [lat, ctxp]), lambda bi: bi, last)
        if need_ctx:
            xc = channel_mix(xc, attend(ctxp, [ctxp]), lambda bi: b, False)
        x = x_new
    return x
```

```python
import functools
import math

import numpy as np
import jax
import jax.numpy as jnp
from jax import lax
from jax.experimental import pallas as pl
from jax.experimental.pallas import tpu as pltpu

LANES = 128
MXU_TILE = 256
VMEM_LIMIT = 60 * 1024 * 1024

NORM_EPS = 1e-6
ROPE_THETA = 10000.0
GRID_W = 64
N_MOD = 6
HEAD_DIM = 64
GQA_HEADS, GQA_KV_HEADS = 8, 2
DIFF_HEADS, DIFF_QK_DIM, DIFF_V_DIM = 4, 32, 64
MLA_HEADS, MLA_Q_RANK, MLA_KV_RANK = 4, 192, 128
MLA_NOPE_DIM, MLA_ROPE_DIM, MLA_V_DIM = 64, 32, 64
N_EXPERTS, TOP_K = 8, 2
GATE_ROWS = 16

ROW_TILE = 256
PROJ_ROWS = 512
KEY_CHUNK = MXU_TILE
FFN_CHUNK = MXU_TILE
FFN_ROWS = 512
MOE_ROWS = 1024
MOE_PART = 512
MOE_SLOTS = 256

F32 = jnp.float32
BF16 = jnp.bfloat16
HIGHEST = lax.Precision.HIGHEST


def _params(*sem):
    return pltpu.CompilerParams(dimension_semantics=sem, vmem_limit_bytes=VMEM_LIMIT)


def _lane_iota(shape):
    return lax.broadcasted_iota(jnp.int32, shape, len(shape) - 1)


def _mod_body(c_ref, w_ref, b_ref, o_ref):
    c = c_ref[...]
    sc = c * jax.nn.sigmoid(c)
    o_ref[...] = jnp.dot(sc, w_ref[...], precision=HIGHEST,
                         preferred_element_type=F32) + b_ref[...]


def _modulation(c_all, w, b):
    rows, d = c_all.shape
    n = w.shape[1]
    tn = 1536
    return pl.pallas_call(
        _mod_body,
        grid=(n // tn,),
        in_specs=[pl.BlockSpec((rows, d), lambda j: (0, 0)),
                  pl.BlockSpec((d, tn), lambda j: (0, j)),
                  pl.BlockSpec((1, tn), lambda j: (0, j))],
        out_specs=pl.BlockSpec((rows, tn), lambda j: (0, j)),
        out_shape=jax.ShapeDtypeStruct((rows, n), F32),
        compiler_params=_params("arbitrary"),
        name="modulation",
    )(c_all, w, b.reshape(1, n))


def _rms_rows(x, width):
    return lax.rsqrt(jnp.sum(x * x, axis=-1, keepdims=True) * (1.0 / width) + NORM_EPS)


def _rope(t, cos, sin_signed, half):
    lane = _lane_iota(t.shape)
    partner = jnp.where((lane & half) == 0,
                        pltpu.roll(t, LANES - half, 1), pltpu.roll(t, half, 1))
    return t * cos + partner * sin_signed


def _with_ones(t):
    return jnp.where(_lane_iota(t.shape) == HEAD_DIM, 1.0, t)


def _proj_body(x_ref, mod_ref, g_ref, w_ref, ca_ref, sa_ref, cb_ref, sb_ref,
               gq_ref, gk_ref, bd_ref, gcq_ref, gckv_ref, wuq_ref, wukv_ref,
               qa_ref, ka_ref, va_ref, qb_ref, kb_ref, vb_ref, qm_ref, km_ref, vm_ref, p_ref,
               *, scale_a, scale_b, scale_m):
    @pl.when(pl.program_id(0) == 0)
    def _():
        p_ref[...] = jnp.zeros(p_ref.shape, F32)

    x = x_ref[0]
    d = x.shape[-1]
    mod = mod_ref[0]
    y = x * _rms_rows(x, d) * g_ref[...]
    h = (y * (1.0 + mod[1:2]) + mod[0:1]).astype(BF16)
    p_new = jnp.dot(h, w_ref[...], preferred_element_type=F32)
    p = p_ref

    ca, sa, cb, sb = ca_ref[...], sa_ref[...], cb_ref[...], sb_ref[...]
    bd = bd_ref[...]
    tile = lambda ref, j: (0, slice(None), slice(j * LANES, (j + 1) * LANES))
    cols = lambda a, base, j: a[:, base + j * LANES:base + (j + 1) * LANES]

    def head_norm(t, g):
        sq = t * t
        hi = sq.astype(BF16)
        lo = (sq - hi.astype(F32)).astype(BF16)
        ms = (jnp.dot(hi, bd, preferred_element_type=F32) + jnp.dot(lo, bd, preferred_element_type=F32))
        return t * lax.rsqrt(ms + NORM_EPS) * g

    for j in range(4):
        t = head_norm(cols(p, 0, j), gq_ref[...])
        qa_ref[tile(qa_ref, j)] = (_rope(t, ca, sa, 32) * scale_a).astype(BF16)
    for j in range(2):
        t = head_norm(cols(p, 512, j), gk_ref[...])
        ka_ref[tile(ka_ref, j)] = _rope(t, ca, sa, 32).astype(BF16)
        va_ref[tile(va_ref, j)] = _with_ones(cols(p, 768, j)).astype(BF16)

    for j in range(2):
        qb_ref[tile(qb_ref, j)] = (_rope(cols(p, 1024, j), cb, sb, 16) * scale_b).astype(BF16)
        kb_ref[tile(kb_ref, j)] = _rope(cols(p, 1280, j), cb, sb, 16).astype(BF16)
    for j in range(4):
        vb_ref[tile(vb_ref, j)] = _with_ones(cols(p, 1536, j)).astype(BF16)

    cq = p[:, 2048:2304]
    cqn = (cq * _rms_rows(cq, MLA_Q_RANK) * gcq_ref[...]).astype(BF16)
    uq = jnp.dot(cqn, wuq_ref[...], preferred_element_type=F32)
    ckv = p[:, 2304:2432]
    ckvn = (ckv * _rms_rows(ckv, MLA_KV_RANK) * gckv_ref[...]).astype(BF16)
    ukv = jnp.dot(ckvn, wukv_ref[...], preferred_element_type=F32)
    kr = _rope(p[:, 2432:2560], cb, sb, 16).astype(BF16)
    for pr in range(2):
        qm_ref[tile(qm_ref, 2 * pr)] = (cols(uq, 0, 2 * pr) * scale_m).astype(BF16)
        qr = _rope(cols(uq, 0, 2 * pr + 1), cb, sb, 16)
        qm_ref[tile(qm_ref, 2 * pr + 1)] = (qr * scale_m).astype(BF16)
        km_ref[tile(km_ref, 2 * pr)] = cols(ukv, 0, pr).astype(BF16)
        km_ref[tile(km_ref, 2 * pr + 1)] = kr
    for j in range(4):
        vm_ref[tile(vm_ref, j)] = _with_ones(cols(ukv, 256, j)).astype(BF16)
    p_ref[...] = p_new


def _project(x, mod3, mod_row, g, w, tables, gq2, gk2, bd, gcq, gckv, wuq, wukv):
    b, s, d = x.shape
    tm = min(PROJ_ROWS, s)
    nt = s // tm
    n_tiles = b * nt
    widths = (512, 256, 256, 256, 256, 512, 512, 512, 512)
    log2e = math.log2(math.e)
    body = functools.partial(_proj_body, scale_a=HEAD_DIM ** -0.5 * log2e,
                             scale_b=DIFF_QK_DIM ** -0.5 * log2e,
                             scale_m=(MLA_NOPE_DIM + MLA_ROPE_DIM) ** -0.5 * log2e)
    cur = lambda j: (jnp.minimum(j, n_tiles - 1) // nt, jnp.minimum(j, n_tiles - 1) % nt)
    prev = lambda j: (jnp.maximum(j - 1, 0) // nt, jnp.maximum(j - 1, 0) % nt)
    const = lambda shape: pl.BlockSpec(shape, lambda j: (0,) * len(shape))
    tab = pl.BlockSpec((tm, LANES), lambda j: (prev(j)[1], 0))
    return pl.pallas_call(
        body,
        grid=(n_tiles + 1,),
        in_specs=[pl.BlockSpec((1, tm, d), lambda j: (cur(j)[0], cur(j)[1], 0)),
                  pl.BlockSpec((1, N_MOD, d), lambda j: (mod_row(cur(j)[0]), 0, 0)),
                  const((1, d)), const(w.shape), tab, tab, tab, tab,
                  const((1, LANES)), const((1, LANES)), const((LANES, LANES)),
                  const(gcq.shape), const(gckv.shape), const(wuq.shape), const(wukv.shape)],
        out_specs=[pl.BlockSpec((1, tm, wd), lambda j: (prev(j)[0], prev(j)[1], 0)) for wd in widths],
        out_shape=[jax.ShapeDtypeStruct((b, s, wd), BF16) for wd in widths],
        scratch_shapes=[pltpu.VMEM((tm, w.shape[1]), F32)],
        compiler_params=_params("arbitrary"),
        name="project",
    )(x, mod3, g, w, *tables, gq2, gk2, bd, gcq, gckv, wuq, wukv)


def _attn_step(qs, kv_refs, s_ref, mp_ref, mb_ref, acc_ref):
    step = pl.program_id(0)

    @pl.when(step == 0)
    def _():
        s_ref[...] = jnp.zeros(s_ref.shape, F32)
        mb_ref[...] = jnp.zeros(mb_ref.shape, F32)
        acc_ref[...] = jnp.ones(acc_ref.shape, F32)

    raw = acc_ref[...]
    done = raw * (1.0 / raw[:, HEAD_DIM:HEAD_DIM + 1])

    acc = None
    chunks = [(k_ref, v_ref, slice(i * KEY_CHUNK, (i + 1) * KEY_CHUNK))
              for k_ref, v_ref in kv_refs for i in range(k_ref.shape[1] // KEY_CHUNK)]
    for c, (k_ref, v_ref, keys) in enumerate(chunks):
        s_new = lax.dot_general(qs, k_ref[0, keys, :], (((1,), (1,)), ((), ())),
                                preferred_element_type=F32)
        s_old = s_ref[c]
        m_old = mb_ref[...]
        p0 = jnp.exp2(s_old[:, :LANES] - m_old)
        p1 = jnp.exp2(s_old[:, LANES:] - m_old)
        part = jnp.dot(jnp.concatenate([p0, p1], axis=1).astype(BF16), v_ref[0, keys, :],
                       preferred_element_type=F32)
        acc = part if acc is None else acc + part
        s_ref[c] = s_new
        mc = jnp.maximum(s_new[:, :LANES], s_new[:, LANES:])
        mp_ref[...] = mc if c == 0 else jnp.maximum(mp_ref[...], mc)
    acc_ref[...] = acc
    mb_ref[...] = jnp.broadcast_to(jnp.max(mp_ref[...], axis=-1, keepdims=True), mb_ref.shape)
    return done


def _split_kv(refs, n_kv):
    return [(refs[2 * i], refs[2 * i + 1]) for i in range(n_kv)], refs[2 * n_kv:]


def _pack_heads(even, odd):
    lo = _lane_iota(even.shape) < HEAD_DIM
    return jnp.where(lo, even, pltpu.roll(odd, HEAD_DIM, 1))


def _gqa_body(q_ref, *rest, n_kv, n_tiles):
    tq = q_ref.shape[1]
    lo = _lane_iota((tq, LANES)) < HEAD_DIM
    heads = []
    for j in range(2):
        t = q_ref[0, :, j * LANES:(j + 1) * LANES].astype(F32)
        heads.append(jnp.where(lo, t, 0.0))
        heads.append(jnp.where(lo, pltpu.roll(t, HEAD_DIM, 1), 0.0))
    qs = jnp.concatenate(heads, axis=0).astype(BF16)
    kv, (o_ref, *scratch) = _split_kv(rest, n_kv)
    o = _attn_step(qs, kv, *scratch)
    for j in range(2):
        o_ref[0, :, j * LANES:(j + 1) * LANES] = _pack_heads(
            o[(2 * j) * tq:(2 * j + 1) * tq], o[(2 * j + 1) * tq:(2 * j + 2) * tq]).astype(o_ref.dtype)


def _head_parity(n_tiles):
    step = pl.program_id(0)
    return jnp.minimum(step, n_tiles - 1) % 2, jnp.maximum(step - 2, 0) % 2


def _store_head(o_ref, res, parity):
    @pl.when(parity == 0)
    def _():
        o_ref[0, :, 0:HEAD_DIM] = res[:, 0:HEAD_DIM].astype(o_ref.dtype)

    @pl.when(parity == 1)
    def _():
        o_ref[0, :, HEAD_DIM:LANES] = pltpu.roll(res, HEAD_DIM, 1)[:, HEAD_DIM:LANES].astype(o_ref.dtype)


def _diff_body(q_ref, *rest, n_kv, n_tiles, lam_init):
    kv, (lq1_ref, lk1_ref, lq2_ref, lk2_ref, gsub_ref, o_ref, *scratch) = _split_kv(rest, n_kv)
    tq = q_ref.shape[1]
    par, par_done = _head_parity(n_tiles)
    lane = _lane_iota((tq, LANES))
    t = q_ref[0]
    zero = jnp.zeros_like(t)
    qs = jnp.concatenate([jnp.where((lane // DIFF_QK_DIM) == 2 * par + j, t, zero) for j in range(2)],
                         axis=0)
    o = _attn_step(qs, kv, *scratch)
    lam = (jnp.exp(jnp.sum(lq1_ref[...] * lk1_ref[...], axis=-1, keepdims=True))
           - jnp.exp(jnp.sum(lq2_ref[...] * lk2_ref[...], axis=-1, keepdims=True)) + lam_init)
    d = o[0:tq] - lam * o[tq:2 * tq]
    ms = jnp.sum(jnp.where(lane < DIFF_V_DIM, d * d, 0.0), axis=-1, keepdims=True) * (1.0 / DIFF_V_DIM)
    _store_head(o_ref, (d * lax.rsqrt(ms + NORM_EPS) * gsub_ref[...]) * (1.0 - lam_init), par_done)


def _mla_body(q_ref, *rest, n_kv, n_tiles):
    kv, (o_ref, *scratch) = _split_kv(rest, n_kv)
    par, par_done = _head_parity(n_tiles)
    t = q_ref[0]
    lane = _lane_iota(t.shape)
    nope0 = MLA_NOPE_DIM * par
    rope0 = LANES + MLA_ROPE_DIM * par
    mine = ((lane >= nope0) & (lane < nope0 + MLA_NOPE_DIM)) | (
        (lane >= rope0) & (lane < rope0 + MLA_ROPE_DIM))
    qs = jnp.where(mine, t, jnp.zeros_like(t))
    _store_head(o_ref, _attn_step(qs, kv, *scratch), par_done)


def _attention(body, q, kvs, extra, *, n_groups, wq, wk, n_s, tq, per_head):
    b, q_rows, _ = q.shape
    q_tiles = q_rows // tq
    wo = LANES if per_head else wq
    heads = 2 if per_head else 1
    n_chunks = sum(k.shape[1] // KEY_CHUNK for k, _ in kvs)
    m_rows = n_s * tq
    n_tiles = b * n_groups * q_tiles * heads

    def split(t):
        t, par = t // heads, t % heads
        return t // (n_groups * q_tiles), (t // q_tiles) % n_groups, t % q_tiles, par

    cur = lambda j: split(jnp.minimum(j, n_tiles - 1))
    prev = lambda j: split(jnp.clip(j - 1, 0, n_tiles - 1))
    done = lambda j: split(jnp.maximum(j - 2, 0))
    kv_specs, kv_args = [], []
    for k, v in kvs:
        kv_specs += [pl.BlockSpec((1, k.shape[1], wk), lambda j: (cur(j)[0], 0, cur(j)[1])),
                     pl.BlockSpec((1, v.shape[1], LANES),
                                  lambda j: (prev(j)[0], 0, prev(j)[1] * heads + prev(j)[3]))]
        kv_args += [k, v]
    extra_specs = [pl.BlockSpec(e.shape, lambda j: (0, 0)) for e in extra]
    return pl.pallas_call(
        functools.partial(body, n_kv=len(kvs), n_tiles=n_tiles),
        grid=(n_tiles + 2,),
        in_specs=[pl.BlockSpec((1, tq, wq), lambda j: (cur(j)[0], cur(j)[2], cur(j)[1]))]
        + kv_specs + extra_specs,
        out_specs=pl.BlockSpec((1, tq, wo), lambda j: (done(j)[0], done(j)[2], done(j)[1])),
        out_shape=jax.ShapeDtypeStruct((b, q_rows, n_groups * wo), BF16),
        scratch_shapes=[pltpu.VMEM((n_chunks, m_rows, KEY_CHUNK), F32),
                        pltpu.VMEM((m_rows, LANES), F32),
                        pltpu.VMEM((m_rows, LANES), F32),
                        pltpu.VMEM((m_rows, LANES), F32)],
        compiler_params=_params("arbitrary"),
        name=body.__name__.strip("_"),
    )(q, *kv_args, *extra)


def _router_gates(logits):
    lane = _lane_iota(logits.shape).astype(F32)
    neg = jnp.float32(-jnp.inf)
    z = jnp.where(lane < N_EXPERTS, logits, neg)
    m1 = jnp.max(z, axis=-1, keepdims=True)
    i1 = jnp.min(jnp.where(z == m1, lane, float(LANES)), axis=-1, keepdims=True)
    z2 = jnp.where(lane == i1, neg, z)
    m2 = jnp.max(z2, axis=-1, keepdims=True)
    i2 = jnp.min(jnp.where(z2 == m2, lane, float(LANES)), axis=-1, keepdims=True)
    e2 = jnp.exp(m2 - m1)
    den = 1.0 + e2
    return jnp.where(lane == i1, 1.0 / den, 0.0) + jnp.where(lane == i2, e2 / den, 0.0)


def _mixer_residual(x_ref, oa_ref, ob_ref, om_ref, w_ref, mod, g_ref):
    wa, wb = oa_ref.shape[2], ob_ref.shape[2]
    y = jnp.dot(oa_ref[0], w_ref[0:wa, :], preferred_element_type=F32)
    y += jnp.dot(ob_ref[0], w_ref[wa:wa + wb, :], preferred_element_type=F32)
    y += jnp.dot(om_ref[0], w_ref[wa + wb:, :], preferred_element_type=F32)
    x = x_ref[0] + mod[2:3] * y
    n = x * _rms_rows(x, x.shape[-1]) * g_ref[...]
    return x, n * (1.0 + mod[4:5]) + mod[3:4]


def _mixer_specs(x, oa, ob, om, w, mod_row, tm):
    d = x.shape[2]
    row = lambda bi, i: (bi, i, 0)
    return [pl.BlockSpec((1, tm, d), row),
            pl.BlockSpec((1, tm, oa.shape[2]), row),
            pl.BlockSpec((1, tm, ob.shape[2]), row),
            pl.BlockSpec((1, tm, om.shape[2]), row),
            pl.BlockSpec(w.shape, lambda bi, i: (0, 0)),
            pl.BlockSpec((1, N_MOD, d), lambda bi, i: (mod_row(bi), 0, 0)),
            pl.BlockSpec((1, d), lambda bi, i: (0, 0))]


def _outproj_body(x_ref, oa_ref, ob_ref, om_ref, w_ref, mod_ref, g_ref, wr_ref,
                  xo_ref, h_ref, gates_ref, gates_t_ref, hprev_ref):
    @pl.when(pl.program_id(0) == 0)
    def _():
        hprev_ref[...] = jnp.zeros(hprev_ref.shape, F32)

    h_prev = hprev_ref[...]
    h_hi = h_prev.astype(BF16)
    h_lo = (h_prev - h_hi.astype(F32)).astype(BF16)
    logits = (jnp.dot(h_hi, wr_ref[0], preferred_element_type=F32)
              + jnp.dot(h_lo, wr_ref[0], preferred_element_type=F32)
              + jnp.dot(h_hi, wr_ref[1], preferred_element_type=F32))
    gates = _router_gates(logits)
    gates_ref[0] = gates
    gates_t_ref[0] = gates.T[:gates_t_ref.shape[1]]

    x, h = _mixer_residual(x_ref, oa_ref, ob_ref, om_ref, w_ref, mod_ref[0], g_ref)
    xo_ref[0] = x
    h_ref[0] = h.astype(BF16)
    hprev_ref[...] = h


def _out_project_routed(x, oa, ob, om, w, mod3, g, mod_row, w_router):
    b, s, d = x.shape
    tm = min(ROW_TILE, s)
    nt = s // tm
    n_tiles = b * nt
    cur = lambda j: (jnp.minimum(j, n_tiles - 1) // nt, jnp.minimum(j, n_tiles - 1) % nt)
    prev = lambda j: (jnp.maximum(j - 1, 0) // nt, jnp.maximum(j - 1, 0) % nt)
    flat = lambda spec: pl.BlockSpec(spec.block_shape, lambda j, f=spec.index_map: f(*cur(j)))
    row = lambda j: (cur(j)[0], cur(j)[1], 0)
    return pl.pallas_call(
        _outproj_body,
        grid=(n_tiles + 1,),
        in_specs=[flat(sp) for sp in _mixer_specs(x, oa, ob, om, w, mod_row, tm)]
        + [pl.BlockSpec(w_router.shape, lambda j: (0, 0, 0))],
        out_specs=[pl.BlockSpec((1, tm, d), row), pl.BlockSpec((1, tm, d), row),
                   pl.BlockSpec((1, tm, LANES), lambda j: (prev(j)[0], prev(j)[1], 0)),
                   pl.BlockSpec((1, GATE_ROWS, tm), lambda j: (prev(j)[0], 0, prev(j)[1]))],
        out_shape=[jax.ShapeDtypeStruct((b, s, d), F32), jax.ShapeDtypeStruct((b, s, d), BF16),
                   jax.ShapeDtypeStruct((b, s, LANES), F32), jax.ShapeDtypeStruct((b, GATE_ROWS, s), F32)],
        scratch_shapes=[pltpu.VMEM((tm, d), F32)],
        compiler_params=_params("arbitrary"),
        name="out_project_routed",
    )(x, oa, ob, om, w, mod3, g, w_router)


def _swiglu_chunks(h, wg_ref, wu_ref, wo_ref, lead):
    acc = None
    for j in range(wo_ref.shape[len(lead)] // FFN_CHUNK):
        cols = slice(j * FFN_CHUNK, (j + 1) * FFN_CHUNK)
        gate = jnp.dot(h, wg_ref[lead + (slice(None), cols)], preferred_element_type=F32)
        up = jnp.dot(h, wu_ref[lead + (slice(None), cols)], preferred_element_type=F32)
        a = (gate * jax.nn.sigmoid(gate) * up).astype(BF16)
        part = jnp.dot(a, wo_ref[lead + (cols, slice(None))], preferred_element_type=F32)
        acc = part if acc is None else acc + part
    return acc


def _residual_out(x, gate, y, gf_ref, final_norm):
    x = x + gate * y
    if final_norm:
        x = x * _rms_rows(x, x.shape[-1]) * gf_ref[...]
    return x


def _ffn_body(x_ref, oa_ref, ob_ref, om_ref, w_ref, mod_ref, g_ref, wg_ref, wu_ref, wo_ref, gf_ref,
              o_ref, *, final_norm):
    mod = mod_ref[0]
    x, h = _mixer_residual(x_ref, oa_ref, ob_ref, om_ref, w_ref, mod, g_ref)
    y = _swiglu_chunks(h.astype(BF16), wg_ref, wu_ref, wo_ref, ())
    o_ref[0] = _residual_out(x, mod[5:6], y, gf_ref, final_norm)


def _mix_dense_ffn(x, oa, ob, om, w, mod3, g, mod_row, wg, wu, wo, g_final, final_norm):
    b, s, d = x.shape
    tm = min(FFN_ROWS, s)
    resident = lambda a: pl.BlockSpec(a.shape, lambda bi, i: (0, 0), pipeline_mode=pl.Buffered(1))
    return pl.pallas_call(
        functools.partial(_ffn_body, final_norm=final_norm),
        grid=(b, s // tm),
        in_specs=_mixer_specs(x, oa, ob, om, w, mod_row, tm)
        + [resident(wg), resident(wu), resident(wo), pl.BlockSpec((1, d), lambda bi, i: (0, 0))],
        out_specs=pl.BlockSpec((1, tm, d), lambda bi, i: (bi, i, 0)),
        out_shape=jax.ShapeDtypeStruct((b, s, d), F32),
        compiler_params=_params("parallel", "arbitrary"),
        name="dense_ffn",
    )(x, oa, ob, om, w, mod3, g, wg, wu, wo, g_final)


def _moe_body(x_ref, h_ref, gates_ref, gates_t_ref, tri_ref, tri_t_ref, wg_ref, wu_ref, wo_ref,
              mod_ref, gf_ref, o_ref, tot_ref, rank_ref, rank_t_ref, *, final_norm):
    e = pl.program_id(2)
    tm = h_ref.shape[1]
    parts = [slice(k * MOE_PART, (k + 1) * MOE_PART) for k in range(tm // MOE_PART)]
    gates = gates_ref[0]
    lane = _lane_iota(gates.shape)

    @pl.when(e == 0)
    def _():
        tot_ref[...] = jnp.zeros(tot_ref.shape, F32)
        live = (gates > 0.0).astype(BF16)
        live_t = (gates_t_ref[0] > 0.0).astype(BF16)
        for part in parts:
            rank_ref[part, :] = jnp.dot(tri_ref[...], live[part], preferred_element_type=F32)
            rank_t_ref[:, part] = jnp.dot(live_t[:, part], tri_t_ref[...], preferred_element_type=F32)

    pick = lambda a: jnp.sum(jnp.where(lane == e, a, 0.0), axis=-1, keepdims=True)
    rank_row = rank_t_ref[pl.ds(e, 1), :]
    fits = jnp.max(rank_row) <= float(MOE_SLOTS)

    @pl.when(fits)
    def _():
        gate_e = pick(gates)
        rank_e = pick(rank_ref[...])
        live_row = gates_t_ref[0, pl.ds(e, 1), :] > 0.0
        slot_sub = lax.broadcasted_iota(jnp.int32, (MOE_SLOTS, MOE_PART), 0).astype(F32) + 1.0
        rows = []
        for part in parts:
            gather = jnp.where((rank_row[:, part] == slot_sub) & live_row[:, part], 1.0, 0.0)
            rows.append(jnp.dot(gather.astype(BF16), h_ref[0, part, :],
                                preferred_element_type=F32).astype(BF16))
        y = _swiglu_chunks(jnp.concatenate(rows, axis=0), wg_ref, wu_ref, wo_ref, (0,))
        y_hi = y.astype(BF16)
        y_lo = (y - y_hi.astype(F32)).astype(BF16)
        slot_lane = _lane_iota((MOE_PART, MOE_SLOTS)).astype(F32) + 1.0
        for k, part in enumerate(parts):
            slots = slice(k * MOE_SLOTS, (k + 1) * MOE_SLOTS)
            scatter = jnp.where((rank_e[part] == slot_lane) & (gate_e[part] > 0.0), 1.0, 0.0).astype(BF16)
            back = (jnp.dot(scatter, y_hi[slots], preferred_element_type=F32)
                    + jnp.dot(scatter, y_lo[slots], preferred_element_type=F32))
            tot_ref[part, :] += gate_e[part] * back

    @pl.when(jnp.logical_not(fits))
    def _():
        tot_ref[...] += pick(gates) * _swiglu_chunks(h_ref[0], wg_ref, wu_ref, wo_ref, (0,))

    @pl.when(e == pl.num_programs(2) - 1)
    def _():
        o_ref[0] = _residual_out(x_ref[0], mod_ref[0][5:6], tot_ref[...], gf_ref, final_norm)


def _moe_ffn(x, h, gates, gates_t, wg, wu, wo, mod3, mod_row, g_final, final_norm):
    b, s, d = x.shape
    tm = min(MOE_ROWS, s)
    tp = MOE_PART
    tri = jnp.asarray(np.tril(np.ones((tp, tp), np.float32)), BF16)
    row = lambda bi, i, e: (bi, i, 0)
    const = lambda bi, i, e: (0, 0)
    expert = lambda a: pl.BlockSpec((1,) + a.shape[1:], lambda bi, i, e: (e, 0, 0))
    return pl.pallas_call(
        functools.partial(_moe_body, final_norm=final_norm),
        grid=(b, s // tm, wg.shape[0]),
        in_specs=[pl.BlockSpec((1, tm, d), row, pipeline_mode=pl.Buffered(1)),
                  pl.BlockSpec((1, tm, d), row),
                  pl.BlockSpec((1, tm, LANES), row),
                  pl.BlockSpec((1, GATE_ROWS, tm), lambda bi, i, e: (bi, 0, i)),
                  pl.BlockSpec((tp, tp), const), pl.BlockSpec((tp, tp), const),
                  expert(wg), expert(wu), expert(wo),
                  pl.BlockSpec((1, N_MOD, d), lambda bi, i, e: (mod_row(bi), 0, 0)),
                  pl.BlockSpec((1, d), const)],
        out_specs=pl.BlockSpec((1, tm, d), row),
        out_shape=jax.ShapeDtypeStruct((b, s, d), F32),
        scratch_shapes=[pltpu.VMEM((tm, d), F32), pltpu.VMEM((tm, LANES), F32),
                        pltpu.VMEM((GATE_ROWS, tm), F32)],
        compiler_params=_params("parallel", "parallel", "arbitrary"),
        name="moe_ffn",
    )(x, h, gates, gates_t, tri, tri.T, wg, wu, wo, mod3, g_final)


def _deinterleave(n):
    return np.concatenate([np.arange(0, n, 2), np.arange(1, n, 2)])


def _in_proj_columns(pad):
    pads = lambda n: np.full((n,), pad)
    cols = []
    for hd in range(GQA_HEADS):
        cols.append(64 * hd + _deinterleave(64))
    for g in range(GQA_KV_HEADS):
        cols += [512 + 64 * g + _deinterleave(64), pads(64)]
    for g in range(GQA_KV_HEADS):
        cols += [640 + 64 * g + np.arange(64), pads(64)]
    for u in range(2 * DIFF_HEADS):
        cols.append(768 + 32 * u + _deinterleave(32))
    for u in range(2 * DIFF_HEADS):
        cols.append(1024 + 32 * u + _deinterleave(32))
    for hd in range(DIFF_HEADS):
        cols += [1280 + 64 * hd + np.arange(64), pads(64)]
    cols += [1536 + np.arange(MLA_Q_RANK), pads(64)]
    cols.append(1728 + np.arange(MLA_KV_RANK))
    cols += [1856 + _deinterleave(32), 1856 + _deinterleave(32), pads(64)]
    return np.concatenate(cols)


def _uq_columns(pad):
    per = MLA_NOPE_DIM + MLA_ROPE_DIM
    cols = []
    for pr in range(2):
        h0, h1 = 2 * pr, 2 * pr + 1
        cols += [per * h0 + np.arange(64), per * h1 + np.arange(64),
                 per * h0 + 64 + _deinterleave(32), per * h1 + 64 + _deinterleave(32),
                 np.full((64,), pad)]
    return np.concatenate(cols)


def _ukv_columns(pad):
    per = MLA_NOPE_DIM + MLA_V_DIM
    k = [per * hd + np.arange(64) for hd in range(MLA_HEADS)]
    v = []
    for hd in range(MLA_HEADS):
        v += [per * hd + 64 + np.arange(64), np.full((64,), pad)]
    return np.concatenate(k + v)


def _take_cols(w, cols):
    w_ext = jnp.concatenate([w, jnp.zeros((w.shape[0], 1), w.dtype)], axis=1)
    return jnp.take(w_ext, jnp.asarray(cols), axis=1)


def _rope_tables(s):
    t = np.arange(s)
    rows = (t // GRID_W).astype(np.float32)
    cols = (t % GRID_W).astype(np.float32)
    out = []
    for dim in (HEAD_DIM, DIFF_QK_DIM):
        quarter = dim // 4
        half = dim // 2
        inv_freq = (ROPE_THETA ** (-np.arange(quarter, dtype=np.float32) / quarter)).astype(np.float32)
        ang = np.concatenate([rows[:, None] * inv_freq, cols[:, None] * inv_freq], axis=-1)
        lane = np.arange(LANES)
        idx = (lane % dim) % half
        sign = np.where((lane % dim) < half, -1.0, 1.0).astype(np.float32)
        out += [jnp.asarray(np.cos(ang)[:, idx], F32), jnp.asarray(np.sin(ang)[:, idx] * sign, F32)]
    return out


def kernel(x, c, ctx, c_ctx, w_mod, b_mod, g_attn, g_ffn, w_in, w_out, gqa_gq, gqa_gk,
           diff_lq1, diff_lk1, diff_lq2, diff_lk2, diff_gsub, mla_gcq, mla_gckv, mla_wuq, mla_wukv,
           ffn_w_in, ffn_w_out, moe_router, moe_w_in, moe_w_out, g_final):
    b, s, d = x.shape
    n_ctx = ctx.shape[1]
    depth = w_mod.shape[0]
    tables = _rope_tables(s)
    no_rotation = [jnp.ones((n_ctx, LANES), F32), jnp.zeros((n_ctx, LANES), F32)] * 2
    in_cols = _in_proj_columns(w_in.shape[2])
    uq_cols = _uq_columns(mla_wuq.shape[2])
    ukv_cols = _ukv_columns(mla_wukv.shape[2])
    perm64 = _deinterleave(HEAD_DIM)
    bd = jnp.asarray(np.kron(np.eye(LANES // HEAD_DIM), np.full((HEAD_DIM, HEAD_DIM), 1.0 / HEAD_DIM)), F32)

    mod_rows = 16
    c_all = jnp.zeros((mod_rows, d), F32).at[:b].set(c).at[b].set(c_ctx)
    pad_lanes = lambda v: jnp.zeros((1, LANES), F32).at[0, :v.shape[0]].set(v)

    xc = ctx
    for l in range(depth):
        need_ctx = l < depth - 1
        lam_init = 0.8 - 0.6 * math.exp(-0.3 * l)
        mod3 = _modulation(c_all, w_mod[l], b_mod[l]).reshape(mod_rows, N_MOD, d)

        w_in_p = _take_cols(w_in[l], in_cols).astype(BF16)
        wuq_p = jnp.zeros((2 * LANES, 512), F32).at[:MLA_Q_RANK].set(
            _take_cols(mla_wuq[l], uq_cols)).astype(BF16)
        wukv_p = _take_cols(mla_wukv[l], ukv_cols).astype(BF16)
        gq2 = jnp.tile(gqa_gq[l][perm64], 2).reshape(1, LANES)
        gk2 = jnp.tile(gqa_gk[l][perm64], 2).reshape(1, LANES)
        gcq = jnp.zeros((1, 2 * LANES), F32).at[0, :MLA_Q_RANK].set(mla_gcq[l])
        gckv = mla_gckv[l].reshape(1, MLA_KV_RANK)

        proj_w = (g_attn[l].reshape(1, d), w_in_p)
        proj_aux = (gq2, gk2, bd, gcq, gckv, wuq_p, wukv_p)
        lat = _project(x, mod3, lambda bi: bi, *proj_w, tables, *proj_aux)
        ctxp = _project(xc, mod3, lambda bi: b, *proj_w, no_rotation, *proj_aux)

        diff_extra = [pad_lanes(diff_lq1[l]), pad_lanes(diff_lk1[l]), pad_lanes(diff_lq2[l]),
                      pad_lanes(diff_lk2[l]), pad_lanes(diff_gsub[l])]
        diff_body = functools.partial(_diff_body, lam_init=lam_init)
        diff_body.__name__ = "_diff_body"

        def attend(q, streams):
            kv = lambda ik, iv: [(p[ik], p[iv]) for p in streams]
            rows = q[0].shape[1]
            stack = 8 * ROW_TILE
            oa = _attention(_gqa_body, q[0], kv(1, 2), [], n_groups=2, wq=256, wk=LANES,
                            n_s=4, tq=min(stack // 4, rows), per_head=False)
            ob = _attention(diff_body, q[3], kv(4, 5), diff_extra, n_groups=2, wq=LANES, wk=LANES,
                            n_s=2, tq=min(stack // 2, rows), per_head=True)
            om = _attention(_mla_body, q[6], kv(7, 8), [], n_groups=2, wq=256, wk=256,
                            n_s=1, tq=min(stack, rows), per_head=True)
            return oa, ob, om

        w_out_b = w_out[l].astype(BF16)
        g2 = g_ffn[l].reshape(1, d)
        dense = l % 2 == 0
        if dense:
            n_hidden = ffn_w_out.shape[1]
            wg = ffn_w_in[l // 2][:, :n_hidden].astype(BF16)
            wu = ffn_w_in[l // 2][:, n_hidden:].astype(BF16)
            wo = ffn_w_out[l // 2].astype(BF16)
            w_router = None
        else:
            n_hidden = moe_w_out.shape[2]
            wi = moe_w_in[l // 2]
            wg = wi[:, :, :n_hidden].astype(BF16)
            wu = wi[:, :, n_hidden:].astype(BF16)
            wo = moe_w_out[l // 2].astype(BF16)
            wr = jnp.zeros((d, LANES), F32).at[:, :N_EXPERTS].set(moe_router[l // 2])
            wr_hi = wr.astype(BF16)
            w_router = jnp.stack([wr_hi, (wr - wr_hi.astype(F32)).astype(BF16)])
        last = l == depth - 1

        def channel_mix(xs, attn_out, mod_row, final_norm):
            gf = g_final.reshape(1, d)
            if dense:
                return _mix_dense_ffn(xs, *attn_out, w_out_b, mod3, g2, mod_row, wg, wu, wo, gf, final_norm)
            res = _out_project_routed(xs, *attn_out, w_out_b, mod3, g2, mod_row, w_router)
            return _moe_ffn(*res, wg, wu, wo, mod3, mod_row, gf, final_norm)

        x_new = channel_mix(x, attend(lat, [lat, ctxp]), lambda bi: bi, last)
        if need_ctx:
            xc = channel_mix(xc, attend(ctxp, [ctxp]), lambda bi: b, False)
        x = x_new
    return x
```

```python
import functools
import math

import numpy as np
import jax
import jax.numpy as jnp
from jax import lax
from jax.experimental import pallas as pl
from jax.experimental.pallas import tpu as pltpu

LANES = 128
MXU_TILE = 256
VMEM_LIMIT = 60 * 1024 * 1024

NORM_EPS = 1e-6
ROPE_THETA = 10000.0
GRID_W = 64
N_MOD = 6
HEAD_DIM = 64
GQA_HEADS, GQA_KV_HEADS = 8, 2
DIFF_HEADS, DIFF_QK_DIM, DIFF_V_DIM = 4, 32, 64
MLA_HEADS, MLA_Q_RANK, MLA_KV_RANK = 4, 192, 128
MLA_NOPE_DIM, MLA_ROPE_DIM, MLA_V_DIM = 64, 32, 64
N_EXPERTS, TOP_K = 8, 2
GATE_ROWS = 16

ROW_TILE = 256
PROJ_ROWS = 512
KEY_CHUNK = MXU_TILE
FFN_CHUNK = MXU_TILE
FFN_ROWS = 512
MOE_ROWS = 1024
MOE_PART = 512
MOE_SLOTS = 256

F32 = jnp.float32
BF16 = jnp.bfloat16
HIGHEST = lax.Precision.HIGHEST


def _params(*sem):
    return pltpu.CompilerParams(dimension_semantics=sem, vmem_limit_bytes=VMEM_LIMIT)


def _lane_iota(shape):
    return lax.broadcasted_iota(jnp.int32, shape, len(shape) - 1)


def _mod_body(c_ref, w_ref, b_ref, o_ref):
    c = c_ref[...]
    sc = c * jax.nn.sigmoid(c)
    o_ref[...] = jnp.dot(sc, w_ref[...], precision=HIGHEST,
                         preferred_element_type=F32) + b_ref[...]


def _modulation(c_all, w, b):
    rows, d = c_all.shape
    n = w.shape[1]
    tn = 1536
    return pl.pallas_call(
        _mod_body,
        grid=(n // tn,),
        in_specs=[pl.BlockSpec((rows, d), lambda j: (0, 0)),
                  pl.BlockSpec((d, tn), lambda j: (0, j)),
                  pl.BlockSpec((1, tn), lambda j: (0, j))],
        out_specs=pl.BlockSpec((rows, tn), lambda j: (0, j)),
        out_shape=jax.ShapeDtypeStruct((rows, n), F32),
        compiler_params=_params("arbitrary"),
        name="modulation",
    )(c_all, w, b.reshape(1, n))


def _rms_rows(x, width):
    return lax.rsqrt(jnp.sum(x * x, axis=-1, keepdims=True) * (1.0 / width) + NORM_EPS)


def _rope(t, cos, sin_signed, half):
    lane = _lane_iota(t.shape)
    partner = jnp.where((lane & half) == 0,
                        pltpu.roll(t, LANES - half, 1), pltpu.roll(t, half, 1))
    return t * cos + partner * sin_signed


def _with_ones(t):
    return jnp.where(_lane_iota(t.shape) == HEAD_DIM, 1.0, t)


def _proj_body(x_ref, mod_ref, g_ref, w_ref, ca_ref, sa_ref, cb_ref, sb_ref,
               gq_ref, gk_ref, bd_ref, gcq_ref, gckv_ref, wuq_ref, wukv_ref,
               qa_ref, ka_ref, va_ref, qb_ref, kb_ref, vb_ref, qm_ref, km_ref, vm_ref, p_ref,
               *, scale_a, scale_b, scale_m):
    @pl.when(pl.program_id(0) == 0)
    def _():
        p_ref[...] = jnp.zeros(p_ref.shape, F32)

    p = p_ref

    ca, sa, cb, sb = ca_ref[...], sa_ref[...], cb_ref[...], sb_ref[...]
    bd = bd_ref[...]
    tile = lambda ref, j: (0, slice(None), slice(j * LANES, (j + 1) * LANES))
    cols = lambda a, base, j: a[:, base + j * LANES:base + (j + 1) * LANES]

    def head_norm(t, g):
        sq = t * t
        hi = sq.astype(BF16)
        lo = (sq - hi.astype(F32)).astype(BF16)
        ms = (jnp.dot(hi, bd, preferred_element_type=F32) + jnp.dot(lo, bd, preferred_element_type=F32))
        return t * lax.rsqrt(ms + NORM_EPS) * g

    for j in range(4):
        t = head_norm(cols(p, 0, j), gq_ref[...])
        qa_ref[tile(qa_ref, j)] = (_rope(t, ca, sa, 32) * scale_a).astype(BF16)
    for j in range(2):
        t = head_norm(cols(p, 512, j), gk_ref[...])
        ka_ref[tile(ka_ref, j)] = _rope(t, ca, sa, 32).astype(BF16)
        va_ref[tile(va_ref, j)] = _with_ones(cols(p, 768, j)).astype(BF16)

    for j in range(2):
        qb_ref[tile(qb_ref, j)] = (_rope(cols(p, 1024, j), cb, sb, 16) * scale_b).astype(BF16)
        kb_ref[tile(kb_ref, j)] = _rope(cols(p, 1280, j), cb, sb, 16).astype(BF16)
    for j in range(4):
        vb_ref[tile(vb_ref, j)] = _with_ones(cols(p, 1536, j)).astype(BF16)

    cq = p[:, 2048:2304]
    cqn = (cq * _rms_rows(cq, MLA_Q_RANK) * gcq_ref[...]).astype(BF16)
    uq = jnp.dot(cqn, wuq_ref[...], preferred_element_type=F32)
    ckv = p[:, 2304:2432]
    ckvn = (ckv * _rms_rows(ckv, MLA_KV_RANK) * gckv_ref[...]).astype(BF16)
    ukv = jnp.dot(ckvn, wukv_ref[...], preferred_element_type=F32)
    kr = _rope(p[:, 2432:2560], cb, sb, 16).astype(BF16)
    for pr in range(2):
        qm_ref[tile(qm_ref, 2 * pr)] = (cols(uq, 0, 2 * pr) * scale_m).astype(BF16)
        qr = _rope(cols(uq, 0, 2 * pr + 1), cb, sb, 16)
        qm_ref[tile(qm_ref, 2 * pr + 1)] = (qr * scale_m).astype(BF16)
        km_ref[tile(km_ref, 2 * pr)] = cols(ukv, 0, pr).astype(BF16)
        km_ref[tile(km_ref, 2 * pr + 1)] = kr
    for j in range(4):
        vm_ref[tile(vm_ref, j)] = _with_ones(cols(ukv, 256, j)).astype(BF16)

    x = x_ref[0]
    mod = mod_ref[0]
    y = x * _rms_rows(x, x.shape[-1]) * g_ref[...]
    h = (y * (1.0 + mod[1:2]) + mod[0:1]).astype(BF16)
    p_ref[...] = jnp.dot(h, w_ref[...], preferred_element_type=F32)


def _project(x, mod3, mod_row, g, w, tables, gq2, gk2, bd, gcq, gckv, wuq, wukv):
    b, s, d = x.shape
    tm = min(PROJ_ROWS, s)
    nt = s // tm
    n_tiles = b * nt
    widths = (512, 256, 256, 256, 256, 512, 512, 512, 512)
    log2e = math.log2(math.e)
    body = functools.partial(_proj_body, scale_a=HEAD_DIM ** -0.5 * log2e,
                             scale_b=DIFF_QK_DIM ** -0.5 * log2e,
                             scale_m=(MLA_NOPE_DIM + MLA_ROPE_DIM) ** -0.5 * log2e)
    cur = lambda j: (jnp.minimum(j, n_tiles - 1) // nt, jnp.minimum(j, n_tiles - 1) % nt)
    prev = lambda j: (jnp.maximum(j - 1, 0) // nt, jnp.maximum(j - 1, 0) % nt)
    const = lambda shape: pl.BlockSpec(shape, lambda j: (0,) * len(shape))
    tab = pl.BlockSpec((tm, LANES), lambda j: (prev(j)[1], 0))
    return pl.pallas_call(
        body,
        grid=(n_tiles + 1,),
        in_specs=[pl.BlockSpec((1, tm, d), lambda j: (cur(j)[0], cur(j)[1], 0)),
                  pl.BlockSpec((1, N_MOD, d), lambda j: (mod_row(cur(j)[0]), 0, 0)),
                  const((1, d)), const(w.shape), tab, tab, tab, tab,
                  const((1, LANES)), const((1, LANES)), const((LANES, LANES)),
                  const(gcq.shape), const(gckv.shape), const(wuq.shape), const(wukv.shape)],
        out_specs=[pl.BlockSpec((1, tm, wd), lambda j: (prev(j)[0], prev(j)[1], 0)) for wd in widths],
        out_shape=[jax.ShapeDtypeStruct((b, s, wd), BF16) for wd in widths],
        scratch_shapes=[pltpu.VMEM((tm, w.shape[1]), F32)],
        compiler_params=_params("arbitrary"),
        name="project",
    )(x, mod3, g, w, *tables, gq2, gk2, bd, gcq, gckv, wuq, wukv)


def _attn_step(qs, kv_refs, s_ref, mp_ref, mb_ref, acc_ref):
    step = pl.program_id(0)

    @pl.when(step == 0)
    def _():
        s_ref[...] = jnp.zeros(s_ref.shape, F32)
        mb_ref[...] = jnp.zeros(mb_ref.shape, F32)
        acc_ref[...] = jnp.ones(acc_ref.shape, F32)

    raw = acc_ref[...]
    done = raw * (1.0 / raw[:, HEAD_DIM:HEAD_DIM + 1])

    acc = None
    chunks = [(k_ref, v_ref, slice(i * KEY_CHUNK, (i + 1) * KEY_CHUNK))
              for k_ref, v_ref in kv_refs for i in range(k_ref.shape[1] // KEY_CHUNK)]
    for c, (k_ref, v_ref, keys) in enumerate(chunks):
        s_new = lax.dot_general(qs, k_ref[0, keys, :], (((1,), (1,)), ((), ())),
                                preferred_element_type=F32)
        s_old = s_ref[c]
        m_old = mb_ref[...]
        p0 = jnp.exp2(s_old[:, :LANES] - m_old)
        p1 = jnp.exp2(s_old[:, LANES:] - m_old)
        part = jnp.dot(jnp.concatenate([p0, p1], axis=1).astype(BF16), v_ref[0, keys, :],
                       preferred_element_type=F32)
        acc = part if acc is None else acc + part
        s_ref[c] = s_new
        mc = jnp.maximum(s_new[:, :LANES], s_new[:, LANES:])
        mp_ref[...] = mc if c == 0 else jnp.maximum(mp_ref[...], mc)
    acc_ref[...] = acc
    mb_ref[...] = jnp.broadcast_to(jnp.max(mp_ref[...], axis=-1, keepdims=True), mb_ref.shape)
    return done


def _split_kv(refs, n_kv):
    return [(refs[2 * i], refs[2 * i + 1]) for i in range(n_kv)], refs[2 * n_kv:]


def _pack_heads(even, odd):
    lo = _lane_iota(even.shape) < HEAD_DIM
    return jnp.where(lo, even, pltpu.roll(odd, HEAD_DIM, 1))


def _gqa_body(q_ref, *rest, n_kv, n_tiles):
    tq = q_ref.shape[1]
    lo = _lane_iota((tq, LANES)) < HEAD_DIM
    heads = []
    for j in range(2):
        t = q_ref[0, :, j * LANES:(j + 1) * LANES].astype(F32)
        heads.append(jnp.where(lo, t, 0.0))
        heads.append(jnp.where(lo, pltpu.roll(t, HEAD_DIM, 1), 0.0))
    qs = jnp.concatenate(heads, axis=0).astype(BF16)
    kv, (o_ref, *scratch) = _split_kv(rest, n_kv)
    o = _attn_step(qs, kv, *scratch)
    for j in range(2):
        o_ref[0, :, j * LANES:(j + 1) * LANES] = _pack_heads(
            o[(2 * j) * tq:(2 * j + 1) * tq], o[(2 * j + 1) * tq:(2 * j + 2) * tq]).astype(o_ref.dtype)


def _head_parity(n_tiles):
    step = pl.program_id(0)
    return jnp.minimum(step, n_tiles - 1) % 2, jnp.maximum(step - 2, 0) % 2


def _store_head(o_ref, res, parity):
    @pl.when(parity == 0)
    def _():
        o_ref[0, :, 0:HEAD_DIM] = res[:, 0:HEAD_DIM].astype(o_ref.dtype)

    @pl.when(parity == 1)
    def _():
        o_ref[0, :, HEAD_DIM:LANES] = pltpu.roll(res, HEAD_DIM, 1)[:, HEAD_DIM:LANES].astype(o_ref.dtype)


def _diff_body(q_ref, *rest, n_kv, n_tiles, lam_init):
    kv, (lq1_ref, lk1_ref, lq2_ref, lk2_ref, gsub_ref, o_ref, *scratch) = _split_kv(rest, n_kv)
    tq = q_ref.shape[1]
    par, par_done = _head_parity(n_tiles)
    lane = _lane_iota((tq, LANES))
    t = q_ref[0]
    zero = jnp.zeros_like(t)
    qs = jnp.concatenate([jnp.where((lane // DIFF_QK_DIM) == 2 * par + j, t, zero) for j in range(2)],
                         axis=0)
    o = _attn_step(qs, kv, *scratch)
    lam = (jnp.exp(jnp.sum(lq1_ref[...] * lk1_ref[...], axis=-1, keepdims=True))
           - jnp.exp(jnp.sum(lq2_ref[...] * lk2_ref[...], axis=-1, keepdims=True)) + lam_init)
    d = o[0:tq] - lam * o[tq:2 * tq]
    ms = jnp.sum(jnp.where(lane < DIFF_V_DIM, d * d, 0.0), axis=-1, keepdims=True) * (1.0 / DIFF_V_DIM)
    _store_head(o_ref, (d * lax.rsqrt(ms + NORM_EPS) * gsub_ref[...]) * (1.0 - lam_init), par_done)


def _mla_body(q_ref, *rest, n_kv, n_tiles):
    kv, (o_ref, *scratch) = _split_kv(rest, n_kv)
    par, par_done = _head_parity(n_tiles)
    t = q_ref[0]
    lane = _lane_iota(t.shape)
    nope0 = MLA_NOPE_DIM * par
    rope0 = LANES + MLA_ROPE_DIM * par
    mine = ((lane >= nope0) & (lane < nope0 + MLA_NOPE_DIM)) | (
        (lane >= rope0) & (lane < rope0 + MLA_ROPE_DIM))
    qs = jnp.where(mine, t, jnp.zeros_like(t))
    _store_head(o_ref, _attn_step(qs, kv, *scratch), par_done)


def _attention(body, q, kvs, extra, *, n_groups, wq, wk, n_s, tq, per_head):
    b, q_rows, _ = q.shape
    q_tiles = q_rows // tq
    wo = LANES if per_head else wq
    heads = 2 if per_head else 1
    n_chunks = sum(k.shape[1] // KEY_CHUNK for k, _ in kvs)
    m_rows = n_s * tq
    n_tiles = b * n_groups * q_tiles * heads

    def split(t):
        t, par = t // heads, t % heads
        return t // (n_groups * q_tiles), (t // q_tiles) % n_groups, t % q_tiles, par

    cur = lambda j: split(jnp.minimum(j, n_tiles - 1))
    prev = lambda j: split(jnp.clip(j - 1, 0, n_tiles - 1))
    done = lambda j: split(jnp.maximum(j - 2, 0))
    kv_specs, kv_args = [], []
    for k, v in kvs:
        kv_specs += [pl.BlockSpec((1, k.shape[1], wk), lambda j: (cur(j)[0], 0, cur(j)[1])),
                     pl.BlockSpec((1, v.shape[1], LANES),
                                  lambda j: (prev(j)[0], 0, prev(j)[1] * heads + prev(j)[3]))]
        kv_args += [k, v]
    extra_specs = [pl.BlockSpec(e.shape, lambda j: (0, 0)) for e in extra]
    return pl.pallas_call(
        functools.partial(body, n_kv=len(kvs), n_tiles=n_tiles),
        grid=(n_tiles + 2,),
        in_specs=[pl.BlockSpec((1, tq, wq), lambda j: (cur(j)[0], cur(j)[2], cur(j)[1]))]
        + kv_specs + extra_specs,
        out_specs=pl.BlockSpec((1, tq, wo), lambda j: (done(j)[0], done(j)[2], done(j)[1])),
        out_shape=jax.ShapeDtypeStruct((b, q_rows, n_groups * wo), BF16),
        scratch_shapes=[pltpu.VMEM((n_chunks, m_rows, KEY_CHUNK), F32),
                        pltpu.VMEM((m_rows, LANES), F32),
                        pltpu.VMEM((m_rows, LANES), F32),
                        pltpu.VMEM((m_rows, LANES), F32)],
        compiler_params=_params("arbitrary"),
        name=body.__name__.strip("_"),
    )(q, *kv_args, *extra)


def _router_gates(logits):
    lane = _lane_iota(logits.shape).astype(F32)
    neg = jnp.float32(-jnp.inf)
    z = jnp.where(lane < N_EXPERTS, logits, neg)
    m1 = jnp.max(z, axis=-1, keepdims=True)
    i1 = jnp.min(jnp.where(z == m1, lane, float(LANES)), axis=-1, keepdims=True)
    z2 = jnp.where(lane == i1, neg, z)
    m2 = jnp.max(z2, axis=-1, keepdims=True)
    i2 = jnp.min(jnp.where(z2 == m2, lane, float(LANES)), axis=-1, keepdims=True)
    e2 = jnp.exp(m2 - m1)
    den = 1.0 + e2
    return jnp.where(lane == i1, 1.0 / den, 0.0) + jnp.where(lane == i2, e2 / den, 0.0)


def _mixer_residual(x_ref, oa_ref, ob_ref, om_ref, w_ref, mod, g_ref):
    wa, wb = oa_ref.shape[2], ob_ref.shape[2]
    y = jnp.dot(oa_ref[0], w_ref[0:wa, :], preferred_element_type=F32)
    y += jnp.dot(ob_ref[0], w_ref[wa:wa + wb, :], preferred_element_type=F32)
    y += jnp.dot(om_ref[0], w_ref[wa + wb:, :], preferred_element_type=F32)
    x = x_ref[0] + mod[2:3] * y
    n = x * _rms_rows(x, x.shape[-1]) * g_ref[...]
    return x, n * (1.0 + mod[4:5]) + mod[3:4]


def _mixer_specs(x, oa, ob, om, w, mod_row, tm):
    d = x.shape[2]
    row = lambda bi, i: (bi, i, 0)
    return [pl.BlockSpec((1, tm, d), row),
            pl.BlockSpec((1, tm, oa.shape[2]), row),
            pl.BlockSpec((1, tm, ob.shape[2]), row),
            pl.BlockSpec((1, tm, om.shape[2]), row),
            pl.BlockSpec(w.shape, lambda bi, i: (0, 0)),
            pl.BlockSpec((1, N_MOD, d), lambda bi, i: (mod_row(bi), 0, 0)),
            pl.BlockSpec((1, d), lambda bi, i: (0, 0))]


def _outproj_body(x_ref, oa_ref, ob_ref, om_ref, w_ref, mod_ref, g_ref, wr_ref,
                  xo_ref, h_ref, gates_ref, gates_t_ref, hprev_ref):
    @pl.when(pl.program_id(0) == 0)
    def _():
        hprev_ref[...] = jnp.zeros(hprev_ref.shape, F32)

    h_prev = hprev_ref[...]
    h_hi = h_prev.astype(BF16)
    h_lo = (h_prev - h_hi.astype(F32)).astype(BF16)
    logits = (jnp.dot(h_hi, wr_ref[0], preferred_element_type=F32)
              + jnp.dot(h_lo, wr_ref[0], preferred_element_type=F32)
              + jnp.dot(h_hi, wr_ref[1], preferred_element_type=F32))
    gates = _router_gates(logits)
    gates_ref[0] = gates
    gates_t_ref[0] = gates.T[:gates_t_ref.shape[1]]

    x, h = _mixer_residual(x_ref, oa_ref, ob_ref, om_ref, w_ref, mod_ref[0], g_ref)
    xo_ref[0] = x
    h_ref[0] = h.astype(BF16)
    hprev_ref[...] = h


def _out_project_routed(x, oa, ob, om, w, mod3, g, mod_row, w_router):
    b, s, d = x.shape
    tm = min(PROJ_ROWS, s)
    nt = s // tm
    n_tiles = b * nt
    cur = lambda j: (jnp.minimum(j, n_tiles - 1) // nt, jnp.minimum(j, n_tiles - 1) % nt)
    prev = lambda j: (jnp.maximum(j - 1, 0) // nt, jnp.maximum(j - 1, 0) % nt)
    flat = lambda spec: pl.BlockSpec(spec.block_shape, lambda j, f=spec.index_map: f(*cur(j)))
    row = lambda j: (cur(j)[0], cur(j)[1], 0)
    return pl.pallas_call(
        _outproj_body,
        grid=(n_tiles + 1,),
        in_specs=[flat(sp) for sp in _mixer_specs(x, oa, ob, om, w, mod_row, tm)]
        + [pl.BlockSpec(w_router.shape, lambda j: (0, 0, 0))],
        out_specs=[pl.BlockSpec((1, tm, d), row), pl.BlockSpec((1, tm, d), row),
                   pl.BlockSpec((1, tm, LANES), lambda j: (prev(j)[0], prev(j)[1], 0)),
                   pl.BlockSpec((1, GATE_ROWS, tm), lambda j: (prev(j)[0], 0, prev(j)[1]))],
        out_shape=[jax.ShapeDtypeStruct((b, s, d), F32), jax.ShapeDtypeStruct((b, s, d), BF16),
                   jax.ShapeDtypeStruct((b, s, LANES), F32), jax.ShapeDtypeStruct((b, GATE_ROWS, s), F32)],
        scratch_shapes=[pltpu.VMEM((tm, d), F32)],
        compiler_params=_params("arbitrary"),
        name="out_project_routed",
    )(x, oa, ob, om, w, mod3, g, w_router)


def _swiglu_chunks(h, wg_ref, wu_ref, wo_ref, lead):
    acc = None
    for j in range(wo_ref.shape[len(lead)] // FFN_CHUNK):
        cols = slice(j * FFN_CHUNK, (j + 1) * FFN_CHUNK)
        gate = jnp.dot(h, wg_ref[lead + (slice(None), cols)], preferred_element_type=F32)
        up = jnp.dot(h, wu_ref[lead + (slice(None), cols)], preferred_element_type=F32)
        a = (gate * jax.nn.sigmoid(gate) * up).astype(BF16)
        part = jnp.dot(a, wo_ref[lead + (cols, slice(None))], preferred_element_type=F32)
        acc = part if acc is None else acc + part
    return acc


def _residual_out(x, gate, y, gf_ref, final_norm):
    x = x + gate * y
    if final_norm:
        x = x * _rms_rows(x, x.shape[-1]) * gf_ref[...]
    return x


def _ffn_body(x_ref, oa_ref, ob_ref, om_ref, w_ref, mod_ref, g_ref, wg_ref, wu_ref, wo_ref, gf_ref,
              o_ref, *, final_norm):
    mod = mod_ref[0]
    x, h = _mixer_residual(x_ref, oa_ref, ob_ref, om_ref, w_ref, mod, g_ref)
    y = _swiglu_chunks(h.astype(BF16), wg_ref, wu_ref, wo_ref, ())
    o_ref[0] = _residual_out(x, mod[5:6], y, gf_ref, final_norm)


def _mix_dense_ffn(x, oa, ob, om, w, mod3, g, mod_row, wg, wu, wo, g_final, final_norm):
    b, s, d = x.shape
    tm = min(FFN_ROWS, s)
    resident = lambda a: pl.BlockSpec(a.shape, lambda bi, i: (0, 0), pipeline_mode=pl.Buffered(1))
    return pl.pallas_call(
        functools.partial(_ffn_body, final_norm=final_norm),
        grid=(b, s // tm),
        in_specs=_mixer_specs(x, oa, ob, om, w, mod_row, tm)
        + [resident(wg), resident(wu), resident(wo), pl.BlockSpec((1, d), lambda bi, i: (0, 0))],
        out_specs=pl.BlockSpec((1, tm, d), lambda bi, i: (bi, i, 0)),
        out_shape=jax.ShapeDtypeStruct((b, s, d), F32),
        compiler_params=_params("parallel", "arbitrary"),
        name="dense_ffn",
    )(x, oa, ob, om, w, mod3, g, wg, wu, wo, g_final)


def _moe_body(x_ref, h_ref, gates_ref, gates_t_ref, tri_ref, tri_t_ref, wg_ref, wu_ref, wo_ref,
              mod_ref, gf_ref, o_ref, tot_ref, rank_ref, rank_t_ref, *, final_norm):
    e = pl.program_id(2)
    tm = h_ref.shape[1]
    parts = [slice(k * MOE_PART, (k + 1) * MOE_PART) for k in range(tm // MOE_PART)]
    gates = gates_ref[0]
    lane = _lane_iota(gates.shape)

    @pl.when(e == 0)
    def _():
        tot_ref[...] = jnp.zeros(tot_ref.shape, F32)
        live = (gates > 0.0).astype(BF16)
        live_t = (gates_t_ref[0] > 0.0).astype(BF16)
        for part in parts:
            rank_ref[part, :] = jnp.dot(tri_ref[...], live[part], preferred_element_type=F32)
            rank_t_ref[:, part] = jnp.dot(live_t[:, part], tri_t_ref[...], preferred_element_type=F32)

    pick = lambda a: jnp.sum(jnp.where(lane == e, a, 0.0), axis=-1, keepdims=True)
    rank_row = rank_t_ref[pl.ds(e, 1), :]
    fits = jnp.max(rank_row) <= float(MOE_SLOTS)

    @pl.when(fits)
    def _():
        gate_e = pick(gates)
        rank_e = pick(rank_ref[...])
        live_row = gates_t_ref[0, pl.ds(e, 1), :] > 0.0
        slot_sub = lax.broadcasted_iota(jnp.int32, (MOE_SLOTS, MOE_PART), 0).astype(F32) + 1.0
        rows = []
        for part in parts:
            gather = jnp.where((rank_row[:, part] == slot_sub) & live_row[:, part], 1.0, 0.0)
            rows.append(jnp.dot(gather.astype(BF16), h_ref[0, part, :],
                                preferred_element_type=F32).astype(BF16))
        y = _swiglu_chunks(jnp.concatenate(rows, axis=0), wg_ref, wu_ref, wo_ref, (0,))
        y_hi = y.astype(BF16)
        y_lo = (y - y_hi.astype(F32)).astype(BF16)
        slot_lane = _lane_iota((MOE_PART, MOE_SLOTS)).astype(F32) + 1.0
        for k, part in enumerate(parts):
            slots = slice(k * MOE_SLOTS, (k + 1) * MOE_SLOTS)
            scatter = jnp.where((rank_e[part] == slot_lane) & (gate_e[part] > 0.0), 1.0, 0.0).astype(BF16)
            back = (jnp.dot(scatter, y_hi[slots], preferred_element_type=F32)
                    + jnp.dot(scatter, y_lo[slots], preferred_element_type=F32))
            tot_ref[part, :] += gate_e[part] * back

    @pl.when(jnp.logical_not(fits))
    def _():
        tot_ref[...] += pick(gates) * _swiglu_chunks(h_ref[0], wg_ref, wu_ref, wo_ref, (0,))

    @pl.when(e == pl.num_programs(2) - 1)
    def _():
        o_ref[0] = _residual_out(x_ref[0], mod_ref[0][5:6], tot_ref[...], gf_ref, final_norm)


def _moe_ffn(x, h, gates, gates_t, wg, wu, wo, mod3, mod_row, g_final, final_norm):
    b, s, d = x.shape
    tm = min(MOE_ROWS, s)
    tp = MOE_PART
    tri = jnp.asarray(np.tril(np.ones((tp, tp), np.float32)), BF16)
    row = lambda bi, i, e: (bi, i, 0)
    const = lambda bi, i, e: (0, 0)
    expert = lambda a: pl.BlockSpec((1,) + a.shape[1:], lambda bi, i, e: (e, 0, 0))
    return pl.pallas_call(
        functools.partial(_moe_body, final_norm=final_norm),
        grid=(b, s // tm, wg.shape[0]),
        in_specs=[pl.BlockSpec((1, tm, d), row, pipeline_mode=pl.Buffered(1)),
                  pl.BlockSpec((1, tm, d), row),
                  pl.BlockSpec((1, tm, LANES), row),
                  pl.BlockSpec((1, GATE_ROWS, tm), lambda bi, i, e: (bi, 0, i)),
                  pl.BlockSpec((tp, tp), const), pl.BlockSpec((tp, tp), const),
                  expert(wg), expert(wu), expert(wo),
                  pl.BlockSpec((1, N_MOD, d), lambda bi, i, e: (mod_row(bi), 0, 0)),
                  pl.BlockSpec((1, d), const)],
        out_specs=pl.BlockSpec((1, tm, d), row),
        out_shape=jax.ShapeDtypeStruct((b, s, d), F32),
        scratch_shapes=[pltpu.VMEM((tm, d), F32), pltpu.VMEM((tm, LANES), F32),
                        pltpu.VMEM((GATE_ROWS, tm), F32)],
        compiler_params=_params("parallel", "parallel", "arbitrary"),
        name="moe_ffn",
    )(x, h, gates, gates_t, tri, tri.T, wg, wu, wo, mod3, g_final)


def _deinterleave(n):
    return np.concatenate([np.arange(0, n, 2), np.arange(1, n, 2)])


def _in_proj_columns(pad):
    pads = lambda n: np.full((n,), pad)
    cols = []
    for hd in range(GQA_HEADS):
        cols.append(64 * hd + _deinterleave(64))
    for g in range(GQA_KV_HEADS):
        cols += [512 + 64 * g + _deinterleave(64), pads(64)]
    for g in range(GQA_KV_HEADS):
        cols += [640 + 64 * g + np.arange(64), pads(64)]
    for u in range(2 * DIFF_HEADS):
        cols.append(768 + 32 * u + _deinterleave(32))
    for u in range(2 * DIFF_HEADS):
        cols.append(1024 + 32 * u + _deinterleave(32))
    for hd in range(DIFF_HEADS):
        cols += [1280 + 64 * hd + np.arange(64), pads(64)]
    cols += [1536 + np.arange(MLA_Q_RANK), pads(64)]
    cols.append(1728 + np.arange(MLA_KV_RANK))
    cols += [1856 + _deinterleave(32), 1856 + _deinterleave(32), pads(64)]
    return np.concatenate(cols)


def _uq_columns(pad):
    per = MLA_NOPE_DIM + MLA_ROPE_DIM
    cols = []
    for pr in range(2):
        h0, h1 = 2 * pr, 2 * pr + 1
        cols += [per * h0 + np.arange(64), per * h1 + np.arange(64),
                 per * h0 + 64 + _deinterleave(32), per * h1 + 64 + _deinterleave(32),
                 np.full((64,), pad)]
    return np.concatenate(cols)


def _ukv_columns(pad):
    per = MLA_NOPE_DIM + MLA_V_DIM
    k = [per * hd + np.arange(64) for hd in range(MLA_HEADS)]
    v = []
    for hd in range(MLA_HEADS):
        v += [per * hd + 64 + np.arange(64), np.full((64,), pad)]
    return np.concatenate(k + v)


def _take_cols(w, cols):
    w_ext = jnp.concatenate([w, jnp.zeros((w.shape[0], 1), w.dtype)], axis=1)
    return jnp.take(w_ext, jnp.asarray(cols), axis=1)


def _rope_tables(s):
    t = np.arange(s)
    rows = (t // GRID_W).astype(np.float32)
    cols = (t % GRID_W).astype(np.float32)
    out = []
    for dim in (HEAD_DIM, DIFF_QK_DIM):
        quarter = dim // 4
        half = dim // 2
        inv_freq = (ROPE_THETA ** (-np.arange(quarter, dtype=np.float32) / quarter)).astype(np.float32)
        ang = np.concatenate([rows[:, None] * inv_freq, cols[:, None] * inv_freq], axis=-1)
        lane = np.arange(LANES)
        idx = (lane % dim) % half
        sign = np.where((lane % dim) < half, -1.0, 1.0).astype(np.float32)
        out += [jnp.asarray(np.cos(ang)[:, idx], F32), jnp.asarray(np.sin(ang)[:, idx] * sign, F32)]
    return out


def kernel(x, c, ctx, c_ctx, w_mod, b_mod, g_attn, g_ffn, w_in, w_out, gqa_gq, gqa_gk,
           diff_lq1, diff_lk1, diff_lq2, diff_lk2, diff_gsub, mla_gcq, mla_gckv, mla_wuq, mla_wukv,
           ffn_w_in, ffn_w_out, moe_router, moe_w_in, moe_w_out, g_final):
    b, s, d = x.shape
    n_ctx = ctx.shape[1]
    depth = w_mod.shape[0]
    tables = _rope_tables(s)
    no_rotation = [jnp.ones((n_ctx, LANES), F32), jnp.zeros((n_ctx, LANES), F32)] * 2
    in_cols = _in_proj_columns(w_in.shape[2])
    uq_cols = _uq_columns(mla_wuq.shape[2])
    ukv_cols = _ukv_columns(mla_wukv.shape[2])
    perm64 = _deinterleave(HEAD_DIM)
    bd = jnp.asarray(np.kron(np.eye(LANES // HEAD_DIM), np.full((HEAD_DIM, HEAD_DIM), 1.0 / HEAD_DIM)), F32)

    mod_rows = 16
    c_all = jnp.zeros((mod_rows, d), F32).at[:b].set(c).at[b].set(c_ctx)
    pad_lanes = lambda v: jnp.zeros((1, LANES), F32).at[0, :v.shape[0]].set(v)

    xc = ctx
    for l in range(depth):
        need_ctx = l < depth - 1
        lam_init = 0.8 - 0.6 * math.exp(-0.3 * l)
        mod3 = _modulation(c_all, w_mod[l], b_mod[l]).reshape(mod_rows, N_MOD, d)

        w_in_p = _take_cols(w_in[l], in_cols).astype(BF16)
        wuq_p = jnp.zeros((2 * LANES, 512), F32).at[:MLA_Q_RANK].set(
            _take_cols(mla_wuq[l], uq_cols)).astype(BF16)
        wukv_p = _take_cols(mla_wukv[l], ukv_cols).astype(BF16)
        gq2 = jnp.tile(gqa_gq[l][perm64], 2).reshape(1, LANES)
        gk2 = jnp.tile(gqa_gk[l][perm64], 2).reshape(1, LANES)
        gcq = jnp.zeros((1, 2 * LANES), F32).at[0, :MLA_Q_RANK].set(mla_gcq[l])
        gckv = mla_gckv[l].reshape(1, MLA_KV_RANK)

        proj_w = (g_attn[l].reshape(1, d), w_in_p)
        proj_aux = (gq2, gk2, bd, gcq, gckv, wuq_p, wukv_p)
        lat = _project(x, mod3, lambda bi: bi, *proj_w, tables, *proj_aux)
        ctxp = _project(xc, mod3, lambda bi: b, *proj_w, no_rotation, *proj_aux)

        diff_extra = [pad_lanes(diff_lq1[l]), pad_lanes(diff_lk1[l]), pad_lanes(diff_lq2[l]),
                      pad_lanes(diff_lk2[l]), pad_lanes(diff_gsub[l])]
        diff_body = functools.partial(_diff_body, lam_init=lam_init)
        diff_body.__name__ = "_diff_body"

        def attend(q, streams):
            kv = lambda ik, iv: [(p[ik], p[iv]) for p in streams]
            rows = q[0].shape[1]
            stack = 8 * ROW_TILE
            oa = _attention(_gqa_body, q[0], kv(1, 2), [], n_groups=2, wq=256, wk=LANES,
                            n_s=4, tq=min(stack // 4, rows), per_head=False)
            ob = _attention(diff_body, q[3], kv(4, 5), diff_extra, n_groups=2, wq=LANES, wk=LANES,
                            n_s=2, tq=min(stack // 2, rows), per_head=True)
            om = _attention(_mla_body, q[6], kv(7, 8), [], n_groups=2, wq=256, wk=256,
                            n_s=1, tq=min(stack, rows), per_head=True)
            return oa, ob, om

        w_out_b = w_out[l].astype(BF16)
        g2 = g_ffn[l].reshape(1, d)
        dense = l % 2 == 0
        if dense:
            n_hidden = ffn_w_out.shape[1]
            wg = ffn_w_in[l // 2][:, :n_hidden].astype(BF16)
            wu = ffn_w_in[l // 2][:, n_hidden:].astype(BF16)
            wo = ffn_w_out[l // 2].astype(BF16)
            w_router = None
        else:
            n_hidden = moe_w_out.shape[2]
            wi = moe_w_in[l // 2]
            wg = wi[:, :, :n_hidden].astype(BF16)
            wu = wi[:, :, n_hidden:].astype(BF16)
            wo = moe_w_out[l // 2].astype(BF16)
            wr = jnp.zeros((d, LANES), F32).at[:, :N_EXPERTS].set(moe_router[l // 2])
            wr_hi = wr.astype(BF16)
            w_router = jnp.stack([wr_hi, (wr - wr_hi.astype(F32)).astype(BF16)])
        last = l == depth - 1

        def channel_mix(xs, attn_out, mod_row, final_norm):
            gf = g_final.reshape(1, d)
            if dense:
                return _mix_dense_ffn(xs, *attn_out, w_out_b, mod3, g2, mod_row, wg, wu, wo, gf, final_norm)
            res = _out_project_routed(xs, *attn_out, w_out_b, mod3, g2, mod_row, w_router)
            return _moe_ffn(*res, wg, wu, wo, mod3, mod_row, gf, final_norm)

        x_new = channel_mix(x, attend(lat, [lat, ctxp]), lambda bi: bi, last)
        if need_ctx:
            xc = channel_mix(xc, attend(ctxp, [ctxp]), lambda bi: b, False)
        x = x_new
    return x
```

```python
import functools
import math

import numpy as np
import jax
import jax.numpy as jnp
from jax import lax
from jax.experimental import pallas as pl
from jax.experimental.pallas import tpu as pltpu

LANES = 128
MXU_TILE = 256
VMEM_LIMIT = 60 * 1024 * 1024

NORM_EPS = 1e-6
ROPE_THETA = 10000.0
GRID_W = 64
N_MOD = 6
HEAD_DIM = 64
GQA_HEADS, GQA_KV_HEADS = 8, 2
DIFF_HEADS, DIFF_QK_DIM, DIFF_V_DIM = 4, 32, 64
MLA_HEADS, MLA_Q_RANK, MLA_KV_RANK = 4, 192, 128
MLA_NOPE_DIM, MLA_ROPE_DIM, MLA_V_DIM = 64, 32, 64
N_EXPERTS, TOP_K = 8, 2
GATE_ROWS = 16

ROW_TILE = 256
PROJ_ROWS = 512
KEY_CHUNK = MXU_TILE
FFN_CHUNK = MXU_TILE
FFN_ROWS = 512
MOE_ROWS = 1024
MOE_PART = 512
MOE_SLOTS = 256

F32 = jnp.float32
BF16 = jnp.bfloat16
HIGHEST = lax.Precision.HIGHEST


def _params(*sem):
    return pltpu.CompilerParams(dimension_semantics=sem, vmem_limit_bytes=VMEM_LIMIT)


def _lane_iota(shape):
    return lax.broadcasted_iota(jnp.int32, shape, len(shape) - 1)


def _mod_body(c_ref, w_ref, b_ref, o_ref):
    c = c_ref[...]
    sc = c * jax.nn.sigmoid(c)
    o_ref[...] = jnp.dot(sc, w_ref[...], precision=HIGHEST,
                         preferred_element_type=F32) + b_ref[...]


def _modulation(c_all, w, b):
    rows, d = c_all.shape
    n = w.shape[1]
    tn = 1536
    return pl.pallas_call(
        _mod_body,
        grid=(n // tn,),
        in_specs=[pl.BlockSpec((rows, d), lambda j: (0, 0)),
                  pl.BlockSpec((d, tn), lambda j: (0, j)),
                  pl.BlockSpec((1, tn), lambda j: (0, j))],
        out_specs=pl.BlockSpec((rows, tn), lambda j: (0, j)),
        out_shape=jax.ShapeDtypeStruct((rows, n), F32),
        compiler_params=_params("arbitrary"),
        name="modulation",
    )(c_all, w, b.reshape(1, n))


def _rms_rows(x, width):
    return lax.rsqrt(jnp.sum(x * x, axis=-1, keepdims=True) * (1.0 / width) + NORM_EPS)


def _rope(t, cos, sin_signed, half):
    lane = _lane_iota(t.shape)
    partner = jnp.where((lane & half) == 0,
                        pltpu.roll(t, LANES - half, 1), pltpu.roll(t, half, 1))
    return t * cos + partner * sin_signed


def _with_ones(t):
    return jnp.where(_lane_iota(t.shape) == HEAD_DIM, 1.0, t)


def _proj_body(x_ref, mod_ref, g_ref, w_ref, ca_ref, sa_ref, cb_ref, sb_ref,
               gq_ref, gk_ref, bd_ref, gcq_ref, gckv_ref, wuq_ref, wukv_ref,
               qa_ref, ka_ref, va_ref, qb_ref, kb_ref, vb_ref, qm_ref, km_ref, vm_ref, p_ref,
               *, scale_a, scale_b, scale_m):
    @pl.when(pl.program_id(0) == 0)
    def _():
        p_ref[...] = jnp.zeros(p_ref.shape, F32)

    x = x_ref[0]
    d = x.shape[-1]
    mod = mod_ref[0]
    y = x * _rms_rows(x, d) * g_ref[...]
    h = (y * (1.0 + mod[1:2]) + mod[0:1]).astype(BF16)
    p_new = jnp.dot(h, w_ref[...], preferred_element_type=F32)
    p = p_ref

    ca, sa, cb, sb = ca_ref[...], sa_ref[...], cb_ref[...], sb_ref[...]
    bd = bd_ref[...]
    tile = lambda ref, j: (0, slice(None), slice(j * LANES, (j + 1) * LANES))
    cols = lambda a, base, j: a[:, base + j * LANES:base + (j + 1) * LANES]

    def head_norm(t, g):
        sq = t * t
        hi = sq.astype(BF16)
        lo = (sq - hi.astype(F32)).astype(BF16)
        ms = (jnp.dot(hi, bd, preferred_element_type=F32) + jnp.dot(lo, bd, preferred_element_type=F32))
        return t * lax.rsqrt(ms + NORM_EPS) * g

    for j in range(4):
        t = head_norm(cols(p, 0, j), gq_ref[...])
        qa_ref[tile(qa_ref, j)] = (_rope(t, ca, sa, 32) * scale_a).astype(BF16)
    for j in range(2):
        t = head_norm(cols(p, 512, j), gk_ref[...])
        ka_ref[tile(ka_ref, j)] = _rope(t, ca, sa, 32).astype(BF16)
        va_ref[tile(va_ref, j)] = _with_ones(cols(p, 768, j)).astype(BF16)

    for j in range(2):
        qb_ref[tile(qb_ref, j)] = (_rope(cols(p, 1024, j), cb, sb, 16) * scale_b).astype(BF16)
        kb_ref[tile(kb_ref, j)] = _rope(cols(p, 1280, j), cb, sb, 16).astype(BF16)
    for j in range(4):
        vb_ref[tile(vb_ref, j)] = _with_ones(cols(p, 1536, j)).astype(BF16)

    cq = p[:, 2048:2304]
    cqn = (cq * _rms_rows(cq, MLA_Q_RANK) * gcq_ref[...]).astype(BF16)
    uq = jnp.dot(cqn, wuq_ref[...], preferred_element_type=F32)
    ckv = p[:, 2304:2432]
    ckvn = (ckv * _rms_rows(ckv, MLA_KV_RANK) * gckv_ref[...]).astype(BF16)
    ukv = jnp.dot(ckvn, wukv_ref[...], preferred_element_type=F32)
    kr = _rope(p[:, 2432:2560], cb, sb, 16).astype(BF16)
    for pr in range(2):
        qm_ref[tile(qm_ref, 2 * pr)] = (cols(uq, 0, 2 * pr) * scale_m).astype(BF16)
        qr = _rope(cols(uq, 0, 2 * pr + 1), cb, sb, 16)
        qm_ref[tile(qm_ref, 2 * pr + 1)] = (qr * scale_m).astype(BF16)
        km_ref[tile(km_ref, 2 * pr)] = cols(ukv, 0, pr).astype(BF16)
        km_ref[tile(km_ref, 2 * pr + 1)] = kr
    for j in range(4):
        vm_ref[tile(vm_ref, j)] = _with_ones(cols(ukv, 256, j)).astype(BF16)
    p_ref[...] = p_new


def _project(x, mod3, mod_row, g, w, tables, gq2, gk2, bd, gcq, gckv, wuq, wukv):
    b, s, d = x.shape
    tm = min(PROJ_ROWS, s)
    nt = s // tm
    n_tiles = b * nt
    widths = (512, 256, 256, 256, 256, 512, 512, 512, 512)
    log2e = math.log2(math.e)
    body = functools.partial(_proj_body, scale_a=HEAD_DIM ** -0.5 * log2e,
                             scale_b=DIFF_QK_DIM ** -0.5 * log2e,
                             scale_m=(MLA_NOPE_DIM + MLA_ROPE_DIM) ** -0.5 * log2e)
    cur = lambda j: (jnp.minimum(j, n_tiles - 1) // nt, jnp.minimum(j, n_tiles - 1) % nt)
    prev = lambda j: (jnp.maximum(j - 1, 0) // nt, jnp.maximum(j - 1, 0) % nt)
    const = lambda shape: pl.BlockSpec(shape, lambda j: (0,) * len(shape))
    tab = pl.BlockSpec((tm, LANES), lambda j: (prev(j)[1], 0))
    return pl.pallas_call(
        body,
        grid=(n_tiles + 1,),
        in_specs=[pl.BlockSpec((1, tm, d), lambda j: (cur(j)[0], cur(j)[1], 0)),
                  pl.BlockSpec((1, N_MOD, d), lambda j: (mod_row(cur(j)[0]), 0, 0)),
                  const((1, d)), const(w.shape), tab, tab, tab, tab,
                  const((1, LANES)), const((1, LANES)), const((LANES, LANES)),
                  const(gcq.shape), const(gckv.shape), const(wuq.shape), const(wukv.shape)],
        out_specs=[pl.BlockSpec((1, tm, wd), lambda j: (prev(j)[0], prev(j)[1], 0)) for wd in widths],
        out_shape=[jax.ShapeDtypeStruct((b, s, wd), BF16) for wd in widths],
        scratch_shapes=[pltpu.VMEM((tm, w.shape[1]), F32)],
        compiler_params=_params("arbitrary"),
        name="project",
    )(x, mod3, g, w, *tables, gq2, gk2, bd, gcq, gckv, wuq, wukv)


def _attn_step(qs, kv_refs, s_ref, mp_ref, mb_ref, acc_ref):
    step = pl.program_id(0)

    @pl.when(step == 0)
    def _():
        s_ref[...] = jnp.zeros(s_ref.shape, F32)
        mb_ref[...] = jnp.zeros(mb_ref.shape, F32)
        acc_ref[...] = jnp.ones(acc_ref.shape, F32)

    raw = acc_ref[...]
    done = raw * (1.0 / raw[:, HEAD_DIM:HEAD_DIM + 1])

    acc = None
    chunks = [(k_ref, v_ref, slice(i * KEY_CHUNK, (i + 1) * KEY_CHUNK))
              for k_ref, v_ref in kv_refs for i in range(k_ref.shape[1] // KEY_CHUNK)]
    for c, (k_ref, v_ref, keys) in enumerate(chunks):
        s_new = lax.dot_general(qs, k_ref[0, keys, :], (((1,), (1,)), ((), ())),
                                preferred_element_type=F32)
        s_old = s_ref[c]
        m_old = mb_ref[...]
        p0 = jnp.exp2(s_old[:, :LANES] - m_old)
        p1 = jnp.exp2(s_old[:, LANES:] - m_old)
        part = jnp.dot(jnp.concatenate([p0, p1], axis=1).astype(BF16), v_ref[0, keys, :],
                       preferred_element_type=F32)
        acc = part if acc is None else acc + part
        s_ref[c] = s_new
        mc = jnp.maximum(s_new[:, :LANES], s_new[:, LANES:])
        mp_ref[...] = mc if c == 0 else jnp.maximum(mp_ref[...], mc)
    acc_ref[...] = acc
    mb_ref[...] = jnp.broadcast_to(jnp.max(mp_ref[...], axis=-1, keepdims=True), mb_ref.shape)
    return done


def _split_kv(refs, n_kv):
    return [(refs[2 * i], refs[2 * i + 1]) for i in range(n_kv)], refs[2 * n_kv:]


def _pack_heads(even, odd):
    lo = _lane_iota(even.shape) < HEAD_DIM
    return jnp.where(lo, even, pltpu.roll(odd, HEAD_DIM, 1))


def _gqa_body(q_ref, *rest, n_kv, n_tiles):
    tq = q_ref.shape[1]
    lo = _lane_iota((tq, LANES)) < HEAD_DIM
    heads = []
    for j in range(2):
        t = q_ref[0, :, j * LANES:(j + 1) * LANES].astype(F32)
        heads.append(jnp.where(lo, t, 0.0))
        heads.append(jnp.where(lo, pltpu.roll(t, HEAD_DIM, 1), 0.0))
    qs = jnp.concatenate(heads, axis=0).astype(BF16)
    kv, (o_ref, *scratch) = _split_kv(rest, n_kv)
    o = _attn_step(qs, kv, *scratch)
    for j in range(2):
        o_ref[0, :, j * LANES:(j + 1) * LANES] = _pack_heads(
            o[(2 * j) * tq:(2 * j + 1) * tq], o[(2 * j + 1) * tq:(2 * j + 2) * tq]).astype(o_ref.dtype)


def _head_parity(n_tiles):
    step = pl.program_id(0)
    return jnp.minimum(step, n_tiles - 1) % 2, jnp.maximum(step - 2, 0) % 2


def _store_head(o_ref, res, parity):
    @pl.when(parity == 0)
    def _():
        o_ref[0, :, 0:HEAD_DIM] = res[:, 0:HEAD_DIM].astype(o_ref.dtype)

    @pl.when(parity == 1)
    def _():
        o_ref[0, :, HEAD_DIM:LANES] = pltpu.roll(res, HEAD_DIM, 1)[:, HEAD_DIM:LANES].astype(o_ref.dtype)


def _diff_body(q_ref, *rest, n_kv, n_tiles, lam_init):
    kv, (lq1_ref, lk1_ref, lq2_ref, lk2_ref, gsub_ref, o_ref, *scratch) = _split_kv(rest, n_kv)
    tq = q_ref.shape[1]
    par, par_done = _head_parity(n_tiles)
    lane = _lane_iota((tq, LANES))
    t = q_ref[0]
    zero = jnp.zeros_like(t)
    qs = jnp.concatenate([jnp.where((lane // DIFF_QK_DIM) == 2 * par + j, t, zero) for j in range(2)],
                         axis=0)
    o = _attn_step(qs, kv, *scratch)
    lam = (jnp.exp(jnp.sum(lq1_ref[...] * lk1_ref[...], axis=-1, keepdims=True))
           - jnp.exp(jnp.sum(lq2_ref[...] * lk2_ref[...], axis=-1, keepdims=True)) + lam_init)
    d = o[0:tq] - lam * o[tq:2 * tq]
    ms = jnp.sum(jnp.where(lane < DIFF_V_DIM, d * d, 0.0), axis=-1, keepdims=True) * (1.0 / DIFF_V_DIM)
    _store_head(o_ref, (d * lax.rsqrt(ms + NORM_EPS) * gsub_ref[...]) * (1.0 - lam_init), par_done)


def _mla_body(q_ref, *rest, n_kv, n_tiles):
    kv, (o_ref, *scratch) = _split_kv(rest, n_kv)
    par, par_done = _head_parity(n_tiles)
    t = q_ref[0]
    lane = _lane_iota(t.shape)
    nope0 = MLA_NOPE_DIM * par
    rope0 = LANES + MLA_ROPE_DIM * par
    mine = ((lane >= nope0) & (lane < nope0 + MLA_NOPE_DIM)) | (
        (lane >= rope0) & (lane < rope0 + MLA_ROPE_DIM))
    qs = jnp.where(mine, t, jnp.zeros_like(t))
    _store_head(o_ref, _attn_step(qs, kv, *scratch), par_done)


def _attention(body, q, kvs, extra, *, n_groups, wq, wk, n_s, tq, per_head):
    b, q_rows, _ = q.shape
    q_tiles = q_rows // tq
    wo = LANES if per_head else wq
    heads = 2 if per_head else 1
    n_chunks = sum(k.shape[1] // KEY_CHUNK for k, _ in kvs)
    m_rows = n_s * tq
    n_tiles = b * n_groups * q_tiles * heads

    def split(t):
        t, par = t // heads, t % heads
        return t // (n_groups * q_tiles), (t // q_tiles) % n_groups, t % q_tiles, par

    cur = lambda j: split(jnp.minimum(j, n_tiles - 1))
    prev = lambda j: split(jnp.clip(j - 1, 0, n_tiles - 1))
    done = lambda j: split(jnp.maximum(j - 2, 0))
    kv_specs, kv_args = [], []
    for k, v in kvs:
        kv_specs += [pl.BlockSpec((1, k.shape[1], wk), lambda j: (cur(j)[0], 0, cur(j)[1])),
                     pl.BlockSpec((1, v.shape[1], LANES),
                                  lambda j: (prev(j)[0], 0, prev(j)[1] * heads + prev(j)[3]))]
        kv_args += [k, v]
    extra_specs = [pl.BlockSpec(e.shape, lambda j: (0, 0)) for e in extra]
    return pl.pallas_call(
        functools.partial(body, n_kv=len(kvs), n_tiles=n_tiles),
        grid=(n_tiles + 2,),
        in_specs=[pl.BlockSpec((1, tq, wq), lambda j: (cur(j)[0], cur(j)[2], cur(j)[1]))]
        + kv_specs + extra_specs,
        out_specs=pl.BlockSpec((1, tq, wo), lambda j: (done(j)[0], done(j)[2], done(j)[1])),
        out_shape=jax.ShapeDtypeStruct((b, q_rows, n_groups * wo), BF16),
        scratch_shapes=[pltpu.VMEM((n_chunks, m_rows, KEY_CHUNK), F32),
                        pltpu.VMEM((m_rows, LANES), F32),
                        pltpu.VMEM((m_rows, LANES), F32),
                        pltpu.VMEM((m_rows, LANES), F32)],
        compiler_params=_params("arbitrary"),
        name=body.__name__.strip("_"),
    )(q, *kv_args, *extra)


def _router_gates(logits):
    lane = _lane_iota(logits.shape).astype(F32)
    neg = jnp.float32(-jnp.inf)
    z = jnp.where(lane < N_EXPERTS, logits, neg)
    m1 = jnp.max(z, axis=-1, keepdims=True)
    i1 = jnp.min(jnp.where(z == m1, lane, float(LANES)), axis=-1, keepdims=True)
    z2 = jnp.where(lane == i1, neg, z)
    m2 = jnp.max(z2, axis=-1, keepdims=True)
    i2 = jnp.min(jnp.where(z2 == m2, lane, float(LANES)), axis=-1, keepdims=True)
    e2 = jnp.exp(m2 - m1)
    den = 1.0 + e2
    return jnp.where(lane == i1, 1.0 / den, 0.0) + jnp.where(lane == i2, e2 / den, 0.0)


def _mixer_residual(x_ref, oa_ref, ob_ref, om_ref, w_ref, mod, g_ref):
    wa, wb = oa_ref.shape[2], ob_ref.shape[2]
    y = jnp.dot(oa_ref[0], w_ref[0:wa, :], preferred_element_type=F32)
    y += jnp.dot(ob_ref[0], w_ref[wa:wa + wb, :], preferred_element_type=F32)
    y += jnp.dot(om_ref[0], w_ref[wa + wb:, :], preferred_element_type=F32)
    x = x_ref[0] + mod[2:3] * y
    n = x * _rms_rows(x, x.shape[-1]) * g_ref[...]
    return x, n * (1.0 + mod[4:5]) + mod[3:4]


def _mixer_specs(x, oa, ob, om, w, mod_row, tm):
    d = x.shape[2]
    row = lambda bi, i: (bi, i, 0)
    return [pl.BlockSpec((1, tm, d), row),
            pl.BlockSpec((1, tm, oa.shape[2]), row),
            pl.BlockSpec((1, tm, ob.shape[2]), row),
            pl.BlockSpec((1, tm, om.shape[2]), row),
            pl.BlockSpec(w.shape, lambda bi, i: (0, 0)),
            pl.BlockSpec((1, N_MOD, d), lambda bi, i: (mod_row(bi), 0, 0)),
            pl.BlockSpec((1, d), lambda bi, i: (0, 0))]


def _outproj_body(x_ref, oa_ref, ob_ref, om_ref, w_ref, mod_ref, g_ref, wr_ref,
                  xo_ref, h_ref, gates_ref, gates_t_ref, hprev_ref):
    @pl.when(pl.program_id(0) == 0)
    def _():
        hprev_ref[...] = jnp.zeros(hprev_ref.shape, F32)

    h_prev = hprev_ref[...]
    h_hi = h_prev.astype(BF16)
    h_lo = (h_prev - h_hi.astype(F32)).astype(BF16)
    logits = (jnp.dot(h_hi, wr_ref[0], preferred_element_type=F32)
              + jnp.dot(h_lo, wr_ref[0], preferred_element_type=F32)
              + jnp.dot(h_hi, wr_ref[1], preferred_element_type=F32))
    gates = _router_gates(logits)
    gates_ref[0] = gates
    gates_t_ref[0] = gates.T[:gates_t_ref.shape[1]]

    x, h = _mixer_residual(x_ref, oa_ref, ob_ref, om_ref, w_ref, mod_ref[0], g_ref)
    xo_ref[0] = x
    h_ref[0] = h.astype(BF16)
    hprev_ref[...] = h


def _out_project_routed(x, oa, ob, om, w, mod3, g, mod_row, w_router):
    b, s, d = x.shape
    tm = min(PROJ_ROWS, s)
    nt = s // tm
    n_tiles = b * nt
    cur = lambda j: (jnp.minimum(j, n_tiles - 1) // nt, jnp.minimum(j, n_tiles - 1) % nt)
    prev = lambda j: (jnp.maximum(j - 1, 0) // nt, jnp.maximum(j - 1, 0) % nt)
    flat = lambda spec: pl.BlockSpec(spec.block_shape, lambda j, f=spec.index_map: f(*cur(j)))
    row = lambda j: (cur(j)[0], cur(j)[1], 0)
    return pl.pallas_call(
        _outproj_body,
        grid=(n_tiles + 1,),
        in_specs=[flat(sp) for sp in _mixer_specs(x, oa, ob, om, w, mod_row, tm)]
        + [pl.BlockSpec(w_router.shape, lambda j: (0, 0, 0))],
        out_specs=[pl.BlockSpec((1, tm, d), row), pl.BlockSpec((1, tm, d), row),
                   pl.BlockSpec((1, tm, LANES), lambda j: (prev(j)[0], prev(j)[1], 0)),
                   pl.BlockSpec((1, GATE_ROWS, tm), lambda j: (prev(j)[0], 0, prev(j)[1]))],
        out_shape=[jax.ShapeDtypeStruct((b, s, d), F32), jax.ShapeDtypeStruct((b, s, d), BF16),
                   jax.ShapeDtypeStruct((b, s, LANES), F32), jax.ShapeDtypeStruct((b, GATE_ROWS, s), F32)],
        scratch_shapes=[pltpu.VMEM((tm, d), F32)],
        compiler_params=_params("arbitrary"),
        name="out_project_routed",
    )(x, oa, ob, om, w, mod3, g, w_router)


def _swiglu_chunks(h, wg_ref, wu_ref, wo_ref, lead):
    acc = None
    for j in range(wo_ref.shape[len(lead)] // FFN_CHUNK):
        cols = slice(j * FFN_CHUNK, (j + 1) * FFN_CHUNK)
        gate = jnp.dot(h, wg_ref[lead + (slice(None), cols)], preferred_element_type=F32)
        up = jnp.dot(h, wu_ref[lead + (slice(None), cols)], preferred_element_type=F32)
        a = (gate * jax.nn.sigmoid(gate) * up).astype(BF16)
        part = jnp.dot(a, wo_ref[lead + (cols, slice(None))], preferred_element_type=F32)
        acc = part if acc is None else acc + part
    return acc


def _residual_out(x, gate, y, gf_ref, final_norm):
    x = x + gate * y
    if final_norm:
        x = x * _rms_rows(x, x.shape[-1]) * gf_ref[...]
    return x


def _ffn_body(x_ref, oa_ref, ob_ref, om_ref, w_ref, mod_ref, g_ref, wg_ref, wu_ref, wo_ref, gf_ref,
              o_ref, *, final_norm):
    mod = mod_ref[0]
    x, h = _mixer_residual(x_ref, oa_ref, ob_ref, om_ref, w_ref, mod, g_ref)
    y = _swiglu_chunks(h.astype(BF16), wg_ref, wu_ref, wo_ref, ())
    o_ref[0] = _residual_out(x, mod[5:6], y, gf_ref, final_norm)


def _mix_dense_ffn(x, oa, ob, om, w, mod3, g, mod_row, wg, wu, wo, g_final, final_norm):
    b, s, d = x.shape
    tm = min(FFN_ROWS, s)
    resident = lambda a: pl.BlockSpec(a.shape, lambda bi, i: (0, 0), pipeline_mode=pl.Buffered(1))
    return pl.pallas_call(
        functools.partial(_ffn_body, final_norm=final_norm),
        grid=(b, s // tm),
        in_specs=_mixer_specs(x, oa, ob, om, w, mod_row, tm)
        + [resident(wg), resident(wu), resident(wo), pl.BlockSpec((1, d), lambda bi, i: (0, 0))],
        out_specs=pl.BlockSpec((1, tm, d), lambda bi, i: (bi, i, 0)),
        out_shape=jax.ShapeDtypeStruct((b, s, d), F32),
        compiler_params=_params("parallel", "arbitrary"),
        name="dense_ffn",
    )(x, oa, ob, om, w, mod3, g, wg, wu, wo, g_final)


def _moe_body(x_ref, h_ref, gates_ref, gates_t_ref, tri_ref, tri_t_ref, wg_ref, wu_ref, wo_ref,
              mod_ref, gf_ref, o_ref, tot_ref, rank_ref, rank_t_ref, *, final_norm):
    e = pl.program_id(2)
    tm = h_ref.shape[1]
    parts = [slice(k * MOE_PART, (k + 1) * MOE_PART) for k in range(tm // MOE_PART)]
    gates = gates_ref[0]
    lane = _lane_iota(gates.shape)

    @pl.when(e == 0)
    def _():
        tot_ref[...] = jnp.zeros(tot_ref.shape, F32)
        live = (gates > 0.0).astype(BF16)
        live_t = (gates_t_ref[0] > 0.0).astype(BF16)
        for part in parts:
            rank_ref[part, :] = jnp.dot(tri_ref[...], live[part], preferred_element_type=F32)
            rank_t_ref[:, part] = jnp.dot(live_t[:, part], tri_t_ref[...], preferred_element_type=F32)

    pick = lambda a: jnp.sum(jnp.where(lane == e, a, 0.0), axis=-1, keepdims=True)
    rank_row = rank_t_ref[pl.ds(e, 1), :]
    fits = jnp.max(rank_row) <= float(MOE_SLOTS)

    @pl.when(fits)
    def _():
        gate_e = pick(gates)
        rank_e = pick(rank_ref[...])
        live_row = gates_t_ref[0, pl.ds(e, 1), :] > 0.0
        slot_sub = lax.broadcasted_iota(jnp.int32, (MOE_SLOTS, MOE_PART), 0).astype(F32) + 1.0
        rows = []
        for part in parts:
            gather = jnp.where((rank_row[:, part] == slot_sub) & live_row[:, part], 1.0, 0.0)
            rows.append(jnp.dot(gather.astype(BF16), h_ref[0, part, :],
                                preferred_element_type=F32).astype(BF16))
        y = _swiglu_chunks(jnp.concatenate(rows, axis=0), wg_ref, wu_ref, wo_ref, (0,))
        y_hi = y.astype(BF16)
        y_lo = (y - y_hi.astype(F32)).astype(BF16)
        slot_lane = _lane_iota((MOE_PART, MOE_SLOTS)).astype(F32) + 1.0
        for k, part in enumerate(parts):
            slots = slice(k * MOE_SLOTS, (k + 1) * MOE_SLOTS)
            scatter = jnp.where((rank_e[part] == slot_lane) & (gate_e[part] > 0.0), 1.0, 0.0).astype(BF16)
            back = (jnp.dot(scatter, y_hi[slots], preferred_element_type=F32)
                    + jnp.dot(scatter, y_lo[slots], preferred_element_type=F32))
            tot_ref[part, :] += gate_e[part] * back

    @pl.when(jnp.logical_not(fits))
    def _():
        tot_ref[...] += pick(gates) * _swiglu_chunks(h_ref[0], wg_ref, wu_ref, wo_ref, (0,))

    @pl.when(e == pl.num_programs(2) - 1)
    def _():
        o_ref[0] = _residual_out(x_ref[0], mod_ref[0][5:6], tot_ref[...], gf_ref, final_norm)


def _moe_ffn(x, h, gates, gates_t, wg, wu, wo, mod3, mod_row, g_final, final_norm):
    b, s, d = x.shape
    tm = min(MOE_ROWS, s)
    tp = MOE_PART
    tri = jnp.asarray(np.tril(np.ones((tp, tp), np.float32)), BF16)
    row = lambda bi, i, e: (bi, i, 0)
    const = lambda bi, i, e: (0, 0)
    expert = lambda a: pl.BlockSpec((1,) + a.shape[1:], lambda bi, i, e: (e, 0, 0))
    return pl.pallas_call(
        functools.partial(_moe_body, final_norm=final_norm),
        grid=(b, s // tm, wg.shape[0]),
        in_specs=[pl.BlockSpec((1, tm, d), row, pipeline_mode=pl.Buffered(1)),
                  pl.BlockSpec((1, tm, d), row),
                  pl.BlockSpec((1, tm, LANES), row),
                  pl.BlockSpec((1, GATE_ROWS, tm), lambda bi, i, e: (bi, 0, i)),
                  pl.BlockSpec((tp, tp), const), pl.BlockSpec((tp, tp), const),
                  expert(wg), expert(wu), expert(wo),
                  pl.BlockSpec((1, N_MOD, d), lambda bi, i, e: (mod_row(bi), 0, 0)),
                  pl.BlockSpec((1, d), const)],
        out_specs=pl.BlockSpec((1, tm, d), row),
        out_shape=jax.ShapeDtypeStruct((b, s, d), F32),
        scratch_shapes=[pltpu.VMEM((tm, d), F32), pltpu.VMEM((tm, LANES), F32),
                        pltpu.VMEM((GATE_ROWS, tm), F32)],
        compiler_params=_params("parallel", "parallel", "arbitrary"),
        name="moe_ffn",
    )(x, h, gates, gates_t, tri, tri.T, wg, wu, wo, mod3, g_final)


def _deinterleave(n):
    return np.concatenate([np.arange(0, n, 2), np.arange(1, n, 2)])


def _in_proj_columns(pad):
    pads = lambda n: np.full((n,), pad)
    cols = []
    for hd in range(GQA_HEADS):
        cols.append(64 * hd + _deinterleave(64))
    for g in range(GQA_KV_HEADS):
        cols += [512 + 64 * g + _deinterleave(64), pads(64)]
    for g in range(GQA_KV_HEADS):
        cols += [640 + 64 * g + np.arange(64), pads(64)]
    for u in range(2 * DIFF_HEADS):
        cols.append(768 + 32 * u + _deinterleave(32))
    for u in range(2 * DIFF_HEADS):
        cols.append(1024 + 32 * u + _deinterleave(32))
    for hd in range(DIFF_HEADS):
        cols += [1280 + 64 * hd + np.arange(64), pads(64)]
    cols += [1536 + np.arange(MLA_Q_RANK), pads(64)]
    cols.append(1728 + np.arange(MLA_KV_RANK))
    cols += [1856 + _deinterleave(32), 1856 + _deinterleave(32), pads(64)]
    return np.concatenate(cols)


def _uq_columns(pad):
    per = MLA_NOPE_DIM + MLA_ROPE_DIM
    cols = []
    for pr in range(2):
        h0, h1 = 2 * pr, 2 * pr + 1
        cols += [per * h0 + np.arange(64), per * h1 + np.arange(64),
                 per * h0 + 64 + _deinterleave(32), per * h1 + 64 + _deinterleave(32),
                 np.full((64,), pad)]
    return np.concatenate(cols)


def _ukv_columns(pad):
    per = MLA_NOPE_DIM + MLA_V_DIM
    k = [per * hd + np.arange(64) for hd in range(MLA_HEADS)]
    v = []
    for hd in range(MLA_HEADS):
        v += [per * hd + 64 + np.arange(64), np.full((64,), pad)]
    return np.concatenate(k + v)


def _take_cols(w, cols):
    w_ext = jnp.concatenate([w, jnp.zeros((w.shape[0], 1), w.dtype)], axis=1)
    return jnp.take(w_ext, jnp.asarray(cols), axis=1)


def _rope_tables(s):
    t = np.arange(s)
    rows = (t // GRID_W).astype(np.float32)
    cols = (t % GRID_W).astype(np.float32)
    out = []
    for dim in (HEAD_DIM, DIFF_QK_DIM):
        quarter = dim // 4
        half = dim // 2
        inv_freq = (ROPE_THETA ** (-np.arange(quarter, dtype=np.float32) / quarter)).astype(np.float32)
        ang = np.concatenate([rows[:, None] * inv_freq, cols[:, None] * inv_freq], axis=-1)
        lane = np.arange(LANES)
        idx = (lane % dim) % half
        sign = np.where((lane % dim) < half, -1.0, 1.0).astype(np.float32)
        out += [jnp.asarray(np.cos(ang)[:, idx], F32), jnp.asarray(np.sin(ang)[:, idx] * sign, F32)]
    return out


def kernel(x, c, ctx, c_ctx, w_mod, b_mod, g_attn, g_ffn, w_in, w_out, gqa_gq, gqa_gk,
           diff_lq1, diff_lk1, diff_lq2, diff_lk2, diff_gsub, mla_gcq, mla_gckv, mla_wuq, mla_wukv,
           ffn_w_in, ffn_w_out, moe_router, moe_w_in, moe_w_out, g_final):
    b, s, d = x.shape
    n_ctx = ctx.shape[1]
    depth = w_mod.shape[0]
    tables = _rope_tables(s)
    no_rotation = [jnp.ones((n_ctx, LANES), F32), jnp.zeros((n_ctx, LANES), F32)] * 2
    in_cols = _in_proj_columns(w_in.shape[2])
    uq_cols = _uq_columns(mla_wuq.shape[2])
    ukv_cols = _ukv_columns(mla_wukv.shape[2])
    perm64 = _deinterleave(HEAD_DIM)
    bd = jnp.asarray(np.kron(np.eye(LANES // HEAD_DIM), np.full((HEAD_DIM, HEAD_DIM), 1.0 / HEAD_DIM)), F32)

    mod_rows = 16
    c_all = jnp.zeros((mod_rows, d), F32).at[:b].set(c).at[b].set(c_ctx)
    pad_lanes = lambda v: jnp.zeros((1, LANES), F32).at[0, :v.shape[0]].set(v)

    xc = ctx
    for l in range(depth):
        need_ctx = l < depth - 1
        lam_init = 0.8 - 0.6 * math.exp(-0.3 * l)
        mod3 = _modulation(c_all, w_mod[l], b_mod[l]).reshape(mod_rows, N_MOD, d)

        w_in_p = _take_cols(w_in[l], in_cols).astype(BF16)
        wuq_p = jnp.zeros((2 * LANES, 512), F32).at[:MLA_Q_RANK].set(
            _take_cols(mla_wuq[l], uq_cols)).astype(BF16)
        wukv_p = _take_cols(mla_wukv[l], ukv_cols).astype(BF16)
        gq2 = jnp.tile(gqa_gq[l][perm64], 2).reshape(1, LANES)
        gk2 = jnp.tile(gqa_gk[l][perm64], 2).reshape(1, LANES)
        gcq = jnp.zeros((1, 2 * LANES), F32).at[0, :MLA_Q_RANK].set(mla_gcq[l])
        gckv = mla_gckv[l].reshape(1, MLA_KV_RANK)

        proj_w = (g_attn[l].reshape(1, d), w_in_p)
        proj_aux = (gq2, gk2, bd, gcq, gckv, wuq_p, wukv_p)
        lat = _project(x, mod3, lambda bi: bi, *proj_w, tables, *proj_aux)
        ctxp = _project(xc, mod3, lambda bi: b, *proj_w, no_rotation, *proj_aux)

        diff_extra = [pad_lanes(diff_lq1[l]), pad_lanes(diff_lk1[l]), pad_lanes(diff_lq2[l]),
                      pad_lanes(diff_lk2[l]), pad_lanes(diff_gsub[l])]
        diff_body = functools.partial(_diff_body, lam_init=lam_init)
        diff_body.__name__ = "_diff_body"

        def attend(q, streams):
            kv = lambda ik, iv: [(p[ik], p[iv]) for p in streams]
            rows = q[0].shape[1]
            stack = 8 * ROW_TILE
            oa = _attention(_gqa_body, q[0], kv(1, 2), [], n_groups=2, wq=256, wk=LANES,
                            n_s=4, tq=min(stack // 4, rows), per_head=False)
            ob = _attention(diff_body, q[3], kv(4, 5), diff_extra, n_groups=2, wq=LANES, wk=LANES,
                            n_s=2, tq=min(stack // 2, rows), per_head=True)
            om = _attention(_mla_body, q[6], kv(7, 8), [], n_groups=2, wq=256, wk=256,
                            n_s=1, tq=min(stack, rows), per_head=True)
            return oa, ob, om

        w_out_b = w_out[l].astype(BF16)
        g2 = g_ffn[l].reshape(1, d)
        dense = l % 2 == 0
        if dense:
            n_hidden = ffn_w_out.shape[1]
            wg = ffn_w_in[l // 2][:, :n_hidden].astype(BF16)
            wu = ffn_w_in[l // 2][:, n_hidden:].astype(BF16)
            wo = ffn_w_out[l // 2].astype(BF16)
            w_router = None
        else:
            n_hidden = moe_w_out.shape[2]
            wi = moe_w_in[l // 2]
            wg = wi[:, :, :n_hidden].astype(BF16)
            wu = wi[:, :, n_hidden:].astype(BF16)
            wo = moe_w_out[l // 2].astype(BF16)
            wr = jnp.zeros((d, LANES), F32).at[:, :N_EXPERTS].set(moe_router[l // 2])
            wr_hi = wr.astype(BF16)
            w_router = jnp.stack([wr_hi, (wr - wr_hi.astype(F32)).astype(BF16)])
        last = l == depth - 1

        def channel_mix(xs, attn_out, mod_row, final_norm):
            gf = g_final.reshape(1, d)
            if dense:
                return _mix_dense_ffn(xs, *attn_out, w_out_b, mod3, g2, mod_row, wg, wu, wo, gf, final_norm)
            res = _out_project_routed(xs, *attn_out, w_out_b, mod3, g2, mod_row, w_router)
            return _moe_ffn(*res, wg, wu, wo, mod3, mod_row, gf, final_norm)

        x_new = channel_mix(x, attend(lat, [lat, ctxp]), lambda bi: bi, last)
        if need_ctx:
            xc = channel_mix(xc, attend(ctxp, [ctxp]), lambda bi: b, False)
        x = x_new
    return x
```

```python
import functools
import math

import numpy as np
import jax
import jax.numpy as jnp
from jax import lax
from jax.experimental import pallas as pl
from jax.experimental.pallas import tpu as pltpu

LANES = 128
MXU_TILE = 256
VMEM_LIMIT = 60 * 1024 * 1024

NORM_EPS = 1e-6
ROPE_THETA = 10000.0
GRID_W = 64
N_MOD = 6
HEAD_DIM = 64
GQA_HEADS, GQA_KV_HEADS = 8, 2
DIFF_HEADS, DIFF_QK_DIM, DIFF_V_DIM = 4, 32, 64
MLA_HEADS, MLA_Q_RANK, MLA_KV_RANK = 4, 192, 128
MLA_NOPE_DIM, MLA_ROPE_DIM, MLA_V_DIM = 64, 32, 64
N_EXPERTS = 8
GATE_ROWS = 16

ATTN_ROWS = 2048
PROJ_ROWS = 512
KEY_CHUNK = MXU_TILE
FFN_CHUNK = MXU_TILE
FFN_ROWS = 512
MOE_ROWS = 1024
MOE_PART = 512
MOE_SLOTS = 256

F32 = jnp.float32
BF16 = jnp.bfloat16
HIGHEST = lax.Precision.HIGHEST


def _params(*sem):
    return pltpu.CompilerParams(dimension_semantics=sem, vmem_limit_bytes=VMEM_LIMIT)


def _lane_iota(shape):
    return lax.broadcasted_iota(jnp.int32, shape, len(shape) - 1)


def _mod_body(c_ref, w_ref, b_ref, o_ref):
    c = c_ref[...]
    sc = c * jax.nn.sigmoid(c)
    o_ref[...] = jnp.dot(sc, w_ref[...], precision=HIGHEST,
                         preferred_element_type=F32) + b_ref[...]


def _modulation(c_all, w, b):
    rows, d = c_all.shape
    n = w.shape[1]
    tn = n // 4
    return pl.pallas_call(
        _mod_body,
        grid=(n // tn,),
        in_specs=[pl.BlockSpec((rows, d), lambda j: (0, 0)),
                  pl.BlockSpec((d, tn), lambda j: (0, j)),
                  pl.BlockSpec((1, tn), lambda j: (0, j))],
        out_specs=pl.BlockSpec((rows, tn), lambda j: (0, j)),
        out_shape=jax.ShapeDtypeStruct((rows, n), F32),
        compiler_params=_params("arbitrary"),
        name="modulation",
    )(c_all, w, b.reshape(1, n))


def _rms_rows(x, width):
    return lax.rsqrt(jnp.sum(x * x, axis=-1, keepdims=True) * (1.0 / width) + NORM_EPS)


def _rope(t, cos, sin_signed, half):
    lane = _lane_iota(t.shape)
    partner = jnp.where((lane & half) == 0,
                        pltpu.roll(t, LANES - half, 1), pltpu.roll(t, half, 1))
    return t * cos + partner * sin_signed


def _with_ones(t):
    return jnp.where(_lane_iota(t.shape) == HEAD_DIM, 1.0, t)


def _proj_body(x_ref, mod_ref, g_ref, w_ref, ca_ref, sa_ref, cb_ref, sb_ref,
               gq_ref, gk_ref, bd_ref, gcq_ref, gckv_ref, wuq_ref, wukv_ref,
               qa_ref, ka_ref, va_ref, qb_ref, kb_ref, vb_ref, qm_ref, km_ref, vm_ref, p_ref,
               *, scale_a, scale_b, scale_m):
    @pl.when(pl.program_id(0) == 0)
    def _():
        p_ref[...] = jnp.zeros(p_ref.shape, F32)

    x = x_ref[0]
    d = x.shape[-1]
    mod = mod_ref[0]
    y = x * _rms_rows(x, d) * g_ref[...]
    h = (y * (1.0 + mod[1:2]) + mod[0:1]).astype(BF16)
    p_new = jnp.dot(h, w_ref[...], preferred_element_type=F32)
    p = p_ref

    ca, sa, cb, sb = ca_ref[...], sa_ref[...], cb_ref[...], sb_ref[...]
    bd = bd_ref[...]
    tile = lambda ref, j: (0, slice(None), slice(j * LANES, (j + 1) * LANES))
    cols = lambda a, base, j: a[:, base + j * LANES:base + (j + 1) * LANES]

    def head_norm(t, g):
        sq = t * t
        hi = sq.astype(BF16)
        lo = (sq - hi.astype(F32)).astype(BF16)
        ms = (jnp.dot(hi, bd, preferred_element_type=F32) + jnp.dot(lo, bd, preferred_element_type=F32))
        return t * lax.rsqrt(ms + NORM_EPS) * g

    for j in range(4):
        t = head_norm(cols(p, 0, j), gq_ref[...])
        qa_ref[tile(qa_ref, j)] = (_rope(t, ca, sa, 32) * scale_a).astype(BF16)
    for j in range(2):
        t = head_norm(cols(p, 512, j), gk_ref[...])
        ka_ref[tile(ka_ref, j)] = _rope(t, ca, sa, 32).astype(BF16)
        va_ref[tile(va_ref, j)] = _with_ones(cols(p, 768, j)).astype(BF16)

    for j in range(2):
        qb_ref[tile(qb_ref, j)] = (_rope(cols(p, 1024, j), cb, sb, 16) * scale_b).astype(BF16)
        kb_ref[tile(kb_ref, j)] = _rope(cols(p, 1280, j), cb, sb, 16).astype(BF16)
    for j in range(4):
        vb_ref[tile(vb_ref, j)] = _with_ones(cols(p, 1536, j)).astype(BF16)

    cq = p[:, 2048:2304]
    cqn = (cq * _rms_rows(cq, MLA_Q_RANK) * gcq_ref[...]).astype(BF16)
    uq = jnp.dot(cqn, wuq_ref[...], preferred_element_type=F32)
    ckv = p[:, 2304:2432]
    ckvn = (ckv * _rms_rows(ckv, MLA_KV_RANK) * gckv_ref[...]).astype(BF16)
    ukv = jnp.dot(ckvn, wukv_ref[...], preferred_element_type=F32)
    kr = _rope(p[:, 2432:2560], cb, sb, 16).astype(BF16)
    for pr in range(2):
        qm_ref[tile(qm_ref, 2 * pr)] = (cols(uq, 0, 2 * pr) * scale_m).astype(BF16)
        qr = _rope(cols(uq, 0, 2 * pr + 1), cb, sb, 16)
        qm_ref[tile(qm_ref, 2 * pr + 1)] = (qr * scale_m).astype(BF16)
        km_ref[tile(km_ref, 2 * pr)] = cols(ukv, 0, pr).astype(BF16)
        km_ref[tile(km_ref, 2 * pr + 1)] = kr
    for j in range(4):
        vm_ref[tile(vm_ref, j)] = _with_ones(cols(ukv, 256, j)).astype(BF16)
    p_ref[...] = p_new


def _project(x, mod3, mod_row, g, w, tables, gq2, gk2, bd, gcq, gckv, wuq, wukv):
    b, s, d = x.shape
    tm = min(PROJ_ROWS, s)
    nt = s // tm
    n_tiles = b * nt
    widths = (512, 256, 256, 256, 256, 512, 512, 512, 512)
    log2e = math.log2(math.e)
    body = functools.partial(_proj_body, scale_a=HEAD_DIM ** -0.5 * log2e,
                             scale_b=DIFF_QK_DIM ** -0.5 * log2e,
                             scale_m=(MLA_NOPE_DIM + MLA_ROPE_DIM) ** -0.5 * log2e)
    cur = lambda j: (jnp.minimum(j, n_tiles - 1) // nt, jnp.minimum(j, n_tiles - 1) % nt)
    prev = lambda j: (jnp.maximum(j - 1, 0) // nt, jnp.maximum(j - 1, 0) % nt)
    const = lambda shape: pl.BlockSpec(shape, lambda j: (0,) * len(shape))
    tab = pl.BlockSpec((tm, LANES), lambda j: (prev(j)[1], 0))
    return pl.pallas_call(
        body,
        grid=(n_tiles + 1,),
        in_specs=[pl.BlockSpec((1, tm, d), lambda j: (cur(j)[0], cur(j)[1], 0)),
                  pl.BlockSpec((1, N_MOD, d), lambda j: (mod_row(cur(j)[0]), 0, 0)),
                  const((1, d)), const(w.shape), tab, tab, tab, tab,
                  const((1, LANES)), const((1, LANES)), const((LANES, LANES)),
                  const(gcq.shape), const(gckv.shape), const(wuq.shape), const(wukv.shape)],
        out_specs=[pl.BlockSpec((1, tm, wd), lambda j: (prev(j)[0], prev(j)[1], 0)) for wd in widths],
        out_shape=[jax.ShapeDtypeStruct((b, s, wd), BF16) for wd in widths],
        scratch_shapes=[pltpu.VMEM((tm, w.shape[1]), F32)],
        compiler_params=_params("arbitrary"),
        name="project",
    )(x, mod3, g, w, *tables, gq2, gk2, bd, gcq, gckv, wuq, wukv)


def _attn_step(qs, kv_refs, s_ref, mp_ref, mb_ref, acc_ref):
    step = pl.program_id(0)

    @pl.when(step == 0)
    def _():
        s_ref[...] = jnp.zeros(s_ref.shape, F32)
        mb_ref[...] = jnp.zeros(mb_ref.shape, F32)
        acc_ref[...] = jnp.ones(acc_ref.shape, F32)

    raw = acc_ref[...]
    done = raw * (1.0 / raw[:, HEAD_DIM:HEAD_DIM + 1])

    acc = None
    chunks = [(k_ref, v_ref, slice(i * KEY_CHUNK, (i + 1) * KEY_CHUNK))
              for k_ref, v_ref in kv_refs for i in range(k_ref.shape[1] // KEY_CHUNK)]
    for c, (k_ref, v_ref, keys) in enumerate(chunks):
        s_new = lax.dot_general(qs, k_ref[0, keys, :], (((1,), (1,)), ((), ())),
                                preferred_element_type=F32)
        s_old = s_ref[c]
        m_old = mb_ref[...]
        p0 = jnp.exp2(s_old[:, :LANES] - m_old)
        p1 = jnp.exp2(s_old[:, LANES:] - m_old)
        part = jnp.dot(jnp.concatenate([p0, p1], axis=1).astype(BF16), v_ref[0, keys, :],
                       preferred_element_type=F32)
        acc = part if acc is None else acc + part
        s_ref[c] = s_new
        mc = jnp.maximum(s_new[:, :LANES], s_new[:, LANES:])
        mp_ref[...] = mc if c == 0 else jnp.maximum(mp_ref[...], mc)
    acc_ref[...] = acc
    mb_ref[...] = jnp.broadcast_to(jnp.max(mp_ref[...], axis=-1, keepdims=True), mb_ref.shape)
    return done


def _split_kv(refs, n_kv):
    return [(refs[2 * i], refs[2 * i + 1]) for i in range(n_kv)], refs[2 * n_kv:]


def _pack_heads(even, odd):
    lo = _lane_iota(even.shape) < HEAD_DIM
    return jnp.where(lo, even, pltpu.roll(odd, HEAD_DIM, 1))


def _gqa_body(q_ref, *rest, n_kv, n_tiles):
    tq = q_ref.shape[1]
    lo = _lane_iota((tq, LANES)) < HEAD_DIM
    heads = []
    for j in range(2):
        t = q_ref[0, :, j * LANES:(j + 1) * LANES].astype(F32)
        heads.append(jnp.where(lo, t, 0.0))
        heads.append(jnp.where(lo, pltpu.roll(t, HEAD_DIM, 1), 0.0))
    qs = jnp.concatenate(heads, axis=0).astype(BF16)
    kv, (o_ref, *scratch) = _split_kv(rest, n_kv)
    o = _attn_step(qs, kv, *scratch)
    for j in range(2):
        o_ref[0, :, j * LANES:(j + 1) * LANES] = _pack_heads(
            o[(2 * j) * tq:(2 * j + 1) * tq], o[(2 * j + 1) * tq:(2 * j + 2) * tq]).astype(o_ref.dtype)


def _head_parity(n_tiles):
    step = pl.program_id(0)
    return jnp.minimum(step, n_tiles - 1) % 2, jnp.maximum(step - 2, 0) % 2


def _store_head(o_ref, res, parity):
    @pl.when(parity == 0)
    def _():
        o_ref[0, :, 0:HEAD_DIM] = res[:, 0:HEAD_DIM].astype(o_ref.dtype)

    @pl.when(parity == 1)
    def _():
        o_ref[0, :, HEAD_DIM:LANES] = pltpu.roll(res, HEAD_DIM, 1)[:, HEAD_DIM:LANES].astype(o_ref.dtype)


def _diff_body(q_ref, *rest, n_kv, n_tiles, lam_init):
    kv, (lq1_ref, lk1_ref, lq2_ref, lk2_ref, gsub_ref, o_ref, *scratch) = _split_kv(rest, n_kv)
    tq = q_ref.shape[1]
    par, par_done = _head_parity(n_tiles)
    lane = _lane_iota((tq, LANES))
    t = q_ref[0]
    zero = jnp.zeros_like(t)
    qs = jnp.concatenate([jnp.where((lane // DIFF_QK_DIM) == 2 * par + j, t, zero) for j in range(2)],
                         axis=0)
    o = _attn_step(qs, kv, *scratch)
    lam = (jnp.exp(jnp.sum(lq1_ref[...] * lk1_ref[...], axis=-1, keepdims=True))
           - jnp.exp(jnp.sum(lq2_ref[...] * lk2_ref[...], axis=-1, keepdims=True)) + lam_init)
    d = o[0:tq] - lam * o[tq:2 * tq]
    ms = jnp.sum(jnp.where(lane < DIFF_V_DIM, d * d, 0.0), axis=-1, keepdims=True) * (1.0 / DIFF_V_DIM)
    _store_head(o_ref, (d * lax.rsqrt(ms + NORM_EPS) * gsub_ref[...]) * (1.0 - lam_init), par_done)


def _mla_body(q_ref, *rest, n_kv, n_tiles):
    kv, (o_ref, *scratch) = _split_kv(rest, n_kv)
    par, par_done = _head_parity(n_tiles)
    t = q_ref[0]
    lane = _lane_iota(t.shape)
    nope0 = MLA_NOPE_DIM * par
    rope0 = LANES + MLA_ROPE_DIM * par
    mine = ((lane >= nope0) & (lane < nope0 + MLA_NOPE_DIM)) | (
        (lane >= rope0) & (lane < rope0 + MLA_ROPE_DIM))
    qs = jnp.where(mine, t, jnp.zeros_like(t))
    _store_head(o_ref, _attn_step(qs, kv, *scratch), par_done)


def _attention(body, q, kvs, extra, *, n_groups, wq, wk, n_s, tq, per_head):
    b, q_rows, _ = q.shape
    q_tiles = q_rows // tq
    wo = LANES if per_head else wq
    heads = 2 if per_head else 1
    n_chunks = sum(k.shape[1] // KEY_CHUNK for k, _ in kvs)
    m_rows = n_s * tq
    n_tiles = b * n_groups * q_tiles * heads

    def split(t):
        t, par = t // heads, t % heads
        return t // (n_groups * q_tiles), (t // q_tiles) % n_groups, t % q_tiles, par

    cur = lambda j: split(jnp.minimum(j, n_tiles - 1))
    prev = lambda j: split(jnp.clip(j - 1, 0, n_tiles - 1))
    done = lambda j: split(jnp.maximum(j - 2, 0))
    kv_specs, kv_args = [], []
    for k, v in kvs:
        kv_specs += [pl.BlockSpec((1, k.shape[1], wk), lambda j: (cur(j)[0], 0, cur(j)[1])),
                     pl.BlockSpec((1, v.shape[1], LANES),
                                  lambda j: (prev(j)[0], 0, prev(j)[1] * heads + prev(j)[3]))]
        kv_args += [k, v]
    extra_specs = [pl.BlockSpec(e.shape, lambda j: (0, 0)) for e in extra]
    return pl.pallas_call(
        functools.partial(body, n_kv=len(kvs), n_tiles=n_tiles),
        grid=(n_tiles + 2,),
        in_specs=[pl.BlockSpec((1, tq, wq), lambda j: (cur(j)[0], cur(j)[2], cur(j)[1]))]
        + kv_specs + extra_specs,
        out_specs=pl.BlockSpec((1, tq, wo), lambda j: (done(j)[0], done(j)[2], done(j)[1])),
        out_shape=jax.ShapeDtypeStruct((b, q_rows, n_groups * wo), BF16),
        scratch_shapes=[pltpu.VMEM((n_chunks, m_rows, KEY_CHUNK), F32),
                        pltpu.VMEM((m_rows, LANES), F32),
                        pltpu.VMEM((m_rows, LANES), F32),
                        pltpu.VMEM((m_rows, LANES), F32)],
        compiler_params=_params("arbitrary"),
        name=body.__name__.strip("_"),
    )(q, *kv_args, *extra)


def _router_gates(logits):
    lane = _lane_iota(logits.shape).astype(F32)
    neg = jnp.float32(-jnp.inf)
    z = jnp.where(lane < N_EXPERTS, logits, neg)
    m1 = jnp.max(z, axis=-1, keepdims=True)
    i1 = jnp.min(jnp.where(z == m1, lane, float(LANES)), axis=-1, keepdims=True)
    z2 = jnp.where(lane == i1, neg, z)
    m2 = jnp.max(z2, axis=-1, keepdims=True)
    i2 = jnp.min(jnp.where(z2 == m2, lane, float(LANES)), axis=-1, keepdims=True)
    e2 = jnp.exp(m2 - m1)
    den = 1.0 + e2
    return jnp.where(lane == i1, 1.0 / den, 0.0) + jnp.where(lane == i2, e2 / den, 0.0)


def _mixer_residual(x_ref, oa_ref, ob_ref, om_ref, w_ref, mod, g_ref):
    wa, wb = oa_ref.shape[2], ob_ref.shape[2]
    y = jnp.dot(oa_ref[0], w_ref[0:wa, :], preferred_element_type=F32)
    y += jnp.dot(ob_ref[0], w_ref[wa:wa + wb, :], preferred_element_type=F32)
    y += jnp.dot(om_ref[0], w_ref[wa + wb:, :], preferred_element_type=F32)
    x = x_ref[0] + mod[2:3] * y
    n = x * _rms_rows(x, x.shape[-1]) * g_ref[...]
    return x, n * (1.0 + mod[4:5]) + mod[3:4]


def _mixer_specs(x, oa, ob, om, w, mod_row, tm):
    d = x.shape[2]
    row = lambda bi, i: (bi, i, 0)
    return [pl.BlockSpec((1, tm, d), row),
            pl.BlockSpec((1, tm, oa.shape[2]), row),
            pl.BlockSpec((1, tm, ob.shape[2]), row),
            pl.BlockSpec((1, tm, om.shape[2]), row),
            pl.BlockSpec(w.shape, lambda bi, i: (0, 0)),
            pl.BlockSpec((1, N_MOD, d), lambda bi, i: (mod_row(bi), 0, 0)),
            pl.BlockSpec((1, d), lambda bi, i: (0, 0))]


def _outproj_body(x_ref, oa_ref, ob_ref, om_ref, w_ref, mod_ref, g_ref, wr_ref,
                  xo_ref, h_ref, gates_ref, gates_t_ref, hprev_ref):
    @pl.when(pl.program_id(0) == 0)
    def _():
        hprev_ref[...] = jnp.zeros(hprev_ref.shape, F32)

    h_prev = hprev_ref[...]
    h_hi = h_prev.astype(BF16)
    h_lo = (h_prev - h_hi.astype(F32)).astype(BF16)
    logits = (jnp.dot(h_hi, wr_ref[0], preferred_element_type=F32)
              + jnp.dot(h_lo, wr_ref[0], preferred_element_type=F32)
              + jnp.dot(h_hi, wr_ref[1], preferred_element_type=F32))
    gates = _router_gates(logits)
    gates_ref[0] = gates
    gates_t_ref[0] = gates.T[:gates_t_ref.shape[1]]

    x, h = _mixer_residual(x_ref, oa_ref, ob_ref, om_ref, w_ref, mod_ref[0], g_ref)
    xo_ref[0] = x
    h_ref[0] = h.astype(BF16)
    hprev_ref[...] = h


def _out_project_routed(x, oa, ob, om, w, mod3, g, mod_row, w_router):
    b, s, d = x.shape
    tm = min(PROJ_ROWS, s)
    nt = s // tm
    n_tiles = b * nt
    cur = lambda j: (jnp.minimum(j, n_tiles - 1) // nt, jnp.minimum(j, n_tiles - 1) % nt)
    prev = lambda j: (jnp.maximum(j - 1, 0) // nt, jnp.maximum(j - 1, 0) % nt)
    flat = lambda spec: pl.BlockSpec(spec.block_shape, lambda j, f=spec.index_map: f(*cur(j)))
    row = lambda j: (cur(j)[0], cur(j)[1], 0)
    return pl.pallas_call(
        _outproj_body,
        grid=(n_tiles + 1,),
        in_specs=[flat(sp) for sp in _mixer_specs(x, oa, ob, om, w, mod_row, tm)]
        + [pl.BlockSpec(w_router.shape, lambda j: (0, 0, 0))],
        out_specs=[pl.BlockSpec((1, tm, d), row), pl.BlockSpec((1, tm, d), row),
                   pl.BlockSpec((1, tm, LANES), lambda j: (prev(j)[0], prev(j)[1], 0)),
                   pl.BlockSpec((1, GATE_ROWS, tm), lambda j: (prev(j)[0], 0, prev(j)[1]))],
        out_shape=[jax.ShapeDtypeStruct((b, s, d), F32), jax.ShapeDtypeStruct((b, s, d), BF16),
                   jax.ShapeDtypeStruct((b, s, LANES), F32), jax.ShapeDtypeStruct((b, GATE_ROWS, s), F32)],
        scratch_shapes=[pltpu.VMEM((tm, d), F32)],
        compiler_params=_params("arbitrary"),
        name="out_project_routed",
    )(x, oa, ob, om, w, mod3, g, w_router)


def _swiglu_chunks(h, wg_ref, wu_ref, wo_ref, lead):
    acc = None
    for j in range(wo_ref.shape[len(lead)] // FFN_CHUNK):
        cols = slice(j * FFN_CHUNK, (j + 1) * FFN_CHUNK)
        gate = jnp.dot(h, wg_ref[lead + (slice(None), cols)], preferred_element_type=F32)
        up = jnp.dot(h, wu_ref[lead + (slice(None), cols)], preferred_element_type=F32)
        a = (gate * jax.nn.sigmoid(gate) * up).astype(BF16)
        part = jnp.dot(a, wo_ref[lead + (cols, slice(None))], preferred_element_type=F32)
        acc = part if acc is None else acc + part
    return acc


def _residual_out(x, gate, y, gf_ref, final_norm):
    x = x + gate * y
    if final_norm:
        x = x * _rms_rows(x, x.shape[-1]) * gf_ref[...]
    return x


def _ffn_body(x_ref, oa_ref, ob_ref, om_ref, w_ref, mod_ref, g_ref, wg_ref, wu_ref, wo_ref, gf_ref,
              o_ref, *, final_norm):
    mod = mod_ref[0]
    x, h = _mixer_residual(x_ref, oa_ref, ob_ref, om_ref, w_ref, mod, g_ref)
    y = _swiglu_chunks(h.astype(BF16), wg_ref, wu_ref, wo_ref, ())
    o_ref[0] = _residual_out(x, mod[5:6], y, gf_ref, final_norm)


def _mix_dense_ffn(x, oa, ob, om, w, mod3, g, mod_row, wg, wu, wo, g_final, final_norm):
    b, s, d = x.shape
    tm = min(FFN_ROWS, s)
    resident = lambda a: pl.BlockSpec(a.shape, lambda bi, i: (0, 0), pipeline_mode=pl.Buffered(1))
    return pl.pallas_call(
        functools.partial(_ffn_body, final_norm=final_norm),
        grid=(b, s // tm),
        in_specs=_mixer_specs(x, oa, ob, om, w, mod_row, tm)
        + [resident(wg), resident(wu), resident(wo), pl.BlockSpec((1, d), lambda bi, i: (0, 0))],
        out_specs=pl.BlockSpec((1, tm, d), lambda bi, i: (bi, i, 0)),
        out_shape=jax.ShapeDtypeStruct((b, s, d), F32),
        compiler_params=_params("parallel", "arbitrary"),
        name="dense_ffn",
    )(x, oa, ob, om, w, mod3, g, wg, wu, wo, g_final)


def _moe_body(x_ref, h_ref, gates_ref, gates_t_ref, tri_ref, tri_t_ref, wg_ref, wu_ref, wo_ref,
              mod_ref, gf_ref, o_ref, tot_ref, rank_ref, rank_t_ref, *, final_norm):
    e = pl.program_id(2)
    tm = h_ref.shape[1]
    parts = [slice(k * MOE_PART, (k + 1) * MOE_PART) for k in range(tm // MOE_PART)]
    gates = gates_ref[0]
    lane = _lane_iota(gates.shape)

    @pl.when(e == 0)
    def _():
        tot_ref[...] = jnp.zeros(tot_ref.shape, F32)
        live = (gates > 0.0).astype(BF16)
        live_t = (gates_t_ref[0] > 0.0).astype(BF16)
        for part in parts:
            rank_ref[part, :] = jnp.dot(tri_ref[...], live[part], preferred_element_type=F32)
            rank_t_ref[:, part] = jnp.dot(live_t[:, part], tri_t_ref[...], preferred_element_type=F32)

    pick = lambda a: jnp.sum(jnp.where(lane == e, a, 0.0), axis=-1, keepdims=True)
    rank_row = rank_t_ref[pl.ds(e, 1), :]
    fits = jnp.max(rank_row) <= float(MOE_SLOTS)

    @pl.when(fits)
    def _():
        gate_e = pick(gates)
        rank_e = pick(rank_ref[...])
        live_row = gates_t_ref[0, pl.ds(e, 1), :] > 0.0
        slot_sub = lax.broadcasted_iota(jnp.int32, (MOE_SLOTS, MOE_PART), 0).astype(F32) + 1.0
        rows = []
        for part in parts:
            gather = jnp.where((rank_row[:, part] == slot_sub) & live_row[:, part], 1.0, 0.0)
            rows.append(jnp.dot(gather.astype(BF16), h_ref[0, part, :],
                                preferred_element_type=F32).astype(BF16))
        y = _swiglu_chunks(jnp.concatenate(rows, axis=0), wg_ref, wu_ref, wo_ref, (0,))
        y_hi = y.astype(BF16)
        y_lo = (y - y_hi.astype(F32)).astype(BF16)
        slot_lane = _lane_iota((MOE_PART, MOE_SLOTS)).astype(F32) + 1.0
        for k, part in enumerate(parts):
            slots = slice(k * MOE_SLOTS, (k + 1) * MOE_SLOTS)
            scatter = jnp.where((rank_e[part] == slot_lane) & (gate_e[part] > 0.0), 1.0, 0.0).astype(BF16)
            back = (jnp.dot(scatter, y_hi[slots], preferred_element_type=F32)
                    + jnp.dot(scatter, y_lo[slots], preferred_element_type=F32))
            tot_ref[part, :] += gate_e[part] * back

    @pl.when(jnp.logical_not(fits))
    def _():
        tot_ref[...] += pick(gates) * _swiglu_chunks(h_ref[0], wg_ref, wu_ref, wo_ref, (0,))

    @pl.when(e == pl.num_programs(2) - 1)
    def _():
        o_ref[0] = _residual_out(x_ref[0], mod_ref[0][5:6], tot_ref[...], gf_ref, final_norm)


def _moe_ffn(x, h, gates, gates_t, wg, wu, wo, mod3, mod_row, g_final, final_norm):
    b, s, d = x.shape
    tm = min(MOE_ROWS, s)
    tp = MOE_PART
    tri = jnp.asarray(np.tril(np.ones((tp, tp), np.float32)), BF16)
    row = lambda bi, i, e: (bi, i, 0)
    const = lambda bi, i, e: (0, 0)
    expert = lambda a: pl.BlockSpec((1,) + a.shape[1:], lambda bi, i, e: (e, 0, 0))
    return pl.pallas_call(
        functools.partial(_moe_body, final_norm=final_norm),
        grid=(b, s // tm, wg.shape[0]),
        in_specs=[pl.BlockSpec((1, tm, d), row, pipeline_mode=pl.Buffered(1)),
                  pl.BlockSpec((1, tm, d), row),
                  pl.BlockSpec((1, tm, LANES), row),
                  pl.BlockSpec((1, GATE_ROWS, tm), lambda bi, i, e: (bi, 0, i)),
                  pl.BlockSpec((tp, tp), const), pl.BlockSpec((tp, tp), const),
                  expert(wg), expert(wu), expert(wo),
                  pl.BlockSpec((1, N_MOD, d), lambda bi, i, e: (mod_row(bi), 0, 0)),
                  pl.BlockSpec((1, d), const)],
        out_specs=pl.BlockSpec((1, tm, d), row),
        out_shape=jax.ShapeDtypeStruct((b, s, d), F32),
        scratch_shapes=[pltpu.VMEM((tm, d), F32), pltpu.VMEM((tm, LANES), F32),
                        pltpu.VMEM((GATE_ROWS, tm), F32)],
        compiler_params=_params("parallel", "parallel", "arbitrary"),
        name="moe_ffn",
    )(x, h, gates, gates_t, tri, tri.T, wg, wu, wo, mod3, g_final)


def _deinterleave(n):
    return np.concatenate([np.arange(0, n, 2), np.arange(1, n, 2)])


def _in_proj_columns(pad):
    pads = lambda n: np.full((n,), pad)
    cols = []
    for hd in range(GQA_HEADS):
        cols.append(64 * hd + _deinterleave(64))
    for g in range(GQA_KV_HEADS):
        cols += [512 + 64 * g + _deinterleave(64), pads(64)]
    for g in range(GQA_KV_HEADS):
        cols += [640 + 64 * g + np.arange(64), pads(64)]
    for u in range(2 * DIFF_HEADS):
        cols.append(768 + 32 * u + _deinterleave(32))
    for u in range(2 * DIFF_HEADS):
        cols.append(1024 + 32 * u + _deinterleave(32))
    for hd in range(DIFF_HEADS):
        cols += [1280 + 64 * hd + np.arange(64), pads(64)]
    cols += [1536 + np.arange(MLA_Q_RANK), pads(64)]
    cols.append(1728 + np.arange(MLA_KV_RANK))
    cols += [1856 + _deinterleave(32), 1856 + _deinterleave(32), pads(64)]
    return np.concatenate(cols)


def _uq_columns(pad):
    per = MLA_NOPE_DIM + MLA_ROPE_DIM
    cols = []
    for pr in range(2):
        h0, h1 = 2 * pr, 2 * pr + 1
        cols += [per * h0 + np.arange(64), per * h1 + np.arange(64),
                 per * h0 + 64 + _deinterleave(32), per * h1 + 64 + _deinterleave(32),
                 np.full((64,), pad)]
    return np.concatenate(cols)


def _ukv_columns(pad):
    per = MLA_NOPE_DIM + MLA_V_DIM
    k = [per * hd + np.arange(64) for hd in range(MLA_HEADS)]
    v = []
    for hd in range(MLA_HEADS):
        v += [per * hd + 64 + np.arange(64), np.full((64,), pad)]
    return np.concatenate(k + v)


def _take_cols(w, cols):
    w_ext = jnp.concatenate([w, jnp.zeros((w.shape[0], 1), w.dtype)], axis=1)
    return jnp.take(w_ext, jnp.asarray(cols), axis=1)


def _rope_tables(s):
    t = np.arange(s)
    rows = (t // GRID_W).astype(np.float32)
    cols = (t % GRID_W).astype(np.float32)
    out = []
    for dim in (HEAD_DIM, DIFF_QK_DIM):
        quarter = dim // 4
        half = dim // 2
        inv_freq = (ROPE_THETA ** (-np.arange(quarter, dtype=np.float32) / quarter)).astype(np.float32)
        ang = np.concatenate([rows[:, None] * inv_freq, cols[:, None] * inv_freq], axis=-1)
        lane = np.arange(LANES)
        idx = (lane % dim) % half
        sign = np.where((lane % dim) < half, -1.0, 1.0).astype(np.float32)
        out += [jnp.asarray(np.cos(ang)[:, idx], F32), jnp.asarray(np.sin(ang)[:, idx] * sign, F32)]
    return out


def kernel(x, c, ctx, c_ctx, w_mod, b_mod, g_attn, g_ffn, w_in, w_out, gqa_gq, gqa_gk,
           diff_lq1, diff_lk1, diff_lq2, diff_lk2, diff_gsub, mla_gcq, mla_gckv, mla_wuq, mla_wukv,
           ffn_w_in, ffn_w_out, moe_router, moe_w_in, moe_w_out, g_final):
    b, s, d = x.shape
    n_ctx = ctx.shape[1]
    depth = w_mod.shape[0]
    tables = _rope_tables(s)
    no_rotation = [jnp.ones((n_ctx, LANES), F32), jnp.zeros((n_ctx, LANES), F32)] * 2
    in_cols = _in_proj_columns(w_in.shape[2])
    uq_cols = _uq_columns(mla_wuq.shape[2])
    ukv_cols = _ukv_columns(mla_wukv.shape[2])
    perm64 = _deinterleave(HEAD_DIM)
    bd = jnp.asarray(np.kron(np.eye(LANES // HEAD_DIM), np.full((HEAD_DIM, HEAD_DIM), 1.0 / HEAD_DIM)), F32)

    mod_rows = 16
    c_all = jnp.zeros((mod_rows, d), F32).at[:b].set(c).at[b].set(c_ctx)
    pad_lanes = lambda v: jnp.zeros((1, LANES), F32).at[0, :v.shape[0]].set(v)

    xc = ctx
    for l in range(depth):
        need_ctx = l < depth - 1
        lam_init = 0.8 - 0.6 * math.exp(-0.3 * l)
        mod3 = _modulation(c_all, w_mod[l], b_mod[l]).reshape(mod_rows, N_MOD, d)

        w_in_p = _take_cols(w_in[l], in_cols).astype(BF16)
        wuq_p = jnp.zeros((2 * LANES, 512), F32).at[:MLA_Q_RANK].set(
            _take_cols(mla_wuq[l], uq_cols)).astype(BF16)
        wukv_p = _take_cols(mla_wukv[l], ukv_cols).astype(BF16)
        gq2 = jnp.tile(gqa_gq[l][perm64], 2).reshape(1, LANES)
        gk2 = jnp.tile(gqa_gk[l][perm64], 2).reshape(1, LANES)
        gcq = jnp.zeros((1, 2 * LANES), F32).at[0, :MLA_Q_RANK].set(mla_gcq[l])
        gckv = mla_gckv[l].reshape(1, MLA_KV_RANK)

        proj_w = (g_attn[l].reshape(1, d), w_in_p)
        proj_aux = (gq2, gk2, bd, gcq, gckv, wuq_p, wukv_p)
        lat = _project(x, mod3, lambda bi: bi, *proj_w, tables, *proj_aux)
        ctxp = _project(xc, mod3, lambda bi: b, *proj_w, no_rotation, *proj_aux)

        diff_extra = [pad_lanes(diff_lq1[l]), pad_lanes(diff_lk1[l]), pad_lanes(diff_lq2[l]),
                      pad_lanes(diff_lk2[l]), pad_lanes(diff_gsub[l])]
        diff_body = functools.partial(_diff_body, lam_init=lam_init)
        diff_body.__name__ = "_diff_body"

        def attend(q, streams):
            kv = lambda ik, iv: [(p[ik], p[iv]) for p in streams]
            rows = q[0].shape[1]
            oa = _attention(_gqa_body, q[0], kv(1, 2), [], n_groups=2, wq=256, wk=LANES,
                            n_s=4, tq=min(ATTN_ROWS // 4, rows), per_head=False)
            ob = _attention(diff_body, q[3], kv(4, 5), diff_extra, n_groups=2, wq=LANES, wk=LANES,
                            n_s=2, tq=min(ATTN_ROWS // 2, rows), per_head=True)
            om = _attention(_mla_body, q[6], kv(7, 8), [], n_groups=2, wq=256, wk=256,
                            n_s=1, tq=min(ATTN_ROWS, rows), per_head=True)
            return oa, ob, om

        w_out_b = w_out[l].astype(BF16)
        g2 = g_ffn[l].reshape(1, d)
        dense = l % 2 == 0
        if dense:
            n_hidden = ffn_w_out.shape[1]
            wg = ffn_w_in[l // 2][:, :n_hidden].astype(BF16)
            wu = ffn_w_in[l // 2][:, n_hidden:].astype(BF16)
            wo = ffn_w_out[l // 2].astype(BF16)
            w_router = None
        else:
            n_hidden = moe_w_out.shape[2]
            wi = moe_w_in[l // 2]
            wg = wi[:, :, :n_hidden].astype(BF16)
            wu = wi[:, :, n_hidden:].astype(BF16)
            wo = moe_w_out[l // 2].astype(BF16)
            wr = jnp.zeros((d, LANES), F32).at[:, :N_EXPERTS].set(moe_router[l // 2])
            wr_hi = wr.astype(BF16)
            w_router = jnp.stack([wr_hi, (wr - wr_hi.astype(F32)).astype(BF16)])
        last = l == depth - 1

        def channel_mix(xs, attn_out, mod_row, final_norm):
            gf = g_final.reshape(1, d)
            if dense:
                return _mix_dense_ffn(xs, *attn_out, w_out_b, mod3, g2, mod_row, wg, wu, wo, gf, final_norm)
            res = _out_project_routed(xs, *attn_out, w_out_b, mod3, g2, mod_row, w_router)
            return _moe_ffn(*res, wg, wu, wo, mod3, mod_row, gf, final_norm)

        x_new = channel_mix(x, attend(lat, [lat, ctxp]), lambda bi: bi, last)
        if need_ctx:
            xc = channel_mix(xc, attend(ctxp, [ctxp]), lambda bi: b, False)
        x = x_new
    return x
```

```python
import functools
import math

import numpy as np
import jax
import jax.numpy as jnp
from jax import lax
from jax.experimental import pallas as pl
from jax.experimental.pallas import tpu as pltpu

LANES = 128
MXU_TILE = 256
VMEM_LIMIT = 60 * 1024 * 1024

NORM_EPS = 1e-6
ROPE_THETA = 10000.0
GRID_W = 64
N_MOD = 6
HEAD_DIM = 64
GQA_HEADS, GQA_KV_HEADS = 8, 2
DIFF_HEADS, DIFF_QK_DIM, DIFF_V_DIM = 4, 32, 64
MLA_HEADS, MLA_Q_RANK, MLA_KV_RANK = 4, 192, 128
MLA_NOPE_DIM, MLA_ROPE_DIM, MLA_V_DIM = 64, 32, 64
N_EXPERTS = 8
GATE_ROWS = 16

ATTN_ROWS = 2048
PROJ_ROWS = 512
KEY_CHUNK = MXU_TILE
FFN_CHUNK = MXU_TILE
FFN_ROWS = 512
MOE_ROWS = 1024
MOE_PART = 512
MOE_SLOTS = 256

F32 = jnp.float32
BF16 = jnp.bfloat16
HIGHEST = lax.Precision.HIGHEST


def _params(*sem):
    return pltpu.CompilerParams(dimension_semantics=sem, vmem_limit_bytes=VMEM_LIMIT)


def _lane_iota(shape):
    return lax.broadcasted_iota(jnp.int32, shape, len(shape) - 1)


def _mod_body(c_ref, w_ref, b_ref, o_ref):
    c = c_ref[...]
    sc = c * jax.nn.sigmoid(c)
    o_ref[...] = jnp.dot(sc, w_ref[...], precision=HIGHEST,
                         preferred_element_type=F32) + b_ref[...]


def _modulation(c_all, w, b):
    rows, d = c_all.shape
    n = w.shape[1]
    tn = n // 4
    return pl.pallas_call(
        _mod_body,
        grid=(n // tn,),
        in_specs=[pl.BlockSpec((rows, d), lambda j: (0, 0)),
                  pl.BlockSpec((d, tn), lambda j: (0, j)),
                  pl.BlockSpec((1, tn), lambda j: (0, j))],
        out_specs=pl.BlockSpec((rows, tn), lambda j: (0, j)),
        out_shape=jax.ShapeDtypeStruct((rows, n), F32),
        compiler_params=_params("arbitrary"),
        name="modulation",
    )(c_all, w, b.reshape(1, n))


def _rms_rows(x, width):
    return lax.rsqrt(jnp.sum(x * x, axis=-1, keepdims=True) * (1.0 / width) + NORM_EPS)


def _rope(t, cos, sin_signed, half):
    lane = _lane_iota(t.shape)
    partner = jnp.where((lane & half) == 0,
                        pltpu.roll(t, LANES - half, 1), pltpu.roll(t, half, 1))
    return t * cos + partner * sin_signed


def _with_ones(t):
    return jnp.where(_lane_iota(t.shape) == HEAD_DIM, 1.0, t)


def _proj_body(x_ref, mod_ref, g_ref, w_ref, ca_ref, sa_ref, cb_ref, sb_ref,
               gq_ref, gk_ref, bd_ref, gcq_ref, gckv_ref, wuq_ref, wukv_ref,
               qa_ref, ka_ref, va_ref, qb_ref, kb_ref, vb_ref, qm_ref, km_ref, vm_ref, p_ref,
               *, scale_a, scale_b, scale_m):
    @pl.when(pl.program_id(0) == 0)
    def _():
        p_ref[...] = jnp.zeros(p_ref.shape, F32)

    x = x_ref[0]
    d = x.shape[-1]
    mod = mod_ref[0]
    y = x * _rms_rows(x, d) * g_ref[...]
    h = (y * (1.0 + mod[1:2]) + mod[0:1]).astype(BF16)
    p_new = jnp.dot(h, w_ref[...], preferred_element_type=F32)
    p = p_ref

    ca, sa, cb, sb = ca_ref[...], sa_ref[...], cb_ref[...], sb_ref[...]
    bd = bd_ref[...]
    tile = lambda ref, j: (0, slice(None), slice(j * LANES, (j + 1) * LANES))
    cols = lambda a, base, j: a[:, base + j * LANES:base + (j + 1) * LANES]

    def head_norm(t, g):
        sq = t * t
        hi = sq.astype(BF16)
        lo = (sq - hi.astype(F32)).astype(BF16)
        ms = (jnp.dot(hi, bd, preferred_element_type=F32) + jnp.dot(lo, bd, preferred_element_type=F32))
        return t * lax.rsqrt(ms + NORM_EPS) * g

    for j in range(4):
        t = head_norm(cols(p, 0, j), gq_ref[...])
        qa_ref[tile(qa_ref, j)] = (_rope(t, ca, sa, 32) * scale_a).astype(BF16)
    for j in range(2):
        t = head_norm(cols(p, 512, j), gk_ref[...])
        ka_ref[tile(ka_ref, j)] = _rope(t, ca, sa, 32).astype(BF16)
        va_ref[tile(va_ref, j)] = _with_ones(cols(p, 768, j)).astype(BF16)

    for j in range(2):
        qb_ref[tile(qb_ref, j)] = (_rope(cols(p, 1024, j), cb, sb, 16) * scale_b).astype(BF16)
        kb_ref[tile(kb_ref, j)] = _rope(cols(p, 1280, j), cb, sb, 16).astype(BF16)
    for j in range(4):
        vb_ref[tile(vb_ref, j)] = _with_ones(cols(p, 1536, j)).astype(BF16)

    cq = p[:, 2048:2304]
    cqn = (cq * _rms_rows(cq, MLA_Q_RANK) * gcq_ref[...]).astype(BF16)
    uq = jnp.dot(cqn, wuq_ref[...], preferred_element_type=F32)
    ckv = p[:, 2304:2432]
    ckvn = (ckv * _rms_rows(ckv, MLA_KV_RANK) * gckv_ref[...]).astype(BF16)
    ukv = jnp.dot(ckvn, wukv_ref[...], preferred_element_type=F32)
    kr = _rope(p[:, 2432:2560], cb, sb, 16).astype(BF16)
    for pr in range(2):
        qm_ref[tile(qm_ref, 2 * pr)] = (cols(uq, 0, 2 * pr) * scale_m).astype(BF16)
        qr = _rope(cols(uq, 0, 2 * pr + 1), cb, sb, 16)
        qm_ref[tile(qm_ref, 2 * pr + 1)] = (qr * scale_m).astype(BF16)
        km_ref[tile(km_ref, 2 * pr)] = cols(ukv, 0, pr).astype(BF16)
        km_ref[tile(km_ref, 2 * pr + 1)] = kr
    for j in range(4):
        vm_ref[tile(vm_ref, j)] = _with_ones(cols(ukv, 256, j)).astype(BF16)
    p_ref[...] = p_new


def _project(x, mod3, mod_row, g, w, tables, gq2, gk2, bd, gcq, gckv, wuq, wukv):
    b, s, d = x.shape
    tm = min(PROJ_ROWS, s)
    nt = s // tm
    n_tiles = b * nt
    widths = (512, 256, 256, 256, 256, 512, 512, 512, 512)
    log2e = math.log2(math.e)
    body = functools.partial(_proj_body, scale_a=HEAD_DIM ** -0.5 * log2e,
                             scale_b=DIFF_QK_DIM ** -0.5 * log2e,
                             scale_m=(MLA_NOPE_DIM + MLA_ROPE_DIM) ** -0.5 * log2e)
    cur = lambda j: (jnp.minimum(j, n_tiles - 1) // nt, jnp.minimum(j, n_tiles - 1) % nt)
    prev = lambda j: (jnp.maximum(j - 1, 0) // nt, jnp.maximum(j - 1, 0) % nt)
    const = lambda shape: pl.BlockSpec(shape, lambda j: (0,) * len(shape))
    tab = pl.BlockSpec((tm, LANES), lambda j: (prev(j)[1], 0))
    return pl.pallas_call(
        body,
        grid=(n_tiles + 1,),
        in_specs=[pl.BlockSpec((1, tm, d), lambda j: (cur(j)[0], cur(j)[1], 0)),
                  pl.BlockSpec((1, N_MOD, d), lambda j: (mod_row(cur(j)[0]), 0, 0)),
                  const((1, d)), const(w.shape), tab, tab, tab, tab,
                  const((1, LANES)), const((1, LANES)), const((LANES, LANES)),
                  const(gcq.shape), const(gckv.shape), const(wuq.shape), const(wukv.shape)],
        out_specs=[pl.BlockSpec((1, tm, wd), lambda j: (prev(j)[0], prev(j)[1], 0)) for wd in widths],
        out_shape=[jax.ShapeDtypeStruct((b, s, wd), BF16) for wd in widths],
        scratch_shapes=[pltpu.VMEM((tm, w.shape[1]), F32)],
        compiler_params=_params("arbitrary"),
        name="project",
    )(x, mod3, g, w, *tables, gq2, gk2, bd, gcq, gckv, wuq, wukv)


def _attn_step(qs, kv_refs, s_ref, mp_ref, mb_ref, acc_ref, write, n_tiles):
    step = pl.program_id(0)

    @pl.when(step == 0)
    def _():
        s_ref[...] = jnp.zeros(s_ref.shape, F32)
        mb_ref[...] = jnp.zeros(mb_ref.shape, F32)
        acc_ref[...] = jnp.ones(acc_ref.shape, F32)

    def finish():
        raw = acc_ref[...]
        write(raw * (1.0 / raw[:, HEAD_DIM:HEAD_DIM + 1]))

    @pl.when(step > n_tiles)
    def _():
        finish()

    @pl.when(step <= n_tiles)
    def _():
        finish()
        _attn_scan(qs, kv_refs, s_ref, mp_ref, mb_ref, acc_ref)


def _attn_scan(qs, kv_refs, s_ref, mp_ref, mb_ref, acc_ref):
    acc = None
    chunks = [(k_ref, v_ref, slice(i * KEY_CHUNK, (i + 1) * KEY_CHUNK))
              for k_ref, v_ref in kv_refs for i in range(k_ref.shape[1] // KEY_CHUNK)]
    for c, (k_ref, v_ref, keys) in enumerate(chunks):
        s_new = lax.dot_general(qs, k_ref[0, keys, :], (((1,), (1,)), ((), ())),
                                preferred_element_type=F32)
        s_old = s_ref[c]
        m_old = mb_ref[...]
        p0 = jnp.exp2(s_old[:, :LANES] - m_old)
        p1 = jnp.exp2(s_old[:, LANES:] - m_old)
        part = jnp.dot(jnp.concatenate([p0, p1], axis=1).astype(BF16), v_ref[0, keys, :],
                       preferred_element_type=F32)
        acc = part if acc is None else acc + part
        s_ref[c] = s_new
        mc = jnp.maximum(s_new[:, :LANES], s_new[:, LANES:])
        mp_ref[...] = mc if c == 0 else jnp.maximum(mp_ref[...], mc)
    acc_ref[...] = acc
    mb_ref[...] = jnp.broadcast_to(jnp.max(mp_ref[...], axis=-1, keepdims=True), mb_ref.shape)


def _split_kv(refs, n_kv):
    return [(refs[2 * i], refs[2 * i + 1]) for i in range(n_kv)], refs[2 * n_kv:]


def _pack_heads(even, odd):
    lo = _lane_iota(even.shape) < HEAD_DIM
    return jnp.where(lo, even, pltpu.roll(odd, HEAD_DIM, 1))


def _gqa_body(q_ref, *rest, n_kv, n_tiles):
    tq = q_ref.shape[1]
    lo = _lane_iota((tq, LANES)) < HEAD_DIM
    heads = []
    for j in range(2):
        t = q_ref[0, :, j * LANES:(j + 1) * LANES].astype(F32)
        heads.append(jnp.where(lo, t, 0.0))
        heads.append(jnp.where(lo, pltpu.roll(t, HEAD_DIM, 1), 0.0))
    qs = jnp.concatenate(heads, axis=0).astype(BF16)
    kv, (o_ref, *scratch) = _split_kv(rest, n_kv)

    def write(o):
        for j in range(2):
            o_ref[0, :, j * LANES:(j + 1) * LANES] = _pack_heads(
                o[(2 * j) * tq:(2 * j + 1) * tq], o[(2 * j + 1) * tq:(2 * j + 2) * tq]).astype(o_ref.dtype)

    _attn_step(qs, kv, *scratch, write, n_tiles)


def _head_parity(n_tiles):
    step = pl.program_id(0)
    return jnp.minimum(step, n_tiles - 1) % 2, jnp.maximum(step - 2, 0) % 2


def _store_head(o_ref, res, parity):
    @pl.when(parity == 0)
    def _():
        o_ref[0, :, 0:HEAD_DIM] = res[:, 0:HEAD_DIM].astype(o_ref.dtype)

    @pl.when(parity == 1)
    def _():
        o_ref[0, :, HEAD_DIM:LANES] = pltpu.roll(res, HEAD_DIM, 1)[:, HEAD_DIM:LANES].astype(o_ref.dtype)


def _diff_body(q_ref, *rest, n_kv, n_tiles, lam_init):
    kv, (lq1_ref, lk1_ref, lq2_ref, lk2_ref, gsub_ref, o_ref, *scratch) = _split_kv(rest, n_kv)
    tq = q_ref.shape[1]
    par, par_done = _head_parity(n_tiles)
    lane = _lane_iota((tq, LANES))
    t = q_ref[0]
    zero = jnp.zeros_like(t)
    qs = jnp.concatenate([jnp.where((lane // DIFF_QK_DIM) == 2 * par + j, t, zero) for j in range(2)],
                         axis=0)

    def write(o):
        lam = (jnp.exp(jnp.sum(lq1_ref[...] * lk1_ref[...], axis=-1, keepdims=True))
               - jnp.exp(jnp.sum(lq2_ref[...] * lk2_ref[...], axis=-1, keepdims=True)) + lam_init)
        d = o[0:tq] - lam * o[tq:2 * tq]
        ms = jnp.sum(jnp.where(lane < DIFF_V_DIM, d * d, 0.0), axis=-1, keepdims=True) * (1.0 / DIFF_V_DIM)
        _store_head(o_ref, (d * lax.rsqrt(ms + NORM_EPS) * gsub_ref[...]) * (1.0 - lam_init), par_done)

    _attn_step(qs, kv, *scratch, write, n_tiles)


def _mla_body(q_ref, *rest, n_kv, n_tiles):
    kv, (o_ref, *scratch) = _split_kv(rest, n_kv)
    par, par_done = _head_parity(n_tiles)
    t = q_ref[0]
    lane = _lane_iota(t.shape)
    nope0 = MLA_NOPE_DIM * par
    rope0 = LANES + MLA_ROPE_DIM * par
    mine = ((lane >= nope0) & (lane < nope0 + MLA_NOPE_DIM)) | (
        (lane >= rope0) & (lane < rope0 + MLA_ROPE_DIM))
    qs = jnp.where(mine, t, jnp.zeros_like(t))
    _attn_step(qs, kv, *scratch, lambda o: _store_head(o_ref, o, par_done), n_tiles)


def _attention(body, q, kvs, extra, *, n_groups, wq, wk, n_s, tq, per_head):
    b, q_rows, _ = q.shape
    q_tiles = q_rows // tq
    wo = LANES if per_head else wq
    heads = 2 if per_head else 1
    n_chunks = sum(k.shape[1] // KEY_CHUNK for k, _ in kvs)
    m_rows = n_s * tq
    n_tiles = b * n_groups * q_tiles * heads

    def split(t):
        t, par = t // heads, t % heads
        return t // (n_groups * q_tiles), (t // q_tiles) % n_groups, t % q_tiles, par

    cur = lambda j: split(jnp.minimum(j, n_tiles - 1))
    prev = lambda j: split(jnp.clip(j - 1, 0, n_tiles - 1))
    done = lambda j: split(jnp.maximum(j - 2, 0))
    kv_specs, kv_args = [], []
    for k, v in kvs:
        kv_specs += [pl.BlockSpec((1, k.shape[1], wk), lambda j: (cur(j)[0], 0, cur(j)[1])),
                     pl.BlockSpec((1, v.shape[1], LANES),
                                  lambda j: (prev(j)[0], 0, prev(j)[1] * heads + prev(j)[3]))]
        kv_args += [k, v]
    extra_specs = [pl.BlockSpec(e.shape, lambda j: (0, 0)) for e in extra]
    return pl.pallas_call(
        functools.partial(body, n_kv=len(kvs), n_tiles=n_tiles),
        grid=(n_tiles + 2,),
        in_specs=[pl.BlockSpec((1, tq, wq), lambda j: (cur(j)[0], cur(j)[2], cur(j)[1]))]
        + kv_specs + extra_specs,
        out_specs=pl.BlockSpec((1, tq, wo), lambda j: (done(j)[0], done(j)[2], done(j)[1])),
        out_shape=jax.ShapeDtypeStruct((b, q_rows, n_groups * wo), BF16),
        scratch_shapes=[pltpu.VMEM((n_chunks, m_rows, KEY_CHUNK), F32),
                        pltpu.VMEM((m_rows, LANES), F32),
                        pltpu.VMEM((m_rows, LANES), F32),
                        pltpu.VMEM((m_rows, LANES), F32)],
        compiler_params=_params("arbitrary"),
        name=body.__name__.strip("_"),
    )(q, *kv_args, *extra)


def _router_gates(logits):
    lane = _lane_iota(logits.shape).astype(F32)
    neg = jnp.float32(-jnp.inf)
    z = jnp.where(lane < N_EXPERTS, logits, neg)
    m1 = jnp.max(z, axis=-1, keepdims=True)
    i1 = jnp.min(jnp.where(z == m1, lane, float(LANES)), axis=-1, keepdims=True)
    z2 = jnp.where(lane == i1, neg, z)
    m2 = jnp.max(z2, axis=-1, keepdims=True)
    i2 = jnp.min(jnp.where(z2 == m2, lane, float(LANES)), axis=-1, keepdims=True)
    e2 = jnp.exp(m2 - m1)
    den = 1.0 + e2
    return jnp.where(lane == i1, 1.0 / den, 0.0) + jnp.where(lane == i2, e2 / den, 0.0)


def _mixer_residual(x_ref, oa_ref, ob_ref, om_ref, w_ref, mod, g_ref):
    wa, wb = oa_ref.shape[2], ob_ref.shape[2]
    y = jnp.dot(oa_ref[0], w_ref[0:wa, :], preferred_element_type=F32)
    y += jnp.dot(ob_ref[0], w_ref[wa:wa + wb, :], preferred_element_type=F32)
    y += jnp.dot(om_ref[0], w_ref[wa + wb:, :], preferred_element_type=F32)
    x = x_ref[0] + mod[2:3] * y
    n = x * _rms_rows(x, x.shape[-1]) * g_ref[...]
    return x, n * (1.0 + mod[4:5]) + mod[3:4]


def _mixer_specs(x, oa, ob, om, w, mod_row, tm):
    d = x.shape[2]
    row = lambda bi, i: (bi, i, 0)
    return [pl.BlockSpec((1, tm, d), row),
            pl.BlockSpec((1, tm, oa.shape[2]), row),
            pl.BlockSpec((1, tm, ob.shape[2]), row),
            pl.BlockSpec((1, tm, om.shape[2]), row),
            pl.BlockSpec(w.shape, lambda bi, i: (0, 0)),
            pl.BlockSpec((1, N_MOD, d), lambda bi, i: (mod_row(bi), 0, 0)),
            pl.BlockSpec((1, d), lambda bi, i: (0, 0))]


def _outproj_body(x_ref, oa_ref, ob_ref, om_ref, w_ref, mod_ref, g_ref, wr_ref,
                  xo_ref, h_ref, gates_ref, gates_t_ref, hprev_ref):
    @pl.when(pl.program_id(0) == 0)
    def _():
        hprev_ref[...] = jnp.zeros(hprev_ref.shape, F32)

    h_prev = hprev_ref[...]
    h_hi = h_prev.astype(BF16)
    h_lo = (h_prev - h_hi.astype(F32)).astype(BF16)
    logits = (jnp.dot(h_hi, wr_ref[0], preferred_element_type=F32)
              + jnp.dot(h_lo, wr_ref[0], preferred_element_type=F32)
              + jnp.dot(h_hi, wr_ref[1], preferred_element_type=F32))
    gates = _router_gates(logits)
    gates_ref[0] = gates
    gates_t_ref[0] = gates.T[:gates_t_ref.shape[1]]

    x, h = _mixer_residual(x_ref, oa_ref, ob_ref, om_ref, w_ref, mod_ref[0], g_ref)
    xo_ref[0] = x
    h_ref[0] = h.astype(BF16)
    hprev_ref[...] = h


def _out_project_routed(x, oa, ob, om, w, mod3, g, mod_row, w_router):
    b, s, d = x.shape
    tm = min(PROJ_ROWS, s)
    nt = s // tm
    n_tiles = b * nt
    cur = lambda j: (jnp.minimum(j, n_tiles - 1) // nt, jnp.minimum(j, n_tiles - 1) % nt)
    prev = lambda j: (jnp.maximum(j - 1, 0) // nt, jnp.maximum(j - 1, 0) % nt)
    flat = lambda spec: pl.BlockSpec(spec.block_shape, lambda j, f=spec.index_map: f(*cur(j)))
    row = lambda j: (cur(j)[0], cur(j)[1], 0)
    return pl.pallas_call(
        _outproj_body,
        grid=(n_tiles + 1,),
        in_specs=[flat(sp) for sp in _mixer_specs(x, oa, ob, om, w, mod_row, tm)]
        + [pl.BlockSpec(w_router.shape, lambda j: (0, 0, 0))],
        out_specs=[pl.BlockSpec((1, tm, d), row), pl.BlockSpec((1, tm, d), row),
                   pl.BlockSpec((1, tm, LANES), lambda j: (prev(j)[0], prev(j)[1], 0)),
                   pl.BlockSpec((1, GATE_ROWS, tm), lambda j: (prev(j)[0], 0, prev(j)[1]))],
        out_shape=[jax.ShapeDtypeStruct((b, s, d), F32), jax.ShapeDtypeStruct((b, s, d), BF16),
                   jax.ShapeDtypeStruct((b, s, LANES), F32), jax.ShapeDtypeStruct((b, GATE_ROWS, s), F32)],
        scratch_shapes=[pltpu.VMEM((tm, d), F32)],
        compiler_params=_params("arbitrary"),
        name="out_project_routed",
    )(x, oa, ob, om, w, mod3, g, w_router)


def _swiglu_chunks(h, wg_ref, wu_ref, wo_ref, lead):
    acc = None
    for j in range(wo_ref.shape[len(lead)] // FFN_CHUNK):
        cols = slice(j * FFN_CHUNK, (j + 1) * FFN_CHUNK)
        gate = jnp.dot(h, wg_ref[lead + (slice(None), cols)], preferred_element_type=F32)
        up = jnp.dot(h, wu_ref[lead + (slice(None), cols)], preferred_element_type=F32)
        a = (gate * jax.nn.sigmoid(gate) * up).astype(BF16)
        part = jnp.dot(a, wo_ref[lead + (cols, slice(None))], preferred_element_type=F32)
        acc = part if acc is None else acc + part
    return acc


def _residual_out(x, gate, y, gf_ref, final_norm):
    x = x + gate * y
    if final_norm:
        x = x * _rms_rows(x, x.shape[-1]) * gf_ref[...]
    return x


def _ffn_body(x_ref, oa_ref, ob_ref, om_ref, w_ref, mod_ref, g_ref, wg_ref, wu_ref, wo_ref, gf_ref,
              o_ref, *, final_norm):
    mod = mod_ref[0]
    x, h = _mixer_residual(x_ref, oa_ref, ob_ref, om_ref, w_ref, mod, g_ref)
    y = _swiglu_chunks(h.astype(BF16), wg_ref, wu_ref, wo_ref, ())
    o_ref[0] = _residual_out(x, mod[5:6], y, gf_ref, final_norm)


def _mix_dense_ffn(x, oa, ob, om, w, mod3, g, mod_row, wg, wu, wo, g_final, final_norm):
    b, s, d = x.shape
    tm = min(FFN_ROWS, s)
    resident = lambda a: pl.BlockSpec(a.shape, lambda bi, i: (0, 0), pipeline_mode=pl.Buffered(1))
    return pl.pallas_call(
        functools.partial(_ffn_body, final_norm=final_norm),
        grid=(b, s // tm),
        in_specs=_mixer_specs(x, oa, ob, om, w, mod_row, tm)
        + [resident(wg), resident(wu), resident(wo), pl.BlockSpec((1, d), lambda bi, i: (0, 0))],
        out_specs=pl.BlockSpec((1, tm, d), lambda bi, i: (bi, i, 0)),
        out_shape=jax.ShapeDtypeStruct((b, s, d), F32),
        compiler_params=_params("parallel", "arbitrary"),
        name="dense_ffn",
    )(x, oa, ob, om, w, mod3, g, wg, wu, wo, g_final)


def _moe_body(x_ref, h_ref, gates_ref, gates_t_ref, tri_ref, tri_t_ref, wg_ref, wu_ref, wo_ref,
              mod_ref, gf_ref, o_ref, tot_ref, rank_ref, rank_t_ref, *, final_norm):
    e = pl.program_id(2)
    tm = h_ref.shape[1]
    parts = [slice(k * MOE_PART, (k + 1) * MOE_PART) for k in range(tm // MOE_PART)]
    gates = gates_ref[0]
    lane = _lane_iota(gates.shape)

    @pl.when(e == 0)
    def _():
        tot_ref[...] = jnp.zeros(tot_ref.shape, F32)
        live = (gates > 0.0).astype(BF16)
        live_t = (gates_t_ref[0] > 0.0).astype(BF16)
        for part in parts:
            rank_ref[part, :] = jnp.dot(tri_ref[...], live[part], preferred_element_type=F32)
            rank_t_ref[:, part] = jnp.dot(live_t[:, part], tri_t_ref[...], preferred_element_type=F32)

    pick = lambda a: jnp.sum(jnp.where(lane == e, a, 0.0), axis=-1, keepdims=True)
    rank_row = rank_t_ref[pl.ds(e, 1), :]
    fits = jnp.max(rank_row) <= float(MOE_SLOTS)

    @pl.when(fits)
    def _():
        gate_e = pick(gates)
        rank_e = pick(rank_ref[...])
        live_row = gates_t_ref[0, pl.ds(e, 1), :] > 0.0
        slot_sub = lax.broadcasted_iota(jnp.int32, (MOE_SLOTS, MOE_PART), 0).astype(F32) + 1.0
        rows = []
        for part in parts:
            gather = jnp.where((rank_row[:, part] == slot_sub) & live_row[:, part], 1.0, 0.0)
            rows.append(jnp.dot(gather.astype(BF16), h_ref[0, part, :],
                                preferred_element_type=F32).astype(BF16))
        y = _swiglu_chunks(jnp.concatenate(rows, axis=0), wg_ref, wu_ref, wo_ref, (0,))
        y_hi = y.astype(BF16)
        y_lo = (y - y_hi.astype(F32)).astype(BF16)
        slot_lane = _lane_iota((MOE_PART, MOE_SLOTS)).astype(F32) + 1.0
        for k, part in enumerate(parts):
            slots = slice(k * MOE_SLOTS, (k + 1) * MOE_SLOTS)
            scatter = jnp.where((rank_e[part] == slot_lane) & (gate_e[part] > 0.0), 1.0, 0.0).astype(BF16)
            back = (jnp.dot(scatter, y_hi[slots], preferred_element_type=F32)
                    + jnp.dot(scatter, y_lo[slots], preferred_element_type=F32))
            tot_ref[part, :] += gate_e[part] * back

    @pl.when(jnp.logical_not(fits))
    def _():
        tot_ref[...] += pick(gates) * _swiglu_chunks(h_ref[0], wg_ref, wu_ref, wo_ref, (0,))

    @pl.when(e == pl.num_programs(2) - 1)
    def _():
        o_ref[0] = _residual_out(x_ref[0], mod_ref[0][5:6], tot_ref[...], gf_ref, final_norm)


def _moe_ffn(x, h, gates, gates_t, wg, wu, wo, mod3, mod_row, g_final, final_norm):
    b, s, d = x.shape
    tm = min(MOE_ROWS, s)
    tp = MOE_PART
    tri = jnp.asarray(np.tril(np.ones((tp, tp), np.float32)), BF16)
    row = lambda bi, i, e: (bi, i, 0)
    const = lambda bi, i, e: (0, 0)
    expert = lambda a: pl.BlockSpec((1,) + a.shape[1:], lambda bi, i, e: (e, 0, 0))
    return pl.pallas_call(
        functools.partial(_moe_body, final_norm=final_norm),
        grid=(b, s // tm, wg.shape[0]),
        in_specs=[pl.BlockSpec((1, tm, d), row),
                  pl.BlockSpec((1, tm, d), row),
                  pl.BlockSpec((1, tm, LANES), row),
                  pl.BlockSpec((1, GATE_ROWS, tm), lambda bi, i, e: (bi, 0, i)),
                  pl.BlockSpec((tp, tp), const), pl.BlockSpec((tp, tp), const),
                  expert(wg), expert(wu), expert(wo),
                  pl.BlockSpec((1, N_MOD, d), lambda bi, i, e: (mod_row(bi), 0, 0)),
                  pl.BlockSpec((1, d), const)],
        out_specs=pl.BlockSpec((1, tm, d), row),
        out_shape=jax.ShapeDtypeStruct((b, s, d), F32),
        scratch_shapes=[pltpu.VMEM((tm, d), F32), pltpu.VMEM((tm, LANES), F32),
                        pltpu.VMEM((GATE_ROWS, tm), F32)],
        compiler_params=_params("parallel", "parallel", "arbitrary"),
        name="moe_ffn",
    )(x, h, gates, gates_t, tri, tri.T, wg, wu, wo, mod3, g_final)


def _deinterleave(n):
    return np.concatenate([np.arange(0, n, 2), np.arange(1, n, 2)])


def _in_proj_columns(pad):
    pads = lambda n: np.full((n,), pad)
    cols = []
    for hd in range(GQA_HEADS):
        cols.append(64 * hd + _deinterleave(64))
    for g in range(GQA_KV_HEADS):
        cols += [512 + 64 * g + _deinterleave(64), pads(64)]
    for g in range(GQA_KV_HEADS):
        cols += [640 + 64 * g + np.arange(64), pads(64)]
    for u in range(2 * DIFF_HEADS):
        cols.append(768 + 32 * u + _deinterleave(32))
    for u in range(2 * DIFF_HEADS):
        cols.append(1024 + 32 * u + _deinterleave(32))
    for hd in range(DIFF_HEADS):
        cols += [1280 + 64 * hd + np.arange(64), pads(64)]
    cols += [1536 + np.arange(MLA_Q_RANK), pads(64)]
    cols.append(1728 + np.arange(MLA_KV_RANK))
    cols += [1856 + _deinterleave(32), 1856 + _deinterleave(32), pads(64)]
    return np.concatenate(cols)


def _uq_columns(pad):
    per = MLA_NOPE_DIM + MLA_ROPE_DIM
    cols = []
    for pr in range(2):
        h0, h1 = 2 * pr, 2 * pr + 1
        cols += [per * h0 + np.arange(64), per * h1 + np.arange(64),
                 per * h0 + 64 + _deinterleave(32), per * h1 + 64 + _deinterleave(32),
                 np.full((64,), pad)]
    return np.concatenate(cols)


def _ukv_columns(pad):
    per = MLA_NOPE_DIM + MLA_V_DIM
    k = [per * hd + np.arange(64) for hd in range(MLA_HEADS)]
    v = []
    for hd in range(MLA_HEADS):
        v += [per * hd + 64 + np.arange(64), np.full((64,), pad)]
    return np.concatenate(k + v)


def _take_cols(w, cols):
    w_ext = jnp.concatenate([w, jnp.zeros((w.shape[0], 1), w.dtype)], axis=1)
    return jnp.take(w_ext, jnp.asarray(cols), axis=1)


def _rope_tables(s):
    t = np.arange(s)
    rows = (t // GRID_W).astype(np.float32)
    cols = (t % GRID_W).astype(np.float32)
    out = []
    for dim in (HEAD_DIM, DIFF_QK_DIM):
        quarter = dim // 4
        half = dim // 2
        inv_freq = (ROPE_THETA ** (-np.arange(quarter, dtype=np.float32) / quarter)).astype(np.float32)
        ang = np.concatenate([rows[:, None] * inv_freq, cols[:, None] * inv_freq], axis=-1)
        lane = np.arange(LANES)
        idx = (lane % dim) % half
        sign = np.where((lane % dim) < half, -1.0, 1.0).astype(np.float32)
        out += [jnp.asarray(np.cos(ang)[:, idx], F32), jnp.asarray(np.sin(ang)[:, idx] * sign, F32)]
    return out


def kernel(x, c, ctx, c_ctx, w_mod, b_mod, g_attn, g_ffn, w_in, w_out, gqa_gq, gqa_gk,
           diff_lq1, diff_lk1, diff_lq2, diff_lk2, diff_gsub, mla_gcq, mla_gckv, mla_wuq, mla_wukv,
           ffn_w_in, ffn_w_out, moe_router, moe_w_in, moe_w_out, g_final):
    b, s, d = x.shape
    n_ctx = ctx.shape[1]
    depth = w_mod.shape[0]
    tables = _rope_tables(s)
    no_rotation = [jnp.ones((n_ctx, LANES), F32), jnp.zeros((n_ctx, LANES), F32)] * 2
    in_cols = _in_proj_columns(w_in.shape[2])
    uq_cols = _uq_columns(mla_wuq.shape[2])
    ukv_cols = _ukv_columns(mla_wukv.shape[2])
    perm64 = _deinterleave(HEAD_DIM)
    bd = jnp.asarray(np.kron(np.eye(LANES // HEAD_DIM), np.full((HEAD_DIM, HEAD_DIM), 1.0 / HEAD_DIM)), F32)

    mod_rows = 16
    c_all = jnp.zeros((mod_rows, d), F32).at[:b].set(c).at[b].set(c_ctx)
    pad_lanes = lambda v: jnp.zeros((1, LANES), F32).at[0, :v.shape[0]].set(v)

    xc = ctx
    for l in range(depth):
        need_ctx = l < depth - 1
        lam_init = 0.8 - 0.6 * math.exp(-0.3 * l)
        mod3 = _modulation(c_all, w_mod[l], b_mod[l]).reshape(mod_rows, N_MOD, d)

        w_in_p = _take_cols(w_in[l], in_cols).astype(BF16)
        wuq_p = jnp.zeros((2 * LANES, 512), F32).at[:MLA_Q_RANK].set(
            _take_cols(mla_wuq[l], uq_cols)).astype(BF16)
        wukv_p = _take_cols(mla_wukv[l], ukv_cols).astype(BF16)
        gq2 = jnp.tile(gqa_gq[l][perm64], 2).reshape(1, LANES)
        gk2 = jnp.tile(gqa_gk[l][perm64], 2).reshape(1, LANES)
        gcq = jnp.zeros((1, 2 * LANES), F32).at[0, :MLA_Q_RANK].set(mla_gcq[l])
        gckv = mla_gckv[l].reshape(1, MLA_KV_RANK)

        proj_w = (g_attn[l].reshape(1, d), w_in_p)
        proj_aux = (gq2, gk2, bd, gcq, gckv, wuq_p, wukv_p)
        lat = _project(x, mod3, lambda bi: bi, *proj_w, tables, *proj_aux)
        ctxp = _project(xc, mod3, lambda bi: b, *proj_w, no_rotation, *proj_aux)

        diff_extra = [pad_lanes(diff_lq1[l]), pad_lanes(diff_lk1[l]), pad_lanes(diff_lq2[l]),
                      pad_lanes(diff_lk2[l]), pad_lanes(diff_gsub[l])]
        diff_body = functools.partial(_diff_body, lam_init=lam_init)
        diff_body.__name__ = "_diff_body"

        def attend(q, streams):
            kv = lambda ik, iv: [(p[ik], p[iv]) for p in streams]
            rows = q[0].shape[1]
            oa = _attention(_gqa_body, q[0], kv(1, 2), [], n_groups=2, wq=256, wk=LANES,
                            n_s=4, tq=min(ATTN_ROWS // 4, rows), per_head=False)
            ob = _attention(diff_body, q[3], kv(4, 5), diff_extra, n_groups=2, wq=LANES, wk=LANES,
                            n_s=2, tq=min(ATTN_ROWS // 2, rows), per_head=True)
            om = _attention(_mla_body, q[6], kv(7, 8), [], n_groups=2, wq=256, wk=256,
                            n_s=1, tq=min(ATTN_ROWS, rows), per_head=True)
            return oa, ob, om

        w_out_b = w_out[l].astype(BF16)
        g2 = g_ffn[l].reshape(1, d)
        dense = l % 2 == 0
        if dense:
            n_hidden = ffn_w_out.shape[1]
            wg = ffn_w_in[l // 2][:, :n_hidden].astype(BF16)
            wu = ffn_w_in[l // 2][:, n_hidden:].astype(BF16)
            wo = ffn_w_out[l // 2].astype(BF16)
            w_router = None
        else:
            n_hidden = moe_w_out.shape[2]
            wi = moe_w_in[l // 2]
            wg = wi[:, :, :n_hidden].astype(BF16)
            wu = wi[:, :, n_hidden:].astype(BF16)
            wo = moe_w_out[l // 2].astype(BF16)
            wr = jnp.zeros((d, LANES), F32).at[:, :N_EXPERTS].set(moe_router[l // 2])
            wr_hi = wr.astype(BF16)
            w_router = jnp.stack([wr_hi, (wr - wr_hi.astype(F32)).astype(BF16)])
        last = l == depth - 1

        def channel_mix(xs, attn_out, mod_row, final_norm):
            gf = g_final.reshape(1, d)
            if dense:
                return _mix_dense_ffn(xs, *attn_out, w_out_b, mod3, g2, mod_row, wg, wu, wo, gf, final_norm)
            res = _out_project_routed(xs, *attn_out, w_out_b, mod3, g2, mod_row, w_router)
            return _moe_ffn(*res, wg, wu, wo, mod3, mod_row, gf, final_norm)

        x_new = channel_mix(x, attend(lat, [lat, ctxp]), lambda bi: bi, last)
        if need_ctx:
            xc = channel_mix(xc, attend(ctxp, [ctxp]), lambda bi: b, False)
        x = x_new
    return x
```

```python
import functools
import math

import numpy as np
import jax
import jax.numpy as jnp
from jax import lax
from jax.experimental import pallas as pl
from jax.experimental.pallas import tpu as pltpu

LANES = 128
MXU_TILE = 256
VMEM_LIMIT = 60 * 1024 * 1024

NORM_EPS = 1e-6
ROPE_THETA = 10000.0
GRID_W = 64
N_MOD = 6
HEAD_DIM = 64
GQA_HEADS, GQA_KV_HEADS = 8, 2
DIFF_HEADS, DIFF_QK_DIM, DIFF_V_DIM = 4, 32, 64
MLA_HEADS, MLA_Q_RANK, MLA_KV_RANK = 4, 192, 128
MLA_NOPE_DIM, MLA_ROPE_DIM, MLA_V_DIM = 64, 32, 64
N_EXPERTS = 8
GATE_ROWS = 16

ATTN_ROWS = 2048
PROJ_ROWS = 512
KEY_CHUNK = MXU_TILE
FFN_CHUNK = MXU_TILE
FFN_ROWS = 512
MOE_ROWS = 1024
MOE_PART = 512
MOE_SLOTS = 256

F32 = jnp.float32
BF16 = jnp.bfloat16
HIGHEST = lax.Precision.HIGHEST


def _params(*sem):
    return pltpu.CompilerParams(dimension_semantics=sem, vmem_limit_bytes=VMEM_LIMIT)


def _lane_iota(shape):
    return lax.broadcasted_iota(jnp.int32, shape, len(shape) - 1)


def _mod_body(c_ref, w_ref, b_ref, o_ref):
    c = c_ref[...]
    sc = c * jax.nn.sigmoid(c)
    o_ref[0] = jnp.dot(sc, w_ref[0], precision=HIGHEST,
                       preferred_element_type=F32) + b_ref[0]


def _modulation(c_all, w, b):
    rows, d = c_all.shape
    depth, _, n = w.shape
    tn = n // 4
    return pl.pallas_call(
        _mod_body,
        grid=(depth, n // tn),
        in_specs=[pl.BlockSpec((rows, d), lambda l, j: (0, 0)),
                  pl.BlockSpec((1, d, tn), lambda l, j: (l, 0, j)),
                  pl.BlockSpec((1, 1, tn), lambda l, j: (l, 0, j))],
        out_specs=pl.BlockSpec((1, rows, tn), lambda l, j: (l, 0, j)),
        out_shape=jax.ShapeDtypeStruct((depth, rows, n), F32),
        compiler_params=_params("arbitrary", "arbitrary"),
        name="modulation",
    )(c_all, w, b.reshape(depth, 1, n))


def _rms_rows(x, width):
    return lax.rsqrt(jnp.sum(x * x, axis=-1, keepdims=True) * (1.0 / width) + NORM_EPS)


def _rope(t, cos, sin_signed, half):
    lane = _lane_iota(t.shape)
    partner = jnp.where((lane & half) == 0,
                        pltpu.roll(t, LANES - half, 1), pltpu.roll(t, half, 1))
    return t * cos + partner * sin_signed


def _with_ones(t):
    return jnp.where(_lane_iota(t.shape) == HEAD_DIM, 1.0, t)


def _proj_body(x_ref, mod_ref, g_ref, w_ref, ca_ref, sa_ref, cb_ref, sb_ref,
               gq_ref, gk_ref, bd_ref, gcq_ref, gckv_ref, wuq_ref, wukv_ref,
               qa_ref, ka_ref, va_ref, qb_ref, kb_ref, vb_ref, qm_ref, km_ref, vm_ref, p_ref,
               *, scale_a, scale_b, scale_m):
    @pl.when(pl.program_id(0) == 0)
    def _():
        p_ref[...] = jnp.zeros(p_ref.shape, F32)

    x = x_ref[0]
    d = x.shape[-1]
    mod = mod_ref[0]
    y = x * _rms_rows(x, d) * g_ref[...]
    h = (y * (1.0 + mod[1:2]) + mod[0:1]).astype(BF16)
    p_new = jnp.dot(h, w_ref[...], preferred_element_type=F32)
    p = p_ref

    ca, sa, cb, sb = ca_ref[...], sa_ref[...], cb_ref[...], sb_ref[...]
    bd = bd_ref[...]
    tile = lambda ref, j: (0, slice(None), slice(j * LANES, (j + 1) * LANES))
    cols = lambda a, base, j: a[:, base + j * LANES:base + (j + 1) * LANES]

    def head_norm(t, g):
        sq = t * t
        hi = sq.astype(BF16)
        lo = (sq - hi.astype(F32)).astype(BF16)
        ms = (jnp.dot(hi, bd, preferred_element_type=F32) + jnp.dot(lo, bd, preferred_element_type=F32))
        return t * lax.rsqrt(ms + NORM_EPS) * g

    for j in range(4):
        t = head_norm(cols(p, 0, j), gq_ref[...])
        qa_ref[tile(qa_ref, j)] = (_rope(t, ca, sa, 32) * scale_a).astype(BF16)
    for j in range(2):
        t = head_norm(cols(p, 512, j), gk_ref[...])
        ka_ref[tile(ka_ref, j)] = _rope(t, ca, sa, 32).astype(BF16)
        va_ref[tile(va_ref, j)] = _with_ones(cols(p, 768, j)).astype(BF16)

    for j in range(2):
        qb_ref[tile(qb_ref, j)] = (_rope(cols(p, 1024, j), cb, sb, 16) * scale_b).astype(BF16)
        kb_ref[tile(kb_ref, j)] = _rope(cols(p, 1280, j), cb, sb, 16).astype(BF16)
    for j in range(4):
        vb_ref[tile(vb_ref, j)] = _with_ones(cols(p, 1536, j)).astype(BF16)

    cq = p[:, 2048:2304]
    cqn = (cq * _rms_rows(cq, MLA_Q_RANK) * gcq_ref[...]).astype(BF16)
    uq = jnp.dot(cqn, wuq_ref[...], preferred_element_type=F32)
    ckv = p[:, 2304:2432]
    ckvn = (ckv * _rms_rows(ckv, MLA_KV_RANK) * gckv_ref[...]).astype(BF16)
    ukv = jnp.dot(ckvn, wukv_ref[...], preferred_element_type=F32)
    kr = _rope(p[:, 2432:2560], cb, sb, 16).astype(BF16)
    for pr in range(2):
        qm_ref[tile(qm_ref, 2 * pr)] = (cols(uq, 0, 2 * pr) * scale_m).astype(BF16)
        qr = _rope(cols(uq, 0, 2 * pr + 1), cb, sb, 16)
        qm_ref[tile(qm_ref, 2 * pr + 1)] = (qr * scale_m).astype(BF16)
        km_ref[tile(km_ref, 2 * pr)] = cols(ukv, 0, pr).astype(BF16)
        km_ref[tile(km_ref, 2 * pr + 1)] = kr
    for j in range(4):
        vm_ref[tile(vm_ref, j)] = _with_ones(cols(ukv, 256, j)).astype(BF16)
    p_ref[...] = p_new


def _project(x, mod3, mod_row, g, w, tables, gq2, gk2, bd, gcq, gckv, wuq, wukv):
    b, s, d = x.shape
    tm = min(PROJ_ROWS, s)
    nt = s // tm
    n_tiles = b * nt
    widths = (512, 256, 256, 256, 256, 512, 512, 512, 512)
    log2e = math.log2(math.e)
    body = functools.partial(_proj_body, scale_a=HEAD_DIM ** -0.5 * log2e,
                             scale_b=DIFF_QK_DIM ** -0.5 * log2e,
                             scale_m=(MLA_NOPE_DIM + MLA_ROPE_DIM) ** -0.5 * log2e)
    cur = lambda j: (jnp.minimum(j, n_tiles - 1) // nt, jnp.minimum(j, n_tiles - 1) % nt)
    prev = lambda j: (jnp.maximum(j - 1, 0) // nt, jnp.maximum(j - 1, 0) % nt)
    const = lambda shape: pl.BlockSpec(shape, lambda j: (0,) * len(shape))
    tab = pl.BlockSpec((tm, LANES), lambda j: (prev(j)[1], 0))
    return pl.pallas_call(
        body,
        grid=(n_tiles + 1,),
        in_specs=[pl.BlockSpec((1, tm, d), lambda j: (cur(j)[0], cur(j)[1], 0)),
                  pl.BlockSpec((1, N_MOD, d), lambda j: (mod_row(cur(j)[0]), 0, 0)),
                  const((1, d)), const(w.shape), tab, tab, tab, tab,
                  const((1, LANES)), const((1, LANES)), const((LANES, LANES)),
                  const(gcq.shape), const(gckv.shape), const(wuq.shape), const(wukv.shape)],
        out_specs=[pl.BlockSpec((1, tm, wd), lambda j: (prev(j)[0], prev(j)[1], 0)) for wd in widths],
        out_shape=[jax.ShapeDtypeStruct((b, s, wd), BF16) for wd in widths],
        scratch_shapes=[pltpu.VMEM((tm, w.shape[1]), F32)],
        compiler_params=_params("arbitrary"),
        name="project",
    )(x, mod3, g, w, *tables, gq2, gk2, bd, gcq, gckv, wuq, wukv)


def _attn_step(qs, kv_refs, s_ref, mp_ref, mb_ref, acc_ref):
    step = pl.program_id(0)

    @pl.when(step == 0)
    def _():
        s_ref[...] = jnp.zeros(s_ref.shape, F32)
        mb_ref[...] = jnp.zeros(mb_ref.shape, F32)
        acc_ref[...] = jnp.ones(acc_ref.shape, F32)

    raw = acc_ref[...]
    done = raw * (1.0 / raw[:, HEAD_DIM:HEAD_DIM + 1])

    acc = None
    chunks = [(k_ref, v_ref, slice(i * KEY_CHUNK, (i + 1) * KEY_CHUNK))
              for k_ref, v_ref in kv_refs for i in range(k_ref.shape[1] // KEY_CHUNK)]
    for c, (k_ref, v_ref, keys) in enumerate(chunks):
        s_new = lax.dot_general(qs, k_ref[0, keys, :], (((1,), (1,)), ((), ())),
                                preferred_element_type=F32)
        s_old = s_ref[c]
        m_old = mb_ref[...]
        p0 = jnp.exp2(s_old[:, :LANES] - m_old)
        p1 = jnp.exp2(s_old[:, LANES:] - m_old)
        part = jnp.dot(jnp.concatenate([p0, p1], axis=1).astype(BF16), v_ref[0, keys, :],
                       preferred_element_type=F32)
        acc = part if acc is None else acc + part
        s_ref[c] = s_new
        mc = jnp.maximum(s_new[:, :LANES], s_new[:, LANES:])
        mp_ref[...] = mc if c == 0 else jnp.maximum(mp_ref[...], mc)
    acc_ref[...] = acc
    mb_ref[...] = jnp.broadcast_to(jnp.max(mp_ref[...], axis=-1, keepdims=True), mb_ref.shape)
    return done


def _split_kv(refs, n_kv):
    return [(refs[2 * i], refs[2 * i + 1]) for i in range(n_kv)], refs[2 * n_kv:]


def _pack_heads(even, odd):
    lo = _lane_iota(even.shape) < HEAD_DIM
    return jnp.where(lo, even, pltpu.roll(odd, HEAD_DIM, 1))


def _gqa_body(q_ref, *rest, n_kv, n_tiles):
    tq = q_ref.shape[1]
    lo = _lane_iota((tq, LANES)) < HEAD_DIM
    heads = []
    for j in range(2):
        t = q_ref[0, :, j * LANES:(j + 1) * LANES].astype(F32)
        heads.append(jnp.where(lo, t, 0.0))
        heads.append(jnp.where(lo, pltpu.roll(t, HEAD_DIM, 1), 0.0))
    qs = jnp.concatenate(heads, axis=0).astype(BF16)
    kv, (o_ref, *scratch) = _split_kv(rest, n_kv)
    o = _attn_step(qs, kv, *scratch)
    for j in range(2):
        o_ref[0, :, j * LANES:(j + 1) * LANES] = _pack_heads(
            o[(2 * j) * tq:(2 * j + 1) * tq], o[(2 * j + 1) * tq:(2 * j + 2) * tq]).astype(o_ref.dtype)


def _head_parity(n_tiles):
    step = pl.program_id(0)
    return jnp.minimum(step, n_tiles - 1) % 2, jnp.maximum(step - 2, 0) % 2


def _store_head(o_ref, res, parity):
    @pl.when(parity == 0)
    def _():
        o_ref[0, :, 0:HEAD_DIM] = res[:, 0:HEAD_DIM].astype(o_ref.dtype)

    @pl.when(parity == 1)
    def _():
        o_ref[0, :, HEAD_DIM:LANES] = pltpu.roll(res, HEAD_DIM, 1)[:, HEAD_DIM:LANES].astype(o_ref.dtype)


def _diff_body(q_ref, *rest, n_kv, n_tiles, lam_init):
    kv, (lq1_ref, lk1_ref, lq2_ref, lk2_ref, gsub_ref, o_ref, *scratch) = _split_kv(rest, n_kv)
    tq = q_ref.shape[1]
    par, par_done = _head_parity(n_tiles)
    lane = _lane_iota((tq, LANES))
    t = q_ref[0]
    zero = jnp.zeros_like(t)
    qs = jnp.concatenate([jnp.where((lane // DIFF_QK_DIM) == 2 * par + j, t, zero) for j in range(2)],
                         axis=0)
    o = _attn_step(qs, kv, *scratch)
    lam = (jnp.exp(jnp.sum(lq1_ref[...] * lk1_ref[...], axis=-1, keepdims=True))
           - jnp.exp(jnp.sum(lq2_ref[...] * lk2_ref[...], axis=-1, keepdims=True)) + lam_init)
    d = o[0:tq] - lam * o[tq:2 * tq]
    ms = jnp.sum(jnp.where(lane < DIFF_V_DIM, d * d, 0.0), axis=-1, keepdims=True) * (1.0 / DIFF_V_DIM)
    _store_head(o_ref, (d * lax.rsqrt(ms + NORM_EPS) * gsub_ref[...]) * (1.0 - lam_init), par_done)


def _mla_body(q_ref, *rest, n_kv, n_tiles):
    kv, (o_ref, *scratch) = _split_kv(rest, n_kv)
    par, par_done = _head_parity(n_tiles)
    t = q_ref[0]
    lane = _lane_iota(t.shape)
    nope0 = MLA_NOPE_DIM * par
    rope0 = LANES + MLA_ROPE_DIM * par
    mine = ((lane >= nope0) & (lane < nope0 + MLA_NOPE_DIM)) | (
        (lane >= rope0) & (lane < rope0 + MLA_ROPE_DIM))
    qs = jnp.where(mine, t, jnp.zeros_like(t))
    _store_head(o_ref, _attn_step(qs, kv, *scratch), par_done)


def _attention(body, q, kvs, extra, *, n_groups, wq, wk, n_s, tq, per_head):
    b, q_rows, _ = q.shape
    q_tiles = q_rows // tq
    wo = LANES if per_head else wq
    heads = 2 if per_head else 1
    n_chunks = sum(k.shape[1] // KEY_CHUNK for k, _ in kvs)
    m_rows = n_s * tq
    n_tiles = b * n_groups * q_tiles * heads

    def split(t):
        t, par = t // heads, t % heads
        return t // (n_groups * q_tiles), (t // q_tiles) % n_groups, t % q_tiles, par

    cur = lambda j: split(jnp.minimum(j, n_tiles - 1))
    prev = lambda j: split(jnp.clip(j - 1, 0, n_tiles - 1))
    done = lambda j: split(jnp.maximum(j - 2, 0))
    kv_specs, kv_args = [], []
    for k, v in kvs:
        kv_specs += [pl.BlockSpec((1, k.shape[1], wk), lambda j: (cur(j)[0], 0, cur(j)[1])),
                     pl.BlockSpec((1, v.shape[1], LANES),
                                  lambda j: (prev(j)[0], 0, prev(j)[1] * heads + prev(j)[3]))]
        kv_args += [k, v]
    extra_specs = [pl.BlockSpec(e.shape, lambda j: (0, 0)) for e in extra]
    return pl.pallas_call(
        functools.partial(body, n_kv=len(kvs), n_tiles=n_tiles),
        grid=(n_tiles + 2,),
        in_specs=[pl.BlockSpec((1, tq, wq), lambda j: (cur(j)[0], cur(j)[2], cur(j)[1]))]
        + kv_specs + extra_specs,
        out_specs=pl.BlockSpec((1, tq, wo), lambda j: (done(j)[0], done(j)[2], done(j)[1])),
        out_shape=jax.ShapeDtypeStruct((b, q_rows, n_groups * wo), BF16),
        scratch_shapes=[pltpu.VMEM((n_chunks, m_rows, KEY_CHUNK), F32),
                        pltpu.VMEM((m_rows, LANES), F32),
                        pltpu.VMEM((m_rows, LANES), F32),
                        pltpu.VMEM((m_rows, LANES), F32)],
        compiler_params=_params("arbitrary"),
        name=body.__name__.strip("_"),
    )(q, *kv_args, *extra)


def _router_gates(logits):
    lane = _lane_iota(logits.shape).astype(F32)
    neg = jnp.float32(-jnp.inf)
    z = jnp.where(lane < N_EXPERTS, logits, neg)
    m1 = jnp.max(z, axis=-1, keepdims=True)
    i1 = jnp.min(jnp.where(z == m1, lane, float(LANES)), axis=-1, keepdims=True)
    z2 = jnp.where(lane == i1, neg, z)
    m2 = jnp.max(z2, axis=-1, keepdims=True)
    i2 = jnp.min(jnp.where(z2 == m2, lane, float(LANES)), axis=-1, keepdims=True)
    e2 = jnp.exp(m2 - m1)
    den = 1.0 + e2
    return jnp.where(lane == i1, 1.0 / den, 0.0) + jnp.where(lane == i2, e2 / den, 0.0)


def _mixer_residual(x_ref, oa_ref, ob_ref, om_ref, w_ref, mod, g_ref):
    wa, wb = oa_ref.shape[2], ob_ref.shape[2]
    y = jnp.dot(oa_ref[0], w_ref[0:wa, :], preferred_element_type=F32)
    y += jnp.dot(ob_ref[0], w_ref[wa:wa + wb, :], preferred_element_type=F32)
    y += jnp.dot(om_ref[0], w_ref[wa + wb:, :], preferred_element_type=F32)
    x = x_ref[0] + mod[2:3] * y
    n = x * _rms_rows(x, x.shape[-1]) * g_ref[...]
    return x, n * (1.0 + mod[4:5]) + mod[3:4]


def _mixer_specs(x, oa, ob, om, w, mod_row, tm):
    d = x.shape[2]
    row = lambda bi, i: (bi, i, 0)
    return [pl.BlockSpec((1, tm, d), row),
            pl.BlockSpec((1, tm, oa.shape[2]), row),
            pl.BlockSpec((1, tm, ob.shape[2]), row),
            pl.BlockSpec((1, tm, om.shape[2]), row),
            pl.BlockSpec(w.shape, lambda bi, i: (0, 0)),
            pl.BlockSpec((1, N_MOD, d), lambda bi, i: (mod_row(bi), 0, 0)),
            pl.BlockSpec((1, d), lambda bi, i: (0, 0))]


def _outproj_body(x_ref, oa_ref, ob_ref, om_ref, w_ref, mod_ref, g_ref, wr_ref,
                  xo_ref, h_ref, gates_ref, gates_t_ref, hprev_ref):
    @pl.when(pl.program_id(0) == 0)
    def _():
        hprev_ref[...] = jnp.zeros(hprev_ref.shape, F32)

    h_prev = hprev_ref[...]
    h_hi = h_prev.astype(BF16)
    h_lo = (h_prev - h_hi.astype(F32)).astype(BF16)
    logits = (jnp.dot(h_hi, wr_ref[0], preferred_element_type=F32)
              + jnp.dot(h_lo, wr_ref[0], preferred_element_type=F32)
              + jnp.dot(h_hi, wr_ref[1], preferred_element_type=F32))
    gates = _router_gates(logits)
    gates_ref[0] = gates
    gates_t_ref[0] = gates.T[:gates_t_ref.shape[1]]

    x, h = _mixer_residual(x_ref, oa_ref, ob_ref, om_ref, w_ref, mod_ref[0], g_ref)
    xo_ref[0] = x
    h_ref[0] = h.astype(BF16)
    hprev_ref[...] = h


def _out_project_routed(x, oa, ob, om, w, mod3, g, mod_row, w_router):
    b, s, d = x.shape
    tm = min(PROJ_ROWS, s)
    nt = s // tm
    n_tiles = b * nt
    cur = lambda j: (jnp.minimum(j, n_tiles - 1) // nt, jnp.minimum(j, n_tiles - 1) % nt)
    prev = lambda j: (jnp.maximum(j - 1, 0) // nt, jnp.maximum(j - 1, 0) % nt)
    flat = lambda spec: pl.BlockSpec(spec.block_shape, lambda j, f=spec.index_map: f(*cur(j)))
    row = lambda j: (cur(j)[0], cur(j)[1], 0)
    return pl.pallas_call(
        _outproj_body,
        grid=(n_tiles + 1,),
        in_specs=[flat(sp) for sp in _mixer_specs(x, oa, ob, om, w, mod_row, tm)]
        + [pl.BlockSpec(w_router.shape, lambda j: (0, 0, 0))],
        out_specs=[pl.BlockSpec((1, tm, d), row), pl.BlockSpec((1, tm, d), row),
                   pl.BlockSpec((1, tm, LANES), lambda j: (prev(j)[0], prev(j)[1], 0)),
                   pl.BlockSpec((1, GATE_ROWS, tm), lambda j: (prev(j)[0], 0, prev(j)[1]))],
        out_shape=[jax.ShapeDtypeStruct((b, s, d), F32), jax.ShapeDtypeStruct((b, s, d), BF16),
                   jax.ShapeDtypeStruct((b, s, LANES), F32), jax.ShapeDtypeStruct((b, GATE_ROWS, s), F32)],
        scratch_shapes=[pltpu.VMEM((tm, d), F32)],
        compiler_params=_params("arbitrary"),
        name="out_project_routed",
    )(x, oa, ob, om, w, mod3, g, w_router)


def _swiglu_chunks(h, wg_ref, wu_ref, wo_ref, lead):
    acc = None
    for j in range(wo_ref.shape[len(lead)] // FFN_CHUNK):
        cols = slice(j * FFN_CHUNK, (j + 1) * FFN_CHUNK)
        gate = jnp.dot(h, wg_ref[lead + (slice(None), cols)], preferred_element_type=F32)
        up = jnp.dot(h, wu_ref[lead + (slice(None), cols)], preferred_element_type=F32)
        a = (gate * jax.nn.sigmoid(gate) * up).astype(BF16)
        part = jnp.dot(a, wo_ref[lead + (cols, slice(None))], preferred_element_type=F32)
        acc = part if acc is None else acc + part
    return acc


def _residual_out(x, gate, y, gf_ref, final_norm):
    x = x + gate * y
    if final_norm:
        x = x * _rms_rows(x, x.shape[-1]) * gf_ref[...]
    return x


def _ffn_body(x_ref, oa_ref, ob_ref, om_ref, w_ref, mod_ref, g_ref, wg_ref, wu_ref, wo_ref, gf_ref,
              o_ref, *, final_norm):
    mod = mod_ref[0]
    x, h = _mixer_residual(x_ref, oa_ref, ob_ref, om_ref, w_ref, mod, g_ref)
    y = _swiglu_chunks(h.astype(BF16), wg_ref, wu_ref, wo_ref, ())
    o_ref[0] = _residual_out(x, mod[5:6], y, gf_ref, final_norm)


def _mix_dense_ffn(x, oa, ob, om, w, mod3, g, mod_row, wg, wu, wo, g_final, final_norm):
    b, s, d = x.shape
    tm = min(FFN_ROWS, s)
    resident = lambda a: pl.BlockSpec(a.shape, lambda bi, i: (0, 0), pipeline_mode=pl.Buffered(1))
    return pl.pallas_call(
        functools.partial(_ffn_body, final_norm=final_norm),
        grid=(b, s // tm),
        in_specs=_mixer_specs(x, oa, ob, om, w, mod_row, tm)
        + [resident(wg), resident(wu), resident(wo), pl.BlockSpec((1, d), lambda bi, i: (0, 0))],
        out_specs=pl.BlockSpec((1, tm, d), lambda bi, i: (bi, i, 0)),
        out_shape=jax.ShapeDtypeStruct((b, s, d), F32),
        compiler_params=_params("parallel", "arbitrary"),
        name="dense_ffn",
    )(x, oa, ob, om, w, mod3, g, wg, wu, wo, g_final)


def _moe_body(x_ref, h_ref, gates_ref, gates_t_ref, tri_ref, tri_t_ref, wg_ref, wu_ref, wo_ref,
              mod_ref, gf_ref, o_ref, tot_ref, rank_ref, rank_t_ref, *, final_norm):
    e = pl.program_id(2)
    tm = h_ref.shape[1]
    parts = [slice(k * MOE_PART, (k + 1) * MOE_PART) for k in range(tm // MOE_PART)]
    gates = gates_ref[0]
    lane = _lane_iota(gates.shape)

    @pl.when(e == 0)
    def _():
        tot_ref[...] = jnp.zeros(tot_ref.shape, F32)
        live = (gates > 0.0).astype(BF16)
        live_t = (gates_t_ref[0] > 0.0).astype(BF16)
        for part in parts:
            rank_ref[part, :] = jnp.dot(tri_ref[...], live[part], preferred_element_type=F32)
            rank_t_ref[:, part] = jnp.dot(live_t[:, part], tri_t_ref[...], preferred_element_type=F32)

    pick = lambda a: jnp.sum(jnp.where(lane == e, a, 0.0), axis=-1, keepdims=True)
    rank_row = rank_t_ref[pl.ds(e, 1), :]
    fits = jnp.max(rank_row) <= float(MOE_SLOTS)

    @pl.when(fits)
    def _():
        gate_e = pick(gates)
        rank_e = pick(rank_ref[...])
        live_row = gates_t_ref[0, pl.ds(e, 1), :] > 0.0
        slot_sub = lax.broadcasted_iota(jnp.int32, (MOE_SLOTS, MOE_PART), 0).astype(F32) + 1.0
        rows = []
        for part in parts:
            gather = jnp.where((rank_row[:, part] == slot_sub) & live_row[:, part], 1.0, 0.0)
            rows.append(jnp.dot(gather.astype(BF16), h_ref[0, part, :],
                                preferred_element_type=F32).astype(BF16))
        y = _swiglu_chunks(jnp.concatenate(rows, axis=0), wg_ref, wu_ref, wo_ref, (0,))
        y_hi = y.astype(BF16)
        y_lo = (y - y_hi.astype(F32)).astype(BF16)
        slot_lane = _lane_iota((MOE_PART, MOE_SLOTS)).astype(F32) + 1.0
        for k, part in enumerate(parts):
            slots = slice(k * MOE_SLOTS, (k + 1) * MOE_SLOTS)
            scatter = jnp.where((rank_e[part] == slot_lane) & (gate_e[part] > 0.0), 1.0, 0.0).astype(BF16)
            back = (jnp.dot(scatter, y_hi[slots], preferred_element_type=F32)
                    + jnp.dot(scatter, y_lo[slots], preferred_element_type=F32))
            tot_ref[part, :] += gate_e[part] * back

    @pl.when(jnp.logical_not(fits))
    def _():
        tot_ref[...] += pick(gates) * _swiglu_chunks(h_ref[0], wg_ref, wu_ref, wo_ref, (0,))

    @pl.when(e == pl.num_programs(2) - 1)
    def _():
        o_ref[0] = _residual_out(x_ref[0], mod_ref[0][5:6], tot_ref[...], gf_ref, final_norm)


def _moe_ffn(x, h, gates, gates_t, wg, wu, wo, mod3, mod_row, g_final, final_norm):
    b, s, d = x.shape
    tm = min(MOE_ROWS, s)
    tp = MOE_PART
    tri = jnp.asarray(np.tril(np.ones((tp, tp), np.float32)), BF16)
    row = lambda bi, i, e: (bi, i, 0)
    const = lambda bi, i, e: (0, 0)
    expert = lambda a: pl.BlockSpec((1,) + a.shape[1:], lambda bi, i, e: (e, 0, 0))
    return pl.pallas_call(
        functools.partial(_moe_body, final_norm=final_norm),
        grid=(b, s // tm, wg.shape[0]),
        in_specs=[pl.BlockSpec((1, tm, d), row),
                  pl.BlockSpec((1, tm, d), row),
                  pl.BlockSpec((1, tm, LANES), row),
                  pl.BlockSpec((1, GATE_ROWS, tm), lambda bi, i, e: (bi, 0, i)),
                  pl.BlockSpec((tp, tp), const), pl.BlockSpec((tp, tp), const),
                  expert(wg), expert(wu), expert(wo),
                  pl.BlockSpec((1, N_MOD, d), lambda bi, i, e: (mod_row(bi), 0, 0)),
                  pl.BlockSpec((1, d), const)],
        out_specs=pl.BlockSpec((1, tm, d), row),
        out_shape=jax.ShapeDtypeStruct((b, s, d), F32),
        scratch_shapes=[pltpu.VMEM((tm, d), F32), pltpu.VMEM((tm, LANES), F32),
                        pltpu.VMEM((GATE_ROWS, tm), F32)],
        compiler_params=_params("parallel", "parallel", "arbitrary"),
        name="moe_ffn",
    )(x, h, gates, gates_t, tri, tri.T, wg, wu, wo, mod3, g_final)


def _deinterleave(n):
    return np.concatenate([np.arange(0, n, 2), np.arange(1, n, 2)])


def _in_proj_columns(pad):
    pads = lambda n: np.full((n,), pad)
    cols = []
    for hd in range(GQA_HEADS):
        cols.append(64 * hd + _deinterleave(64))
    for g in range(GQA_KV_HEADS):
        cols += [512 + 64 * g + _deinterleave(64), pads(64)]
    for g in range(GQA_KV_HEADS):
        cols += [640 + 64 * g + np.arange(64), pads(64)]
    for u in range(2 * DIFF_HEADS):
        cols.append(768 + 32 * u + _deinterleave(32))
    for u in range(2 * DIFF_HEADS):
        cols.append(1024 + 32 * u + _deinterleave(32))
    for hd in range(DIFF_HEADS):
        cols += [1280 + 64 * hd + np.arange(64), pads(64)]
    cols += [1536 + np.arange(MLA_Q_RANK), pads(64)]
    cols.append(1728 + np.arange(MLA_KV_RANK))
    cols += [1856 + _deinterleave(32), 1856 + _deinterleave(32), pads(64)]
    return np.concatenate(cols)


def _uq_columns(pad):
    per = MLA_NOPE_DIM + MLA_ROPE_DIM
    cols = []
    for pr in range(2):
        h0, h1 = 2 * pr, 2 * pr + 1
        cols += [per * h0 + np.arange(64), per * h1 + np.arange(64),
                 per * h0 + 64 + _deinterleave(32), per * h1 + 64 + _deinterleave(32),
                 np.full((64,), pad)]
    return np.concatenate(cols)


def _ukv_columns(pad):
    per = MLA_NOPE_DIM + MLA_V_DIM
    k = [per * hd + np.arange(64) for hd in range(MLA_HEADS)]
    v = []
    for hd in range(MLA_HEADS):
        v += [per * hd + 64 + np.arange(64), np.full((64,), pad)]
    return np.concatenate(k + v)


def _take_cols(w, cols):
    w_ext = jnp.concatenate([w, jnp.zeros((w.shape[0], 1), w.dtype)], axis=1)
    return jnp.take(w_ext, jnp.asarray(cols), axis=1)


def _rope_tables(s):
    t = np.arange(s)
    rows = (t // GRID_W).astype(np.float32)
    cols = (t % GRID_W).astype(np.float32)
    out = []
    for dim in (HEAD_DIM, DIFF_QK_DIM):
        quarter = dim // 4
        half = dim // 2
        inv_freq = (ROPE_THETA ** (-np.arange(quarter, dtype=np.float32) / quarter)).astype(np.float32)
        ang = np.concatenate([rows[:, None] * inv_freq, cols[:, None] * inv_freq], axis=-1)
        lane = np.arange(LANES)
        idx = (lane % dim) % half
        sign = np.where((lane % dim) < half, -1.0, 1.0).astype(np.float32)
        out += [jnp.asarray(np.cos(ang)[:, idx], F32), jnp.asarray(np.sin(ang)[:, idx] * sign, F32)]
    return out


def kernel(x, c, ctx, c_ctx, w_mod, b_mod, g_attn, g_ffn, w_in, w_out, gqa_gq, gqa_gk,
           diff_lq1, diff_lk1, diff_lq2, diff_lk2, diff_gsub, mla_gcq, mla_gckv, mla_wuq, mla_wukv,
           ffn_w_in, ffn_w_out, moe_router, moe_w_in, moe_w_out, g_final):
    b, s, d = x.shape
    n_ctx = ctx.shape[1]
    depth = w_mod.shape[0]
    tables = _rope_tables(s)
    no_rotation = [jnp.ones((n_ctx, LANES), F32), jnp.zeros((n_ctx, LANES), F32)] * 2
    in_cols = _in_proj_columns(w_in.shape[2])
    uq_cols = _uq_columns(mla_wuq.shape[2])
    ukv_cols = _ukv_columns(mla_wukv.shape[2])
    perm64 = _deinterleave(HEAD_DIM)
    bd = jnp.asarray(np.kron(np.eye(LANES // HEAD_DIM), np.full((HEAD_DIM, HEAD_DIM), 1.0 / HEAD_DIM)), F32)

    mod_rows = 16
    c_all = jnp.zeros((mod_rows, d), F32).at[:b].set(c).at[b].set(c_ctx)
    pad_lanes = lambda v: jnp.zeros((1, LANES), F32).at[0, :v.shape[0]].set(v)

    mods = _modulation(c_all, w_mod, b_mod)
    xc = ctx
    for l in range(depth):
        need_ctx = l < depth - 1
        lam_init = 0.8 - 0.6 * math.exp(-0.3 * l)
        mod3 = mods[l].reshape(mod_rows, N_MOD, d)

        w_in_p = _take_cols(w_in[l], in_cols).astype(BF16)
        wuq_p = jnp.zeros((2 * LANES, 512), F32).at[:MLA_Q_RANK].set(
            _take_cols(mla_wuq[l], uq_cols)).astype(BF16)
        wukv_p = _take_cols(mla_wukv[l], ukv_cols).astype(BF16)
        gq2 = jnp.tile(gqa_gq[l][perm64], 2).reshape(1, LANES)
        gk2 = jnp.tile(gqa_gk[l][perm64], 2).reshape(1, LANES)
        gcq = jnp.zeros((1, 2 * LANES), F32).at[0, :MLA_Q_RANK].set(mla_gcq[l])
        gckv = mla_gckv[l].reshape(1, MLA_KV_RANK)

        proj_w = (g_attn[l].reshape(1, d), w_in_p)
        proj_aux = (gq2, gk2, bd, gcq, gckv, wuq_p, wukv_p)
        lat = _project(x, mod3, lambda bi: bi, *proj_w, tables, *proj_aux)
        ctxp = _project(xc, mod3, lambda bi: b, *proj_w, no_rotation, *proj_aux)

        diff_extra = [pad_lanes(diff_lq1[l]), pad_lanes(diff_lk1[l]), pad_lanes(diff_lq2[l]),
                      pad_lanes(diff_lk2[l]), pad_lanes(diff_gsub[l])]
        diff_body = functools.partial(_diff_body, lam_init=lam_init)
        diff_body.__name__ = "_diff_body"

        def attend(q, streams):
            kv = lambda ik, iv: [(p[ik], p[iv]) for p in streams]
            rows = q[0].shape[1]
            oa = _attention(_gqa_body, q[0], kv(1, 2), [], n_groups=2, wq=256, wk=LANES,
                            n_s=4, tq=min(ATTN_ROWS // 4, rows), per_head=False)
            ob = _attention(diff_body, q[3], kv(4, 5), diff_extra, n_groups=2, wq=LANES, wk=LANES,
                            n_s=2, tq=min(ATTN_ROWS // 2, rows), per_head=True)
            om = _attention(_mla_body, q[6], kv(7, 8), [], n_groups=2, wq=256, wk=256,
                            n_s=1, tq=min(ATTN_ROWS, rows), per_head=True)
            return oa, ob, om

        w_out_b = w_out[l].astype(BF16)
        g2 = g_ffn[l].reshape(1, d)
        dense = l % 2 == 0
        if dense:
            n_hidden = ffn_w_out.shape[1]
            wg = ffn_w_in[l // 2][:, :n_hidden].astype(BF16)
            wu = ffn_w_in[l // 2][:, n_hidden:].astype(BF16)
            wo = ffn_w_out[l // 2].astype(BF16)
            w_router = None
        else:
            n_hidden = moe_w_out.shape[2]
            wi = moe_w_in[l // 2]
            wg = wi[:, :, :n_hidden].astype(BF16)
            wu = wi[:, :, n_hidden:].astype(BF16)
            wo = moe_w_out[l // 2].astype(BF16)
            wr = jnp.zeros((d, LANES), F32).at[:, :N_EXPERTS].set(moe_router[l // 2])
            wr_hi = wr.astype(BF16)
            w_router = jnp.stack([wr_hi, (wr - wr_hi.astype(F32)).astype(BF16)])
        last = l == depth - 1

        def channel_mix(xs, attn_out, mod_row, final_norm):
            gf = g_final.reshape(1, d)
            if dense:
                return _mix_dense_ffn(xs, *attn_out, w_out_b, mod3, g2, mod_row, wg, wu, wo, gf, final_norm)
            res = _out_project_routed(xs, *attn_out, w_out_b, mod3, g2, mod_row, w_router)
            return _moe_ffn(*res, wg, wu, wo, mod3, mod_row, gf, final_norm)

        x_new = channel_mix(x, attend(lat, [lat, ctxp]), lambda bi: bi, last)
        if need_ctx:
            xc = channel_mix(xc, attend(ctxp, [ctxp]), lambda bi: b, False)
        x = x_new
    return x
```
